```python
import math
import jax, jax.numpy as jnp
from jax import lax
import numpy as np

D_MODEL = 1024
BATCH = 4
SEQ = 8192
DEPTH = 2

CHUNK = 64
EPS = 1e-6
F32 = jnp.float32
NEG_INF_SCORE = -1e30

SSM_WIDTH = 256
SSM_GROUP = 16
SSM_GROUPS = SSM_WIDTH // SSM_GROUP
SSM_STATE = 64
SSM_DT_MIN = 1e-3
SSM_DT_MAX = 1e-1

DSA_HEADS = 6
DSA_HEAD_DIM = 64
DSA_WIDTH = DSA_HEADS * DSA_HEAD_DIM
IDX_HEADS = 4
IDX_DIM = 32
DSA_TOPK = 256
Q_BLOCK = 128
ROPE_THETA = 500000.0
ROPE_FRACTION = 4

RET_HEADS = 4
RET_QK_DIM = 48
RET_V_DIM = 96
RET_WIDTH = RET_HEADS * RET_V_DIM
RET_THETA = 10000.0

N_BRANCHES = 3
D_FF = 4 * D_MODEL

IN_SIZES = (
    SSM_WIDTH,
    DSA_WIDTH,
    DSA_HEAD_DIM,
    DSA_HEAD_DIM,
    IDX_HEADS * IDX_DIM,
    IDX_DIM,
    IDX_HEADS,
    RET_HEADS * RET_QK_DIM,
    RET_HEADS * RET_QK_DIM,
    RET_WIDTH,
    RET_WIDTH,
    N_BRANCHES * D_MODEL,
)
IN_DIM = sum(IN_SIZES)

kernel_name = 'hybrid_s5_dsa_retention_block'


def rms_norm(x, g):
    xf = x.astype(F32)
    y = xf * lax.rsqrt(jnp.mean(xf * xf, axis=-1, keepdims=True) + EPS)
    return (y * g.astype(F32)).astype(x.dtype)


def rope(x, positions, rot_dim, theta):
    half = rot_dim // 2
    inv = jnp.exp(-math.log(theta) * jnp.arange(half, dtype=F32) * (2.0 / rot_dim))
    ang = positions.astype(F32)[:, :, None, None] * inv
    cos, sin = jnp.cos(ang), jnp.sin(ang)
    xf = x.astype(F32)
    x1 = xf[..., :half]
    x2 = xf[..., half:rot_dim]
    out = jnp.concatenate([x1 * cos - x2 * sin, x2 * cos + x1 * sin, xf[..., rot_dim:]], axis=-1)
    return out.astype(x.dtype)


def split_cols(z):
    parts = []
    off = 0
    for n in IN_SIZES:
        parts.append(z[..., off:off + n])
        off += n
    return parts


def s5_mixer(u, lam_re, lam_im, log_step, b_re, b_im, c_re, c_im, d_skip, glu_w, glu_b):
    bsz, seq_len, _ = u.shape
    uf = u.astype(F32).reshape(bsz, seq_len, SSM_GROUPS, SSM_GROUP)
    lam_re = lam_re.astype(F32)
    lam_im = lam_im.astype(F32)
    step = jnp.exp(log_step.astype(F32))[:, None]
    mag = jnp.exp(lam_re * step)
    lb_re = mag * jnp.cos(lam_im * step)
    lb_im = mag * jnp.sin(lam_im * step)
    den = lam_re * lam_re + lam_im * lam_im
    f_re = ((lb_re - 1.0) * lam_re + lb_im * lam_im) / den
    f_im = (lb_im * lam_re - (lb_re - 1.0) * lam_im) / den
    b_re = b_re.astype(F32)
    b_im = b_im.astype(F32)
    bb_re = f_re[..., None] * b_re - f_im[..., None] * b_im
    bb_im = f_re[..., None] * b_im + f_im[..., None] * b_re
    bu_re = jnp.einsum('blgc,gpc->blgp', uf, bb_re)
    bu_im = jnp.einsum('blgc,gpc->blgp', uf, bb_im)
    a_re = jnp.broadcast_to(lb_re[None, None], (1, seq_len, SSM_GROUPS, SSM_STATE))
    a_im = jnp.broadcast_to(lb_im[None, None], (1, seq_len, SSM_GROUPS, SSM_STATE))

    def combine(ei, ej):
        ar_i, ai_i, br_i, bi_i = ei
        ar_j, ai_j, br_j, bi_j = ej
        ar = ar_j * ar_i - ai_j * ai_i
        ai = ar_j * ai_i + ai_j * ar_i
        br = ar_j * br_i - ai_j * bi_i + br_j
        bi = ar_j * bi_i + ai_j * br_i + bi_j
        return (ar, ai, br, bi)

    _, _, x_re, x_im = lax.associative_scan(combine, (a_re, a_im, bu_re, bu_im), axis=1)
    y = (jnp.einsum('blgp,gcp->blgc', x_re, c_re.astype(F32))
         - jnp.einsum('blgp,gcp->blgc', x_im, c_im.astype(F32))
         + d_skip.astype(F32) * uf)
    y = jax.nn.gelu(y.reshape(bsz, seq_len, SSM_WIDTH))
    y = y * jax.nn.sigmoid(y @ glu_w.astype(F32) + glu_b.astype(F32))
    return y.astype(u.dtype)


def dsa_mixer(q, k, v, q_idx, k_idx, w_idx):
    bsz, seq_len = q.shape[:2]
    topk = min(DSA_TOPK, seq_len // 4)
    nblk = seq_len // Q_BLOCK
    key_pos = jnp.arange(seq_len)
    kif = k_idx.astype(F32)

    def to_blocks(a):
        return a.reshape((bsz, nblk, Q_BLOCK) + a.shape[2:]).swapaxes(0, 1)

    def block(args):
        qb, qib, wb, start = args
        qpos = start + jnp.arange(Q_BLOCK)
        visible_end = (qpos // CHUNK + 1) * CHUNK
        allowed = key_pos[None, :] < visible_end[:, None]
        logits = jnp.einsum('bqhd,bsd->bqhs', qib.astype(F32), kif) * (IDX_DIM ** -0.5)
        score = jnp.einsum('bqhs,bqh->bqs', jax.nn.relu(logits), wb.astype(F32) * (IDX_HEADS ** -0.5))
        score = jnp.where(allowed[None], score, -jnp.inf)
        top_val, top_idx = lax.top_k(score, topk)
        valid = top_val > -jnp.inf
        kg = jax.vmap(lambda a, i: a[i])(k, top_idx)
        vg = jax.vmap(lambda a, i: a[i])(v, top_idx)
        s = jnp.einsum('bqhd,bqkd->bqhk', qb, kg).astype(F32) * (DSA_HEAD_DIM ** -0.5)
        s = jnp.where(valid[:, :, None, :], s, NEG_INF_SCORE)
        p = jax.nn.softmax(s, axis=-1).astype(vg.dtype)
        return jnp.einsum('bqhk,bqkd->bqhd', p, vg)

    starts = jnp.arange(nblk) * Q_BLOCK
    out = lax.map(block, (to_blocks(q), to_blocks(q_idx), to_blocks(w_idx), starts))
    return out.swapaxes(0, 1).reshape(bsz, seq_len, DSA_WIDTH)


def retention_mixer(q, k, v, gate, norm_g):
    bsz, seq_len = q.shape[:2]
    nc = seq_len // CHUNK
    qf = q.astype(F32)
    kf = k.astype(F32) * (RET_QK_DIM ** -0.5)
    vf = v.astype(F32)

    def to_chunks(a):
        return a.reshape(bsz, nc, CHUNK, RET_HEADS, a.shape[-1]).transpose(1, 0, 3, 2, 4)

    log_g = jnp.log1p(-jnp.exp2(-5.0 - jnp.arange(RET_HEADS, dtype=F32)))
    pos = jnp.arange(CHUNK, dtype=F32)
    diff = pos[:, None] - pos[None, :]
    intra = jnp.where(diff >= 0, jnp.exp(log_g[:, None, None] * jnp.maximum(diff, 0.0)), 0.0)
    q_dec = jnp.exp(log_g[:, None] * (pos + 1.0))[None, :, :, None]
    k_dec = jnp.exp(log_g[:, None] * (CHUNK - 1.0 - pos))[None, :, :, None]
    c_dec = jnp.exp(log_g * CHUNK)[None, :, None, None]

    def step(state, inp):
        qc, kc, vc = inp
        inner = jnp.einsum('bhcm,bhme->bhce', jnp.einsum('bhcd,bhmd->bhcm', qc, kc) * intra, vc)
        cross = jnp.einsum('bhcd,bhde->bhce', qc, state) * q_dec
        state = state * c_dec + jnp.einsum('bhmd,bhme->bhde', kc * k_dec, vc)
        return state, inner + cross

    init = jnp.zeros((bsz, RET_HEADS, RET_QK_DIM, RET_V_DIM), F32)
    _, ys = lax.scan(step, init, (to_chunks(qf), to_chunks(kf), to_chunks(vf)))
    y = ys.transpose(1, 0, 3, 2, 4).reshape(bsz, seq_len, RET_HEADS, RET_V_DIM)
    mu = jnp.mean(y, axis=-1, keepdims=True)
    var = jnp.mean(jnp.square(y - mu), axis=-1, keepdims=True)
    yn = ((y - mu) * lax.rsqrt(var + EPS)).reshape(bsz, seq_len, RET_WIDTH) * norm_g.astype(F32)
    return (jax.nn.silu(gate.astype(F32)) * yn).astype(gate.dtype)


def setup_inputs(seed: int = 0) -> dict:
    key = jax.random.key(seed)
    ks = jax.random.split(key, 24)

    def nrm(k, shape, scale):
        return jax.random.normal(k, shape, F32) * scale

    x = nrm(ks[0], (BATCH, SEQ, D_MODEL), 1.0)
    start = jax.random.randint(ks[1], (BATCH, 1), 0, 64, dtype=jnp.int32) * CHUNK
    positions = (start + jnp.arange(SEQ, dtype=jnp.int32)[None, :]).astype(jnp.int32)
    norm1_g = 1.0 + nrm(ks[2], (DEPTH, D_MODEL), 0.02)
    w_in = nrm(ks[3], (DEPTH, D_MODEL, IN_DIM), D_MODEL ** -0.5)
    ssm_lambda_re = -0.5 + nrm(ks[4], (DEPTH, SSM_GROUPS, SSM_STATE), 0.01)
    ssm_lambda_im = jnp.broadcast_to(math.pi * jnp.arange(SSM_STATE, dtype=F32), (DEPTH, SSM_GROUPS, SSM_STATE))
    ssm_log_step = jax.random.uniform(ks[5], (DEPTH, SSM_GROUPS), F32, math.log(SSM_DT_MIN), math.log(SSM_DT_MAX))
    ssm_b_re = nrm(ks[6], (DEPTH, SSM_GROUPS, SSM_STATE, SSM_GROUP), (2 * SSM_GROUP) ** -0.5)
    ssm_b_im = nrm(ks[7], (DEPTH, SSM_GROUPS, SSM_STATE, SSM_GROUP), (2 * SSM_GROUP) ** -0.5)
    ssm_c_re = nrm(ks[8], (DEPTH, SSM_GROUPS, SSM_GROUP, SSM_STATE), SSM_STATE ** -0.5)
    ssm_c_im = nrm(ks[9], (DEPTH, SSM_GROUPS, SSM_GROUP, SSM_STATE), SSM_STATE ** -0.5)
    ssm_d = nrm(ks[10], (DEPTH, SSM_GROUPS, SSM_GROUP), 1.0)
    ssm_glu_w = nrm(ks[11], (DEPTH, SSM_WIDTH, SSM_WIDTH), SSM_WIDTH ** -0.5)
    ssm_glu_b = nrm(ks[12], (DEPTH, SSM_WIDTH), 0.02)
    ret_norm_g = 1.0 + nrm(ks[13], (DEPTH, RET_WIDTH), 0.02)
    w_proj_a = nrm(ks[14], (DEPTH, SSM_WIDTH, D_MODEL), SSM_WIDTH ** -0.5)
    w_proj_b = nrm(ks[15], (DEPTH, DSA_WIDTH, D_MODEL), DSA_WIDTH ** -0.5)
    w_proj_c = nrm(ks[16], (DEPTH, RET_WIDTH, D_MODEL), RET_WIDTH ** -0.5)
    w_out = nrm(ks[17], (DEPTH, D_MODEL, D_MODEL), D_MODEL ** -0.5)
    norm2_g = 1.0 + nrm(ks[18], (DEPTH, D_MODEL), 0.02)
    w_ff1 = nrm(ks[19], (DEPTH, D_MODEL, D_FF), D_MODEL ** -0.5)
    w_ff2 = nrm(ks[20], (DEPTH, D_FF, D_MODEL), D_FF ** -0.5)
    final_norm_g = 1.0 + nrm(ks[21], (D_MODEL,), 0.02)
    return {'x': x, 'positions': positions, 'norm1_g': norm1_g, 'w_in': w_in,
            'ssm_lambda_re': ssm_lambda_re, 'ssm_lambda_im': ssm_lambda_im, 'ssm_log_step': ssm_log_step,
            'ssm_b_re': ssm_b_re, 'ssm_b_im': ssm_b_im, 'ssm_c_re': ssm_c_re, 'ssm_c_im': ssm_c_im,
            'ssm_d': ssm_d, 'ssm_glu_w': ssm_glu_w, 'ssm_glu_b': ssm_glu_b, 'ret_norm_g': ret_norm_g,
            'w_proj_a': w_proj_a, 'w_proj_b': w_proj_b, 'w_proj_c': w_proj_c, 'w_out': w_out,
            'norm2_g': norm2_g, 'w_ff1': w_ff1, 'w_ff2': w_ff2, 'final_norm_g': final_norm_g}


def reference(x, positions, norm1_g, w_in, ssm_lambda_re, ssm_lambda_im, ssm_log_step,
              ssm_b_re, ssm_b_im, ssm_c_re, ssm_c_im, ssm_d, ssm_glu_w, ssm_glu_b, ret_norm_g,
              w_proj_a, w_proj_b, w_proj_c, w_out, norm2_g, w_ff1, w_ff2, final_norm_g):
    bsz, seq_len, _ = x.shape
    for l in range(DEPTH):
        h = rms_norm(x, norm1_g[l])
        z = h @ w_in[l]
        (u_ssm, dq, dk, dv, iq, ik, iw, rq, rk, rv, rg, gates) = split_cols(z)

        ya = s5_mixer(u_ssm, ssm_lambda_re[l], ssm_lambda_im[l], ssm_log_step[l], ssm_b_re[l], ssm_b_im[l],
                      ssm_c_re[l], ssm_c_im[l], ssm_d[l], ssm_glu_w[l], ssm_glu_b[l])

        dq = rope(dq.reshape(bsz, seq_len, DSA_HEADS, DSA_HEAD_DIM), positions, DSA_HEAD_DIM // ROPE_FRACTION, ROPE_THETA)
        dk = rope(dk.reshape(bsz, seq_len, 1, DSA_HEAD_DIM), positions, DSA_HEAD_DIM // ROPE_FRACTION, ROPE_THETA)[:, :, 0]
        iq = rope(iq.reshape(bsz, seq_len, IDX_HEADS, IDX_DIM), positions, IDX_DIM // ROPE_FRACTION, ROPE_THETA)
        ik = rope(ik.reshape(bsz, seq_len, 1, IDX_DIM), positions, IDX_DIM // ROPE_FRACTION, ROPE_THETA)[:, :, 0]
        yb = dsa_mixer(dq, dk, dv, iq, ik, iw)

        rq = rope(rq.reshape(bsz, seq_len, RET_HEADS, RET_QK_DIM), positions, RET_QK_DIM, RET_THETA)
        rk = rope(rk.reshape(bsz, seq_len, RET_HEADS, RET_QK_DIM), positions, RET_QK_DIM, RET_THETA)
        yc = retention_mixer(rq, rk, rv.reshape(bsz, seq_len, RET_HEADS, RET_V_DIM), rg, ret_norm_g[l])

        g = jax.nn.sigmoid(gates)
        merged = (g[..., :D_MODEL] * (ya @ w_proj_a[l])
                  + g[..., D_MODEL:2 * D_MODEL] * (yb @ w_proj_b[l])
                  + g[..., 2 * D_MODEL:] * (yc @ w_proj_c[l]))
        x = x + merged @ w_out[l]

        h2 = rms_norm(x, norm2_g[l])
        x = x + jnp.square(jax.nn.relu(h2 @ w_ff1[l])) @ w_ff2[l]
    return rms_norm(x, final_norm_g)
```

```python
import functools
import math

import jax
import jax.numpy as jnp
from jax import lax
from jax.experimental import pallas as pl
from jax.experimental.pallas import tpu as pltpu

F32 = jnp.float32
_BF = jnp.bfloat16
LANES = 128
SUBLANES = 8
VMEM_LIMIT = 56 * 1024 * 1024

D_MODEL = 1024
CHUNK = 64
EPS = 1e-6
NEG_INF_SCORE = -1e30
INT_MIN = -2 ** 31

SSM_WIDTH = 256
SSM_GROUP = 16
SSM_GROUPS = 16
SSM_STATE = 64
NSTATE = SSM_GROUPS * SSM_STATE

DSA_HEADS = 6
DSA_HEAD_DIM = 64
DSA_WIDTH = 384
IDX_HEADS = 4
IDX_DIM = 32
DSA_TOPK = 256
ROPE_THETA = 500000.0

RET_HEADS = 4
RET_QK_DIM = 48
RET_V_DIM = 96
RET_WIDTH = 384
RET_THETA = 10000.0
RET_QK_PAD = 256

D_FF = 4096

TM_IN = 256
S5_TILE = 512
S5_SEG = S5_TILE // SUBLANES
QB = 128
KT = 512
RET_C = 128
TM_OUT = 512
FF_CHUNK = 1024


def _rms(x, g):
    return x * lax.rsqrt(jnp.mean(x * x, axis=-1, keepdims=True) + EPS) * g


def _dot(a, b):
    return jnp.dot(a, b, preferred_element_type=F32)


def _dot_nt(a, b):
    return lax.dot_general(a, b, (((1,), (1,)), ((), ())), preferred_element_type=F32)


def _const_spec(shape):
    nd = len(shape)
    return pl.BlockSpec(shape, lambda *_: (0,) * nd)


def _params(sem):
    return pltpu.CompilerParams(dimension_semantics=sem, vmem_limit_bytes=VMEM_LIMIT)


def _rope_roll(z, cos, sin, sgn, shift):
    outs = []
    for c in range(z.shape[1] // LANES):
        zc = z[:, c * LANES:(c + 1) * LANES]
        rot = (pltpu.roll(zc, LANES - shift, 1) * sgn[0:1, :]
               + pltpu.roll(zc, shift, 1) * sgn[1:2, :])
        outs.append(zc * cos + rot * sin)
    return outs[0] if len(outs) == 1 else jnp.concatenate(outs, axis=1)


def _inproj_kernel(x_ref, g_ref, cd_ref, sd_ref, ci_ref, si_ref, cr_ref, sr_ref, sgd_ref, sgi_ref,
                   wu_ref, wdq_ref, wdk_ref, wdvt_ref, wiq_ref, wik_ref, wiwt_ref,
                   wrq_ref, wrk_ref, wrv_ref, wrg_ref,
                   u_ref, dq_ref, dk_ref, dvt_ref, iq_ref, ik_ref, iwt_ref,
                   rq_ref, rk_ref, rv_ref, rg_ref):
    h = _rms(x_ref[...], g_ref[...]).astype(_BF)
    u_ref[...] = _dot(h, wu_ref[...])

    cd, sd, sgd = cd_ref[...], sd_ref[...], sgd_ref[...]
    dq = _rope_roll(_dot(h, wdq_ref[...]), cd, sd, sgd, 8)
    dq_ref[...] = (dq * (DSA_HEAD_DIM ** -0.5)).astype(dq_ref.dtype)
    dk_ref[...] = _rope_roll(_dot(h, wdk_ref[...]), cd, sd, sgd, 8).astype(dk_ref.dtype)
    dvt_ref[...] = _dot_nt(wdvt_ref[...], h).astype(dvt_ref.dtype)

    ci, si, sgi = ci_ref[...], si_ref[...], sgi_ref[...]
    iq_ref[...] = _rope_roll(_dot(h, wiq_ref[...]), ci, si, sgi, 4) * (IDX_DIM ** -0.5)
    ik_ref[...] = _rope_roll(_dot(h, wik_ref[...]), ci, si, sgi, 4)
    iwt_ref[...] = _dot_nt(wiwt_ref[...], h) * (IDX_HEADS ** -0.5)

    cr, sr = cr_ref[...], sr_ref[...]

    def rope_split(z):
        x1, x2 = z[:, :LANES], z[:, LANES:]
        return jnp.concatenate([x1 * cr - x2 * sr, x2 * cr + x1 * sr], axis=1)

    rq_ref[...] = rope_split(_dot(h, wrq_ref[...]))
    rk_ref[...] = rope_split(_dot(h, wrk_ref[...])) * (RET_QK_DIM ** -0.5)
    rv_ref[...] = _dot(h, wrv_ref[...])
    rg_ref[...] = _dot(h, wrg_ref[...])


def _inproj(x, g, tabs, ws):
    bsz, seq, _ = x.shape
    tm = TM_IN
    grid = (bsz, seq // tm)
    row = lambda w: pl.BlockSpec((None, tm, w), lambda b, i: (b, i, 0))
    colT = lambda r: pl.BlockSpec((None, r, tm), lambda b, i: (b, 0, i))
    in_specs = ([row(D_MODEL), _const_spec(g.shape)] + [row(LANES)] * 6
                + [_const_spec(t.shape) for t in tabs[6:]] + [_const_spec(w.shape) for w in ws])
    out_shape = [
        jax.ShapeDtypeStruct((bsz, seq, SSM_WIDTH), F32),
        jax.ShapeDtypeStruct((bsz, seq, DSA_WIDTH), _BF),
        jax.ShapeDtypeStruct((bsz, seq, LANES), _BF),
        jax.ShapeDtypeStruct((bsz, LANES, seq), _BF),
        jax.ShapeDtypeStruct((bsz, seq, LANES), F32),
        jax.ShapeDtypeStruct((bsz, seq, LANES), F32),
        jax.ShapeDtypeStruct((bsz, SUBLANES, seq), F32),
        jax.ShapeDtypeStruct((bsz, seq, RET_QK_PAD), F32),
        jax.ShapeDtypeStruct((bsz, seq, RET_QK_PAD), F32),
        jax.ShapeDtypeStruct((bsz, seq, RET_WIDTH), F32),
        jax.ShapeDtypeStruct((bsz, seq, RET_WIDTH), F32),
    ]
    out_specs = [row(SSM_WIDTH), row(DSA_WIDTH), row(LANES), colT(LANES), row(LANES), row(LANES),
                 colT(SUBLANES), row(RET_QK_PAD), row(RET_QK_PAD), row(RET_WIDTH), row(RET_WIDTH)]
    return pl.pallas_call(
        _inproj_kernel, grid=grid, in_specs=in_specs, out_specs=out_specs, out_shape=out_shape,
        compiler_params=_params(("parallel", "parallel")), name="inproj",
    )(x, g, *tabs, *ws)


def _s5_kernel(u_ref, bbd_ref, cbd_ref, lam_ref, lam64_ref, pow_ref, d_ref, gw_ref, gb_ref,
               o_ref, st_ref, xm_ref, carry_ref):
    n = NSTATE

    @pl.when(pl.program_id(1) == 0)
    def _():
        carry_ref[...] = jnp.zeros_like(carry_ref)

    u = u_ref[...]
    st_ref[...] = _dot(u.astype(_BF), bbd_ref[...])

    lr = jnp.broadcast_to(lam_ref[:, :n], (SUBLANES, n))
    li = jnp.broadcast_to(lam_ref[:, n:], (SUBLANES, n))

    def step(j, c):
        xr, xi = c
        off = pl.multiple_of(j * SUBLANES, SUBLANES)
        ar = st_ref[pl.ds(off, SUBLANES), :n]
        ai = st_ref[pl.ds(off, SUBLANES), n:]
        nr = lr * xr - li * xi + ar
        ni = lr * xi + li * xr + ai
        st_ref[pl.ds(off, SUBLANES), :n] = nr
        st_ref[pl.ds(off, SUBLANES), n:] = ni
        return nr, ni

    zero = jnp.zeros((SUBLANES, n), F32)
    xr, xi = lax.fori_loop(0, S5_SEG, step, (zero, zero))

    l64r, l64i = lam64_ref[:, :n], lam64_ref[:, n:]
    cr, ci = carry_ref[:, :n], carry_ref[:, n:]
    for i in range(SUBLANES):
        xm_ref[i:i + 1, :n] = cr
        xm_ref[i:i + 1, n:] = ci
        er, ei = xr[i:i + 1, :], xi[i:i + 1, :]
        cr, ci = er + l64r * cr - l64i * ci, ei + l64r * ci + l64i * cr
    carry_ref[:, :n] = cr
    carry_ref[:, n:] = ci

    hr, hi = xm_ref[:, :n], xm_ref[:, n:]

    def fix(j, _):
        off = pl.multiple_of(j * SUBLANES, SUBLANES)
        pr = pow_ref[pl.ds(j, 1), :n]
        pi = pow_ref[pl.ds(j, 1), n:]
        st_ref[pl.ds(off, SUBLANES), :n] = st_ref[pl.ds(off, SUBLANES), :n] + pr * hr - pi * hi
        st_ref[pl.ds(off, SUBLANES), n:] = st_ref[pl.ds(off, SUBLANES), n:] + pr * hi + pi * hr
        return 0

    lax.fori_loop(0, S5_SEG, fix, 0)

    y = d_ref[...] * u
    kc = 512
    for k in range(2 * n // kc):
        y = y + _dot(st_ref[:, k * kc:(k + 1) * kc].astype(_BF), cbd_ref[k * kc:(k + 1) * kc, :])
    y = jax.nn.gelu(y)
    o_ref[...] = y * jax.nn.sigmoid(_dot(y.astype(_BF), gw_ref[...]) + gb_ref[...])


def _s5(u_perm, prm):
    bsz, seq, _ = u_perm.shape
    grid = (bsz, seq // S5_TILE)
    row = pl.BlockSpec((None, S5_TILE, SSM_WIDTH), lambda b, i: (b, i, 0))
    return pl.pallas_call(
        _s5_kernel, grid=grid,
        in_specs=[row] + [_const_spec(p.shape) for p in prm],
        out_specs=row,
        out_shape=jax.ShapeDtypeStruct((bsz, seq, SSM_WIDTH), F32),
        scratch_shapes=[pltpu.VMEM((S5_TILE, 2 * NSTATE), F32),
                        pltpu.VMEM((SUBLANES, 2 * NSTATE), F32),
                        pltpu.VMEM((1, 2 * NSTATE), F32)],
        compiler_params=_params(("arbitrary", "arbitrary")), name="s5",
    )(u_perm, *prm)


def _s5_params(lam_re, lam_im, log_step, b_re, b_im, c_re, c_im, d_skip, glu_w, glu_b):
    step = jnp.exp(log_step.astype(F32))[:, None]
    ere, eim = lam_re * step, lam_im * step
    mag = jnp.exp(ere)
    lb_re, lb_im = mag * jnp.cos(eim), mag * jnp.sin(eim)
    den = lam_re * lam_re + lam_im * lam_im
    f_re = ((lb_re - 1.0) * lam_re + lb_im * lam_im) / den
    f_im = (lb_im * lam_re - (lb_re - 1.0) * lam_im) / den
    bb_re = f_re[..., None] * b_re - f_im[..., None] * b_im
    bb_im = f_re[..., None] * b_im + f_im[..., None] * b_re
    eye = jnp.eye(SSM_GROUPS, dtype=F32)
    bbd = jnp.concatenate(
        [jnp.einsum('gpc,gh->gchp', bb, eye).reshape(SSM_WIDTH, NSTATE) for bb in (bb_re, bb_im)], axis=1)
    cbd = jnp.concatenate(
        [jnp.einsum('gcp,gh->hpgc', cc, eye).reshape(NSTATE, SSM_WIDTH) for cc in (c_re, -c_im)], axis=0)
    flat = lambda a: a.reshape(1, NSTATE)
    lam = jnp.concatenate([flat(lb_re), flat(lb_im)], axis=1)

    def power(k):
        k = jnp.asarray(k, F32).reshape(-1, 1)
        m = jnp.exp(k * flat(ere))
        return jnp.concatenate([m * jnp.cos(k * flat(eim)), m * jnp.sin(k * flat(eim))], axis=1)

    return (bbd.astype(_BF), cbd.astype(_BF), lam, power(float(S5_SEG)),
            power(jnp.arange(1, S5_SEG + 1)), d_skip.reshape(1, SSM_WIDTH),
            glu_w.astype(_BF), glu_b.reshape(1, SSM_WIDTH))


def _s5_permute(a):
    bsz, seq, w = a.shape
    return a.reshape(bsz, seq // S5_TILE, SUBLANES, S5_SEG, w).swapaxes(2, 3).reshape(bsz, seq, w)


def _s5_unpermute(a):
    bsz, seq, w = a.shape
    return a.reshape(bsz, seq // S5_TILE, S5_SEG, SUBLANES, w).swapaxes(2, 3).reshape(bsz, seq, w)


def _dsa_kernel(iq_ref, iwt_ref, dq_ref, ik_ref, dk_ref, dvt_ref, o_ref,
                key_ref, bias_ref, acc_ref, j_ref, *, seq, topk):
    qi = pl.program_id(1)
    nk = (qi * QB) // KT + 1
    lane = lax.broadcasted_iota(jnp.int32, (1, QB), 1)
    vis_end = qi * QB + jnp.where(lane < CHUNK, CHUNK, 2 * CHUNK)
    row_iota = lax.broadcasted_iota(jnp.int32, (KT, QB), 0)

    iq = iq_ref[...]
    col = lax.broadcasted_iota(jnp.int32, (QB, LANES), 1)
    qs = jnp.concatenate([jnp.where(col // IDX_DIM == h, iq, 0.0) for h in range(IDX_HEADS)],
                         axis=0).astype(_BF)
    w = iwt_ref[...]

    def score_tile(kt, _):
        off = pl.multiple_of(kt * KT, KT)
        lg = _dot_nt(ik_ref[pl.ds(off, KT), :].astype(_BF), qs)
        sc = jnp.zeros((KT, QB), F32)
        for h in range(IDX_HEADS):
            sc = sc + jnp.maximum(lg[:, h * QB:(h + 1) * QB], 0.0) * w[h:h + 1, :]
        sc = jnp.where(sc == 0.0, 0.0, sc)
        bits = lax.bitcast_convert_type(sc, jnp.int32)
        key = bits ^ ((bits >> 31) & 0x7FFFFFFF)
        key_ref[pl.ds(off, KT), :] = jnp.where(off + row_iota < vis_end, key, INT_MIN)
        return 0

    lax.fori_loop(0, nk, score_tile, 0)

    def count(pred):
        def body(kt, acc):
            off = pl.multiple_of(kt * KT, KT)
            m = pred(key_ref[pl.ds(off, KT), :], off + row_iota).astype(jnp.int32)
            return acc + m.reshape(KT // SUBLANES, SUBLANES, QB).sum(axis=0)
        acc = lax.fori_loop(0, nk, body, jnp.zeros((SUBLANES, QB), jnp.int32))
        return acc.sum(axis=0, keepdims=True)

    def bisect(i, t):
        cand = t + jnp.left_shift(jnp.int32(1), 31 - i)
        c = count(lambda k, _: k >= cand)
        return jnp.where(c >= topk, cand, t)

    t = lax.fori_loop(0, 32, bisect, jnp.full((1, QB), INT_MIN, jnp.int32))

    cnt_gt = count(lambda k, _: k > t)
    cnt_eq = count(lambda k, _: k == t)
    need = topk - cnt_gt
    has_t = t != INT_MIN
    j_ref[0:1, :] = jnp.where(has_t, seq, -1)
    excess = jnp.max(jnp.where(has_t, cnt_eq - need, 0))

    @pl.when(excess > 0)
    def _():
        def tie_bisect(i, j):
            cand = j + jnp.left_shift(jnp.int32(1), (seq.bit_length() - 2) - i)
            c = count(lambda k, idx: (k == t) & (idx < cand))
            return jnp.where(c < need, cand, j)
        j = lax.fori_loop(0, seq.bit_length() - 1, tie_bisect, jnp.zeros((1, QB), jnp.int32))
        j_ref[0:1, :] = jnp.where(has_t, j, -1)

    jlast = j_ref[0:1, :]

    def bias_tile(kt, _):
        off = pl.multiple_of(kt * KT, KT)
        k = key_ref[pl.ds(off, KT), :]
        sel = (k > t) | ((k == t) & (off + row_iota <= jlast))
        bias_ref[pl.ds(off, KT), :] = jnp.where(sel, 0.0, NEG_INF_SCORE)
        return 0

    lax.fori_loop(0, nk, bias_tile, 0)

    q = dq_ref[...]
    half = (col // DSA_HEAD_DIM)
    qst = jnp.concatenate(
        [jnp.where(half == (h % 2), q[:, (h // 2) * LANES:(h // 2 + 1) * LANES], jnp.zeros((), q.dtype))
         for h in range(DSA_HEADS)], axis=0)
    hq = DSA_HEADS * QB
    acc_ref[...] = jnp.zeros_like(acc_ref)

    def att(kt, c):
        m, l = c
        off = pl.multiple_of(kt * KT, KT)
        s = _dot_nt(dk_ref[pl.ds(off, KT), :], qst)
        b = bias_ref[pl.ds(off, KT), :]
        s = s + jnp.concatenate([b] * DSA_HEADS, axis=1)
        m_new = jnp.maximum(m, jnp.max(s, axis=0, keepdims=True))
        alpha = jnp.exp(m - m_new)
        p = jnp.exp(s - m_new)
        l = alpha * l + jnp.sum(p, axis=0, keepdims=True)
        acc_ref[...] = alpha * acc_ref[...] + _dot(dvt_ref[:, pl.ds(off, KT)], p.astype(_BF))
        return m_new, l

    m0 = jnp.full((1, hq), -jnp.inf, F32)
    _, l = lax.fori_loop(0, nk, att, (m0, jnp.zeros((1, hq), F32)))
    o = acc_ref[...] / l
    for jj in range(DSA_HEADS // 2):
        a = o[:, (2 * jj) * QB:(2 * jj + 1) * QB].T
        b = o[:, (2 * jj + 1) * QB:(2 * jj + 2) * QB].T
        o_ref[:, jj * LANES:(jj + 1) * LANES] = jnp.where(col < DSA_HEAD_DIM, a, b)


def _dsa(iq, iwt, dq, ik, dk, dvt):
    bsz, seq, _ = iq.shape
    topk = min(DSA_TOPK, seq // 4)
    grid = (bsz, seq // QB)
    qrow = lambda w: pl.BlockSpec((None, QB, w), lambda b, i: (b, i, 0))
    full = lambda r, c: pl.BlockSpec((None, r, c), lambda b, i: (b, 0, 0))
    return pl.pallas_call(
        functools.partial(_dsa_kernel, seq=seq, topk=topk), grid=grid,
        in_specs=[qrow(LANES), pl.BlockSpec((None, SUBLANES, QB), lambda b, i: (b, 0, i)), qrow(DSA_WIDTH),
                  full(seq, LANES), full(seq, LANES), full(LANES, seq)],
        out_specs=qrow(DSA_WIDTH),
        out_shape=jax.ShapeDtypeStruct((bsz, seq, DSA_WIDTH), F32),
        scratch_shapes=[pltpu.VMEM((seq, QB), jnp.int32), pltpu.VMEM((seq, QB), F32),
                        pltpu.VMEM((LANES, DSA_HEADS * QB), F32), pltpu.VMEM((SUBLANES, QB), jnp.int32)],
        compiler_params=_params(("parallel", "arbitrary")), name="dsa",
    )(iq, iwt, dq, ik, dk, dvt)


def _ret_kernel(rq_ref, rk_ref, rv_ref, rg_ref, qdec_ref, kdec_ref, dmask_ref, cdec_ref, blk_ref,
                hmq_ref, hmv_ref, ones_ref, ng_ref, o_ref, s_ref):
    @pl.when(pl.program_id(1) == 0)
    def _():
        s_ref[...] = jnp.zeros_like(s_ref)

    q, k, v = rq_ref[...], rk_ref[...], rv_ref[...]
    state = s_ref[...]
    y = _dot((q * qdec_ref[...]).astype(_BF), state.astype(_BF))
    kd = (k * kdec_ref[...]).T.astype(_BF)
    vb = v.astype(_BF)
    s_ref[...] = state * cdec_ref[...] + _dot(kd, vb) * blk_ref[...]
    kb = k.astype(_BF)
    for h in range(RET_HEADS):
        a = _dot_nt((q * hmq_ref[h:h + 1, :]).astype(_BF), kb) * dmask_ref[h]
        y = y + _dot(a.astype(_BF), (v * hmv_ref[h:h + 1, :]).astype(_BF))

    ones = ones_ref[...]

    def head_mean(z):
        hi = z.astype(_BF)
        lo = (z - hi.astype(F32)).astype(_BF)
        return (_dot(hi, ones) + _dot(lo, ones)) * (1.0 / RET_V_DIM)

    d = y - head_mean(y)
    yn = d * lax.rsqrt(head_mean(d * d) + EPS) * ng_ref[...]
    o_ref[...] = jax.nn.silu(rg_ref[...]) * yn


def _ret_consts():
    c = RET_C
    log_g = jnp.log1p(-jnp.exp2(-5.0 - jnp.arange(RET_HEADS, dtype=F32)))
    pos = jnp.arange(c, dtype=F32)
    diff = pos[:, None] - pos[None, :]
    dmask = jnp.where(diff >= 0, jnp.exp(log_g[:, None, None] * jnp.maximum(diff, 0.0)), 0.0)
    lane_q = jnp.arange(RET_QK_PAD)
    head_q = jnp.where((lane_q % LANES) < RET_HEADS * (RET_QK_DIM // 2), (lane_q % LANES) // (RET_QK_DIM // 2), -1)
    head_v = jnp.arange(RET_WIDTH) // RET_V_DIM
    hmq = (head_q[None, :] == jnp.arange(RET_HEADS)[:, None]).astype(F32)
    hmv = (head_v[None, :] == jnp.arange(RET_HEADS)[:, None]).astype(F32)
    lg_q = hmq.T @ log_g
    qdec = jnp.exp(lg_q[None, :] * (pos[:, None] + 1.0))
    kdec = jnp.exp(lg_q[None, :] * (c - 1.0 - pos[:, None]))
    blk = hmq.T @ hmv
    cdec = blk * jnp.exp(lg_q * c)[:, None]
    ones = (hmv.T @ hmv).astype(_BF)
    return qdec, kdec, dmask, cdec, blk, hmq, hmv, ones


def _retention(rq, rk, rv, rg, norm_g):
    bsz, seq, _ = rq.shape
    consts = _ret_consts()
    grid = (bsz, seq // RET_C)
    row = lambda w: pl.BlockSpec((None, RET_C, w), lambda b, i: (b, i, 0))
    ng = norm_g.reshape(1, RET_WIDTH)
    return pl.pallas_call(
        _ret_kernel, grid=grid,
        in_specs=[row(RET_QK_PAD), row(RET_QK_PAD), row(RET_WIDTH), row(RET_WIDTH)]
        + [_const_spec(a.shape) for a in consts] + [_const_spec(ng.shape)],
        out_specs=row(RET_WIDTH),
        out_shape=jax.ShapeDtypeStruct((bsz, seq, RET_WIDTH), F32),
        scratch_shapes=[pltpu.VMEM((RET_QK_PAD, RET_WIDTH), F32)],
        compiler_params=_params(("parallel", "arbitrary")), name="retention",
    )(rq, rk, rv, rg, *consts, ng)


def _merge_kernel(x_ref, g_ref, ya_ref, yb_ref, yc_ref, wg_ref, wa_ref, wb_ref, wc_ref, wo_ref, o_ref):
    x = x_ref[...]
    h = _rms(x, g_ref[...]).astype(_BF)
    merged = jnp.zeros(x.shape, F32)
    for n, (y_ref, w_ref) in enumerate(((ya_ref, wa_ref), (yb_ref, wb_ref), (yc_ref, wc_ref))):
        gate = jax.nn.sigmoid(_dot(h, wg_ref[:, n * D_MODEL:(n + 1) * D_MODEL]))
        merged = merged + gate * _dot(y_ref[...].astype(_BF), w_ref[...])
    o_ref[...] = x + _dot(merged.astype(_BF), wo_ref[...])


def _merge(x, g, ya, yb, yc, wg, wa, wb, wc, wo):
    bsz, seq, _ = x.shape
    grid = (bsz, seq // TM_OUT)
    row = lambda w: pl.BlockSpec((None, TM_OUT, w), lambda b, i: (b, i, 0))
    return pl.pallas_call(
        _merge_kernel, grid=grid,
        in_specs=[row(D_MODEL), _const_spec(g.shape), row(SSM_WIDTH), row(DSA_WIDTH), row(RET_WIDTH)]
        + [_const_spec(w.shape) for w in (wg, wa, wb, wc, wo)],
        out_specs=row(D_MODEL),
        out_shape=jax.ShapeDtypeStruct(x.shape, F32),
        compiler_params=_params(("parallel", "parallel")), name="merge",
    )(x, g, ya, yb, yc, wg, wa, wb, wc, wo)


def _mlp_kernel(x_ref, g_ref, w1_ref, w2_ref, fg_ref, o_ref, *, final_norm):
    x = x_ref[...]
    h = _rms(x, g_ref[...]).astype(_BF)
    acc = x
    for f in range(D_FF // FF_CHUNK):
        t = jnp.maximum(_dot(h, w1_ref[:, f * FF_CHUNK:(f + 1) * FF_CHUNK]), 0.0)
        acc = acc + _dot((t * t).astype(_BF), w2_ref[f * FF_CHUNK:(f + 1) * FF_CHUNK, :])
    o_ref[...] = _rms(acc, fg_ref[...]) if final_norm else acc


def _mlp(x, g, w1, w2, fg, final_norm):
    bsz, seq, _ = x.shape
    grid = (bsz, seq // TM_OUT)
    row = pl.BlockSpec((None, TM_OUT, D_MODEL), lambda b, i: (b, i, 0))
    return pl.pallas_call(
        functools.partial(_mlp_kernel, final_norm=final_norm), grid=grid,
        in_specs=[row, _const_spec(g.shape), _const_spec(w1.shape), _const_spec(w2.shape), _const_spec(fg.shape)],
        out_specs=row,
        out_shape=jax.ShapeDtypeStruct(x.shape, F32),
        compiler_params=_params(("parallel", "parallel")), name="mlp",
    )(x, g, w1, w2, fg)


def _rope_tables(positions):
    pos = positions.astype(F32)[..., None]
    lane = jnp.arange(LANES)

    def angles(rot_dim, theta):
        half = rot_dim // 2
        inv = jnp.exp(-math.log(theta) * jnp.arange(half, dtype=F32) * (2.0 / rot_dim))
        return pos * inv

    def roll_tables(head_dim, rot_dim, theta):
        half = rot_dim // 2
        ang = angles(rot_dim, theta)
        ln = lane % head_dim
        rot = ln < rot_dim
        a = jnp.take(ang, ln % half, axis=-1)
        cos = jnp.where(rot, jnp.cos(a), 1.0)
        sin = jnp.where(rot, jnp.sin(a), 0.0)
        sgn = jnp.stack([jnp.where(ln < half, -1.0, 0.0),
                         jnp.where((ln >= half) & rot, 1.0, 0.0)]).astype(F32)
        return cos, sin, sgn

    cd, sd, sgd = roll_tables(DSA_HEAD_DIM, DSA_HEAD_DIM // 4, ROPE_THETA)
    ci, si, sgi = roll_tables(IDX_DIM, IDX_DIM // 4, ROPE_THETA)
    half = RET_QK_DIM // 2
    ang = angles(RET_QK_DIM, RET_THETA)
    a = jnp.take(ang, lane % half, axis=-1)
    used = lane < RET_HEADS * half
    cr = jnp.where(used, jnp.cos(a), 1.0)
    sr = jnp.where(used, jnp.sin(a), 0.0)
    return cd, sd, ci, si, cr, sr, sgd, sgi


def _inproj_weights(w):
    sizes = (SSM_WIDTH, DSA_WIDTH, DSA_HEAD_DIM, DSA_HEAD_DIM, IDX_HEADS * IDX_DIM, IDX_DIM, IDX_HEADS,
             RET_HEADS * RET_QK_DIM, RET_HEADS * RET_QK_DIM, RET_WIDTH, RET_WIDTH, 3 * D_MODEL)
    parts, off = [], 0
    for n in sizes:
        parts.append(w[:, off:off + n])
        off += n
    wu, wdq, wdk, wdv, wiq, wik, wiw, wrq, wrk, wrv, wrg, wgt = parts
    half = RET_QK_DIM // 2

    def ret_split(a):
        a = a.reshape(D_MODEL, RET_HEADS, 2, half)
        pad = jnp.zeros((D_MODEL, LANES - RET_HEADS * half), a.dtype)
        return jnp.concatenate([a[:, :, 0].reshape(D_MODEL, -1), pad, a[:, :, 1].reshape(D_MODEL, -1), pad], axis=1)

    wiwt = jnp.concatenate([wiw.T, jnp.zeros((SUBLANES - IDX_HEADS, D_MODEL), w.dtype)], axis=0)
    ws = (wu, wdq, jnp.concatenate([wdk, wdk], axis=1), jnp.concatenate([wdv.T, wdv.T], axis=0),
          wiq, jnp.concatenate([wik] * IDX_HEADS, axis=1), wiwt,
          ret_split(wrq), ret_split(wrk), wrv, wrg)
    return tuple(a.astype(_BF) for a in ws), wgt.astype(_BF)


def kernel(x, positions, norm1_g, w_in, ssm_lambda_re, ssm_lambda_im, ssm_log_step, ssm_b_re, ssm_b_im,
           ssm_c_re, ssm_c_im, ssm_d, ssm_glu_w, ssm_glu_b, ret_norm_g, w_proj_a, w_proj_b, w_proj_c,
           w_out, norm2_g, w_ff1, w_ff2, final_norm_g):
    depth = w_in.shape[0]
    tabs = _rope_tables(positions)
    fg = final_norm_g.reshape(1, D_MODEL)
    for l in range(depth):
        g1 = norm1_g[l].reshape(1, D_MODEL)
        ws, wgt = _inproj_weights(w_in[l])
        u, dq, dk, dvt, iq, ik, iwt, rq, rk, rv, rg = _inproj(x, g1, tabs, ws)

        s5p = _s5_params(ssm_lambda_re[l], ssm_lambda_im[l], ssm_log_step[l], ssm_b_re[l], ssm_b_im[l],
                         ssm_c_re[l], ssm_c_im[l], ssm_d[l], ssm_glu_w[l], ssm_glu_b[l])
        ya = _s5_unpermute(_s5(_s5_permute(u), s5p))
        yb = _dsa(iq, iwt, dq, ik, dk, dvt)
        yc = _retention(rq, rk, rv, rg, ret_norm_g[l])

        x = _merge(x, g1, ya, yb, yc, wgt, w_proj_a[l].astype(_BF), w_proj_b[l].astype(_BF),
                   w_proj_c[l].astype(_BF), w_out[l].astype(_BF))
        x = _mlp(x, norm2_g[l].reshape(1, D_MODEL), w_ff1[l].astype(_BF), w_ff2[l].astype(_BF), fg,
                 final_norm=(l == depth - 1))
    return x
```

```python
import functools
import math

import jax
import jax.numpy as jnp
from jax import lax
from jax.experimental import pallas as pl
from jax.experimental.pallas import tpu as pltpu

F32 = jnp.float32
_BF = jnp.bfloat16
LANES = 128
SUBLANES = 8
VMEM_LIMIT = 56 * 1024 * 1024

D_MODEL = 1024
CHUNK = 64
EPS = 1e-6
NEG_INF_SCORE = -1e30
INT_MIN = -2 ** 31
I16_MIN = -2 ** 15
PACK = 2 * SUBLANES
N_ACC = 4
V_ROWS = 80

SSM_WIDTH = 256
SSM_GROUP = 16
SSM_GROUPS = 16
SSM_STATE = 64
NSTATE = SSM_GROUPS * SSM_STATE

DSA_HEADS = 6
DSA_HEAD_DIM = 64
DSA_WIDTH = 384
IDX_HEADS = 4
IDX_DIM = 32
DSA_TOPK = 256
ROPE_THETA = 500000.0

RET_HEADS = 4
RET_QK_DIM = 48
RET_V_DIM = 96
RET_WIDTH = 384
RET_THETA = 10000.0
RET_QK_PAD = 256

D_FF = 4096

TM_IN = 256
S5_TILE = 512
S5_SEG = S5_TILE // SUBLANES
QB = 128
KT = 512
RET_C = 128
TM_OUT = 512
FF_CHUNK = 1024


def _rms(x, g):
    return x * lax.rsqrt(jnp.mean(x * x, axis=-1, keepdims=True) + EPS) * g


def _dot(a, b):
    return jnp.dot(a, b, preferred_element_type=F32)


def _dot_nt(a, b):
    return lax.dot_general(a, b, (((1,), (1,)), ((), ())), preferred_element_type=F32)


def _const_spec(shape):
    nd = len(shape)
    return pl.BlockSpec(shape, lambda *_: (0,) * nd)


def _params(sem):
    return pltpu.CompilerParams(dimension_semantics=sem, vmem_limit_bytes=VMEM_LIMIT)


def _rope_roll(z, cos, sin, sgn, shift):
    outs = []
    for c in range(z.shape[1] // LANES):
        zc = z[:, c * LANES:(c + 1) * LANES]
        rot = (pltpu.roll(zc, LANES - shift, 1) * sgn[0:1, :]
               + pltpu.roll(zc, shift, 1) * sgn[1:2, :])
        outs.append(zc * cos + rot * sin)
    return outs[0] if len(outs) == 1 else jnp.concatenate(outs, axis=1)


def _inproj_kernel(x_ref, g_ref, cd_ref, sd_ref, ci_ref, si_ref, cr_ref, sr_ref, sgd_ref, sgi_ref,
                   wu_ref, wdq_ref, wdk_ref, wdvt_ref, wiq_ref, wik_ref, wiwt_ref,
                   wrq_ref, wrk_ref, wrv_ref, wrg_ref,
                   u_ref, dq_ref, dk_ref, dvt_ref, iq_ref, ik_ref, iwt_ref,
                   rq_ref, rk_ref, rv_ref, rg_ref):
    h = _rms(x_ref[...], g_ref[...]).astype(_BF)
    u_ref[...] = _dot(h, wu_ref[...])

    cd, sd, sgd = cd_ref[...], sd_ref[...], sgd_ref[...]
    dq = _rope_roll(_dot(h, wdq_ref[...]), cd, sd, sgd, 8)
    dq_ref[...] = (dq * (DSA_HEAD_DIM ** -0.5 * math.log2(math.e))).astype(dq_ref.dtype)
    dk_ref[...] = _rope_roll(_dot(h, wdk_ref[...]), cd, sd, sgd, 8).astype(dk_ref.dtype)
    vt = _dot_nt(wdvt_ref[...], h)
    ones_row = lax.broadcasted_iota(jnp.int32, vt.shape, 0) == DSA_HEAD_DIM
    dvt_ref[...] = jnp.where(ones_row, 1.0, vt).astype(dvt_ref.dtype)

    ci, si, sgi = ci_ref[...], si_ref[...], sgi_ref[...]
    iq_ref[...] = _rope_roll(_dot(h, wiq_ref[...]), ci, si, sgi, 4) * (IDX_DIM ** -0.5)
    ik_ref[...] = _rope_roll(_dot(h, wik_ref[...]), ci, si, sgi, 4).astype(ik_ref.dtype)
    iwt_ref[...] = _dot_nt(wiwt_ref[...], h) * (IDX_HEADS ** -0.5)

    cr, sr = cr_ref[...], sr_ref[...]

    def rope_split(z):
        x1, x2 = z[:, :LANES], z[:, LANES:]
        return jnp.concatenate([x1 * cr - x2 * sr, x2 * cr + x1 * sr], axis=1)

    rq_ref[...] = rope_split(_dot(h, wrq_ref[...]))
    rk_ref[...] = rope_split(_dot(h, wrk_ref[...])) * (RET_QK_DIM ** -0.5)
    rv_ref[...] = _dot(h, wrv_ref[...])
    rg_ref[...] = _dot(h, wrg_ref[...])


def _inproj(x, g, tabs, ws):
    bsz, seq, _ = x.shape
    tm = TM_IN
    grid = (bsz, seq // tm)
    row = lambda w: pl.BlockSpec((None, tm, w), lambda b, i: (b, i, 0))
    colT = lambda r: pl.BlockSpec((None, r, tm), lambda b, i: (b, 0, i))
    in_specs = ([row(D_MODEL), _const_spec(g.shape)] + [row(LANES)] * 6
                + [_const_spec(t.shape) for t in tabs[6:]] + [_const_spec(w.shape) for w in ws])
    out_shape = [
        jax.ShapeDtypeStruct((bsz, seq, SSM_WIDTH), F32),
        jax.ShapeDtypeStruct((bsz, seq, DSA_WIDTH), _BF),
        jax.ShapeDtypeStruct((bsz, seq, LANES), _BF),
        jax.ShapeDtypeStruct((bsz, LANES, seq), _BF),
        jax.ShapeDtypeStruct((bsz, seq, LANES), F32),
        jax.ShapeDtypeStruct((bsz, seq, LANES), _BF),
        jax.ShapeDtypeStruct((bsz, SUBLANES, seq), F32),
        jax.ShapeDtypeStruct((bsz, seq, RET_QK_PAD), F32),
        jax.ShapeDtypeStruct((bsz, seq, RET_QK_PAD), F32),
        jax.ShapeDtypeStruct((bsz, seq, RET_WIDTH), F32),
        jax.ShapeDtypeStruct((bsz, seq, RET_WIDTH), F32),
    ]
    out_specs = [row(SSM_WIDTH), row(DSA_WIDTH), row(LANES), colT(LANES), row(LANES), row(LANES),
                 colT(SUBLANES), row(RET_QK_PAD), row(RET_QK_PAD), row(RET_WIDTH), row(RET_WIDTH)]
    return pl.pallas_call(
        _inproj_kernel, grid=grid, in_specs=in_specs, out_specs=out_specs, out_shape=out_shape,
        compiler_params=_params(("parallel", "parallel")), name="inproj",
    )(x, g, *tabs, *ws)


def _s5_kernel(u_ref, bbd_ref, cbd_ref, lam_ref, lam64_ref, pow_ref, d_ref, gw_ref, gb_ref,
               o_ref, st_ref, xm_ref, carry_ref):
    n = NSTATE

    @pl.when(pl.program_id(1) == 0)
    def _():
        carry_ref[...] = jnp.zeros_like(carry_ref)

    u = u_ref[...]
    st_ref[...] = _dot(u.astype(_BF), bbd_ref[...])

    lr = jnp.broadcast_to(lam_ref[:, :n], (SUBLANES, n))
    li = jnp.broadcast_to(lam_ref[:, n:], (SUBLANES, n))

    def step(j, c):
        xr, xi = c
        off = pl.multiple_of(j * SUBLANES, SUBLANES)
        ar = st_ref[pl.ds(off, SUBLANES), :n]
        ai = st_ref[pl.ds(off, SUBLANES), n:]
        nr = lr * xr - li * xi + ar
        ni = lr * xi + li * xr + ai
        st_ref[pl.ds(off, SUBLANES), :n] = nr
        st_ref[pl.ds(off, SUBLANES), n:] = ni
        return nr, ni

    zero = jnp.zeros((SUBLANES, n), F32)
    xr, xi = lax.fori_loop(0, S5_SEG, step, (zero, zero))

    l64r, l64i = lam64_ref[:, :n], lam64_ref[:, n:]
    cr, ci = carry_ref[:, :n], carry_ref[:, n:]
    for i in range(SUBLANES):
        xm_ref[i:i + 1, :n] = cr
        xm_ref[i:i + 1, n:] = ci
        er, ei = xr[i:i + 1, :], xi[i:i + 1, :]
        cr, ci = er + l64r * cr - l64i * ci, ei + l64r * ci + l64i * cr
    carry_ref[:, :n] = cr
    carry_ref[:, n:] = ci

    hr, hi = xm_ref[:, :n], xm_ref[:, n:]

    def fix(j, _):
        off = pl.multiple_of(j * SUBLANES, SUBLANES)
        pr = pow_ref[pl.ds(j, 1), :n]
        pi = pow_ref[pl.ds(j, 1), n:]
        st_ref[pl.ds(off, SUBLANES), :n] = st_ref[pl.ds(off, SUBLANES), :n] + pr * hr - pi * hi
        st_ref[pl.ds(off, SUBLANES), n:] = st_ref[pl.ds(off, SUBLANES), n:] + pr * hi + pi * hr
        return 0

    lax.fori_loop(0, S5_SEG, fix, 0)

    y = d_ref[...] * u
    kc = 512
    for k in range(2 * n // kc):
        y = y + _dot(st_ref[:, k * kc:(k + 1) * kc].astype(_BF), cbd_ref[k * kc:(k + 1) * kc, :])
    y = jax.nn.gelu(y)
    o_ref[...] = y * jax.nn.sigmoid(_dot(y.astype(_BF), gw_ref[...]) + gb_ref[...])


def _s5(u_perm, prm):
    bsz, seq, _ = u_perm.shape
    grid = (bsz, seq // S5_TILE)
    row = pl.BlockSpec((None, S5_TILE, SSM_WIDTH), lambda b, i: (b, i, 0))
    return pl.pallas_call(
        _s5_kernel, grid=grid,
        in_specs=[row] + [_const_spec(p.shape) for p in prm],
        out_specs=row,
        out_shape=jax.ShapeDtypeStruct((bsz, seq, SSM_WIDTH), F32),
        scratch_shapes=[pltpu.VMEM((S5_TILE, 2 * NSTATE), F32),
                        pltpu.VMEM((SUBLANES, 2 * NSTATE), F32),
                        pltpu.VMEM((1, 2 * NSTATE), F32)],
        compiler_params=_params(("arbitrary", "arbitrary")), name="s5",
    )(u_perm, *prm)


def _s5_params(lam_re, lam_im, log_step, b_re, b_im, c_re, c_im, d_skip, glu_w, glu_b):
    step = jnp.exp(log_step.astype(F32))[:, None]
    ere, eim = lam_re * step, lam_im * step
    mag = jnp.exp(ere)
    lb_re, lb_im = mag * jnp.cos(eim), mag * jnp.sin(eim)
    den = lam_re * lam_re + lam_im * lam_im
    f_re = ((lb_re - 1.0) * lam_re + lb_im * lam_im) / den
    f_im = (lb_im * lam_re - (lb_re - 1.0) * lam_im) / den
    bb_re = f_re[..., None] * b_re - f_im[..., None] * b_im
    bb_im = f_re[..., None] * b_im + f_im[..., None] * b_re
    eye = jnp.eye(SSM_GROUPS, dtype=F32)
    bbd = jnp.concatenate(
        [jnp.einsum('gpc,gh->gchp', bb, eye).reshape(SSM_WIDTH, NSTATE) for bb in (bb_re, bb_im)], axis=1)
    cbd = jnp.concatenate(
        [jnp.einsum('gcp,gh->hpgc', cc, eye).reshape(NSTATE, SSM_WIDTH) for cc in (c_re, -c_im)], axis=0)
    flat = lambda a: a.reshape(1, NSTATE)
    lam = jnp.concatenate([flat(lb_re), flat(lb_im)], axis=1)

    def power(k):
        k = jnp.asarray(k, F32).reshape(-1, 1)
        m = jnp.exp(k * flat(ere))
        return jnp.concatenate([m * jnp.cos(k * flat(eim)), m * jnp.sin(k * flat(eim))], axis=1)

    return (bbd.astype(_BF), cbd.astype(_BF), lam, power(float(S5_SEG)),
            power(jnp.arange(1, S5_SEG + 1)), d_skip.reshape(1, SSM_WIDTH),
            glu_w.astype(_BF), glu_b.reshape(1, SSM_WIDTH))


def _s5_permute(a):
    bsz, seq, w = a.shape
    return a.reshape(bsz, seq // S5_TILE, SUBLANES, S5_SEG, w).swapaxes(2, 3).reshape(bsz, seq, w)


def _s5_unpermute(a):
    bsz, seq, w = a.shape
    return a.reshape(bsz, seq // S5_TILE, S5_SEG, SUBLANES, w).swapaxes(2, 3).reshape(bsz, seq, w)


def _dsa_kernel(iq_ref, iwt_ref, dq_ref, ik_ref, dk_ref, dvt_ref, o_ref,
                key_ref, hi_ref, lo_ref, bias_ref, acc_ref, j_ref,
                lg0_ref, lg1_ref, s0_ref, s1_ref, p0_ref, p1_ref, *, seq, topk):
    qi = pl.program_id(1)
    nk = (qi * QB) // KT + 1
    lane = lax.broadcasted_iota(jnp.int32, (1, QB), 1)
    vis_end = qi * QB + jnp.where(lane < CHUNK, CHUNK, 2 * CHUNK)
    row_iota = lax.broadcasted_iota(jnp.int32, (KT, QB), 0)

    iq = iq_ref[...]
    col = lax.broadcasted_iota(jnp.int32, (QB, LANES), 1)
    qs = jnp.concatenate([jnp.where(col // IDX_DIM == h, iq, 0.0) for h in range(IDX_HEADS)],
                         axis=0).astype(_BF)
    w = iwt_ref[...]

    nkp = nk + lax.rem(nk, 2)
    last_tile = nkp - 1

    def logits(kt, dst_ref):
        off = pl.multiple_of(kt * KT, KT)
        dst_ref[...] = _dot_nt(ik_ref[pl.ds(off, KT), :], qs)

    def score_tile(kt, src_ref):
        off = pl.multiple_of(kt * KT, KT)
        lg = src_ref[...]
        sc = jnp.zeros((KT, QB), F32)
        for h in range(IDX_HEADS):
            sc = sc + jnp.maximum(lg[:, h * QB:(h + 1) * QB], 0.0) * w[h:h + 1, :]
        sc = jnp.where(sc == 0.0, 0.0, sc)
        bits = lax.bitcast_convert_type(sc, jnp.int32)
        key = bits ^ ((bits >> 31) & 0x7FFFFFFF)
        key = jnp.where(row_iota < vis_end - off, key, INT_MIN)
        key_ref[pl.ds(off, KT), :] = key
        hi_ref[pl.ds(off, KT), :] = (key >> 16).astype(jnp.int16)
        lo_ref[pl.ds(off, KT), :] = (key ^ 0x8000).astype(jnp.int16)

    def score_pair(i, c):
        logits(2 * i + 1, lg1_ref)
        score_tile(2 * i, lg0_ref)
        logits(jnp.minimum(2 * i + 2, last_tile), lg0_ref)
        score_tile(2 * i + 1, lg1_ref)
        return c

    logits(0, lg0_ref)
    lax.fori_loop(0, nkp // 2, score_pair, 0)

    one16 = jnp.ones((PACK, QB), jnp.int16)
    zero16 = jnp.zeros((PACK, QB), jnp.int16)

    def row16(v):
        return jnp.broadcast_to(v, (PACK, QB)).astype(jnp.int16)

    def count16(ref, pred):
        def body(kt, accs):
            off = pl.multiple_of(kt * KT, KT)
            blk = ref[pl.ds(off, KT), :]
            accs = list(accs)
            for r in range(KT // PACK):
                accs[r % N_ACC] = accs[r % N_ACC] + jnp.where(pred(blk[r * PACK:(r + 1) * PACK, :]), one16, zero16)
            return tuple(accs)
        accs = lax.fori_loop(0, nk, body, (zero16,) * N_ACC)
        acc = functools.reduce(lambda a, b: a + b, accs)
        return acc.astype(jnp.int32).sum(axis=0, keepdims=True)

    def kth16(ref, k):
        def body(i, t):
            cand = t + jnp.left_shift(jnp.int32(1), 15 - i)
            c16 = row16(cand)
            c = count16(ref, lambda s: s >= c16)
            return jnp.where(c >= k, cand, t)
        return lax.fori_loop(0, 16, body, jnp.full((1, QB), I16_MIN, jnp.int32))

    h = kth16(hi_ref, topk)
    has_t = h != I16_MIN
    h16 = row16(h)

    def keep_bucket(kt, c):
        off = pl.multiple_of(kt * KT, KT)
        his, los, kept = hi_ref[pl.ds(off, KT), :], lo_ref[pl.ds(off, KT), :], []
        for r in range(KT // PACK):
            kept.append(jnp.where(his[r * PACK:(r + 1) * PACK, :] == h16, los[r * PACK:(r + 1) * PACK, :],
                                  jnp.full((PACK, QB), I16_MIN, jnp.int16)))
        lo_ref[pl.ds(off, KT), :] = jnp.concatenate(kept, axis=0)
        return c

    lax.fori_loop(0, nk, keep_bucket, 0)
    gt_hi = count16(hi_ref, lambda s: s > h16)
    lo = kth16(lo_ref, topk - gt_hi)
    t = jnp.where(has_t, h * 65536 + (lo + 32768), INT_MIN)
    lo16 = row16(lo)
    cnt_gt = gt_hi + count16(lo_ref, lambda s: s > lo16)

    def count(pred):
        def body(kt, acc):
            off = pl.multiple_of(kt * KT, KT)
            m = pred(key_ref[pl.ds(off, KT), :], off + row_iota).astype(jnp.int32)
            return acc + m.reshape(KT // SUBLANES, SUBLANES, QB).sum(axis=0)
        acc = lax.fori_loop(0, nk, body, jnp.zeros((SUBLANES, QB), jnp.int32))
        return acc.sum(axis=0, keepdims=True)

    cnt_eq = count(lambda k, _: k == t)
    need = topk - cnt_gt
    j_ref[0:1, :] = jnp.where(has_t, seq, -1)
    excess = jnp.max(jnp.where(has_t, cnt_eq - need, 0))

    @pl.when(excess > 0)
    def _():
        def tie_bisect(i, j):
            cand = j + jnp.left_shift(jnp.int32(1), (seq.bit_length() - 2) - i)
            c = count(lambda k, idx: (k == t) & (idx < cand))
            return jnp.where(c < need, cand, j)
        j = lax.fori_loop(0, seq.bit_length() - 1, tie_bisect, jnp.zeros((1, QB), jnp.int32))
        j_ref[0:1, :] = jnp.where(has_t, j, -1)

    jlast = j_ref[0:1, :]

    def bias_tile(kt, _):
        off = pl.multiple_of(kt * KT, KT)
        k = key_ref[pl.ds(off, KT), :]
        sel = (k > t) | ((k == t) & (off + row_iota <= jlast))
        bias_ref[pl.ds(off, KT), :] = jnp.where(sel, 0.0, NEG_INF_SCORE)
        return 0

    lax.fori_loop(0, nkp, bias_tile, 0)

    q = dq_ref[...]
    half = (col // DSA_HEAD_DIM)
    qst = jnp.concatenate(
        [jnp.where(half == (h % 2), q[:, (h // 2) * LANES:(h // 2 + 1) * LANES], jnp.zeros((), q.dtype))
         for h in range(DSA_HEADS)], axis=0)
    npair = DSA_HEADS // 2
    hq = DSA_HEADS * QB
    acc_ref[...] = jnp.zeros_like(acc_ref)
    p1_ref[...] = jnp.zeros_like(p1_ref)

    def qk(kt, dst_ref):
        off = pl.multiple_of(kt * KT, KT)
        b = bias_ref[pl.ds(off, KT), :]
        dst_ref[...] = _dot_nt(dk_ref[pl.ds(off, KT), :], qst) + jnp.concatenate([b] * DSA_HEADS, axis=1)

    def numer(src_ref, dst_ref, m):
        s = src_ref[...]
        m_new = jnp.maximum(m, jnp.max(s, axis=0, keepdims=True))
        dst_ref[...] = jnp.exp2(s - m_new).astype(dst_ref.dtype)
        return m_new, jnp.exp2(m - m_new)

    def pv(kt, src_ref, alpha):
        off = pl.multiple_of(kt * KT, KT)
        vt = dvt_ref[0:V_ROWS, pl.ds(off, KT)]
        acc_ref[...] = alpha * acc_ref[...] + _dot(vt, src_ref[...])

    def att_pair(i, c):
        m, alpha = c
        qk(2 * i + 1, s1_ref)
        pv(jnp.maximum(2 * i - 1, 0), p1_ref, alpha)
        m, alpha = numer(s0_ref, p0_ref, m)
        qk(jnp.minimum(2 * i + 2, last_tile), s0_ref)
        pv(2 * i, p0_ref, alpha)
        m, alpha = numer(s1_ref, p1_ref, m)
        return m, alpha

    qk(0, s0_ref)
    c0 = (jnp.full((1, hq), -jnp.inf, F32), jnp.ones((1, hq), F32))
    _, alpha = lax.fori_loop(0, nkp // 2, att_pair, c0)
    pv(last_tile, p1_ref, alpha)
    acc = acc_ref[...]
    o = acc[0:DSA_HEAD_DIM, :] / acc[DSA_HEAD_DIM:DSA_HEAD_DIM + 1, :]
    for j in range(npair):
        o_ref[:, j * LANES:(j + 1) * LANES] = jnp.concatenate(
            [o[:, (2 * j) * QB:(2 * j + 1) * QB], o[:, (2 * j + 1) * QB:(2 * j + 2) * QB]], axis=0).T


def _dsa(iq, iwt, dq, ik, dk, dvt):
    bsz, seq, _ = iq.shape
    topk = min(DSA_TOPK, seq // 4)
    grid = (bsz, seq // QB)
    qrow = lambda w: pl.BlockSpec((None, QB, w), lambda b, i: (b, i, 0))
    full = lambda r, c: pl.BlockSpec((None, r, c), lambda b, i: (b, 0, 0))
    return pl.pallas_call(
        functools.partial(_dsa_kernel, seq=seq, topk=topk), grid=grid,
        in_specs=[qrow(LANES), pl.BlockSpec((None, SUBLANES, QB), lambda b, i: (b, 0, i)), qrow(DSA_WIDTH),
                  full(seq, LANES), full(seq, LANES), full(LANES, seq)],
        out_specs=qrow(DSA_WIDTH),
        out_shape=jax.ShapeDtypeStruct((bsz, seq, DSA_WIDTH), F32),
        scratch_shapes=[pltpu.VMEM((seq, QB), jnp.int32), pltpu.VMEM((seq, QB), jnp.int16),
                        pltpu.VMEM((seq, QB), jnp.int16), pltpu.VMEM((seq, QB), F32),
                        pltpu.VMEM((V_ROWS, DSA_HEADS * QB), F32), pltpu.VMEM((SUBLANES, QB), jnp.int32),
                        pltpu.VMEM((KT, IDX_HEADS * QB), F32), pltpu.VMEM((KT, IDX_HEADS * QB), F32),
                        pltpu.VMEM((KT, DSA_HEADS * QB), F32), pltpu.VMEM((KT, DSA_HEADS * QB), F32),
                        pltpu.VMEM((KT, DSA_HEADS * QB), _BF), pltpu.VMEM((KT, DSA_HEADS * QB), _BF)],
        compiler_params=_params(("parallel", "arbitrary")), name="dsa",
    )(iq, iwt, dq, ik, dk, dvt)


def _ret_kernel(rq_ref, rk_ref, rv_ref, rg_ref, qdec_ref, kdec_ref, dmask_ref, cdec_ref, blk_ref,
                hmq_ref, hmv_ref, ones_ref, ng_ref, o_ref, s_ref):
    @pl.when(pl.program_id(1) == 0)
    def _():
        s_ref[...] = jnp.zeros_like(s_ref)

    q, k, v = rq_ref[...], rk_ref[...], rv_ref[...]
    state = s_ref[...]
    y = _dot((q * qdec_ref[...]).astype(_BF), state.astype(_BF))
    kd = (k * kdec_ref[...]).T.astype(_BF)
    vb = v.astype(_BF)
    s_ref[...] = state * cdec_ref[...] + _dot(kd, vb) * blk_ref[...]
    kb = k.astype(_BF)
    for h in range(RET_HEADS):
        a = _dot_nt((q * hmq_ref[h:h + 1, :]).astype(_BF), kb) * dmask_ref[h]
        y = y + _dot(a.astype(_BF), (v * hmv_ref[h:h + 1, :]).astype(_BF))

    ones = ones_ref[...]

    def head_mean(z):
        hi = z.astype(_BF)
        lo = (z - hi.astype(F32)).astype(_BF)
        return (_dot(hi, ones) + _dot(lo, ones)) * (1.0 / RET_V_DIM)

    d = y - head_mean(y)
    yn = d * lax.rsqrt(head_mean(d * d) + EPS) * ng_ref[...]
    o_ref[...] = jax.nn.silu(rg_ref[...]) * yn


def _ret_consts():
    c = RET_C
    log_g = jnp.log1p(-jnp.exp2(-5.0 - jnp.arange(RET_HEADS, dtype=F32)))
    pos = jnp.arange(c, dtype=F32)
    diff = pos[:, None] - pos[None, :]
    dmask = jnp.where(diff >= 0, jnp.exp(log_g[:, None, None] * jnp.maximum(diff, 0.0)), 0.0)
    lane_q = jnp.arange(RET_QK_PAD)
    head_q = jnp.where((lane_q % LANES) < RET_HEADS * (RET_QK_DIM // 2), (lane_q % LANES) // (RET_QK_DIM // 2), -1)
    head_v = jnp.arange(RET_WIDTH) // RET_V_DIM
    hmq = (head_q[None, :] == jnp.arange(RET_HEADS)[:, None]).astype(F32)
    hmv = (head_v[None, :] == jnp.arange(RET_HEADS)[:, None]).astype(F32)
    lg_q = hmq.T @ log_g
    qdec = jnp.exp(lg_q[None, :] * (pos[:, None] + 1.0))
    kdec = jnp.exp(lg_q[None, :] * (c - 1.0 - pos[:, None]))
    blk = hmq.T @ hmv
    cdec = blk * jnp.exp(lg_q * c)[:, None]
    ones = (hmv.T @ hmv).astype(_BF)
    return qdec, kdec, dmask, cdec, blk, hmq, hmv, ones


def _retention(rq, rk, rv, rg, norm_g):
    bsz, seq, _ = rq.shape
    consts = _ret_consts()
    grid = (bsz, seq // RET_C)
    row = lambda w: pl.BlockSpec((None, RET_C, w), lambda b, i: (b, i, 0))
    ng = norm_g.reshape(1, RET_WIDTH)
    return pl.pallas_call(
        _ret_kernel, grid=grid,
        in_specs=[row(RET_QK_PAD), row(RET_QK_PAD), row(RET_WIDTH), row(RET_WIDTH)]
        + [_const_spec(a.shape) for a in consts] + [_const_spec(ng.shape)],
        out_specs=row(RET_WIDTH),
        out_shape=jax.ShapeDtypeStruct((bsz, seq, RET_WIDTH), F32),
        scratch_shapes=[pltpu.VMEM((RET_QK_PAD, RET_WIDTH), F32)],
        compiler_params=_params(("parallel", "arbitrary")), name="retention",
    )(rq, rk, rv, rg, *consts, ng)


def _merge_kernel(x_ref, g_ref, ya_ref, yb_ref, yc_ref, wg_ref, wa_ref, wb_ref, wc_ref, wo_ref, o_ref):
    x = x_ref[...]
    h = _rms(x, g_ref[...]).astype(_BF)
    merged = jnp.zeros(x.shape, F32)
    for n, (y_ref, w_ref) in enumerate(((ya_ref, wa_ref), (yb_ref, wb_ref), (yc_ref, wc_ref))):
        gate = jax.nn.sigmoid(_dot(h, wg_ref[:, n * D_MODEL:(n + 1) * D_MODEL]))
        merged = merged + gate * _dot(y_ref[...].astype(_BF), w_ref[...])
    o_ref[...] = x + _dot(merged.astype(_BF), wo_ref[...])


def _merge(x, g, ya, yb, yc, wg, wa, wb, wc, wo):
    bsz, seq, _ = x.shape
    grid = (bsz, seq // TM_OUT)
    row = lambda w: pl.BlockSpec((None, TM_OUT, w), lambda b, i: (b, i, 0))
    return pl.pallas_call(
        _merge_kernel, grid=grid,
        in_specs=[row(D_MODEL), _const_spec(g.shape), row(SSM_WIDTH), row(DSA_WIDTH), row(RET_WIDTH)]
        + [_const_spec(w.shape) for w in (wg, wa, wb, wc, wo)],
        out_specs=row(D_MODEL),
        out_shape=jax.ShapeDtypeStruct(x.shape, F32),
        compiler_params=_params(("parallel", "parallel")), name="merge",
    )(x, g, ya, yb, yc, wg, wa, wb, wc, wo)


def _mlp_kernel(x_ref, g_ref, w1_ref, w2_ref, fg_ref, o_ref, *, final_norm):
    x = x_ref[...]
    h = _rms(x, g_ref[...]).astype(_BF)
    acc = x
    for f in range(D_FF // FF_CHUNK):
        t = jnp.maximum(_dot(h, w1_ref[:, f * FF_CHUNK:(f + 1) * FF_CHUNK]), 0.0)
        acc = acc + _dot((t * t).astype(_BF), w2_ref[f * FF_CHUNK:(f + 1) * FF_CHUNK, :])
    o_ref[...] = _rms(acc, fg_ref[...]) if final_norm else acc


def _mlp(x, g, w1, w2, fg, final_norm):
    bsz, seq, _ = x.shape
    grid = (bsz, seq // TM_OUT)
    row = pl.BlockSpec((None, TM_OUT, D_MODEL), lambda b, i: (b, i, 0))
    return pl.pallas_call(
        functools.partial(_mlp_kernel, final_norm=final_norm), grid=grid,
        in_specs=[row, _const_spec(g.shape), _const_spec(w1.shape), _const_spec(w2.shape), _const_spec(fg.shape)],
        out_specs=row,
        out_shape=jax.ShapeDtypeStruct(x.shape, F32),
        compiler_params=_params(("parallel", "parallel")), name="mlp",
    )(x, g, w1, w2, fg)


def _rope_tables(positions):
    pos = positions.astype(F32)[..., None]
    lane = jnp.arange(LANES)

    def angles(rot_dim, theta):
        half = rot_dim // 2
        inv = jnp.exp(-math.log(theta) * jnp.arange(half, dtype=F32) * (2.0 / rot_dim))
        return pos * inv

    def roll_tables(head_dim, rot_dim, theta):
        half = rot_dim // 2
        ang = angles(rot_dim, theta)
        ln = lane % head_dim
        rot = ln < rot_dim
        a = jnp.take(ang, ln % half, axis=-1)
        cos = jnp.where(rot, jnp.cos(a), 1.0)
        sin = jnp.where(rot, jnp.sin(a), 0.0)
        sgn = jnp.stack([jnp.where(ln < half, -1.0, 0.0),
                         jnp.where((ln >= half) & rot, 1.0, 0.0)]).astype(F32)
        return cos, sin, sgn

    cd, sd, sgd = roll_tables(DSA_HEAD_DIM, DSA_HEAD_DIM // 4, ROPE_THETA)
    ci, si, sgi = roll_tables(IDX_DIM, IDX_DIM // 4, ROPE_THETA)
    half = RET_QK_DIM // 2
    ang = angles(RET_QK_DIM, RET_THETA)
    a = jnp.take(ang, lane % half, axis=-1)
    used = lane < RET_HEADS * half
    cr = jnp.where(used, jnp.cos(a), 1.0)
    sr = jnp.where(used, jnp.sin(a), 0.0)
    return cd, sd, ci, si, cr, sr, sgd, sgi


def _inproj_weights(w):
    sizes = (SSM_WIDTH, DSA_WIDTH, DSA_HEAD_DIM, DSA_HEAD_DIM, IDX_HEADS * IDX_DIM, IDX_DIM, IDX_HEADS,
             RET_HEADS * RET_QK_DIM, RET_HEADS * RET_QK_DIM, RET_WIDTH, RET_WIDTH, 3 * D_MODEL)
    parts, off = [], 0
    for n in sizes:
        parts.append(w[:, off:off + n])
        off += n
    wu, wdq, wdk, wdv, wiq, wik, wiw, wrq, wrk, wrv, wrg, wgt = parts
    half = RET_QK_DIM // 2

    def ret_split(a):
        a = a.reshape(D_MODEL, RET_HEADS, 2, half)
        pad = jnp.zeros((D_MODEL, LANES - RET_HEADS * half), a.dtype)
        return jnp.concatenate([a[:, :, 0].reshape(D_MODEL, -1), pad, a[:, :, 1].reshape(D_MODEL, -1), pad], axis=1)

    wiwt = jnp.concatenate([wiw.T, jnp.zeros((SUBLANES - IDX_HEADS, D_MODEL), w.dtype)], axis=0)
    wdvt = jnp.concatenate([wdv.T, jnp.zeros((LANES - DSA_HEAD_DIM, D_MODEL), w.dtype)], axis=0)
    ws = (wu, wdq, jnp.concatenate([wdk, wdk], axis=1), wdvt,
          wiq, jnp.concatenate([wik] * IDX_HEADS, axis=1), wiwt,
          ret_split(wrq), ret_split(wrk), wrv, wrg)
    return tuple(a.astype(_BF) for a in ws), wgt.astype(_BF)


def kernel(x, positions, norm1_g, w_in, ssm_lambda_re, ssm_lambda_im, ssm_log_step, ssm_b_re, ssm_b_im,
           ssm_c_re, ssm_c_im, ssm_d, ssm_glu_w, ssm_glu_b, ret_norm_g, w_proj_a, w_proj_b, w_proj_c,
           w_out, norm2_g, w_ff1, w_ff2, final_norm_g):
    depth = w_in.shape[0]
    tabs = _rope_tables(positions)
    fg = final_norm_g.reshape(1, D_MODEL)
    for l in range(depth):
        g1 = norm1_g[l].reshape(1, D_MODEL)
        ws, wgt = _inproj_weights(w_in[l])
        u, dq, dk, dvt, iq, ik, iwt, rq, rk, rv, rg = _inproj(x, g1, tabs, ws)

        s5p = _s5_params(ssm_lambda_re[l], ssm_lambda_im[l], ssm_log_step[l], ssm_b_re[l], ssm_b_im[l],
                         ssm_c_re[l], ssm_c_im[l], ssm_d[l], ssm_glu_w[l], ssm_glu_b[l])
        ya = _s5_unpermute(_s5(_s5_permute(u), s5p))
        yb = _dsa(iq, iwt, dq, ik, dk, dvt)
        yc = _retention(rq, rk, rv, rg, ret_norm_g[l])

        x = _merge(x, g1, ya, yb, yc, wgt, w_proj_a[l].astype(_BF), w_proj_b[l].astype(_BF),
                   w_proj_c[l].astype(_BF), w_out[l].astype(_BF))
        x = _mlp(x, norm2_g[l].reshape(1, D_MODEL), w_ff1[l].astype(_BF), w_ff2[l].astype(_BF), fg,
                 final_norm=(l == depth - 1))
    return x
```

```python
import functools
import math

import jax
import jax.numpy as jnp
from jax import lax
from jax.experimental import pallas as pl
from jax.experimental.pallas import tpu as pltpu

F32 = jnp.float32
_BF = jnp.bfloat16
LANES = 128
SUBLANES = 8
VMEM_LIMIT = 56 * 1024 * 1024

D_MODEL = 1024
CHUNK = 64
EPS = 1e-6
NEG_INF_SCORE = -1e30
INT_MIN = -2 ** 31
I16_MIN = -2 ** 15
PACK = 2 * SUBLANES
N_ACC = 4
V_ROWS = 80

SSM_WIDTH = 256
SSM_GROUP = 16
SSM_GROUPS = 16
SSM_STATE = 64
NSTATE = SSM_GROUPS * SSM_STATE

DSA_HEADS = 6
DSA_HEAD_DIM = 64
DSA_WIDTH = 384
IDX_HEADS = 4
IDX_DIM = 32
DSA_TOPK = 256
ROPE_THETA = 500000.0

RET_HEADS = 4
RET_QK_DIM = 48
RET_V_DIM = 96
RET_WIDTH = 384
RET_THETA = 10000.0
RET_QK_PAD = 256

D_FF = 4096

TM_IN = 256
S5_TILE = 512
S5_SEG = S5_TILE // SUBLANES
QB = 128
KT = 512
RET_C = 128
TM_OUT = 512
FF_CHUNK = 1024


def _rms(x, g):
    return x * lax.rsqrt(jnp.mean(x * x, axis=-1, keepdims=True) + EPS) * g


def _dot(a, b):
    return jnp.dot(a, b, preferred_element_type=F32)


def _dot_nt(a, b):
    return lax.dot_general(a, b, (((1,), (1,)), ((), ())), preferred_element_type=F32)


def _const_spec(shape):
    nd = len(shape)
    return pl.BlockSpec(shape, lambda *_: (0,) * nd)


def _params(sem):
    return pltpu.CompilerParams(dimension_semantics=sem, vmem_limit_bytes=VMEM_LIMIT)


def _rope_roll(z, cos, sin, sgn, shift):
    outs = []
    for c in range(z.shape[1] // LANES):
        zc = z[:, c * LANES:(c + 1) * LANES]
        rot = (pltpu.roll(zc, LANES - shift, 1) * sgn[0:1, :]
               + pltpu.roll(zc, shift, 1) * sgn[1:2, :])
        outs.append(zc * cos + rot * sin)
    return outs[0] if len(outs) == 1 else jnp.concatenate(outs, axis=1)


def _inproj_kernel(x_ref, g_ref, cd_ref, sd_ref, ci_ref, si_ref, cr_ref, sr_ref, sgd_ref, sgi_ref,
                   wu_ref, wdq_ref, wdk_ref, wdvt_ref, wiq_ref, wik_ref, wiwt_ref,
                   wrq_ref, wrk_ref, wrv_ref, wrg_ref,
                   u_ref, dq_ref, dk_ref, dvt_ref, iq_ref, ik_ref, iwt_ref,
                   rq_ref, rk_ref, rv_ref, rg_ref):
    h = _rms(x_ref[...], g_ref[...]).astype(_BF)
    u_ref[...] = _dot(h, wu_ref[...])

    cd, sd, sgd = cd_ref[...], sd_ref[...], sgd_ref[...]
    dq = _rope_roll(_dot(h, wdq_ref[...]), cd, sd, sgd, 8)
    dq_ref[...] = (dq * (DSA_HEAD_DIM ** -0.5 * math.log2(math.e))).astype(dq_ref.dtype)
    dk_ref[...] = _rope_roll(_dot(h, wdk_ref[...]), cd, sd, sgd, 8).astype(dk_ref.dtype)
    vt = _dot_nt(wdvt_ref[...], h)
    ones_row = lax.broadcasted_iota(jnp.int32, vt.shape, 0) == DSA_HEAD_DIM
    dvt_ref[...] = jnp.where(ones_row, 1.0, vt).astype(dvt_ref.dtype)

    ci, si, sgi = ci_ref[...], si_ref[...], sgi_ref[...]
    iq_ref[...] = _rope_roll(_dot(h, wiq_ref[...]), ci, si, sgi, 4) * (IDX_DIM ** -0.5)
    ik_ref[...] = _rope_roll(_dot(h, wik_ref[...]), ci, si, sgi, 4).astype(ik_ref.dtype)
    iwt_ref[...] = _dot_nt(wiwt_ref[...], h) * (IDX_HEADS ** -0.5)

    cr, sr = cr_ref[...], sr_ref[...]

    def rope_split(z):
        x1, x2 = z[:, :LANES], z[:, LANES:]
        return jnp.concatenate([x1 * cr - x2 * sr, x2 * cr + x1 * sr], axis=1)

    rq_ref[...] = rope_split(_dot(h, wrq_ref[...]))
    rk_ref[...] = rope_split(_dot(h, wrk_ref[...])) * (RET_QK_DIM ** -0.5)
    rv_ref[...] = _dot(h, wrv_ref[...])
    rg_ref[...] = _dot(h, wrg_ref[...])


def _inproj(x, g, tabs, ws):
    bsz, seq, _ = x.shape
    tm = TM_IN
    grid = (bsz, seq // tm)
    row = lambda w: pl.BlockSpec((None, tm, w), lambda b, i: (b, i, 0))
    colT = lambda r: pl.BlockSpec((None, r, tm), lambda b, i: (b, 0, i))
    in_specs = ([row(D_MODEL), _const_spec(g.shape)] + [row(LANES)] * 6
                + [_const_spec(t.shape) for t in tabs[6:]] + [_const_spec(w.shape) for w in ws])
    out_shape = [
        jax.ShapeDtypeStruct((bsz, seq, SSM_WIDTH), F32),
        jax.ShapeDtypeStruct((bsz, seq, DSA_WIDTH), _BF),
        jax.ShapeDtypeStruct((bsz, seq, LANES), _BF),
        jax.ShapeDtypeStruct((bsz, LANES, seq), _BF),
        jax.ShapeDtypeStruct((bsz, seq, LANES), F32),
        jax.ShapeDtypeStruct((bsz, seq, LANES), _BF),
        jax.ShapeDtypeStruct((bsz, SUBLANES, seq), F32),
        jax.ShapeDtypeStruct((bsz, seq, RET_QK_PAD), F32),
        jax.ShapeDtypeStruct((bsz, seq, RET_QK_PAD), F32),
        jax.ShapeDtypeStruct((bsz, seq, RET_WIDTH), F32),
        jax.ShapeDtypeStruct((bsz, seq, RET_WIDTH), F32),
    ]
    out_specs = [row(SSM_WIDTH), row(DSA_WIDTH), row(LANES), colT(LANES), row(LANES), row(LANES),
                 colT(SUBLANES), row(RET_QK_PAD), row(RET_QK_PAD), row(RET_WIDTH), row(RET_WIDTH)]
    return pl.pallas_call(
        _inproj_kernel, grid=grid, in_specs=in_specs, out_specs=out_specs, out_shape=out_shape,
        compiler_params=_params(("parallel", "parallel")), name="inproj",
    )(x, g, *tabs, *ws)


def _s5_kernel(u_ref, bbd_ref, cbd_ref, lam_ref, lam64_ref, pow_ref, d_ref, gw_ref, gb_ref,
               o_ref, st_ref, xm_ref, carry_ref):
    n = NSTATE

    @pl.when(pl.program_id(1) == 0)
    def _():
        carry_ref[...] = jnp.zeros_like(carry_ref)

    u = u_ref[...]
    st_ref[...] = _dot(u.astype(_BF), bbd_ref[...])

    lr = jnp.broadcast_to(lam_ref[:, :n], (SUBLANES, n))
    li = jnp.broadcast_to(lam_ref[:, n:], (SUBLANES, n))

    def step(j, c):
        xr, xi = c
        off = pl.multiple_of(j * SUBLANES, SUBLANES)
        ar = st_ref[pl.ds(off, SUBLANES), :n]
        ai = st_ref[pl.ds(off, SUBLANES), n:]
        nr = lr * xr - li * xi + ar
        ni = lr * xi + li * xr + ai
        st_ref[pl.ds(off, SUBLANES), :n] = nr
        st_ref[pl.ds(off, SUBLANES), n:] = ni
        return nr, ni

    zero = jnp.zeros((SUBLANES, n), F32)
    xr, xi = lax.fori_loop(0, S5_SEG, step, (zero, zero))

    l64r, l64i = lam64_ref[:, :n], lam64_ref[:, n:]
    cr, ci = carry_ref[:, :n], carry_ref[:, n:]
    for i in range(SUBLANES):
        xm_ref[i:i + 1, :n] = cr
        xm_ref[i:i + 1, n:] = ci
        er, ei = xr[i:i + 1, :], xi[i:i + 1, :]
        cr, ci = er + l64r * cr - l64i * ci, ei + l64r * ci + l64i * cr
    carry_ref[:, :n] = cr
    carry_ref[:, n:] = ci

    hr, hi = xm_ref[:, :n], xm_ref[:, n:]

    def fix(j, _):
        off = pl.multiple_of(j * SUBLANES, SUBLANES)
        pr = pow_ref[pl.ds(j, 1), :n]
        pi = pow_ref[pl.ds(j, 1), n:]
        st_ref[pl.ds(off, SUBLANES), :n] = st_ref[pl.ds(off, SUBLANES), :n] + pr * hr - pi * hi
        st_ref[pl.ds(off, SUBLANES), n:] = st_ref[pl.ds(off, SUBLANES), n:] + pr * hi + pi * hr
        return 0

    lax.fori_loop(0, S5_SEG, fix, 0)

    y = d_ref[...] * u
    kc = 512
    for k in range(2 * n // kc):
        y = y + _dot(st_ref[:, k * kc:(k + 1) * kc].astype(_BF), cbd_ref[k * kc:(k + 1) * kc, :])
    y = jax.nn.gelu(y)
    o_ref[...] = y * jax.nn.sigmoid(_dot(y.astype(_BF), gw_ref[...]) + gb_ref[...])


def _s5(u_perm, prm):
    bsz, seq, _ = u_perm.shape
    grid = (bsz, seq // S5_TILE)
    row = pl.BlockSpec((None, S5_TILE, SSM_WIDTH), lambda b, i: (b, i, 0))
    return pl.pallas_call(
        _s5_kernel, grid=grid,
        in_specs=[row] + [_const_spec(p.shape) for p in prm],
        out_specs=row,
        out_shape=jax.ShapeDtypeStruct((bsz, seq, SSM_WIDTH), F32),
        scratch_shapes=[pltpu.VMEM((S5_TILE, 2 * NSTATE), F32),
                        pltpu.VMEM((SUBLANES, 2 * NSTATE), F32),
                        pltpu.VMEM((1, 2 * NSTATE), F32)],
        compiler_params=_params(("arbitrary", "arbitrary")), name="s5",
    )(u_perm, *prm)


def _s5_params(lam_re, lam_im, log_step, b_re, b_im, c_re, c_im, d_skip, glu_w, glu_b):
    step = jnp.exp(log_step.astype(F32))[:, None]
    ere, eim = lam_re * step, lam_im * step
    mag = jnp.exp(ere)
    lb_re, lb_im = mag * jnp.cos(eim), mag * jnp.sin(eim)
    den = lam_re * lam_re + lam_im * lam_im
    f_re = ((lb_re - 1.0) * lam_re + lb_im * lam_im) / den
    f_im = (lb_im * lam_re - (lb_re - 1.0) * lam_im) / den
    bb_re = f_re[..., None] * b_re - f_im[..., None] * b_im
    bb_im = f_re[..., None] * b_im + f_im[..., None] * b_re
    eye = jnp.eye(SSM_GROUPS, dtype=F32)
    bbd = jnp.concatenate(
        [jnp.einsum('gpc,gh->gchp', bb, eye).reshape(SSM_WIDTH, NSTATE) for bb in (bb_re, bb_im)], axis=1)
    cbd = jnp.concatenate(
        [jnp.einsum('gcp,gh->hpgc', cc, eye).reshape(NSTATE, SSM_WIDTH) for cc in (c_re, -c_im)], axis=0)
    flat = lambda a: a.reshape(1, NSTATE)
    lam = jnp.concatenate([flat(lb_re), flat(lb_im)], axis=1)

    def power(k):
        k = jnp.asarray(k, F32).reshape(-1, 1)
        m = jnp.exp(k * flat(ere))
        return jnp.concatenate([m * jnp.cos(k * flat(eim)), m * jnp.sin(k * flat(eim))], axis=1)

    return (bbd.astype(_BF), cbd.astype(_BF), lam, power(float(S5_SEG)),
            power(jnp.arange(1, S5_SEG + 1)), d_skip.reshape(1, SSM_WIDTH),
            glu_w.astype(_BF), glu_b.reshape(1, SSM_WIDTH))


def _s5_permute(a):
    bsz, seq, w = a.shape
    return a.reshape(bsz, seq // S5_TILE, SUBLANES, S5_SEG, w).swapaxes(2, 3).reshape(bsz, seq, w)


def _s5_unpermute(a):
    bsz, seq, w = a.shape
    return a.reshape(bsz, seq // S5_TILE, S5_SEG, SUBLANES, w).swapaxes(2, 3).reshape(bsz, seq, w)


def _dsa_kernel(iq_ref, iwt_ref, dq_ref, ik_ref, dk_ref, dvt_ref, o_ref,
                key_ref, hi_ref, lo_ref, bias_ref, acc_ref, j_ref,
                lg0_ref, lg1_ref, s0_ref, s1_ref, p0_ref, p1_ref, *, seq, topk):
    qi = pl.program_id(1)
    nk = (qi * QB) // KT + 1
    lane = lax.broadcasted_iota(jnp.int32, (1, QB), 1)
    vis_end = qi * QB + jnp.where(lane < CHUNK, CHUNK, 2 * CHUNK)
    row_iota = lax.broadcasted_iota(jnp.int32, (KT, QB), 0)

    iq = iq_ref[...]
    col = lax.broadcasted_iota(jnp.int32, (QB, LANES), 1)
    qs = jnp.concatenate([jnp.where(col // IDX_DIM == h, iq, 0.0) for h in range(IDX_HEADS)],
                         axis=0).astype(_BF)
    w = iwt_ref[...]

    nkp = nk + lax.rem(nk, 2)
    last_tile = nkp - 1

    def logits(kt, dst_ref):
        off = pl.multiple_of(kt * KT, KT)
        dst_ref[...] = _dot_nt(ik_ref[pl.ds(off, KT), :], qs)

    def score_tile(kt, src_ref):
        off = pl.multiple_of(kt * KT, KT)
        lg = src_ref[...]
        sc = jnp.zeros((KT, QB), F32)
        for h in range(IDX_HEADS):
            sc = sc + jnp.maximum(lg[:, h * QB:(h + 1) * QB], 0.0) * w[h:h + 1, :]
        sc = jnp.where(sc == 0.0, 0.0, sc)
        bits = lax.bitcast_convert_type(sc, jnp.int32)
        key = bits ^ ((bits >> 31) & 0x7FFFFFFF)
        key = jnp.where(row_iota < vis_end - off, key, INT_MIN)
        key_ref[pl.ds(off, KT), :] = key
        hi_ref[pl.ds(off, KT), :] = (key >> 16).astype(jnp.int16)
        lo_ref[pl.ds(off, KT), :] = (key ^ 0x8000).astype(jnp.int16)

    def score_pair(i, c):
        logits(2 * i + 1, lg1_ref)
        score_tile(2 * i, lg0_ref)
        logits(jnp.minimum(2 * i + 2, last_tile), lg0_ref)
        score_tile(2 * i + 1, lg1_ref)
        return c

    logits(0, lg0_ref)
    lax.fori_loop(0, nkp // 2, score_pair, 0)

    one16 = jnp.ones((PACK, QB), jnp.int16)
    zero16 = jnp.zeros((PACK, QB), jnp.int16)

    def row16(v):
        return jnp.broadcast_to(v, (PACK, QB)).astype(jnp.int16)

    def count16(ref, pred):
        def body(i, accs):
            off = pl.multiple_of(i * (2 * KT), 2 * KT)
            blk = ref[pl.ds(off, 2 * KT), :]
            accs = list(accs)
            for r in range(2 * KT // PACK):
                accs[r % N_ACC] = accs[r % N_ACC] + jnp.where(pred(blk[r * PACK:(r + 1) * PACK, :]), one16, zero16)
            return tuple(accs)
        accs = lax.fori_loop(0, nkp // 2, body, (zero16,) * N_ACC)
        acc = functools.reduce(lambda a, b: a + b, accs)
        return acc.astype(jnp.int32).sum(axis=0, keepdims=True)

    def kth16(ref, k, t0, nbits):
        def body(i, t):
            cand = t + jnp.left_shift(jnp.int32(1), nbits - 1 - i)
            c16 = row16(cand)
            c = count16(ref, lambda s: s >= c16)
            return jnp.where(c >= k, cand, t)
        return lax.fori_loop(0, nbits, body, jnp.full((1, QB), t0, jnp.int32))

    h = kth16(hi_ref, topk, I16_MIN, 16)
    has_t = h != I16_MIN
    h16 = row16(h)

    def keep_bucket(kt, c):
        off = pl.multiple_of(kt * KT, KT)
        his, los, kept = hi_ref[pl.ds(off, KT), :], lo_ref[pl.ds(off, KT), :], []
        for r in range(KT // PACK):
            kept.append(jnp.where(his[r * PACK:(r + 1) * PACK, :] == h16, los[r * PACK:(r + 1) * PACK, :],
                                  jnp.full((PACK, QB), I16_MIN, jnp.int16)))
        lo_ref[pl.ds(off, KT), :] = jnp.concatenate(kept, axis=0)
        return c

    lax.fori_loop(0, nk, keep_bucket, 0)
    gt_hi = count16(hi_ref, lambda s: s > h16)
    lo = kth16(lo_ref, topk - gt_hi, I16_MIN, 16)
    t = jnp.where(has_t, h * 65536 + (lo + 32768), INT_MIN)
    lo16 = row16(lo)
    cnt_gt = gt_hi + count16(lo_ref, lambda s: s > lo16)

    u_ref = hi_ref

    def tie_digit(kt, c):
        off = pl.multiple_of(kt * KT, KT)
        u = jnp.where(key_ref[pl.ds(off, KT), :] == t, (seq - 1 - off) - row_iota, -1)
        u_ref[pl.ds(off, KT), :] = u.astype(jnp.int16)
        return c

    lax.fori_loop(0, nkp, tie_digit, 0)
    cnt_eq = count16(u_ref, lambda s: s >= zero16)
    need = topk - cnt_gt
    j_ref[0:1, :] = jnp.where(has_t, seq, -1)
    excess = jnp.max(jnp.where(has_t, cnt_eq - need, 0))

    @pl.when(excess > 0)
    def _():
        u = kth16(u_ref, need, 0, seq.bit_length() - 1)
        j_ref[0:1, :] = jnp.where(has_t, seq - 1 - u, -1)

    jlast = j_ref[0:1, :]

    def bias_tile(kt, _):
        off = pl.multiple_of(kt * KT, KT)
        k = key_ref[pl.ds(off, KT), :]
        sel = (k > t) | ((k == t) & (off + row_iota <= jlast))
        bias_ref[pl.ds(off, KT), :] = jnp.where(sel, 0.0, NEG_INF_SCORE)
        return 0

    lax.fori_loop(0, nkp, bias_tile, 0)

    q = dq_ref[...]
    half = (col // DSA_HEAD_DIM)
    qst = jnp.concatenate(
        [jnp.where(half == (h % 2), q[:, (h // 2) * LANES:(h // 2 + 1) * LANES], jnp.zeros((), q.dtype))
         for h in range(DSA_HEADS)], axis=0)
    npair = DSA_HEADS // 2
    hq = DSA_HEADS * QB
    acc_ref[...] = jnp.zeros_like(acc_ref)
    p1_ref[...] = jnp.zeros_like(p1_ref)

    def qk(kt, dst_ref):
        off = pl.multiple_of(kt * KT, KT)
        b = bias_ref[pl.ds(off, KT), :]
        dst_ref[...] = _dot_nt(dk_ref[pl.ds(off, KT), :], qst) + jnp.concatenate([b] * DSA_HEADS, axis=1)

    def numer(src_ref, dst_ref, m):
        s = src_ref[...]
        m_new = jnp.maximum(m, jnp.max(s, axis=0, keepdims=True))
        dst_ref[...] = jnp.exp2(s - m_new).astype(dst_ref.dtype)
        return m_new, jnp.exp2(m - m_new)

    def pv(kt, src_ref, alpha):
        off = pl.multiple_of(kt * KT, KT)
        vt = dvt_ref[0:V_ROWS, pl.ds(off, KT)]
        acc_ref[...] = alpha * acc_ref[...] + _dot(vt, src_ref[...])

    def att_pair(i, c):
        m, alpha = c
        qk(2 * i + 1, s1_ref)
        pv(jnp.maximum(2 * i - 1, 0), p1_ref, alpha)
        m, alpha = numer(s0_ref, p0_ref, m)
        qk(jnp.minimum(2 * i + 2, last_tile), s0_ref)
        pv(2 * i, p0_ref, alpha)
        m, alpha = numer(s1_ref, p1_ref, m)
        return m, alpha

    qk(0, s0_ref)
    c0 = (jnp.full((1, hq), -jnp.inf, F32), jnp.ones((1, hq), F32))
    _, alpha = lax.fori_loop(0, nkp // 2, att_pair, c0)
    pv(last_tile, p1_ref, alpha)
    acc = acc_ref[...]
    o = acc[0:DSA_HEAD_DIM, :] / acc[DSA_HEAD_DIM:DSA_HEAD_DIM + 1, :]
    for j in range(npair):
        o_ref[:, j * LANES:(j + 1) * LANES] = jnp.concatenate(
            [o[:, (2 * j) * QB:(2 * j + 1) * QB], o[:, (2 * j + 1) * QB:(2 * j + 2) * QB]], axis=0).T


def _dsa(iq, iwt, dq, ik, dk, dvt):
    bsz, seq, _ = iq.shape
    topk = min(DSA_TOPK, seq // 4)
    grid = (bsz, seq // QB)
    qrow = lambda w: pl.BlockSpec((None, QB, w), lambda b, i: (b, i, 0))
    full = lambda r, c: pl.BlockSpec((None, r, c), lambda b, i: (b, 0, 0))
    return pl.pallas_call(
        functools.partial(_dsa_kernel, seq=seq, topk=topk), grid=grid,
        in_specs=[qrow(LANES), pl.BlockSpec((None, SUBLANES, QB), lambda b, i: (b, 0, i)), qrow(DSA_WIDTH),
                  full(seq, LANES), full(seq, LANES), full(LANES, seq)],
        out_specs=qrow(DSA_WIDTH),
        out_shape=jax.ShapeDtypeStruct((bsz, seq, DSA_WIDTH), F32),
        scratch_shapes=[pltpu.VMEM((seq, QB), jnp.int32), pltpu.VMEM((seq, QB), jnp.int16),
                        pltpu.VMEM((seq, QB), jnp.int16), pltpu.VMEM((seq, QB), F32),
                        pltpu.VMEM((V_ROWS, DSA_HEADS * QB), F32), pltpu.VMEM((SUBLANES, QB), jnp.int32),
                        pltpu.VMEM((KT, IDX_HEADS * QB), F32), pltpu.VMEM((KT, IDX_HEADS * QB), F32),
                        pltpu.VMEM((KT, DSA_HEADS * QB), F32), pltpu.VMEM((KT, DSA_HEADS * QB), F32),
                        pltpu.VMEM((KT, DSA_HEADS * QB), _BF), pltpu.VMEM((KT, DSA_HEADS * QB), _BF)],
        compiler_params=_params(("parallel", "arbitrary")), name="dsa",
    )(iq, iwt, dq, ik, dk, dvt)


def _ret_kernel(rq_ref, rk_ref, rv_ref, rg_ref, qdec_ref, kdec_ref, dmask_ref, cdec_ref, blk_ref,
                hmq_ref, hmv_ref, ones_ref, ng_ref, o_ref, s_ref):
    @pl.when(pl.program_id(1) == 0)
    def _():
        s_ref[...] = jnp.zeros_like(s_ref)

    q, k, v = rq_ref[...], rk_ref[...], rv_ref[...]
    state = s_ref[...]
    y = _dot((q * qdec_ref[...]).astype(_BF), state.astype(_BF))
    kd = (k * kdec_ref[...]).T.astype(_BF)
    vb = v.astype(_BF)
    s_ref[...] = state * cdec_ref[...] + _dot(kd, vb) * blk_ref[...]
    kb = k.astype(_BF)
    for h in range(RET_HEADS):
        a = _dot_nt((q * hmq_ref[h:h + 1, :]).astype(_BF), kb) * dmask_ref[h]
        y = y + _dot(a.astype(_BF), (v * hmv_ref[h:h + 1, :]).astype(_BF))

    ones = ones_ref[...]

    def head_mean(z):
        hi = z.astype(_BF)
        lo = (z - hi.astype(F32)).astype(_BF)
        return (_dot(hi, ones) + _dot(lo, ones)) * (1.0 / RET_V_DIM)

    d = y - head_mean(y)
    yn = d * lax.rsqrt(head_mean(d * d) + EPS) * ng_ref[...]
    o_ref[...] = jax.nn.silu(rg_ref[...]) * yn


def _ret_consts():
    c = RET_C
    log_g = jnp.log1p(-jnp.exp2(-5.0 - jnp.arange(RET_HEADS, dtype=F32)))
    pos = jnp.arange(c, dtype=F32)
    diff = pos[:, None] - pos[None, :]
    dmask = jnp.where(diff >= 0, jnp.exp(log_g[:, None, None] * jnp.maximum(diff, 0.0)), 0.0)
    lane_q = jnp.arange(RET_QK_PAD)
    head_q = jnp.where((lane_q % LANES) < RET_HEADS * (RET_QK_DIM // 2), (lane_q % LANES) // (RET_QK_DIM // 2), -1)
    head_v = jnp.arange(RET_WIDTH) // RET_V_DIM
    hmq = (head_q[None, :] == jnp.arange(RET_HEADS)[:, None]).astype(F32)
    hmv = (head_v[None, :] == jnp.arange(RET_HEADS)[:, None]).astype(F32)
    lg_q = hmq.T @ log_g
    qdec = jnp.exp(lg_q[None, :] * (pos[:, None] + 1.0))
    kdec = jnp.exp(lg_q[None, :] * (c - 1.0 - pos[:, None]))
    blk = hmq.T @ hmv
    cdec = blk * jnp.exp(lg_q * c)[:, None]
    ones = (hmv.T @ hmv).astype(_BF)
    return qdec, kdec, dmask, cdec, blk, hmq, hmv, ones


def _retention(rq, rk, rv, rg, norm_g):
    bsz, seq, _ = rq.shape
    consts = _ret_consts()
    grid = (bsz, seq // RET_C)
    row = lambda w: pl.BlockSpec((None, RET_C, w), lambda b, i: (b, i, 0))
    ng = norm_g.reshape(1, RET_WIDTH)
    return pl.pallas_call(
        _ret_kernel, grid=grid,
        in_specs=[row(RET_QK_PAD), row(RET_QK_PAD), row(RET_WIDTH), row(RET_WIDTH)]
        + [_const_spec(a.shape) for a in consts] + [_const_spec(ng.shape)],
        out_specs=row(RET_WIDTH),
        out_shape=jax.ShapeDtypeStruct((bsz, seq, RET_WIDTH), F32),
        scratch_shapes=[pltpu.VMEM((RET_QK_PAD, RET_WIDTH), F32)],
        compiler_params=_params(("parallel", "arbitrary")), name="retention",
    )(rq, rk, rv, rg, *consts, ng)


def _merge_kernel(x_ref, g_ref, ya_ref, yb_ref, yc_ref, wg_ref, wa_ref, wb_ref, wc_ref, wo_ref, o_ref):
    x = x_ref[...]
    h = _rms(x, g_ref[...]).astype(_BF)
    merged = jnp.zeros(x.shape, F32)
    for n, (y_ref, w_ref) in enumerate(((ya_ref, wa_ref), (yb_ref, wb_ref), (yc_ref, wc_ref))):
        gate = jax.nn.sigmoid(_dot(h, wg_ref[:, n * D_MODEL:(n + 1) * D_MODEL]))
        merged = merged + gate * _dot(y_ref[...].astype(_BF), w_ref[...])
    o_ref[...] = x + _dot(merged.astype(_BF), wo_ref[...])


def _merge(x, g, ya, yb, yc, wg, wa, wb, wc, wo):
    bsz, seq, _ = x.shape
    grid = (bsz, seq // TM_OUT)
    row = lambda w: pl.BlockSpec((None, TM_OUT, w), lambda b, i: (b, i, 0))
    return pl.pallas_call(
        _merge_kernel, grid=grid,
        in_specs=[row(D_MODEL), _const_spec(g.shape), row(SSM_WIDTH), row(DSA_WIDTH), row(RET_WIDTH)]
        + [_const_spec(w.shape) for w in (wg, wa, wb, wc, wo)],
        out_specs=row(D_MODEL),
        out_shape=jax.ShapeDtypeStruct(x.shape, F32),
        compiler_params=_params(("parallel", "parallel")), name="merge",
    )(x, g, ya, yb, yc, wg, wa, wb, wc, wo)


def _mlp_kernel(x_ref, g_ref, w1_ref, w2_ref, fg_ref, o_ref, *, final_norm):
    x = x_ref[...]
    h = _rms(x, g_ref[...]).astype(_BF)
    acc = x
    for f in range(D_FF // FF_CHUNK):
        t = jnp.maximum(_dot(h, w1_ref[:, f * FF_CHUNK:(f + 1) * FF_CHUNK]), 0.0)
        acc = acc + _dot((t * t).astype(_BF), w2_ref[f * FF_CHUNK:(f + 1) * FF_CHUNK, :])
    o_ref[...] = _rms(acc, fg_ref[...]) if final_norm else acc


def _mlp(x, g, w1, w2, fg, final_norm):
    bsz, seq, _ = x.shape
    grid = (bsz, seq // TM_OUT)
    row = pl.BlockSpec((None, TM_OUT, D_MODEL), lambda b, i: (b, i, 0))
    return pl.pallas_call(
        functools.partial(_mlp_kernel, final_norm=final_norm), grid=grid,
        in_specs=[row, _const_spec(g.shape), _const_spec(w1.shape), _const_spec(w2.shape), _const_spec(fg.shape)],
        out_specs=row,
        out_shape=jax.ShapeDtypeStruct(x.shape, F32),
        compiler_params=_params(("parallel", "parallel")), name="mlp",
    )(x, g, w1, w2, fg)


def _rope_tables(positions):
    pos = positions.astype(F32)[..., None]
    lane = jnp.arange(LANES)

    def angles(rot_dim, theta):
        half = rot_dim // 2
        inv = jnp.exp(-math.log(theta) * jnp.arange(half, dtype=F32) * (2.0 / rot_dim))
        return pos * inv

    def roll_tables(head_dim, rot_dim, theta):
        half = rot_dim // 2
        ang = angles(rot_dim, theta)
        ln = lane % head_dim
        rot = ln < rot_dim
        a = jnp.take(ang, ln % half, axis=-1)
        cos = jnp.where(rot, jnp.cos(a), 1.0)
        sin = jnp.where(rot, jnp.sin(a), 0.0)
        sgn = jnp.stack([jnp.where(ln < half, -1.0, 0.0),
                         jnp.where((ln >= half) & rot, 1.0, 0.0)]).astype(F32)
        return cos, sin, sgn

    cd, sd, sgd = roll_tables(DSA_HEAD_DIM, DSA_HEAD_DIM // 4, ROPE_THETA)
    ci, si, sgi = roll_tables(IDX_DIM, IDX_DIM // 4, ROPE_THETA)
    half = RET_QK_DIM // 2
    ang = angles(RET_QK_DIM, RET_THETA)
    a = jnp.take(ang, lane % half, axis=-1)
    used = lane < RET_HEADS * half
    cr = jnp.where(used, jnp.cos(a), 1.0)
    sr = jnp.where(used, jnp.sin(a), 0.0)
    return cd, sd, ci, si, cr, sr, sgd, sgi


def _inproj_weights(w):
    sizes = (SSM_WIDTH, DSA_WIDTH, DSA_HEAD_DIM, DSA_HEAD_DIM, IDX_HEADS * IDX_DIM, IDX_DIM, IDX_HEADS,
             RET_HEADS * RET_QK_DIM, RET_HEADS * RET_QK_DIM, RET_WIDTH, RET_WIDTH, 3 * D_MODEL)
    parts, off = [], 0
    for n in sizes:
        parts.append(w[:, off:off + n])
        off += n
    wu, wdq, wdk, wdv, wiq, wik, wiw, wrq, wrk, wrv, wrg, wgt = parts
    half = RET_QK_DIM // 2

    def ret_split(a):
        a = a.reshape(D_MODEL, RET_HEADS, 2, half)
        pad = jnp.zeros((D_MODEL, LANES - RET_HEADS * half), a.dtype)
        return jnp.concatenate([a[:, :, 0].reshape(D_MODEL, -1), pad, a[:, :, 1].reshape(D_MODEL, -1), pad], axis=1)

    wiwt = jnp.concatenate([wiw.T, jnp.zeros((SUBLANES - IDX_HEADS, D_MODEL), w.dtype)], axis=0)
    wdvt = jnp.concatenate([wdv.T, jnp.zeros((LANES - DSA_HEAD_DIM, D_MODEL), w.dtype)], axis=0)
    ws = (wu, wdq, jnp.concatenate([wdk, wdk], axis=1), wdvt,
          wiq, jnp.concatenate([wik] * IDX_HEADS, axis=1), wiwt,
          ret_split(wrq), ret_split(wrk), wrv, wrg)
    return tuple(a.astype(_BF) for a in ws), wgt.astype(_BF)


def kernel(x, positions, norm1_g, w_in, ssm_lambda_re, ssm_lambda_im, ssm_log_step, ssm_b_re, ssm_b_im,
           ssm_c_re, ssm_c_im, ssm_d, ssm_glu_w, ssm_glu_b, ret_norm_g, w_proj_a, w_proj_b, w_proj_c,
           w_out, norm2_g, w_ff1, w_ff2, final_norm_g):
    depth = w_in.shape[0]
    tabs = _rope_tables(positions)
    fg = final_norm_g.reshape(1, D_MODEL)
    for l in range(depth):
        g1 = norm1_g[l].reshape(1, D_MODEL)
        ws, wgt = _inproj_weights(w_in[l])
        u, dq, dk, dvt, iq, ik, iwt, rq, rk, rv, rg = _inproj(x, g1, tabs, ws)

        s5p = _s5_params(ssm_lambda_re[l], ssm_lambda_im[l], ssm_log_step[l], ssm_b_re[l], ssm_b_im[l],
                         ssm_c_re[l], ssm_c_im[l], ssm_d[l], ssm_glu_w[l], ssm_glu_b[l])
        ya = _s5_unpermute(_s5(_s5_permute(u), s5p))
        yb = _dsa(iq, iwt, dq, ik, dk, dvt)
        yc = _retention(rq, rk, rv, rg, ret_norm_g[l])

        x = _merge(x, g1, ya, yb, yc, wgt, w_proj_a[l].astype(_BF), w_proj_b[l].astype(_BF),
                   w_proj_c[l].astype(_BF), w_out[l].astype(_BF))
        x = _mlp(x, norm2_g[l].reshape(1, D_MODEL), w_ff1[l].astype(_BF), w_ff2[l].astype(_BF), fg,
                 final_norm=(l == depth - 1))
    return x
```

```python
import functools
import math

import jax
import jax.numpy as jnp
from jax import lax
from jax.experimental import pallas as pl
from jax.experimental.pallas import tpu as pltpu

F32 = jnp.float32
_BF = jnp.bfloat16
LANES = 128
SUBLANES = 8
VMEM_LIMIT = 56 * 1024 * 1024

D_MODEL = 1024
CHUNK = 64
EPS = 1e-6
NEG_INF_SCORE = -1e30
INT_MIN = -2 ** 31
GROUP = 32 * SUBLANES
GROUPS_PER_STEP = 4
V_ROWS = 80

SSM_WIDTH = 256
SSM_GROUP = 16
SSM_GROUPS = 16
SSM_STATE = 64
NSTATE = SSM_GROUPS * SSM_STATE

DSA_HEADS = 6
DSA_HEAD_DIM = 64
DSA_WIDTH = 384
IDX_HEADS = 4
IDX_DIM = 32
DSA_TOPK = 256
ROPE_THETA = 500000.0

RET_HEADS = 4
RET_QK_DIM = 48
RET_V_DIM = 96
RET_WIDTH = 384
RET_THETA = 10000.0
RET_QK_PAD = 256

D_FF = 4096

TM_IN = 256
S5_TILE = 512
S5_SEG = S5_TILE // SUBLANES
QB = 128
KT = 512
RET_C = 128
TM_OUT = 512
FF_CHUNK = 1024


def _rms(x, g):
    return x * lax.rsqrt(jnp.mean(x * x, axis=-1, keepdims=True) + EPS) * g


def _dot(a, b):
    return jnp.dot(a, b, preferred_element_type=F32)


def _dot_nt(a, b):
    return lax.dot_general(a, b, (((1,), (1,)), ((), ())), preferred_element_type=F32)


def _to_key(v):
    bits = lax.bitcast_convert_type(v, jnp.int32)
    return bits ^ ((bits >> 31) & 0x7FFFFFFF)


def _transpose32(a):
    a = list(a)
    j, m = 16, 0x0000FFFF
    while j:
        k = 0
        while k < 32:
            t = (a[k] ^ (a[k + j] >> j)) & m
            a[k] = a[k] ^ t
            a[k + j] = a[k + j] ^ (t << j)
            k = (k + j + 1) & ~j
        j >>= 1
        m = (m ^ (m << j)) & 0xFFFFFFFF
    return a


def _const_spec(shape):
    nd = len(shape)
    return pl.BlockSpec(shape, lambda *_: (0,) * nd)


def _params(sem):
    return pltpu.CompilerParams(dimension_semantics=sem, vmem_limit_bytes=VMEM_LIMIT)


def _rope_roll(z, cos, sin, sgn, shift):
    outs = []
    for c in range(z.shape[1] // LANES):
        zc = z[:, c * LANES:(c + 1) * LANES]
        rot = (pltpu.roll(zc, LANES - shift, 1) * sgn[0:1, :]
               + pltpu.roll(zc, shift, 1) * sgn[1:2, :])
        outs.append(zc * cos + rot * sin)
    return outs[0] if len(outs) == 1 else jnp.concatenate(outs, axis=1)


def _inproj_kernel(x_ref, g_ref, cd_ref, sd_ref, ci_ref, si_ref, cr_ref, sr_ref, sgd_ref, sgi_ref,
                   wu_ref, wdq_ref, wdk_ref, wdvt_ref, wiq_ref, wik_ref, wiwt_ref,
                   wrq_ref, wrk_ref, wrv_ref, wrg_ref,
                   u_ref, dq_ref, dk_ref, dvt_ref, iq_ref, ik_ref, iwt_ref,
                   rq_ref, rk_ref, rv_ref, rg_ref):
    h = _rms(x_ref[...], g_ref[...]).astype(_BF)
    u_ref[...] = _dot(h, wu_ref[...])

    cd, sd, sgd = cd_ref[...], sd_ref[...], sgd_ref[...]
    dq = _rope_roll(_dot(h, wdq_ref[...]), cd, sd, sgd, 8)
    dq_ref[...] = (dq * (DSA_HEAD_DIM ** -0.5 * math.log2(math.e))).astype(dq_ref.dtype)
    dk_ref[...] = _rope_roll(_dot(h, wdk_ref[...]), cd, sd, sgd, 8).astype(dk_ref.dtype)
    vt = _dot_nt(wdvt_ref[...], h)
    ones_row = lax.broadcasted_iota(jnp.int32, vt.shape, 0) == DSA_HEAD_DIM
    dvt_ref[...] = jnp.where(ones_row, 1.0, vt).astype(dvt_ref.dtype)

    ci, si, sgi = ci_ref[...], si_ref[...], sgi_ref[...]
    iq_ref[...] = _rope_roll(_dot(h, wiq_ref[...]), ci, si, sgi, 4) * (IDX_DIM ** -0.5)
    ik_ref[...] = _rope_roll(_dot(h, wik_ref[...]), ci, si, sgi, 4).astype(ik_ref.dtype)
    iwt_ref[...] = _dot_nt(wiwt_ref[...], h) * (IDX_HEADS ** -0.5)

    cr, sr = cr_ref[...], sr_ref[...]

    def rope_split(z):
        x1, x2 = z[:, :LANES], z[:, LANES:]
        return jnp.concatenate([x1 * cr - x2 * sr, x2 * cr + x1 * sr], axis=1)

    rq_ref[...] = rope_split(_dot(h, wrq_ref[...]))
    rk_ref[...] = rope_split(_dot(h, wrk_ref[...])) * (RET_QK_DIM ** -0.5)
    rv_ref[...] = _dot(h, wrv_ref[...])
    rg_ref[...] = _dot(h, wrg_ref[...])


def _inproj(x, g, tabs, ws):
    bsz, seq, _ = x.shape
    tm = TM_IN
    grid = (bsz, seq // tm)
    row = lambda w: pl.BlockSpec((None, tm, w), lambda b, i: (b, i, 0))
    colT = lambda r: pl.BlockSpec((None, r, tm), lambda b, i: (b, 0, i))
    in_specs = ([row(D_MODEL), _const_spec(g.shape)] + [row(LANES)] * 6
                + [_const_spec(t.shape) for t in tabs[6:]] + [_const_spec(w.shape) for w in ws])
    out_shape = [
        jax.ShapeDtypeStruct((bsz, seq, SSM_WIDTH), F32),
        jax.ShapeDtypeStruct((bsz, seq, DSA_WIDTH), _BF),
        jax.ShapeDtypeStruct((bsz, seq, LANES), _BF),
        jax.ShapeDtypeStruct((bsz, LANES, seq), _BF),
        jax.ShapeDtypeStruct((bsz, seq, LANES), F32),
        jax.ShapeDtypeStruct((bsz, seq, LANES), _BF),
        jax.ShapeDtypeStruct((bsz, SUBLANES, seq), F32),
        jax.ShapeDtypeStruct((bsz, seq, RET_QK_PAD), F32),
        jax.ShapeDtypeStruct((bsz, seq, RET_QK_PAD), F32),
        jax.ShapeDtypeStruct((bsz, seq, RET_WIDTH), F32),
        jax.ShapeDtypeStruct((bsz, seq, RET_WIDTH), F32),
    ]
    out_specs = [row(SSM_WIDTH), row(DSA_WIDTH), row(LANES), colT(LANES), row(LANES), row(LANES),
                 colT(SUBLANES), row(RET_QK_PAD), row(RET_QK_PAD), row(RET_WIDTH), row(RET_WIDTH)]
    return pl.pallas_call(
        _inproj_kernel, grid=grid, in_specs=in_specs, out_specs=out_specs, out_shape=out_shape,
        compiler_params=_params(("parallel", "parallel")), name="inproj",
    )(x, g, *tabs, *ws)


def _s5_kernel(u_ref, bbd_ref, cbd_ref, lam_ref, lam64_ref, pow_ref, d_ref, gw_ref, gb_ref,
               o_ref, st_ref, xm_ref, carry_ref, up_ref, yp_ref):
    n = NSTATE

    @pl.when(pl.program_id(1) == 0)
    def _():
        carry_ref[...] = jnp.zeros_like(carry_ref)

    nhalf = SSM_WIDTH // LANES
    for i in range(SUBLANES):
        for c in range(nhalf):
            up_ref[c, pl.ds(i, S5_SEG, stride=SUBLANES), :] = u_ref[i * S5_SEG:(i + 1) * S5_SEG,
                                                                    c * LANES:(c + 1) * LANES]
    u = jnp.concatenate([up_ref[c] for c in range(nhalf)], axis=1)
    st_ref[...] = _dot(u.astype(_BF), bbd_ref[...])

    lr = jnp.broadcast_to(lam_ref[:, :n], (SUBLANES, n))
    li = jnp.broadcast_to(lam_ref[:, n:], (SUBLANES, n))

    def step(j, c):
        xr, xi = c
        off = pl.multiple_of(j * SUBLANES, SUBLANES)
        ar = st_ref[pl.ds(off, SUBLANES), :n]
        ai = st_ref[pl.ds(off, SUBLANES), n:]
        nr = lr * xr - li * xi + ar
        ni = lr * xi + li * xr + ai
        st_ref[pl.ds(off, SUBLANES), :n] = nr
        st_ref[pl.ds(off, SUBLANES), n:] = ni
        return nr, ni

    zero = jnp.zeros((SUBLANES, n), F32)
    xr, xi = lax.fori_loop(0, S5_SEG, step, (zero, zero))

    l64r, l64i = lam64_ref[:, :n], lam64_ref[:, n:]
    cr, ci = carry_ref[:, :n], carry_ref[:, n:]
    for i in range(SUBLANES):
        xm_ref[i:i + 1, :n] = cr
        xm_ref[i:i + 1, n:] = ci
        er, ei = xr[i:i + 1, :], xi[i:i + 1, :]
        cr, ci = er + l64r * cr - l64i * ci, ei + l64r * ci + l64i * cr
    carry_ref[:, :n] = cr
    carry_ref[:, n:] = ci

    hr, hi = xm_ref[:, :n], xm_ref[:, n:]

    def fix(j, _):
        off = pl.multiple_of(j * SUBLANES, SUBLANES)
        pr = pow_ref[pl.ds(j, 1), :n]
        pi = pow_ref[pl.ds(j, 1), n:]
        st_ref[pl.ds(off, SUBLANES), :n] = st_ref[pl.ds(off, SUBLANES), :n] + pr * hr - pi * hi
        st_ref[pl.ds(off, SUBLANES), n:] = st_ref[pl.ds(off, SUBLANES), n:] + pr * hi + pi * hr
        return 0

    lax.fori_loop(0, S5_SEG, fix, 0)

    y = d_ref[...] * u
    kc = 512
    for k in range(2 * n // kc):
        y = y + _dot(st_ref[:, k * kc:(k + 1) * kc].astype(_BF), cbd_ref[k * kc:(k + 1) * kc, :])
    y = jax.nn.gelu(y)
    y = y * jax.nn.sigmoid(_dot(y.astype(_BF), gw_ref[...]) + gb_ref[...])
    for c in range(nhalf):
        yp_ref[c] = y[:, c * LANES:(c + 1) * LANES]
    for i in range(SUBLANES):
        for c in range(nhalf):
            o_ref[i * S5_SEG:(i + 1) * S5_SEG, c * LANES:(c + 1) * LANES] = yp_ref[
                c, pl.ds(i, S5_SEG, stride=SUBLANES), :]


def _s5(u_perm, prm):
    bsz, seq, _ = u_perm.shape
    grid = (bsz, seq // S5_TILE)
    row = pl.BlockSpec((None, S5_TILE, SSM_WIDTH), lambda b, i: (b, i, 0))
    return pl.pallas_call(
        _s5_kernel, grid=grid,
        in_specs=[row] + [_const_spec(p.shape) for p in prm],
        out_specs=row,
        out_shape=jax.ShapeDtypeStruct((bsz, seq, SSM_WIDTH), F32),
        scratch_shapes=[pltpu.VMEM((S5_TILE, 2 * NSTATE), F32),
                        pltpu.VMEM((SUBLANES, 2 * NSTATE), F32),
                        pltpu.VMEM((1, 2 * NSTATE), F32),
                        pltpu.VMEM((SSM_WIDTH // LANES, S5_TILE, LANES), F32),
                        pltpu.VMEM((SSM_WIDTH // LANES, S5_TILE, LANES), F32)],
        compiler_params=_params(("arbitrary", "arbitrary")), name="s5",
    )(u_perm, *prm)


def _s5_params(lam_re, lam_im, log_step, b_re, b_im, c_re, c_im, d_skip, glu_w, glu_b):
    step = jnp.exp(log_step.astype(F32))[:, None]
    ere, eim = lam_re * step, lam_im * step
    mag = jnp.exp(ere)
    lb_re, lb_im = mag * jnp.cos(eim), mag * jnp.sin(eim)
    den = lam_re * lam_re + lam_im * lam_im
    f_re = ((lb_re - 1.0) * lam_re + lb_im * lam_im) / den
    f_im = (lb_im * lam_re - (lb_re - 1.0) * lam_im) / den
    bb_re = f_re[..., None] * b_re - f_im[..., None] * b_im
    bb_im = f_re[..., None] * b_im + f_im[..., None] * b_re
    eye = jnp.eye(SSM_GROUPS, dtype=F32)
    bbd = jnp.concatenate(
        [jnp.einsum('gpc,gh->gchp', bb, eye).reshape(SSM_WIDTH, NSTATE) for bb in (bb_re, bb_im)], axis=1)
    cbd = jnp.concatenate(
        [jnp.einsum('gcp,gh->hpgc', cc, eye).reshape(NSTATE, SSM_WIDTH) for cc in (c_re, -c_im)], axis=0)
    flat = lambda a: a.reshape(1, NSTATE)
    lam = jnp.concatenate([flat(lb_re), flat(lb_im)], axis=1)

    def power(k):
        k = jnp.asarray(k, F32).reshape(-1, 1)
        m = jnp.exp(k * flat(ere))
        return jnp.concatenate([m * jnp.cos(k * flat(eim)), m * jnp.sin(k * flat(eim))], axis=1)

    return (bbd.astype(_BF), cbd.astype(_BF), lam, power(float(S5_SEG)),
            power(jnp.arange(1, S5_SEG + 1)), d_skip.reshape(1, SSM_WIDTH),
            glu_w.astype(_BF), glu_b.reshape(1, SSM_WIDTH))


def _dsa_kernel(iq_ref, iwt_ref, dq_ref, ik_ref, dk_ref, dvt_ref, o_ref,
                planes_ref, cand_ref, pick_ref, bias_ref, acc_ref,
                lg0_ref, lg1_ref, s0_ref, s1_ref, p0_ref, p1_ref, *, seq, topk):
    qi = pl.program_id(1)
    nk = (qi * QB) // KT + 1
    lane = lax.broadcasted_iota(jnp.int32, (1, QB), 1)
    vis_end = qi * QB + jnp.where(lane < CHUNK, CHUNK, 2 * CHUNK)

    iq = iq_ref[...]
    col = lax.broadcasted_iota(jnp.int32, (QB, LANES), 1)
    qs = jnp.concatenate([jnp.where(col // IDX_DIM == h, iq, 0.0) for h in range(IDX_HEADS)],
                         axis=0).astype(_BF)
    w = iwt_ref[...]

    nkp = nk + lax.rem(nk, 2)
    last_tile = nkp - 1

    def logits(kt, dst_ref):
        off = pl.multiple_of(kt * KT, KT)
        dst_ref[...] = _dot_nt(ik_ref[pl.ds(off, KT), :], qs)

    def score_tile(kt, src_ref):
        lg = src_ref[...]
        sc = jnp.zeros((KT, QB), F32)
        for h in range(IDX_HEADS):
            sc = sc + jnp.maximum(lg[:, h * QB:(h + 1) * QB], 0.0) * w[h:h + 1, :]
        sc = jnp.where(sc == 0.0, 0.0, sc)
        ukey = lax.bitcast_convert_type(_to_key(sc) ^ INT_MIN, jnp.uint32)
        for g in range(KT // GROUP):
            words = _transpose32([ukey[g * GROUP + m * SUBLANES:g * GROUP + (m + 1) * SUBLANES, :]
                                  for m in range(32)])
            base = pl.multiple_of((kt * (KT // GROUP) + g) * GROUP, GROUP)
            planes_ref[pl.ds(base, GROUP), :] = lax.bitcast_convert_type(jnp.concatenate(words, axis=0), jnp.int32)

    def score_pair(i, c):
        logits(2 * i + 1, lg1_ref)
        score_tile(2 * i, lg0_ref)
        logits(jnp.minimum(2 * i + 2, last_tile), lg0_ref)
        score_tile(2 * i + 1, lg1_ref)
        return c

    logits(0, lg0_ref)
    lax.fori_loop(0, nkp // 2, score_pair, 0)

    ngrp = nkp * (KT // GROUP)
    sub = lax.broadcasted_iota(jnp.int32, (SUBLANES, QB), 0)
    zero8 = jnp.zeros((SUBLANES, QB), jnp.int32)

    def rows_below(limit, g):
        nm = lax.shift_right_arithmetic(limit - g * GROUP + (SUBLANES - 1), 3)
        top = lax.shift_right_arithmetic(jnp.full((SUBLANES, QB), INT_MIN, jnp.int32),
                                         jnp.clip(nm, 1, 32) - 1)
        return jnp.where(nm <= 0, 0, top)

    def word(ref, g):
        return ref[pl.ds(pl.multiple_of(g * SUBLANES, SUBLANES), SUBLANES), :]

    def put_word(ref, g, v):
        ref[pl.ds(pl.multiple_of(g * SUBLANES, SUBLANES), SUBLANES), :] = v

    def plane(g, i):
        return planes_ref[pl.ds(pl.multiple_of(g * GROUP + i * SUBLANES, SUBLANES), SUBLANES), :]

    def sweep(step):
        def body(s, accs):
            return tuple(a + step(s * GROUPS_PER_STEP + t) for t, a in enumerate(accs))
        accs = lax.fori_loop(0, ngrp // GROUPS_PER_STEP, body, (zero8,) * GROUPS_PER_STEP)
        return functools.reduce(lambda a, b: a + b, accs).sum(axis=0, keepdims=True)

    def init_words(g):
        put_word(cand_ref, g, rows_below(vis_end - sub, g))
        put_word(pick_ref, g, zero8)
        return zero8

    sweep(init_words)

    def count_ones(i):
        return sweep(lambda g: lax.population_count(word(cand_ref, g) & plane(g, i)))

    def decide(c1, rem):
        take = c1 >= rem
        return jnp.where(take, 1, 0), jnp.where(take, rem, rem - c1)

    def narrow(g, i, take):
        take8 = jnp.broadcast_to(take, (SUBLANES, QB)) != 0
        c = word(cand_ref, g)
        ones = c & plane(g, i)
        put_word(pick_ref, g, word(pick_ref, g) | jnp.where(take8, 0, ones))
        c = jnp.where(take8, ones, c ^ ones)
        put_word(cand_ref, g, c)
        return c

    def radix_step(i, carry):
        take8, rem = carry

        def step(g):
            return lax.population_count(narrow(g, i - 1, take8) & plane(g, i))
        return decide(sweep(step), rem)

    take8, rem = decide(count_ones(0), jnp.full((1, QB), topk, jnp.int32))
    take8, rem = lax.fori_loop(1, 32, radix_step, (take8, rem))

    def last_narrow(g):
        narrow(g, 31, take8)
        return zero8

    sweep(last_narrow)

    def tie_step(i, edge):
        trial = edge + jnp.left_shift(jnp.int32(1), (seq.bit_length() - 1) - i)
        limit = trial - sub
        below = sweep(lambda g: lax.population_count(word(cand_ref, g) & rows_below(limit, g)))
        return jnp.where(below < rem, trial, edge)

    edge = lax.fori_loop(0, seq.bit_length(), tie_step, jnp.zeros((1, QB), jnp.int32))
    keep_limit = edge + 1 - sub

    def bias_group(g):
        chosen = word(pick_ref, g) | (word(cand_ref, g) & rows_below(keep_limit, g))
        base = pl.multiple_of(g * GROUP, GROUP)
        bias_ref[pl.ds(base, GROUP), :] = jnp.concatenate(
            [jnp.where(lax.shift_left(chosen, jnp.full_like(chosen, m)) < 0, 0.0, NEG_INF_SCORE)
             for m in range(32)], axis=0)
        return zero8

    sweep(bias_group)

    q = dq_ref[...]
    half = (col // DSA_HEAD_DIM)
    qst = jnp.concatenate(
        [jnp.where(half == (h % 2), q[:, (h // 2) * LANES:(h // 2 + 1) * LANES], jnp.zeros((), q.dtype))
         for h in range(DSA_HEADS)], axis=0)
    npair = DSA_HEADS // 2
    hq = DSA_HEADS * QB
    acc_ref[...] = jnp.zeros_like(acc_ref)
    p1_ref[...] = jnp.zeros_like(p1_ref)

    def qk(kt, dst_ref):
        off = pl.multiple_of(kt * KT, KT)
        b = bias_ref[pl.ds(off, KT), :]
        dst_ref[...] = _dot_nt(dk_ref[pl.ds(off, KT), :], qst) + jnp.concatenate([b] * DSA_HEADS, axis=1)

    def numer(src_ref, dst_ref, m):
        s = src_ref[...]
        m_new = jnp.maximum(m, jnp.max(s, axis=0, keepdims=True))
        dst_ref[...] = jnp.exp2(s - m_new).astype(dst_ref.dtype)
        return m_new, jnp.exp2(m - m_new)

    def pv(kt, src_ref, alpha):
        off = pl.multiple_of(kt * KT, KT)
        vt = dvt_ref[0:V_ROWS, pl.ds(off, KT)]
        acc_ref[...] = alpha * acc_ref[...] + _dot(vt, src_ref[...])

    def att_pair(i, c):
        m, alpha = c
        qk(2 * i + 1, s1_ref)
        pv(jnp.maximum(2 * i - 1, 0), p1_ref, alpha)
        m, alpha = numer(s0_ref, p0_ref, m)
        qk(jnp.minimum(2 * i + 2, last_tile), s0_ref)
        pv(2 * i, p0_ref, alpha)
        m, alpha = numer(s1_ref, p1_ref, m)
        return m, alpha

    qk(0, s0_ref)
    c0 = (jnp.full((1, hq), -jnp.inf, F32), jnp.ones((1, hq), F32))
    _, alpha = lax.fori_loop(0, nkp // 2, att_pair, c0)
    pv(last_tile, p1_ref, alpha)
    acc = acc_ref[...]
    o = acc[0:DSA_HEAD_DIM, :] / acc[DSA_HEAD_DIM:DSA_HEAD_DIM + 1, :]
    for j in range(npair):
        o_ref[:, j * LANES:(j + 1) * LANES] = jnp.concatenate(
            [o[:, (2 * j) * QB:(2 * j + 1) * QB], o[:, (2 * j + 1) * QB:(2 * j + 2) * QB]], axis=0).T


def _dsa(iq, iwt, dq, ik, dk, dvt):
    bsz, seq, _ = iq.shape
    topk = min(DSA_TOPK, seq // 4)
    grid = (bsz, seq // QB)
    qrow = lambda w: pl.BlockSpec((None, QB, w), lambda b, i: (b, i, 0))
    full = lambda r, c: pl.BlockSpec((None, r, c), lambda b, i: (b, 0, 0))
    return pl.pallas_call(
        functools.partial(_dsa_kernel, seq=seq, topk=topk), grid=grid,
        in_specs=[qrow(LANES), pl.BlockSpec((None, SUBLANES, QB), lambda b, i: (b, 0, i)), qrow(DSA_WIDTH),
                  full(seq, LANES), full(seq, LANES), full(LANES, seq)],
        out_specs=qrow(DSA_WIDTH),
        out_shape=jax.ShapeDtypeStruct((bsz, seq, DSA_WIDTH), F32),
        scratch_shapes=[pltpu.VMEM((seq, QB), jnp.int32),
                        pltpu.VMEM((seq // GROUP * SUBLANES, QB), jnp.int32),
                        pltpu.VMEM((seq // GROUP * SUBLANES, QB), jnp.int32),
                        pltpu.VMEM((seq, QB), F32),
                        pltpu.VMEM((V_ROWS, DSA_HEADS * QB), F32),
                        pltpu.VMEM((KT, IDX_HEADS * QB), F32), pltpu.VMEM((KT, IDX_HEADS * QB), F32),
                        pltpu.VMEM((KT, DSA_HEADS * QB), F32), pltpu.VMEM((KT, DSA_HEADS * QB), F32),
                        pltpu.VMEM((KT, DSA_HEADS * QB), _BF), pltpu.VMEM((KT, DSA_HEADS * QB), _BF)],
        compiler_params=_params(("parallel", "arbitrary")), name="dsa",
    )(iq, iwt, dq, ik, dk, dvt)


def _ret_kernel(rq_ref, rk_ref, rv_ref, rg_ref, qdec_ref, kdec_ref, dmask_ref, cdec_ref, blk_ref,
                hmq_ref, hmv_ref, ones_ref, ng_ref, o_ref, s_ref):
    @pl.when(pl.program_id(1) == 0)
    def _():
        s_ref[...] = jnp.zeros_like(s_ref)

    q, k, v = rq_ref[...], rk_ref[...], rv_ref[...]
    state = s_ref[...]
    y = _dot((q * qdec_ref[...]).astype(_BF), state.astype(_BF))
    kd = (k * kdec_ref[...]).T.astype(_BF)
    vb = v.astype(_BF)
    s_ref[...] = state * cdec_ref[...] + _dot(kd, vb) * blk_ref[...]
    kb = k.astype(_BF)
    for h in range(RET_HEADS):
        a = _dot_nt((q * hmq_ref[h:h + 1, :]).astype(_BF), kb) * dmask_ref[h]
        y = y + _dot(a.astype(_BF), (v * hmv_ref[h:h + 1, :]).astype(_BF))

    ones = ones_ref[...]

    def head_mean(z):
        hi = z.astype(_BF)
        lo = (z - hi.astype(F32)).astype(_BF)
        return (_dot(hi, ones) + _dot(lo, ones)) * (1.0 / RET_V_DIM)

    d = y - head_mean(y)
    yn = d * lax.rsqrt(head_mean(d * d) + EPS) * ng_ref[...]
    o_ref[...] = jax.nn.silu(rg_ref[...]) * yn


def _ret_consts():
    c = RET_C
    log_g = jnp.log1p(-jnp.exp2(-5.0 - jnp.arange(RET_HEADS, dtype=F32)))
    pos = jnp.arange(c, dtype=F32)
    diff = pos[:, None] - pos[None, :]
    dmask = jnp.where(diff >= 0, jnp.exp(log_g[:, None, None] * jnp.maximum(diff, 0.0)), 0.0)
    lane_q = jnp.arange(RET_QK_PAD)
    head_q = jnp.where((lane_q % LANES) < RET_HEADS * (RET_QK_DIM // 2), (lane_q % LANES) // (RET_QK_DIM // 2), -1)
    head_v = jnp.arange(RET_WIDTH) // RET_V_DIM
    hmq = (head_q[None, :] == jnp.arange(RET_HEADS)[:, None]).astype(F32)
    hmv = (head_v[None, :] == jnp.arange(RET_HEADS)[:, None]).astype(F32)
    lg_q = hmq.T @ log_g
    qdec = jnp.exp(lg_q[None, :] * (pos[:, None] + 1.0))
    kdec = jnp.exp(lg_q[None, :] * (c - 1.0 - pos[:, None]))
    blk = hmq.T @ hmv
    cdec = blk * jnp.exp(lg_q * c)[:, None]
    ones = (hmv.T @ hmv).astype(_BF)
    return qdec, kdec, dmask, cdec, blk, hmq, hmv, ones


def _retention(rq, rk, rv, rg, norm_g):
    bsz, seq, _ = rq.shape
    consts = _ret_consts()
    grid = (bsz, seq // RET_C)
    row = lambda w: pl.BlockSpec((None, RET_C, w), lambda b, i: (b, i, 0))
    ng = norm_g.reshape(1, RET_WIDTH)
    return pl.pallas_call(
        _ret_kernel, grid=grid,
        in_specs=[row(RET_QK_PAD), row(RET_QK_PAD), row(RET_WIDTH), row(RET_WIDTH)]
        + [_const_spec(a.shape) for a in consts] + [_const_spec(ng.shape)],
        out_specs=row(RET_WIDTH),
        out_shape=jax.ShapeDtypeStruct((bsz, seq, RET_WIDTH), F32),
        scratch_shapes=[pltpu.VMEM((RET_QK_PAD, RET_WIDTH), F32)],
        compiler_params=_params(("parallel", "arbitrary")), name="retention",
    )(rq, rk, rv, rg, *consts, ng)


def _merge_kernel(x_ref, g_ref, ya_ref, yb_ref, yc_ref, wg_ref, wa_ref, wb_ref, wc_ref, wo_ref, o_ref):
    x = x_ref[...]
    h = _rms(x, g_ref[...]).astype(_BF)
    merged = jnp.zeros(x.shape, F32)
    for n, (y_ref, w_ref) in enumerate(((ya_ref, wa_ref), (yb_ref, wb_ref), (yc_ref, wc_ref))):
        gate = jax.nn.sigmoid(_dot(h, wg_ref[:, n * D_MODEL:(n + 1) * D_MODEL]))
        merged = merged + gate * _dot(y_ref[...].astype(_BF), w_ref[...])
    o_ref[...] = x + _dot(merged.astype(_BF), wo_ref[...])


def _merge(x, g, ya, yb, yc, wg, wa, wb, wc, wo):
    bsz, seq, _ = x.shape
    grid = (bsz, seq // TM_OUT)
    row = lambda w: pl.BlockSpec((None, TM_OUT, w), lambda b, i: (b, i, 0))
    return pl.pallas_call(
        _merge_kernel, grid=grid,
        in_specs=[row(D_MODEL), _const_spec(g.shape), row(SSM_WIDTH), row(DSA_WIDTH), row(RET_WIDTH)]
        + [_const_spec(w.shape) for w in (wg, wa, wb, wc, wo)],
        out_specs=row(D_MODEL),
        out_shape=jax.ShapeDtypeStruct(x.shape, F32),
        compiler_params=_params(("parallel", "parallel")), name="merge",
    )(x, g, ya, yb, yc, wg, wa, wb, wc, wo)


def _mlp_kernel(x_ref, g_ref, w1_ref, w2_ref, fg_ref, o_ref, *, final_norm):
    x = x_ref[...]
    h = _rms(x, g_ref[...]).astype(_BF)
    acc = x
    for f in range(D_FF // FF_CHUNK):
        t = jnp.maximum(_dot(h, w1_ref[:, f * FF_CHUNK:(f + 1) * FF_CHUNK]), 0.0)
        acc = acc + _dot((t * t).astype(_BF), w2_ref[f * FF_CHUNK:(f + 1) * FF_CHUNK, :])
    o_ref[...] = _rms(acc, fg_ref[...]) if final_norm else acc


def _mlp(x, g, w1, w2, fg, final_norm):
    bsz, seq, _ = x.shape
    grid = (bsz, seq // TM_OUT)
    row = pl.BlockSpec((None, TM_OUT, D_MODEL), lambda b, i: (b, i, 0))
    return pl.pallas_call(
        functools.partial(_mlp_kernel, final_norm=final_norm), grid=grid,
        in_specs=[row, _const_spec(g.shape), _const_spec(w1.shape), _const_spec(w2.shape), _const_spec(fg.shape)],
        out_specs=row,
        out_shape=jax.ShapeDtypeStruct(x.shape, F32),
        compiler_params=_params(("parallel", "parallel")), name="mlp",
    )(x, g, w1, w2, fg)


def _rope_tables(positions):
    pos = positions.astype(F32)[..., None]
    lane = jnp.arange(LANES)

    def angles(rot_dim, theta):
        half = rot_dim // 2
        inv = jnp.exp(-math.log(theta) * jnp.arange(half, dtype=F32) * (2.0 / rot_dim))
        return pos * inv

    def roll_tables(head_dim, rot_dim, theta):
        half = rot_dim // 2
        ang = angles(rot_dim, theta)
        ln = lane % head_dim
        rot = ln < rot_dim
        a = jnp.take(ang, ln % half, axis=-1)
        cos = jnp.where(rot, jnp.cos(a), 1.0)
        sin = jnp.where(rot, jnp.sin(a), 0.0)
        sgn = jnp.stack([jnp.where(ln < half, -1.0, 0.0),
                         jnp.where((ln >= half) & rot, 1.0, 0.0)]).astype(F32)
        return cos, sin, sgn

    cd, sd, sgd = roll_tables(DSA_HEAD_DIM, DSA_HEAD_DIM // 4, ROPE_THETA)
    ci, si, sgi = roll_tables(IDX_DIM, IDX_DIM // 4, ROPE_THETA)
    half = RET_QK_DIM // 2
    ang = angles(RET_QK_DIM, RET_THETA)
    a = jnp.take(ang, lane % half, axis=-1)
    used = lane < RET_HEADS * half
    cr = jnp.where(used, jnp.cos(a), 1.0)
    sr = jnp.where(used, jnp.sin(a), 0.0)
    return cd, sd, ci, si, cr, sr, sgd, sgi


def _inproj_weights(w):
    sizes = (SSM_WIDTH, DSA_WIDTH, DSA_HEAD_DIM, DSA_HEAD_DIM, IDX_HEADS * IDX_DIM, IDX_DIM, IDX_HEADS,
             RET_HEADS * RET_QK_DIM, RET_HEADS * RET_QK_DIM, RET_WIDTH, RET_WIDTH, 3 * D_MODEL)
    parts, off = [], 0
    for n in sizes:
        parts.append(w[:, off:off + n])
        off += n
    wu, wdq, wdk, wdv, wiq, wik, wiw, wrq, wrk, wrv, wrg, wgt = parts
    half = RET_QK_DIM // 2

    def ret_split(a):
        a = a.reshape(D_MODEL, RET_HEADS, 2, half)
        pad = jnp.zeros((D_MODEL, LANES - RET_HEADS * half), a.dtype)
        return jnp.concatenate([a[:, :, 0].reshape(D_MODEL, -1), pad, a[:, :, 1].reshape(D_MODEL, -1), pad], axis=1)

    wiwt = jnp.concatenate([wiw.T, jnp.zeros((SUBLANES - IDX_HEADS, D_MODEL), w.dtype)], axis=0)
    wdvt = jnp.concatenate([wdv.T, jnp.zeros((LANES - DSA_HEAD_DIM, D_MODEL), w.dtype)], axis=0)
    ws = (wu, wdq, jnp.concatenate([wdk, wdk], axis=1), wdvt,
          wiq, jnp.concatenate([wik] * IDX_HEADS, axis=1), wiwt,
          ret_split(wrq), ret_split(wrk), wrv, wrg)
    return tuple(a.astype(_BF) for a in ws), wgt.astype(_BF)


def kernel(x, positions, norm1_g, w_in, ssm_lambda_re, ssm_lambda_im, ssm_log_step, ssm_b_re, ssm_b_im,
           ssm_c_re, ssm_c_im, ssm_d, ssm_glu_w, ssm_glu_b, ret_norm_g, w_proj_a, w_proj_b, w_proj_c,
           w_out, norm2_g, w_ff1, w_ff2, final_norm_g):
    depth = w_in.shape[0]
    tabs = _rope_tables(positions)
    fg = final_norm_g.reshape(1, D_MODEL)
    for l in range(depth):
        g1 = norm1_g[l].reshape(1, D_MODEL)
        ws, wgt = _inproj_weights(w_in[l])
        u, dq, dk, dvt, iq, ik, iwt, rq, rk, rv, rg = _inproj(x, g1, tabs, ws)

        s5p = _s5_params(ssm_lambda_re[l], ssm_lambda_im[l], ssm_log_step[l], ssm_b_re[l], ssm_b_im[l],
                         ssm_c_re[l], ssm_c_im[l], ssm_d[l], ssm_glu_w[l], ssm_glu_b[l])
        ya = _s5(u, s5p)
        yb = _dsa(iq, iwt, dq, ik, dk, dvt)
        yc = _retention(rq, rk, rv, rg, ret_norm_g[l])

        x = _merge(x, g1, ya, yb, yc, wgt, w_proj_a[l].astype(_BF), w_proj_b[l].astype(_BF),
                   w_proj_c[l].astype(_BF), w_out[l].astype(_BF))
        x = _mlp(x, norm2_g[l].reshape(1, D_MODEL), w_ff1[l].astype(_BF), w_ff2[l].astype(_BF), fg,
                 final_norm=(l == depth - 1))
    return x
```

```python
import functools
import math

import jax
import jax.numpy as jnp
from jax import lax
from jax.experimental import pallas as pl
from jax.experimental.pallas import tpu as pltpu

F32 = jnp.float32
_BF = jnp.bfloat16
LANES = 128
SUBLANES = 8
VMEM_LIMIT = 56 * 1024 * 1024

D_MODEL = 1024
CHUNK = 64
EPS = 1e-6
NEG_INF_SCORE = -1e30
INT_MIN = -2 ** 31
GROUP = 32 * SUBLANES
GROUPS_PER_STEP = 4
V_ROWS = 80

SSM_WIDTH = 256
SSM_GROUP = 16
SSM_GROUPS = 16
SSM_STATE = 64
NSTATE = SSM_GROUPS * SSM_STATE

DSA_HEADS = 6
DSA_HEAD_DIM = 64
DSA_WIDTH = 384
IDX_HEADS = 4
IDX_DIM = 32
DSA_TOPK = 256
ROPE_THETA = 500000.0

RET_HEADS = 4
RET_QK_DIM = 48
RET_V_DIM = 96
RET_WIDTH = 384
RET_THETA = 10000.0
RET_QK_PAD = 256

D_FF = 4096

TM_IN = 256
S5_TILE = 512
S5_SEG = S5_TILE // SUBLANES
QB = 128
KT = 512
RET_C = 128
TM_OUT = 512
FF_CHUNK = 1024


def _rms(x, g):
    return x * lax.rsqrt(jnp.mean(x * x, axis=-1, keepdims=True) + EPS) * g


def _dot(a, b):
    return jnp.dot(a, b, preferred_element_type=F32)


def _dot_nt(a, b):
    return lax.dot_general(a, b, (((1,), (1,)), ((), ())), preferred_element_type=F32)


def _to_ukey(v):
    bits = lax.bitcast_convert_type(v, jnp.int32)
    return lax.bitcast_convert_type(bits ^ ((bits >> 31) | INT_MIN), jnp.uint32)


def _transpose32(a):
    a = list(a)
    j, m = 16, 0x0000FFFF
    while j:
        k = 0
        while k < 32:
            t = (a[k] ^ (a[k + j] >> j)) & m
            a[k] = a[k] ^ t
            a[k + j] = a[k + j] ^ (t << j)
            k = (k + j + 1) & ~j
        j >>= 1
        m = (m ^ (m << j)) & 0xFFFFFFFF
    return a


def _const_spec(shape):
    nd = len(shape)
    return pl.BlockSpec(shape, lambda *_: (0,) * nd)


def _params(sem):
    return pltpu.CompilerParams(dimension_semantics=sem, vmem_limit_bytes=VMEM_LIMIT)


def _rope_roll(z, cos, sin, sgn, shift):
    outs = []
    for c in range(z.shape[1] // LANES):
        zc = z[:, c * LANES:(c + 1) * LANES]
        rot = (pltpu.roll(zc, LANES - shift, 1) * sgn[0:1, :]
               + pltpu.roll(zc, shift, 1) * sgn[1:2, :])
        outs.append(zc * cos + rot * sin)
    return outs[0] if len(outs) == 1 else jnp.concatenate(outs, axis=1)


def _inproj_kernel(x_ref, g_ref, cd_ref, sd_ref, ci_ref, si_ref, cr_ref, sr_ref, sgd_ref, sgi_ref,
                   wu_ref, wdq_ref, wdk_ref, wdvt_ref, wiq_ref, wik_ref, wiwt_ref,
                   wrq_ref, wrk_ref, wrv_ref, wrg_ref,
                   u_ref, dq_ref, dk_ref, dvt_ref, iq_ref, ik_ref, iwt_ref,
                   rq_ref, rk_ref, rv_ref, rg_ref):
    h = _rms(x_ref[...], g_ref[...]).astype(_BF)
    u_ref[...] = _dot(h, wu_ref[...])

    cd, sd, sgd = cd_ref[...], sd_ref[...], sgd_ref[...]
    dq = _rope_roll(_dot(h, wdq_ref[...]), cd, sd, sgd, 8)
    dq_ref[...] = (dq * (DSA_HEAD_DIM ** -0.5 * math.log2(math.e))).astype(dq_ref.dtype)
    dk_ref[...] = _rope_roll(_dot(h, wdk_ref[...]), cd, sd, sgd, 8).astype(dk_ref.dtype)
    vt = _dot(h, wdvt_ref[...]).T
    ones_row = lax.broadcasted_iota(jnp.int32, vt.shape, 0) == DSA_HEAD_DIM
    dvt_ref[...] = jnp.where(ones_row, 1.0, vt).astype(dvt_ref.dtype)

    ci, si, sgi = ci_ref[...], si_ref[...], sgi_ref[...]
    iq_ref[...] = _rope_roll(_dot(h, wiq_ref[...]), ci, si, sgi, 4) * (IDX_DIM ** -0.5)
    ik_ref[...] = _rope_roll(_dot(h, wik_ref[...]), ci, si, sgi, 4).astype(ik_ref.dtype)
    iwt_ref[...] = _dot(h, wiwt_ref[...]).T[:SUBLANES, :] * (IDX_HEADS ** -0.5)

    cr, sr = cr_ref[...], sr_ref[...]

    def rope_split(z):
        x1, x2 = z[:, :LANES], z[:, LANES:]
        return jnp.concatenate([x1 * cr - x2 * sr, x2 * cr + x1 * sr], axis=1)

    rq_ref[...] = rope_split(_dot(h, wrq_ref[...]))
    rk_ref[...] = rope_split(_dot(h, wrk_ref[...])) * (RET_QK_DIM ** -0.5)
    rv_ref[...] = _dot(h, wrv_ref[...])
    rg_ref[...] = _dot(h, wrg_ref[...])


def _inproj(x, g, tabs, ws):
    bsz, seq, _ = x.shape
    tm = TM_IN
    grid = (bsz, seq // tm)
    row = lambda w: pl.BlockSpec((None, tm, w), lambda b, i: (b, i, 0))
    colT = lambda r: pl.BlockSpec((None, r, tm), lambda b, i: (b, 0, i))
    in_specs = ([row(D_MODEL), _const_spec(g.shape)] + [row(LANES)] * 6
                + [_const_spec(t.shape) for t in tabs[6:]] + [_const_spec(w.shape) for w in ws])
    out_shape = [
        jax.ShapeDtypeStruct((bsz, seq, SSM_WIDTH), F32),
        jax.ShapeDtypeStruct((bsz, seq, DSA_WIDTH), _BF),
        jax.ShapeDtypeStruct((bsz, seq, LANES), _BF),
        jax.ShapeDtypeStruct((bsz, LANES, seq), _BF),
        jax.ShapeDtypeStruct((bsz, seq, LANES), F32),
        jax.ShapeDtypeStruct((bsz, seq, LANES), _BF),
        jax.ShapeDtypeStruct((bsz, SUBLANES, seq), F32),
        jax.ShapeDtypeStruct((bsz, seq, RET_QK_PAD), F32),
        jax.ShapeDtypeStruct((bsz, seq, RET_QK_PAD), F32),
        jax.ShapeDtypeStruct((bsz, seq, RET_WIDTH), F32),
        jax.ShapeDtypeStruct((bsz, seq, RET_WIDTH), F32),
    ]
    out_specs = [row(SSM_WIDTH), row(DSA_WIDTH), row(LANES), colT(LANES), row(LANES), row(LANES),
                 colT(SUBLANES), row(RET_QK_PAD), row(RET_QK_PAD), row(RET_WIDTH), row(RET_WIDTH)]
    return pl.pallas_call(
        _inproj_kernel, grid=grid, in_specs=in_specs, out_specs=out_specs, out_shape=out_shape,
        compiler_params=_params(("parallel", "parallel")), name="inproj",
    )(x, g, *tabs, *ws)


def _s5_kernel(u_ref, bbd_ref, cbd_ref, lam_ref, lam64_ref, pow_ref, d_ref, gw_ref, gb_ref,
               o_ref, st_ref, xm_ref, carry_ref, up_ref, yp_ref):
    n = NSTATE

    @pl.when(pl.program_id(1) == 0)
    def _():
        carry_ref[...] = jnp.zeros_like(carry_ref)

    nhalf = SSM_WIDTH // LANES
    for i in range(SUBLANES):
        for c in range(nhalf):
            up_ref[c, pl.ds(i, S5_SEG, stride=SUBLANES), :] = u_ref[i * S5_SEG:(i + 1) * S5_SEG,
                                                                    c * LANES:(c + 1) * LANES]
    u = jnp.concatenate([up_ref[c] for c in range(nhalf)], axis=1)
    st_ref[...] = _dot(u.astype(_BF), bbd_ref[...])

    lr = jnp.broadcast_to(lam_ref[:, :n], (SUBLANES, n))
    li = jnp.broadcast_to(lam_ref[:, n:], (SUBLANES, n))

    def step(j, c):
        xr, xi = c
        off = pl.multiple_of(j * SUBLANES, SUBLANES)
        ar = st_ref[pl.ds(off, SUBLANES), :n]
        ai = st_ref[pl.ds(off, SUBLANES), n:]
        nr = lr * xr - li * xi + ar
        ni = lr * xi + li * xr + ai
        st_ref[pl.ds(off, SUBLANES), :n] = nr
        st_ref[pl.ds(off, SUBLANES), n:] = ni
        return nr, ni

    zero = jnp.zeros((SUBLANES, n), F32)
    xr, xi = lax.fori_loop(0, S5_SEG, step, (zero, zero))

    l64r, l64i = lam64_ref[:, :n], lam64_ref[:, n:]
    cr, ci = carry_ref[:, :n], carry_ref[:, n:]
    for i in range(SUBLANES):
        xm_ref[i:i + 1, :n] = cr
        xm_ref[i:i + 1, n:] = ci
        er, ei = xr[i:i + 1, :], xi[i:i + 1, :]
        cr, ci = er + l64r * cr - l64i * ci, ei + l64r * ci + l64i * cr
    carry_ref[:, :n] = cr
    carry_ref[:, n:] = ci

    hr, hi = xm_ref[:, :n], xm_ref[:, n:]

    def fix(j, _):
        off = pl.multiple_of(j * SUBLANES, SUBLANES)
        pr = pow_ref[pl.ds(j, 1), :n]
        pi = pow_ref[pl.ds(j, 1), n:]
        st_ref[pl.ds(off, SUBLANES), :n] = st_ref[pl.ds(off, SUBLANES), :n] + pr * hr - pi * hi
        st_ref[pl.ds(off, SUBLANES), n:] = st_ref[pl.ds(off, SUBLANES), n:] + pr * hi + pi * hr
        return 0

    lax.fori_loop(0, S5_SEG, fix, 0)

    y = d_ref[...] * u
    kc = 512
    for k in range(2 * n // kc):
        y = y + _dot(st_ref[:, k * kc:(k + 1) * kc].astype(_BF), cbd_ref[k * kc:(k + 1) * kc, :])
    y = jax.nn.gelu(y)
    y = y * jax.nn.sigmoid(_dot(y.astype(_BF), gw_ref[...]) + gb_ref[...])
    for c in range(nhalf):
        yp_ref[c] = y[:, c * LANES:(c + 1) * LANES]
    for i in range(SUBLANES):
        for c in range(nhalf):
            o_ref[i * S5_SEG:(i + 1) * S5_SEG, c * LANES:(c + 1) * LANES] = yp_ref[
                c, pl.ds(i, S5_SEG, stride=SUBLANES), :]


def _s5(u_perm, prm):
    bsz, seq, _ = u_perm.shape
    grid = (bsz, seq // S5_TILE)
    row = pl.BlockSpec((None, S5_TILE, SSM_WIDTH), lambda b, i: (b, i, 0))
    return pl.pallas_call(
        _s5_kernel, grid=grid,
        in_specs=[row] + [_const_spec(p.shape) for p in prm],
        out_specs=row,
        out_shape=jax.ShapeDtypeStruct((bsz, seq, SSM_WIDTH), F32),
        scratch_shapes=[pltpu.VMEM((S5_TILE, 2 * NSTATE), F32),
                        pltpu.VMEM((SUBLANES, 2 * NSTATE), F32),
                        pltpu.VMEM((1, 2 * NSTATE), F32),
                        pltpu.VMEM((SSM_WIDTH // LANES, S5_TILE, LANES), F32),
                        pltpu.VMEM((SSM_WIDTH // LANES, S5_TILE, LANES), F32)],
        compiler_params=_params(("arbitrary", "arbitrary")), name="s5",
    )(u_perm, *prm)


def _s5_params(lam_re, lam_im, log_step, b_re, b_im, c_re, c_im, d_skip, glu_w, glu_b):
    step = jnp.exp(log_step.astype(F32))[:, None]
    ere, eim = lam_re * step, lam_im * step
    mag = jnp.exp(ere)
    lb_re, lb_im = mag * jnp.cos(eim), mag * jnp.sin(eim)
    den = lam_re * lam_re + lam_im * lam_im
    f_re = ((lb_re - 1.0) * lam_re + lb_im * lam_im) / den
    f_im = (lb_im * lam_re - (lb_re - 1.0) * lam_im) / den
    bb_re = f_re[..., None] * b_re - f_im[..., None] * b_im
    bb_im = f_re[..., None] * b_im + f_im[..., None] * b_re
    eye = jnp.eye(SSM_GROUPS, dtype=F32)
    bbd = jnp.concatenate(
        [jnp.einsum('gpc,gh->gchp', bb, eye).reshape(SSM_WIDTH, NSTATE) for bb in (bb_re, bb_im)], axis=1)
    cbd = jnp.concatenate(
        [jnp.einsum('gcp,gh->hpgc', cc, eye).reshape(NSTATE, SSM_WIDTH) for cc in (c_re, -c_im)], axis=0)
    flat = lambda a: a.reshape(1, NSTATE)
    lam = jnp.concatenate([flat(lb_re), flat(lb_im)], axis=1)

    def power(k):
        k = jnp.asarray(k, F32).reshape(-1, 1)
        m = jnp.exp(k * flat(ere))
        return jnp.concatenate([m * jnp.cos(k * flat(eim)), m * jnp.sin(k * flat(eim))], axis=1)

    return (bbd.astype(_BF), cbd.astype(_BF), lam, power(float(S5_SEG)),
            power(jnp.arange(1, S5_SEG + 1)), d_skip.reshape(1, SSM_WIDTH),
            glu_w.astype(_BF), glu_b.reshape(1, SSM_WIDTH))


def _dsa_kernel(iq_ref, iwt_ref, dq_ref, ik_ref, dk_ref, dvt_ref, o_ref,
                planes_ref, cand_ref, pick_ref, bias_ref, acc_ref,
                lg0_ref, lg1_ref, s0_ref, s1_ref, p0_ref, p1_ref, *, seq, topk):
    qi = pl.program_id(1)
    nk = (qi * QB) // KT + 1
    lane = lax.broadcasted_iota(jnp.int32, (1, QB), 1)
    vis_end = qi * QB + jnp.where(lane < CHUNK, CHUNK, 2 * CHUNK)

    iq = iq_ref[...]
    col = lax.broadcasted_iota(jnp.int32, (QB, LANES), 1)
    qs = jnp.concatenate([jnp.where(col // IDX_DIM == h, iq, 0.0) for h in range(IDX_HEADS)],
                         axis=0).astype(_BF)
    w = iwt_ref[...]

    nkp = nk + lax.rem(nk, 2)
    last_tile = nkp - 1

    def logits(kt, dst_ref):
        off = pl.multiple_of(kt * KT, KT)
        dst_ref[...] = _dot_nt(ik_ref[pl.ds(off, KT), :], qs)

    def score_tile(kt, src_ref):
        lg = src_ref[...]
        sc = jnp.zeros((KT, QB), F32)
        for h in range(IDX_HEADS):
            sc = sc + jnp.maximum(lg[:, h * QB:(h + 1) * QB], 0.0) * w[h:h + 1, :]
        sc = jnp.where(sc == 0.0, 0.0, sc)
        ukey = _to_ukey(sc)
        for g in range(KT // GROUP):
            words = _transpose32([ukey[g * GROUP + m * SUBLANES:g * GROUP + (m + 1) * SUBLANES, :]
                                  for m in range(32)])
            base = pl.multiple_of((kt * (KT // GROUP) + g) * GROUP, GROUP)
            planes_ref[pl.ds(base, GROUP), :] = lax.bitcast_convert_type(jnp.concatenate(words, axis=0), jnp.int32)

    def score_pair(i, c):
        logits(2 * i + 1, lg1_ref)
        score_tile(2 * i, lg0_ref)
        logits(jnp.minimum(2 * i + 2, last_tile), lg0_ref)
        score_tile(2 * i + 1, lg1_ref)
        return c

    logits(0, lg0_ref)
    lax.fori_loop(0, nkp // 2, score_pair, 0)

    ngrp = nkp * (KT // GROUP)
    sub = lax.broadcasted_iota(jnp.int32, (SUBLANES, QB), 0)
    zero8 = jnp.zeros((SUBLANES, QB), jnp.int32)

    def rows_below(limit, g):
        nm = lax.shift_right_arithmetic(limit - g * GROUP + (SUBLANES - 1), 3)
        top = lax.shift_right_arithmetic(jnp.full((SUBLANES, QB), INT_MIN, jnp.int32),
                                         jnp.clip(nm, 1, 32) - 1)
        return jnp.where(nm <= 0, 0, top)

    def word(ref, g):
        return ref[pl.ds(pl.multiple_of(g * SUBLANES, SUBLANES), SUBLANES), :]

    def put_word(ref, g, v):
        ref[pl.ds(pl.multiple_of(g * SUBLANES, SUBLANES), SUBLANES), :] = v

    def plane(g, i):
        return planes_ref[pl.ds(pl.multiple_of(g * GROUP + i * SUBLANES, SUBLANES), SUBLANES), :]

    def sweep(step):
        def body(s, accs):
            return tuple(a + step(s * GROUPS_PER_STEP + t) for t, a in enumerate(accs))
        accs = lax.fori_loop(0, ngrp // GROUPS_PER_STEP, body, (zero8,) * GROUPS_PER_STEP)
        return functools.reduce(lambda a, b: a + b, accs).sum(axis=0, keepdims=True)

    def init_words(g):
        put_word(cand_ref, g, rows_below(vis_end - sub, g))
        put_word(pick_ref, g, zero8)
        return zero8

    sweep(init_words)

    def count_ones(i):
        return sweep(lambda g: lax.population_count(word(cand_ref, g) & plane(g, i)))

    def decide(c1, rem):
        take = c1 >= rem
        return jnp.where(take, 1, 0), jnp.where(take, rem, rem - c1)

    def narrow(g, i, take):
        take8 = jnp.broadcast_to(take, (SUBLANES, QB)) != 0
        c = word(cand_ref, g)
        ones = c & plane(g, i)
        put_word(pick_ref, g, word(pick_ref, g) | jnp.where(take8, 0, ones))
        c = jnp.where(take8, ones, c ^ ones)
        put_word(cand_ref, g, c)
        return c

    def radix_step(i, carry):
        take8, rem = carry

        def step(g):
            return lax.population_count(narrow(g, i - 1, take8) & plane(g, i))
        return decide(sweep(step), rem)

    take8, rem = decide(count_ones(0), jnp.full((1, QB), topk, jnp.int32))
    take8, rem = lax.fori_loop(1, 32, radix_step, (take8, rem))

    def last_narrow(g):
        narrow(g, 31, take8)
        return zero8

    sweep(last_narrow)

    def tie_step(i, edge):
        trial = edge + jnp.left_shift(jnp.int32(1), (seq.bit_length() - 1) - i)
        limit = trial - sub
        below = sweep(lambda g: lax.population_count(word(cand_ref, g) & rows_below(limit, g)))
        return jnp.where(below < rem, trial, edge)

    edge = lax.fori_loop(0, seq.bit_length(), tie_step, jnp.zeros((1, QB), jnp.int32))
    keep_limit = edge + 1 - sub

    def bias_group(g):
        chosen = word(pick_ref, g) | (word(cand_ref, g) & rows_below(keep_limit, g))
        base = pl.multiple_of(g * GROUP, GROUP)
        bias_ref[pl.ds(base, GROUP), :] = jnp.concatenate(
            [jnp.where(lax.shift_left(chosen, jnp.full_like(chosen, m)) < 0, 0.0, NEG_INF_SCORE)
             for m in range(32)], axis=0)
        return zero8

    sweep(bias_group)

    q = dq_ref[...]
    half = (col // DSA_HEAD_DIM)
    qst = jnp.concatenate(
        [jnp.where(half == (h % 2), q[:, (h // 2) * LANES:(h // 2 + 1) * LANES], jnp.zeros((), q.dtype))
         for h in range(DSA_HEADS)], axis=0)
    npair = DSA_HEADS // 2
    hq = DSA_HEADS * QB
    acc_ref[...] = jnp.zeros_like(acc_ref)
    p1_ref[...] = jnp.zeros_like(p1_ref)

    def qk(kt, dst_ref):
        off = pl.multiple_of(kt * KT, KT)
        b = bias_ref[pl.ds(off, KT), :]
        dst_ref[...] = _dot_nt(dk_ref[pl.ds(off, KT), :], qst) + jnp.concatenate([b] * DSA_HEADS, axis=1)

    def numer(src_ref, dst_ref, m):
        s = src_ref[...]
        m_new = jnp.maximum(m, jnp.max(s, axis=0, keepdims=True))
        dst_ref[...] = jnp.exp2(s - m_new).astype(dst_ref.dtype)
        return m_new, jnp.exp2(m - m_new)

    def pv(kt, src_ref, alpha):
        off = pl.multiple_of(kt * KT, KT)
        vt = dvt_ref[0:V_ROWS, pl.ds(off, KT)]
        acc_ref[...] = alpha * acc_ref[...] + _dot(vt, src_ref[...])

    def att_pair(i, c):
        m, alpha = c
        qk(2 * i + 1, s1_ref)
        pv(jnp.maximum(2 * i - 1, 0), p1_ref, alpha)
        m, alpha = numer(s0_ref, p0_ref, m)
        qk(jnp.minimum(2 * i + 2, last_tile), s0_ref)
        pv(2 * i, p0_ref, alpha)
        m, alpha = numer(s1_ref, p1_ref, m)
        return m, alpha

    qk(0, s0_ref)
    c0 = (jnp.full((1, hq), -jnp.inf, F32), jnp.ones((1, hq), F32))
    _, alpha = lax.fori_loop(0, nkp // 2, att_pair, c0)
    pv(last_tile, p1_ref, alpha)
    acc = acc_ref[...]
    o = acc[0:DSA_HEAD_DIM, :] / acc[DSA_HEAD_DIM:DSA_HEAD_DIM + 1, :]
    for j in range(npair):
        o_ref[:, j * LANES:(j + 1) * LANES] = jnp.concatenate(
            [o[:, (2 * j) * QB:(2 * j + 1) * QB], o[:, (2 * j + 1) * QB:(2 * j + 2) * QB]], axis=0).T


def _dsa(iq, iwt, dq, ik, dk, dvt):
    bsz, seq, _ = iq.shape
    topk = min(DSA_TOPK, seq // 4)
    grid = (bsz, seq // QB)
    qrow = lambda w: pl.BlockSpec((None, QB, w), lambda b, i: (b, i, 0))
    full = lambda r, c: pl.BlockSpec((None, r, c), lambda b, i: (b, 0, 0))
    return pl.pallas_call(
        functools.partial(_dsa_kernel, seq=seq, topk=topk), grid=grid,
        in_specs=[qrow(LANES), pl.BlockSpec((None, SUBLANES, QB), lambda b, i: (b, 0, i)), qrow(DSA_WIDTH),
                  full(seq, LANES), full(seq, LANES), full(LANES, seq)],
        out_specs=qrow(DSA_WIDTH),
        out_shape=jax.ShapeDtypeStruct((bsz, seq, DSA_WIDTH), F32),
        scratch_shapes=[pltpu.VMEM((seq, QB), jnp.int32),
                        pltpu.VMEM((seq // GROUP * SUBLANES, QB), jnp.int32),
                        pltpu.VMEM((seq // GROUP * SUBLANES, QB), jnp.int32),
                        pltpu.VMEM((seq, QB), F32),
                        pltpu.VMEM((V_ROWS, DSA_HEADS * QB), F32),
                        pltpu.VMEM((KT, IDX_HEADS * QB), F32), pltpu.VMEM((KT, IDX_HEADS * QB), F32),
                        pltpu.VMEM((KT, DSA_HEADS * QB), F32), pltpu.VMEM((KT, DSA_HEADS * QB), F32),
                        pltpu.VMEM((KT, DSA_HEADS * QB), _BF), pltpu.VMEM((KT, DSA_HEADS * QB), _BF)],
        compiler_params=_params(("parallel", "arbitrary")), name="dsa",
    )(iq, iwt, dq, ik, dk, dvt)


def _ret_kernel(rq_ref, rk_ref, rv_ref, rg_ref, qdec_ref, kdec_ref, dmask_ref, cdec_ref, blk_ref,
                hmq_ref, hmv_ref, ones_ref, ng_ref, o_ref, s_ref):
    @pl.when(pl.program_id(0) == 0)
    def _():
        s_ref[...] = jnp.zeros_like(s_ref)

    ones = ones_ref[...]

    def head_mean(z):
        return _dot(z.astype(_BF), ones) * (1.0 / RET_V_DIM)

    for b in range(rq_ref.shape[0]):
        q, k, v = rq_ref[b], rk_ref[b], rv_ref[b]
        state = s_ref[b]
        y = _dot((q * qdec_ref[...]).astype(_BF), state.astype(_BF))
        kd = (k * kdec_ref[...]).T.astype(_BF)
        vb = v.astype(_BF)
        s_ref[b] = state * cdec_ref[...] + _dot(kd, vb) * blk_ref[...]
        kb = k.astype(_BF)
        a = jnp.concatenate([(_dot_nt((q * hmq_ref[h:h + 1, :]).astype(_BF), kb) * dmask_ref[h]).astype(_BF)
                             for h in range(RET_HEADS)], axis=1)
        vh = jnp.concatenate([(v * hmv_ref[h:h + 1, :]).astype(_BF) for h in range(RET_HEADS)], axis=0)
        y = y + _dot(a, vh)
        d = y - head_mean(y)
        yn = d * lax.rsqrt(head_mean(d * d) + EPS) * ng_ref[...]
        o_ref[b] = jax.nn.silu(rg_ref[b]) * yn


def _ret_consts():
    c = RET_C
    log_g = jnp.log1p(-jnp.exp2(-5.0 - jnp.arange(RET_HEADS, dtype=F32)))
    pos = jnp.arange(c, dtype=F32)
    diff = pos[:, None] - pos[None, :]
    dmask = jnp.where(diff >= 0, jnp.exp(log_g[:, None, None] * jnp.maximum(diff, 0.0)), 0.0)
    lane_q = jnp.arange(RET_QK_PAD)
    head_q = jnp.where((lane_q % LANES) < RET_HEADS * (RET_QK_DIM // 2), (lane_q % LANES) // (RET_QK_DIM // 2), -1)
    head_v = jnp.arange(RET_WIDTH) // RET_V_DIM
    hmq = (head_q[None, :] == jnp.arange(RET_HEADS)[:, None]).astype(F32)
    hmv = (head_v[None, :] == jnp.arange(RET_HEADS)[:, None]).astype(F32)
    lg_q = hmq.T @ log_g
    qdec = jnp.exp(lg_q[None, :] * (pos[:, None] + 1.0))
    kdec = jnp.exp(lg_q[None, :] * (c - 1.0 - pos[:, None]))
    blk = hmq.T @ hmv
    cdec = blk * jnp.exp(lg_q * c)[:, None]
    ones = (hmv.T @ hmv).astype(_BF)
    return qdec, kdec, dmask, cdec, blk, hmq, hmv, ones


def _retention(rq, rk, rv, rg, norm_g):
    bsz, seq, _ = rq.shape
    consts = _ret_consts()
    grid = (seq // RET_C,)
    row = lambda w: pl.BlockSpec((bsz, RET_C, w), lambda i: (0, i, 0))
    ng = norm_g.reshape(1, RET_WIDTH)
    return pl.pallas_call(
        _ret_kernel, grid=grid,
        in_specs=[row(RET_QK_PAD), row(RET_QK_PAD), row(RET_WIDTH), row(RET_WIDTH)]
        + [_const_spec(a.shape) for a in consts] + [_const_spec(ng.shape)],
        out_specs=row(RET_WIDTH),
        out_shape=jax.ShapeDtypeStruct((bsz, seq, RET_WIDTH), F32),
        scratch_shapes=[pltpu.VMEM((bsz, RET_QK_PAD, RET_WIDTH), F32)],
        compiler_params=_params(("arbitrary",)), name="retention",
    )(rq, rk, rv, rg, *consts, ng)


def _merge_kernel(x_ref, g_ref, ya_ref, yb_ref, yc_ref, wg_ref, wa_ref, wb_ref, wc_ref, wo_ref, o_ref):
    x = x_ref[...]
    h = _rms(x, g_ref[...]).astype(_BF)
    merged = jnp.zeros(x.shape, F32)
    for n, (y_ref, w_ref) in enumerate(((ya_ref, wa_ref), (yb_ref, wb_ref), (yc_ref, wc_ref))):
        gate = jax.nn.sigmoid(_dot(h, wg_ref[:, n * D_MODEL:(n + 1) * D_MODEL]))
        merged = merged + gate * _dot(y_ref[...].astype(_BF), w_ref[...])
    o_ref[...] = x + _dot(merged.astype(_BF), wo_ref[...])


def _merge(x, g, ya, yb, yc, wg, wa, wb, wc, wo):
    bsz, seq, _ = x.shape
    grid = (bsz, seq // TM_OUT)
    row = lambda w: pl.BlockSpec((None, TM_OUT, w), lambda b, i: (b, i, 0))
    return pl.pallas_call(
        _merge_kernel, grid=grid,
        in_specs=[row(D_MODEL), _const_spec(g.shape), row(SSM_WIDTH), row(DSA_WIDTH), row(RET_WIDTH)]
        + [_const_spec(w.shape) for w in (wg, wa, wb, wc, wo)],
        out_specs=row(D_MODEL),
        out_shape=jax.ShapeDtypeStruct(x.shape, F32),
        compiler_params=_params(("parallel", "parallel")), name="merge",
    )(x, g, ya, yb, yc, wg, wa, wb, wc, wo)


def _mlp_kernel(x_ref, g_ref, w1_ref, w2_ref, fg_ref, o_ref, *, final_norm):
    x = x_ref[...]
    h = _rms(x, g_ref[...]).astype(_BF)
    acc = x
    for f in range(D_FF // FF_CHUNK):
        t = jnp.maximum(_dot(h, w1_ref[:, f * FF_CHUNK:(f + 1) * FF_CHUNK]), 0.0)
        acc = acc + _dot((t * t).astype(_BF), w2_ref[f * FF_CHUNK:(f + 1) * FF_CHUNK, :])
    o_ref[...] = _rms(acc, fg_ref[...]) if final_norm else acc


def _mlp(x, g, w1, w2, fg, final_norm):
    bsz, seq, _ = x.shape
    grid = (bsz, seq // TM_OUT)
    row = pl.BlockSpec((None, TM_OUT, D_MODEL), lambda b, i: (b, i, 0))
    return pl.pallas_call(
        functools.partial(_mlp_kernel, final_norm=final_norm), grid=grid,
        in_specs=[row, _const_spec(g.shape), _const_spec(w1.shape), _const_spec(w2.shape), _const_spec(fg.shape)],
        out_specs=row,
        out_shape=jax.ShapeDtypeStruct(x.shape, F32),
        compiler_params=_params(("parallel", "parallel")), name="mlp",
    )(x, g, w1, w2, fg)


def _rope_tables(positions):
    pos = positions.astype(F32)[..., None]
    lane = jnp.arange(LANES)

    def angles(rot_dim, theta):
        half = rot_dim // 2
        inv = jnp.exp(-math.log(theta) * jnp.arange(half, dtype=F32) * (2.0 / rot_dim))
        return pos * inv

    def roll_tables(head_dim, rot_dim, theta):
        half = rot_dim // 2
        ang = angles(rot_dim, theta)
        ln = lane % head_dim
        rot = ln < rot_dim
        a = jnp.take(ang, ln % half, axis=-1)
        cos = jnp.where(rot, jnp.cos(a), 1.0)
        sin = jnp.where(rot, jnp.sin(a), 0.0)
        sgn = jnp.stack([jnp.where(ln < half, -1.0, 0.0),
                         jnp.where((ln >= half) & rot, 1.0, 0.0)]).astype(F32)
        return cos, sin, sgn

    cd, sd, sgd = roll_tables(DSA_HEAD_DIM, DSA_HEAD_DIM // 4, ROPE_THETA)
    ci, si, sgi = roll_tables(IDX_DIM, IDX_DIM // 4, ROPE_THETA)
    half = RET_QK_DIM // 2
    ang = angles(RET_QK_DIM, RET_THETA)
    a = jnp.take(ang, lane % half, axis=-1)
    used = lane < RET_HEADS * half
    cr = jnp.where(used, jnp.cos(a), 1.0)
    sr = jnp.where(used, jnp.sin(a), 0.0)
    return cd, sd, ci, si, cr, sr, sgd, sgi


def _inproj_weights(w):
    sizes = (SSM_WIDTH, DSA_WIDTH, DSA_HEAD_DIM, DSA_HEAD_DIM, IDX_HEADS * IDX_DIM, IDX_DIM, IDX_HEADS,
             RET_HEADS * RET_QK_DIM, RET_HEADS * RET_QK_DIM, RET_WIDTH, RET_WIDTH, 3 * D_MODEL)
    parts, off = [], 0
    for n in sizes:
        parts.append(w[:, off:off + n])
        off += n
    wu, wdq, wdk, wdv, wiq, wik, wiw, wrq, wrk, wrv, wrg, wgt = parts
    half = RET_QK_DIM // 2

    def ret_split(a):
        a = a.reshape(D_MODEL, RET_HEADS, 2, half)
        pad = jnp.zeros((D_MODEL, LANES - RET_HEADS * half), a.dtype)
        return jnp.concatenate([a[:, :, 0].reshape(D_MODEL, -1), pad, a[:, :, 1].reshape(D_MODEL, -1), pad], axis=1)

    wiwt = jnp.concatenate([wiw, jnp.zeros((D_MODEL, LANES - IDX_HEADS), w.dtype)], axis=1)
    wdvt = jnp.concatenate([wdv, jnp.zeros((D_MODEL, LANES - DSA_HEAD_DIM), w.dtype)], axis=1)
    ws = (wu, wdq, jnp.concatenate([wdk, wdk], axis=1), wdvt,
          wiq, jnp.concatenate([wik] * IDX_HEADS, axis=1), wiwt,
          ret_split(wrq), ret_split(wrk), wrv, wrg)
    return tuple(a.astype(_BF) for a in ws), wgt.astype(_BF)


def kernel(x, positions, norm1_g, w_in, ssm_lambda_re, ssm_lambda_im, ssm_log_step, ssm_b_re, ssm_b_im,
           ssm_c_re, ssm_c_im, ssm_d, ssm_glu_w, ssm_glu_b, ret_norm_g, w_proj_a, w_proj_b, w_proj_c,
           w_out, norm2_g, w_ff1, w_ff2, final_norm_g):
    depth = w_in.shape[0]
    tabs = _rope_tables(positions)
    fg = final_norm_g.reshape(1, D_MODEL)
    for l in range(depth):
        g1 = norm1_g[l].reshape(1, D_MODEL)
        ws, wgt = _inproj_weights(w_in[l])
        u, dq, dk, dvt, iq, ik, iwt, rq, rk, rv, rg = _inproj(x, g1, tabs, ws)

        s5p = _s5_params(ssm_lambda_re[l], ssm_lambda_im[l], ssm_log_step[l], ssm_b_re[l], ssm_b_im[l],
                         ssm_c_re[l], ssm_c_im[l], ssm_d[l], ssm_glu_w[l], ssm_glu_b[l])
        ya = _s5(u, s5p)
        yb = _dsa(iq, iwt, dq, ik, dk, dvt)
        yc = _retention(rq, rk, rv, rg, ret_norm_g[l])

        x = _merge(x, g1, ya, yb, yc, wgt, w_proj_a[l].astype(_BF), w_proj_b[l].astype(_BF),
                   w_proj_c[l].astype(_BF), w_out[l].astype(_BF))
        x = _mlp(x, norm2_g[l].reshape(1, D_MODEL), w_ff1[l].astype(_BF), w_ff2[l].astype(_BF), fg,
                 final_norm=(l == depth - 1))
    return x
```

```python
import functools
import math

import jax
import jax.numpy as jnp
from jax import lax
from jax.experimental import pallas as pl
from jax.experimental.pallas import tpu as pltpu

F32 = jnp.float32
_BF = jnp.bfloat16
LANES = 128
SUBLANES = 8
VMEM_LIMIT = 56 * 1024 * 1024

D_MODEL = 1024
CHUNK = 64
EPS = 1e-6
NEG_INF_SCORE = -1e30
INT_MIN = -2 ** 31
GROUP = 32 * SUBLANES
GROUPS_PER_STEP = 4
V_ROWS = 80

SSM_WIDTH = 256
SSM_GROUP = 16
SSM_GROUPS = 16
SSM_STATE = 64
NSTATE = SSM_GROUPS * SSM_STATE

DSA_HEADS = 6
DSA_HEAD_DIM = 64
DSA_WIDTH = 384
IDX_HEADS = 4
IDX_DIM = 32
DSA_TOPK = 256
ROPE_THETA = 500000.0

RET_HEADS = 4
RET_QK_DIM = 48
RET_V_DIM = 96
RET_WIDTH = 384
RET_THETA = 10000.0
RET_QK_PAD = 256

D_FF = 4096

TM_IN = 256
S5_TILE = 512
S5_SEG = S5_TILE // SUBLANES
QB = 128
KT = 512
RET_C = 128
TM_OUT = 512
FF_CHUNK = 1024


def _rms(x, g):
    return x * lax.rsqrt(jnp.mean(x * x, axis=-1, keepdims=True) + EPS) * g


def _dot(a, b):
    return jnp.dot(a, b, preferred_element_type=F32)


def _dot_nt(a, b):
    return lax.dot_general(a, b, (((1,), (1,)), ((), ())), preferred_element_type=F32)


def _to_ukey(v):
    bits = lax.bitcast_convert_type(v, jnp.int32)
    return lax.bitcast_convert_type(bits ^ ((bits >> 31) | INT_MIN), jnp.uint32)


def _transpose32(a):
    a = list(a)
    j, m = 16, 0x0000FFFF
    while j:
        k = 0
        while k < 32:
            t = (a[k] ^ (a[k + j] >> j)) & m
            a[k] = a[k] ^ t
            a[k + j] = a[k + j] ^ (t << j)
            k = (k + j + 1) & ~j
        j >>= 1
        m = (m ^ (m << j)) & 0xFFFFFFFF
    return a


def _const_spec(shape):
    nd = len(shape)
    return pl.BlockSpec(shape, lambda *_: (0,) * nd)


def _params(sem):
    return pltpu.CompilerParams(dimension_semantics=sem, vmem_limit_bytes=VMEM_LIMIT)


def _rope_roll(z, cos, sin, sgn, shift):
    outs = []
    for c in range(z.shape[1] // LANES):
        zc = z[:, c * LANES:(c + 1) * LANES]
        rot = (pltpu.roll(zc, LANES - shift, 1) * sgn[0:1, :]
               + pltpu.roll(zc, shift, 1) * sgn[1:2, :])
        outs.append(zc * cos + rot * sin)
    return outs[0] if len(outs) == 1 else jnp.concatenate(outs, axis=1)


def _inproj_kernel(x_ref, g_ref, cd_ref, sd_ref, ci_ref, si_ref, cr_ref, sr_ref, sgd_ref, sgi_ref,
                   wu_ref, wdq_ref, wdk_ref, wdvt_ref, wiq_ref, wik_ref, wiwt_ref,
                   wrq_ref, wrk_ref, wrv_ref, wrg_ref,
                   u_ref, dq_ref, dk_ref, dvt_ref, iq_ref, ik_ref, iwt_ref,
                   rq_ref, rk_ref, rv_ref, rg_ref):
    h = _rms(x_ref[...], g_ref[...]).astype(_BF)

    def proj(w_ref):
        return _dot_nt(h, w_ref[...])

    u_ref[...] = proj(wu_ref)

    cd, sd, sgd = cd_ref[...], sd_ref[...], sgd_ref[...]
    dq = _rope_roll(proj(wdq_ref), cd, sd, sgd, 8)
    dq_ref[...] = (dq * (DSA_HEAD_DIM ** -0.5 * math.log2(math.e))).astype(dq_ref.dtype)
    dk_ref[...] = _rope_roll(proj(wdk_ref), cd, sd, sgd, 8).astype(dk_ref.dtype)
    vt = _dot_nt(wdvt_ref[...], h)
    ones_row = lax.broadcasted_iota(jnp.int32, vt.shape, 0) == DSA_HEAD_DIM
    dvt_ref[...] = jnp.where(ones_row, 1.0, vt).astype(dvt_ref.dtype)

    ci, si, sgi = ci_ref[...], si_ref[...], sgi_ref[...]
    iq_ref[...] = _rope_roll(proj(wiq_ref), ci, si, sgi, 4) * (IDX_DIM ** -0.5)
    ik_ref[...] = _rope_roll(proj(wik_ref), ci, si, sgi, 4).astype(ik_ref.dtype)
    iwt_ref[...] = _dot_nt(wiwt_ref[...], h) * (IDX_HEADS ** -0.5)

    cr, sr = cr_ref[...], sr_ref[...]

    def rope_split(z):
        x1, x2 = z[:, :LANES], z[:, LANES:]
        return jnp.concatenate([x1 * cr - x2 * sr, x2 * cr + x1 * sr], axis=1)

    rq_ref[...] = rope_split(proj(wrq_ref))
    rk_ref[...] = rope_split(proj(wrk_ref)) * (RET_QK_DIM ** -0.5)
    rv_ref[...] = proj(wrv_ref)
    rg_ref[...] = proj(wrg_ref)


def _inproj(x, g, tabs, ws):
    bsz, seq, _ = x.shape
    tm = TM_IN
    grid = (bsz, seq // tm)
    row = lambda w: pl.BlockSpec((None, tm, w), lambda b, i: (b, i, 0))
    colT = lambda r: pl.BlockSpec((None, r, tm), lambda b, i: (b, 0, i))
    in_specs = ([row(D_MODEL), _const_spec(g.shape)] + [row(LANES)] * 6
                + [_const_spec(t.shape) for t in tabs[6:]] + [_const_spec(w.shape) for w in ws])
    out_shape = [
        jax.ShapeDtypeStruct((bsz, seq, SSM_WIDTH), F32),
        jax.ShapeDtypeStruct((bsz, seq, DSA_WIDTH), _BF),
        jax.ShapeDtypeStruct((bsz, seq, LANES), _BF),
        jax.ShapeDtypeStruct((bsz, LANES, seq), _BF),
        jax.ShapeDtypeStruct((bsz, seq, LANES), F32),
        jax.ShapeDtypeStruct((bsz, seq, LANES), _BF),
        jax.ShapeDtypeStruct((bsz, SUBLANES, seq), F32),
        jax.ShapeDtypeStruct((bsz, seq, RET_QK_PAD), F32),
        jax.ShapeDtypeStruct((bsz, seq, RET_QK_PAD), F32),
        jax.ShapeDtypeStruct((bsz, seq, RET_WIDTH), F32),
        jax.ShapeDtypeStruct((bsz, seq, RET_WIDTH), F32),
    ]
    out_specs = [row(SSM_WIDTH), row(DSA_WIDTH), row(LANES), colT(LANES), row(LANES), row(LANES),
                 colT(SUBLANES), row(RET_QK_PAD), row(RET_QK_PAD), row(RET_WIDTH), row(RET_WIDTH)]
    return pl.pallas_call(
        _inproj_kernel, grid=grid, in_specs=in_specs, out_specs=out_specs, out_shape=out_shape,
        compiler_params=_params(("parallel", "parallel")), name="inproj",
    )(x, g, *tabs, *ws)


def _s5_kernel(u_ref, bbd_ref, cbd_ref, lam_ref, lam64_ref, pow_ref, d_ref, gw_ref, gb_ref,
               o_ref, st_ref, xm_ref, carry_ref, up_ref, yp_ref):
    n = NSTATE

    @pl.when(pl.program_id(1) == 0)
    def _():
        carry_ref[...] = jnp.zeros_like(carry_ref)

    nhalf = SSM_WIDTH // LANES
    for i in range(SUBLANES):
        for c in range(nhalf):
            up_ref[c, pl.ds(i, S5_SEG, stride=SUBLANES), :] = u_ref[i * S5_SEG:(i + 1) * S5_SEG,
                                                                    c * LANES:(c + 1) * LANES]
    u = jnp.concatenate([up_ref[c] for c in range(nhalf)], axis=1)
    st_ref[...] = _dot(u.astype(_BF), bbd_ref[...])

    lr = jnp.broadcast_to(lam_ref[:, :n], (SUBLANES, n))
    li = jnp.broadcast_to(lam_ref[:, n:], (SUBLANES, n))

    def step(j, c):
        xr, xi = c
        off = pl.multiple_of(j * SUBLANES, SUBLANES)
        ar = st_ref[pl.ds(off, SUBLANES), :n]
        ai = st_ref[pl.ds(off, SUBLANES), n:]
        nr = lr * xr - li * xi + ar
        ni = lr * xi + li * xr + ai
        st_ref[pl.ds(off, SUBLANES), :n] = nr
        st_ref[pl.ds(off, SUBLANES), n:] = ni
        return nr, ni

    zero = jnp.zeros((SUBLANES, n), F32)
    xr, xi = lax.fori_loop(0, S5_SEG, step, (zero, zero))

    l64r, l64i = lam64_ref[:, :n], lam64_ref[:, n:]
    cr, ci = carry_ref[:, :n], carry_ref[:, n:]
    for i in range(SUBLANES):
        xm_ref[i:i + 1, :n] = cr
        xm_ref[i:i + 1, n:] = ci
        er, ei = xr[i:i + 1, :], xi[i:i + 1, :]
        cr, ci = er + l64r * cr - l64i * ci, ei + l64r * ci + l64i * cr
    carry_ref[:, :n] = cr
    carry_ref[:, n:] = ci

    hr, hi = xm_ref[:, :n], xm_ref[:, n:]

    def fix(j, _):
        off = pl.multiple_of(j * SUBLANES, SUBLANES)
        pr = pow_ref[pl.ds(j, 1), :n]
        pi = pow_ref[pl.ds(j, 1), n:]
        st_ref[pl.ds(off, SUBLANES), :n] = st_ref[pl.ds(off, SUBLANES), :n] + pr * hr - pi * hi
        st_ref[pl.ds(off, SUBLANES), n:] = st_ref[pl.ds(off, SUBLANES), n:] + pr * hi + pi * hr
        return 0

    lax.fori_loop(0, S5_SEG, fix, 0)

    y = d_ref[...] * u
    kc = 512
    for k in range(2 * n // kc):
        y = y + _dot(st_ref[:, k * kc:(k + 1) * kc].astype(_BF), cbd_ref[k * kc:(k + 1) * kc, :])
    y = jax.nn.gelu(y)
    y = y * jax.nn.sigmoid(_dot(y.astype(_BF), gw_ref[...]) + gb_ref[...])
    for c in range(nhalf):
        yp_ref[c] = y[:, c * LANES:(c + 1) * LANES]
    for i in range(SUBLANES):
        for c in range(nhalf):
            o_ref[i * S5_SEG:(i + 1) * S5_SEG, c * LANES:(c + 1) * LANES] = yp_ref[
                c, pl.ds(i, S5_SEG, stride=SUBLANES), :]


def _s5(u_perm, prm):
    bsz, seq, _ = u_perm.shape
    grid = (bsz, seq // S5_TILE)
    row = pl.BlockSpec((None, S5_TILE, SSM_WIDTH), lambda b, i: (b, i, 0))
    return pl.pallas_call(
        _s5_kernel, grid=grid,
        in_specs=[row] + [_const_spec(p.shape) for p in prm],
        out_specs=row,
        out_shape=jax.ShapeDtypeStruct((bsz, seq, SSM_WIDTH), F32),
        scratch_shapes=[pltpu.VMEM((S5_TILE, 2 * NSTATE), F32),
                        pltpu.VMEM((SUBLANES, 2 * NSTATE), F32),
                        pltpu.VMEM((1, 2 * NSTATE), F32),
                        pltpu.VMEM((SSM_WIDTH // LANES, S5_TILE, LANES), F32),
                        pltpu.VMEM((SSM_WIDTH // LANES, S5_TILE, LANES), F32)],
        compiler_params=_params(("arbitrary", "arbitrary")), name="s5",
    )(u_perm, *prm)


def _s5_params(lam_re, lam_im, log_step, b_re, b_im, c_re, c_im, d_skip, glu_w, glu_b):
    step = jnp.exp(log_step.astype(F32))[:, None]
    ere, eim = lam_re * step, lam_im * step
    mag = jnp.exp(ere)
    lb_re, lb_im = mag * jnp.cos(eim), mag * jnp.sin(eim)
    den = lam_re * lam_re + lam_im * lam_im
    f_re = ((lb_re - 1.0) * lam_re + lb_im * lam_im) / den
    f_im = (lb_im * lam_re - (lb_re - 1.0) * lam_im) / den
    bb_re = f_re[..., None] * b_re - f_im[..., None] * b_im
    bb_im = f_re[..., None] * b_im + f_im[..., None] * b_re
    eye = jnp.eye(SSM_GROUPS, dtype=F32)
    bbd = jnp.concatenate(
        [jnp.einsum('gpc,gh->gchp', bb, eye).reshape(SSM_WIDTH, NSTATE) for bb in (bb_re, bb_im)], axis=1)
    cbd = jnp.concatenate(
        [jnp.einsum('gcp,gh->hpgc', cc, eye).reshape(NSTATE, SSM_WIDTH) for cc in (c_re, -c_im)], axis=0)
    flat = lambda a: a.reshape(1, NSTATE)
    lam = jnp.concatenate([flat(lb_re), flat(lb_im)], axis=1)

    def power(k):
        k = jnp.asarray(k, F32).reshape(-1, 1)
        m = jnp.exp(k * flat(ere))
        return jnp.concatenate([m * jnp.cos(k * flat(eim)), m * jnp.sin(k * flat(eim))], axis=1)

    return (bbd.astype(_BF), cbd.astype(_BF), lam, power(float(S5_SEG)),
            power(jnp.arange(1, S5_SEG + 1)), d_skip.reshape(1, SSM_WIDTH),
            glu_w.astype(_BF), glu_b.reshape(1, SSM_WIDTH))


def _dsa_kernel(iq_ref, iwt_ref, dq_ref, ik_ref, dk_ref, dvt_ref, o_ref,
                planes_ref, cand_ref, pick_ref, bias_ref, acc_ref,
                lg0_ref, lg1_ref, s0_ref, s1_ref, p0_ref, p1_ref, *, seq, topk):
    qi = pl.program_id(1)
    nk = (qi * QB) // KT + 1
    lane = lax.broadcasted_iota(jnp.int32, (1, QB), 1)
    vis_end = qi * QB + jnp.where(lane < CHUNK, CHUNK, 2 * CHUNK)

    iq = iq_ref[...]
    col = lax.broadcasted_iota(jnp.int32, (QB, LANES), 1)
    qs = jnp.concatenate([jnp.where(col // IDX_DIM == h, iq, 0.0) for h in range(IDX_HEADS)],
                         axis=0).astype(_BF)
    w = iwt_ref[...]

    nkp = nk + lax.rem(nk, 2)
    last_tile = nkp - 1

    def logits(kt, dst_ref):
        off = pl.multiple_of(kt * KT, KT)
        dst_ref[...] = _dot_nt(ik_ref[pl.ds(off, KT), :], qs)

    def score_tile(kt, src_ref):
        lg = src_ref[...]
        sc = jnp.zeros((KT, QB), F32)
        for h in range(IDX_HEADS):
            sc = sc + jnp.maximum(lg[:, h * QB:(h + 1) * QB], 0.0) * w[h:h + 1, :]
        sc = jnp.where(sc == 0.0, 0.0, sc)
        ukey = _to_ukey(sc)
        for g in range(KT // GROUP):
            words = _transpose32([ukey[g * GROUP + m * SUBLANES:g * GROUP + (m + 1) * SUBLANES, :]
                                  for m in range(32)])
            base = pl.multiple_of((kt * (KT // GROUP) + g) * GROUP, GROUP)
            planes_ref[pl.ds(base, GROUP), :] = lax.bitcast_convert_type(jnp.concatenate(words, axis=0), jnp.int32)

    def score_pair(i, c):
        logits(2 * i + 1, lg1_ref)
        score_tile(2 * i, lg0_ref)
        logits(jnp.minimum(2 * i + 2, last_tile), lg0_ref)
        score_tile(2 * i + 1, lg1_ref)
        return c

    logits(0, lg0_ref)
    lax.fori_loop(0, nkp // 2, score_pair, 0)

    ngrp = nkp * (KT // GROUP)
    sub = lax.broadcasted_iota(jnp.int32, (SUBLANES, QB), 0)
    zero8 = jnp.zeros((SUBLANES, QB), jnp.int32)

    def rows_below(limit, g):
        nm = lax.shift_right_arithmetic(limit - g * GROUP + (SUBLANES - 1), 3)
        top = lax.shift_right_arithmetic(jnp.full((SUBLANES, QB), INT_MIN, jnp.int32),
                                         jnp.clip(nm, 1, 32) - 1)
        return jnp.where(nm <= 0, 0, top)

    def word(ref, g):
        return ref[pl.ds(pl.multiple_of(g * SUBLANES, SUBLANES), SUBLANES), :]

    def put_word(ref, g, v):
        ref[pl.ds(pl.multiple_of(g * SUBLANES, SUBLANES), SUBLANES), :] = v

    def plane(g, i):
        return planes_ref[pl.ds(pl.multiple_of(g * GROUP + i * SUBLANES, SUBLANES), SUBLANES), :]

    def sweep(step):
        def body(s, accs):
            return tuple(a + step(s * GROUPS_PER_STEP + t) for t, a in enumerate(accs))
        accs = lax.fori_loop(0, ngrp // GROUPS_PER_STEP, body, (zero8,) * GROUPS_PER_STEP)
        return functools.reduce(lambda a, b: a + b, accs).sum(axis=0, keepdims=True)

    def init_words(g):
        put_word(cand_ref, g, rows_below(vis_end - sub, g))
        put_word(pick_ref, g, zero8)
        return zero8

    sweep(init_words)

    def count_ones(i):
        return sweep(lambda g: lax.population_count(word(cand_ref, g) & plane(g, i)))

    def decide(c1, rem):
        take = c1 >= rem
        return jnp.where(take, 1, 0), jnp.where(take, rem, rem - c1)

    def narrow(g, i, take):
        take8 = jnp.broadcast_to(take, (SUBLANES, QB)) != 0
        c = word(cand_ref, g)
        ones = c & plane(g, i)
        put_word(pick_ref, g, word(pick_ref, g) | jnp.where(take8, 0, ones))
        c = jnp.where(take8, ones, c ^ ones)
        put_word(cand_ref, g, c)
        return c

    def radix_step(i, carry):
        take8, rem = carry

        def step(g):
            return lax.population_count(narrow(g, i - 1, take8) & plane(g, i))
        return decide(sweep(step), rem)

    take8, rem = decide(count_ones(0), jnp.full((1, QB), topk, jnp.int32))
    take8, rem = lax.fori_loop(1, 32, radix_step, (take8, rem))

    def last_narrow(g):
        narrow(g, 31, take8)
        return zero8

    sweep(last_narrow)

    def tie_step(i, edge):
        trial = edge + jnp.left_shift(jnp.int32(1), (seq.bit_length() - 1) - i)
        limit = trial - sub
        below = sweep(lambda g: lax.population_count(word(cand_ref, g) & rows_below(limit, g)))
        return jnp.where(below < rem, trial, edge)

    edge = lax.fori_loop(0, seq.bit_length(), tie_step, jnp.zeros((1, QB), jnp.int32))
    keep_limit = edge + 1 - sub

    def bias_group(g):
        chosen = word(pick_ref, g) | (word(cand_ref, g) & rows_below(keep_limit, g))
        base = pl.multiple_of(g * GROUP, GROUP)
        bias_ref[pl.ds(base, GROUP), :] = jnp.concatenate(
            [jnp.where(lax.shift_left(chosen, jnp.full_like(chosen, m)) < 0, 0.0, NEG_INF_SCORE)
             for m in range(32)], axis=0)
        return zero8

    sweep(bias_group)

    q = dq_ref[...]
    half = (col // DSA_HEAD_DIM)
    qst = jnp.concatenate(
        [jnp.where(half == (h % 2), q[:, (h // 2) * LANES:(h // 2 + 1) * LANES], jnp.zeros((), q.dtype))
         for h in range(DSA_HEADS)], axis=0)
    npair = DSA_HEADS // 2
    hq = DSA_HEADS * QB
    acc_ref[...] = jnp.zeros_like(acc_ref)
    p1_ref[...] = jnp.zeros_like(p1_ref)

    def qk(kt, dst_ref):
        off = pl.multiple_of(kt * KT, KT)
        b = bias_ref[pl.ds(off, KT), :]
        dst_ref[...] = _dot_nt(dk_ref[pl.ds(off, KT), :], qst) + jnp.concatenate([b] * DSA_HEADS, axis=1)

    def numer(src_ref, dst_ref, m):
        s = src_ref[...]
        m_new = jnp.maximum(m, jnp.max(s, axis=0, keepdims=True))
        dst_ref[...] = jnp.exp2(s - m_new).astype(dst_ref.dtype)
        return m_new, jnp.exp2(m - m_new)

    def pv(kt, src_ref, alpha):
        off = pl.multiple_of(kt * KT, KT)
        vt = dvt_ref[0:V_ROWS, pl.ds(off, KT)]
        acc_ref[...] = alpha * acc_ref[...] + _dot(vt, src_ref[...])

    def att_pair(i, c):
        m, alpha = c
        qk(2 * i + 1, s1_ref)
        pv(jnp.maximum(2 * i - 1, 0), p1_ref, alpha)
        m, alpha = numer(s0_ref, p0_ref, m)
        qk(jnp.minimum(2 * i + 2, last_tile), s0_ref)
        pv(2 * i, p0_ref, alpha)
        m, alpha = numer(s1_ref, p1_ref, m)
        return m, alpha

    qk(0, s0_ref)
    c0 = (jnp.full((1, hq), -jnp.inf, F32), jnp.ones((1, hq), F32))
    _, alpha = lax.fori_loop(0, nkp // 2, att_pair, c0)
    pv(last_tile, p1_ref, alpha)
    acc = acc_ref[...]
    o = acc[0:DSA_HEAD_DIM, :] / acc[DSA_HEAD_DIM:DSA_HEAD_DIM + 1, :]
    for j in range(npair):
        o_ref[:, j * LANES:(j + 1) * LANES] = jnp.concatenate(
            [o[:, (2 * j) * QB:(2 * j + 1) * QB], o[:, (2 * j + 1) * QB:(2 * j + 2) * QB]], axis=0).T


def _dsa(iq, iwt, dq, ik, dk, dvt):
    bsz, seq, _ = iq.shape
    topk = min(DSA_TOPK, seq // 4)
    grid = (bsz, seq // QB)
    qrow = lambda w: pl.BlockSpec((None, QB, w), lambda b, i: (b, i, 0))
    full = lambda r, c: pl.BlockSpec((None, r, c), lambda b, i: (b, 0, 0))
    return pl.pallas_call(
        functools.partial(_dsa_kernel, seq=seq, topk=topk), grid=grid,
        in_specs=[qrow(LANES), pl.BlockSpec((None, SUBLANES, QB), lambda b, i: (b, 0, i)), qrow(DSA_WIDTH),
                  full(seq, LANES), full(seq, LANES), full(LANES, seq)],
        out_specs=qrow(DSA_WIDTH),
        out_shape=jax.ShapeDtypeStruct((bsz, seq, DSA_WIDTH), F32),
        scratch_shapes=[pltpu.VMEM((seq, QB), jnp.int32),
                        pltpu.VMEM((seq // GROUP * SUBLANES, QB), jnp.int32),
                        pltpu.VMEM((seq // GROUP * SUBLANES, QB), jnp.int32),
                        pltpu.VMEM((seq, QB), F32),
                        pltpu.VMEM((V_ROWS, DSA_HEADS * QB), F32),
                        pltpu.VMEM((KT, IDX_HEADS * QB), F32), pltpu.VMEM((KT, IDX_HEADS * QB), F32),
                        pltpu.VMEM((KT, DSA_HEADS * QB), F32), pltpu.VMEM((KT, DSA_HEADS * QB), F32),
                        pltpu.VMEM((KT, DSA_HEADS * QB), _BF), pltpu.VMEM((KT, DSA_HEADS * QB), _BF)],
        compiler_params=_params(("parallel", "arbitrary")), name="dsa",
    )(iq, iwt, dq, ik, dk, dvt)


def _ret_kernel(rq_ref, rk_ref, rv_ref, rg_ref, qdec_ref, kdec_ref, dmask_ref, cdec_ref, blk_ref,
                hmq_ref, hmv_ref, ones_ref, ng_ref, o_ref, s_ref):
    @pl.when(pl.program_id(0) == 0)
    def _():
        s_ref[...] = jnp.zeros_like(s_ref)

    ones = ones_ref[...]

    def head_mean(z):
        return _dot(z.astype(_BF), ones) * (1.0 / RET_V_DIM)

    for b in range(rq_ref.shape[0]):
        q, k, v = rq_ref[b], rk_ref[b], rv_ref[b]
        state = s_ref[b]
        y = _dot((q * qdec_ref[...]).astype(_BF), state.astype(_BF))
        kd = (k * kdec_ref[...]).T.astype(_BF)
        vb = v.astype(_BF)
        s_ref[b] = state * cdec_ref[...] + _dot(kd, vb) * blk_ref[...]
        kb = k.astype(_BF)
        a = jnp.concatenate([(_dot_nt((q * hmq_ref[h:h + 1, :]).astype(_BF), kb) * dmask_ref[h]).astype(_BF)
                             for h in range(RET_HEADS)], axis=1)
        vh = jnp.concatenate([(v * hmv_ref[h:h + 1, :]).astype(_BF) for h in range(RET_HEADS)], axis=0)
        y = y + _dot(a, vh)
        d = y - head_mean(y)
        yn = d * lax.rsqrt(head_mean(d * d) + EPS) * ng_ref[...]
        o_ref[b] = jax.nn.silu(rg_ref[b]) * yn


def _ret_consts():
    c = RET_C
    log_g = jnp.log1p(-jnp.exp2(-5.0 - jnp.arange(RET_HEADS, dtype=F32)))
    pos = jnp.arange(c, dtype=F32)
    diff = pos[:, None] - pos[None, :]
    dmask = jnp.where(diff >= 0, jnp.exp(log_g[:, None, None] * jnp.maximum(diff, 0.0)), 0.0)
    lane_q = jnp.arange(RET_QK_PAD)
    head_q = jnp.where((lane_q % LANES) < RET_HEADS * (RET_QK_DIM // 2), (lane_q % LANES) // (RET_QK_DIM // 2), -1)
    head_v = jnp.arange(RET_WIDTH) // RET_V_DIM
    hmq = (head_q[None, :] == jnp.arange(RET_HEADS)[:, None]).astype(F32)
    hmv = (head_v[None, :] == jnp.arange(RET_HEADS)[:, None]).astype(F32)
    lg_q = hmq.T @ log_g
    qdec = jnp.exp(lg_q[None, :] * (pos[:, None] + 1.0))
    kdec = jnp.exp(lg_q[None, :] * (c - 1.0 - pos[:, None]))
    blk = hmq.T @ hmv
    cdec = blk * jnp.exp(lg_q * c)[:, None]
    ones = (hmv.T @ hmv).astype(_BF)
    return qdec, kdec, dmask, cdec, blk, hmq, hmv, ones


def _retention(rq, rk, rv, rg, norm_g):
    bsz, seq, _ = rq.shape
    consts = _ret_consts()
    grid = (seq // RET_C,)
    row = lambda w: pl.BlockSpec((bsz, RET_C, w), lambda i: (0, i, 0))
    ng = norm_g.reshape(1, RET_WIDTH)
    return pl.pallas_call(
        _ret_kernel, grid=grid,
        in_specs=[row(RET_QK_PAD), row(RET_QK_PAD), row(RET_WIDTH), row(RET_WIDTH)]
        + [_const_spec(a.shape) for a in consts] + [_const_spec(ng.shape)],
        out_specs=row(RET_WIDTH),
        out_shape=jax.ShapeDtypeStruct((bsz, seq, RET_WIDTH), F32),
        scratch_shapes=[pltpu.VMEM((bsz, RET_QK_PAD, RET_WIDTH), F32)],
        compiler_params=_params(("arbitrary",)), name="retention",
    )(rq, rk, rv, rg, *consts, ng)


def _merge_kernel(x_ref, g_ref, ya_ref, yb_ref, yc_ref, wg_ref, wa_ref, wb_ref, wc_ref, wo_ref, o_ref):
    x = x_ref[...]
    h = _rms(x, g_ref[...]).astype(_BF)
    merged = jnp.zeros(x.shape, F32)
    for n, (y_ref, w_ref) in enumerate(((ya_ref, wa_ref), (yb_ref, wb_ref), (yc_ref, wc_ref))):
        gate = jax.nn.sigmoid(_dot_nt(h, wg_ref[n * D_MODEL:(n + 1) * D_MODEL, :]))
        merged = merged + gate * _dot(y_ref[...].astype(_BF), w_ref[...])
    o_ref[...] = x + _dot(merged.astype(_BF), wo_ref[...])


def _merge(x, g, ya, yb, yc, wg, wa, wb, wc, wo):
    bsz, seq, _ = x.shape
    grid = (bsz, seq // TM_OUT)
    row = lambda w: pl.BlockSpec((None, TM_OUT, w), lambda b, i: (b, i, 0))
    return pl.pallas_call(
        _merge_kernel, grid=grid,
        in_specs=[row(D_MODEL), _const_spec(g.shape), row(SSM_WIDTH), row(DSA_WIDTH), row(RET_WIDTH)]
        + [_const_spec(w.shape) for w in (wg, wa, wb, wc, wo)],
        out_specs=row(D_MODEL),
        out_shape=jax.ShapeDtypeStruct(x.shape, F32),
        compiler_params=_params(("parallel", "parallel")), name="merge",
    )(x, g, ya, yb, yc, wg, wa, wb, wc, wo)


def _mlp_kernel(x_ref, g_ref, w1_ref, w2_ref, fg_ref, o_ref, *, final_norm):
    x = x_ref[...]
    h = _rms(x, g_ref[...]).astype(_BF)
    acc = x
    for f in range(D_FF // FF_CHUNK):
        t = jnp.maximum(_dot(h, w1_ref[:, f * FF_CHUNK:(f + 1) * FF_CHUNK]), 0.0)
        acc = acc + _dot((t * t).astype(_BF), w2_ref[f * FF_CHUNK:(f + 1) * FF_CHUNK, :])
    o_ref[...] = _rms(acc, fg_ref[...]) if final_norm else acc


def _mlp(x, g, w1, w2, fg, final_norm):
    bsz, seq, _ = x.shape
    grid = (bsz, seq // TM_OUT)
    row = pl.BlockSpec((None, TM_OUT, D_MODEL), lambda b, i: (b, i, 0))
    return pl.pallas_call(
        functools.partial(_mlp_kernel, final_norm=final_norm), grid=grid,
        in_specs=[row, _const_spec(g.shape), _const_spec(w1.shape), _const_spec(w2.shape), _const_spec(fg.shape)],
        out_specs=row,
        out_shape=jax.ShapeDtypeStruct(x.shape, F32),
        compiler_params=_params(("parallel", "parallel")), name="mlp",
    )(x, g, w1, w2, fg)


def _rope_tables(positions):
    pos = positions.astype(F32)[..., None]
    lane = jnp.arange(LANES)

    def angles(rot_dim, theta):
        half = rot_dim // 2
        inv = jnp.exp(-math.log(theta) * jnp.arange(half, dtype=F32) * (2.0 / rot_dim))
        return pos * inv

    def roll_tables(head_dim, rot_dim, theta):
        half = rot_dim // 2
        ang = angles(rot_dim, theta)
        ln = lane % head_dim
        rot = ln < rot_dim
        a = jnp.take(ang, ln % half, axis=-1)
        cos = jnp.where(rot, jnp.cos(a), 1.0)
        sin = jnp.where(rot, jnp.sin(a), 0.0)
        sgn = jnp.stack([jnp.where(ln < half, -1.0, 0.0),
                         jnp.where((ln >= half) & rot, 1.0, 0.0)]).astype(F32)
        return cos, sin, sgn

    cd, sd, sgd = roll_tables(DSA_HEAD_DIM, DSA_HEAD_DIM // 4, ROPE_THETA)
    ci, si, sgi = roll_tables(IDX_DIM, IDX_DIM // 4, ROPE_THETA)
    half = RET_QK_DIM // 2
    ang = angles(RET_QK_DIM, RET_THETA)
    a = jnp.take(ang, lane % half, axis=-1)
    used = lane < RET_HEADS * half
    cr = jnp.where(used, jnp.cos(a), 1.0)
    sr = jnp.where(used, jnp.sin(a), 0.0)
    return cd, sd, ci, si, cr, sr, sgd, sgi


def _inproj_weights(wt):
    sizes = (SSM_WIDTH, DSA_WIDTH, DSA_HEAD_DIM, DSA_HEAD_DIM, IDX_HEADS * IDX_DIM, IDX_DIM, IDX_HEADS,
             RET_HEADS * RET_QK_DIM, RET_HEADS * RET_QK_DIM, RET_WIDTH, RET_WIDTH, 3 * D_MODEL)
    parts, off = [], 0
    for n in sizes:
        parts.append(wt[off:off + n, :])
        off += n
    wu, wdq, wdk, wdv, wiq, wik, wiw, wrq, wrk, wrv, wrg, wgt = parts
    half = RET_QK_DIM // 2
    zrows = lambda n: jnp.zeros((n, D_MODEL), wt.dtype)

    def ret_split(a):
        a = a.reshape(RET_HEADS, 2, half, D_MODEL)
        pad = zrows(LANES - RET_HEADS * half)
        return jnp.concatenate([a[:, 0].reshape(-1, D_MODEL), pad, a[:, 1].reshape(-1, D_MODEL), pad], axis=0)

    ws = (wu, wdq, jnp.concatenate([wdk, wdk], axis=0), jnp.concatenate([wdv, zrows(LANES - DSA_HEAD_DIM)], axis=0),
          wiq, jnp.concatenate([wik] * IDX_HEADS, axis=0), jnp.concatenate([wiw, zrows(SUBLANES - IDX_HEADS)], axis=0),
          ret_split(wrq), ret_split(wrk), wrv, wrg)
    return tuple(a.astype(_BF) for a in ws), wgt.astype(_BF)


def kernel(x, positions, norm1_g, w_in, ssm_lambda_re, ssm_lambda_im, ssm_log_step, ssm_b_re, ssm_b_im,
           ssm_c_re, ssm_c_im, ssm_d, ssm_glu_w, ssm_glu_b, ret_norm_g, w_proj_a, w_proj_b, w_proj_c,
           w_out, norm2_g, w_ff1, w_ff2, final_norm_g):
    depth = w_in.shape[0]
    tabs = _rope_tables(positions)
    fg = final_norm_g.reshape(1, D_MODEL)
    w_in_t = jnp.transpose(w_in, (2, 0, 1))
    for l in range(depth):
        g1 = norm1_g[l].reshape(1, D_MODEL)
        ws, wgt = _inproj_weights(w_in_t[:, l, :])
        u, dq, dk, dvt, iq, ik, iwt, rq, rk, rv, rg = _inproj(x, g1, tabs, ws)

        s5p = _s5_params(ssm_lambda_re[l], ssm_lambda_im[l], ssm_log_step[l], ssm_b_re[l], ssm_b_im[l],
                         ssm_c_re[l], ssm_c_im[l], ssm_d[l], ssm_glu_w[l], ssm_glu_b[l])
        ya = _s5(u, s5p)
        yb = _dsa(iq, iwt, dq, ik, dk, dvt)
        yc = _retention(rq, rk, rv, rg, ret_norm_g[l])

        x = _merge(x, g1, ya, yb, yc, wgt, w_proj_a[l].astype(_BF), w_proj_b[l].astype(_BF),
                   w_proj_c[l].astype(_BF), w_out[l].astype(_BF))
        x = _mlp(x, norm2_g[l].reshape(1, D_MODEL), w_ff1[l].astype(_BF), w_ff2[l].astype(_BF), fg,
                 final_norm=(l == depth - 1))
    return x
```

```python
import functools
import math

import jax
import jax.numpy as jnp
from jax import lax
from jax.experimental import pallas as pl
from jax.experimental.pallas import tpu as pltpu

F32 = jnp.float32
_BF = jnp.bfloat16
LANES = 128
SUBLANES = 8
VMEM_LIMIT = 56 * 1024 * 1024

D_MODEL = 1024
CHUNK = 64
EPS = 1e-6
NEG_INF_SCORE = -1e30
INT_MIN = -2 ** 31
GROUP = 32 * SUBLANES
GROUPS_PER_STEP = 4
V_ROWS = 80

SSM_WIDTH = 256
SSM_GROUP = 16
SSM_GROUPS = 16
SSM_STATE = 64
NSTATE = SSM_GROUPS * SSM_STATE

DSA_HEADS = 6
DSA_HEAD_DIM = 64
DSA_WIDTH = 384
IDX_HEADS = 4
IDX_DIM = 32
DSA_TOPK = 256
ROPE_THETA = 500000.0

RET_HEADS = 4
RET_QK_DIM = 48
RET_V_DIM = 96
RET_WIDTH = 384
RET_THETA = 10000.0
RET_QK_PAD = 256

D_FF = 4096

TM_IN = 512
S5_TILE = 512
S5_SEG = S5_TILE // SUBLANES
QB = 128
KT = 512
RET_C = 128
TM_OUT = 512
FF_CHUNK = 1024


def _rms(x, g):
    return x * lax.rsqrt(jnp.mean(x * x, axis=-1, keepdims=True) + EPS) * g


def _dot(a, b):
    return jnp.dot(a, b, preferred_element_type=F32)


def _dot_nt(a, b):
    return lax.dot_general(a, b, (((1,), (1,)), ((), ())), preferred_element_type=F32)


def _to_ukey(v):
    bits = lax.bitcast_convert_type(v, jnp.int32)
    return lax.bitcast_convert_type(bits ^ ((bits >> 31) | INT_MIN), jnp.uint32)


def _transpose32(a):
    a = list(a)
    j, m = 16, 0x0000FFFF
    while j:
        k = 0
        while k < 32:
            t = (a[k] ^ (a[k + j] >> j)) & m
            a[k] = a[k] ^ t
            a[k + j] = a[k + j] ^ (t << j)
            k = (k + j + 1) & ~j
        j >>= 1
        m = (m ^ (m << j)) & 0xFFFFFFFF
    return a


def _const_spec(shape):
    nd = len(shape)
    return pl.BlockSpec(shape, lambda *_: (0,) * nd)


def _params(sem):
    return pltpu.CompilerParams(dimension_semantics=sem, vmem_limit_bytes=VMEM_LIMIT)


def _rope_roll(z, cos, sin, sgn, shift):
    outs = []
    for c in range(z.shape[1] // LANES):
        zc = z[:, c * LANES:(c + 1) * LANES]
        rot = (pltpu.roll(zc, LANES - shift, 1) * sgn[0:1, :]
               + pltpu.roll(zc, shift, 1) * sgn[1:2, :])
        outs.append(zc * cos + rot * sin)
    return outs[0] if len(outs) == 1 else jnp.concatenate(outs, axis=1)


def _inproj_kernel(x_ref, g_ref, cd_ref, sd_ref, ci_ref, si_ref, cr_ref, sr_ref, sgd_ref, sgi_ref,
                   wu_ref, wdq_ref, wdk_ref, wdvt_ref, wiq_ref, wik_ref, wiwt_ref,
                   wrq_ref, wrk_ref, wrv_ref, wrg_ref,
                   u_ref, dq_ref, dk_ref, dvt_ref, iq_ref, ik_ref, iwt_ref,
                   rq_ref, rk_ref, rv_ref, rg_ref):
    h = _rms(x_ref[...], g_ref[...]).astype(_BF)

    def proj(w_ref):
        return _dot_nt(h, w_ref[...])

    u_ref[...] = proj(wu_ref)

    cd, sd, sgd = cd_ref[...], sd_ref[...], sgd_ref[...]
    dq = _rope_roll(proj(wdq_ref), cd, sd, sgd, 8)
    dq_ref[...] = (dq * (DSA_HEAD_DIM ** -0.5 * math.log2(math.e))).astype(dq_ref.dtype)
    dk_ref[...] = _rope_roll(proj(wdk_ref), cd, sd, sgd, 8).astype(dk_ref.dtype)
    vt = _dot_nt(wdvt_ref[...], h)
    ones_row = lax.broadcasted_iota(jnp.int32, vt.shape, 0) == DSA_HEAD_DIM
    dvt_ref[...] = jnp.where(ones_row, 1.0, vt).astype(dvt_ref.dtype)

    ci, si, sgi = ci_ref[...], si_ref[...], sgi_ref[...]
    iq_ref[...] = _rope_roll(proj(wiq_ref), ci, si, sgi, 4) * (IDX_DIM ** -0.5)
    ik_ref[...] = _rope_roll(proj(wik_ref), ci, si, sgi, 4).astype(ik_ref.dtype)
    iwt_ref[...] = _dot_nt(wiwt_ref[...], h) * (IDX_HEADS ** -0.5)

    cr, sr = cr_ref[...], sr_ref[...]

    def rope_split(z):
        x1, x2 = z[:, :LANES], z[:, LANES:]
        return jnp.concatenate([x1 * cr - x2 * sr, x2 * cr + x1 * sr], axis=1)

    rq_ref[...] = rope_split(proj(wrq_ref))
    rk_ref[...] = rope_split(proj(wrk_ref)) * (RET_QK_DIM ** -0.5)
    rv_ref[...] = proj(wrv_ref)
    rg_ref[...] = proj(wrg_ref)


def _inproj(x, g, tabs, ws):
    bsz, seq, _ = x.shape
    tm = TM_IN
    grid = (bsz, seq // tm)
    row = lambda w: pl.BlockSpec((None, tm, w), lambda b, i: (b, i, 0))
    colT = lambda r: pl.BlockSpec((None, r, tm), lambda b, i: (b, 0, i))
    in_specs = ([row(D_MODEL), _const_spec(g.shape)] + [row(LANES)] * 6
                + [_const_spec(t.shape) for t in tabs[6:]] + [_const_spec(w.shape) for w in ws])
    out_shape = [
        jax.ShapeDtypeStruct((bsz, seq, SSM_WIDTH), F32),
        jax.ShapeDtypeStruct((bsz, seq, DSA_WIDTH), _BF),
        jax.ShapeDtypeStruct((bsz, seq, LANES), _BF),
        jax.ShapeDtypeStruct((bsz, LANES, seq), _BF),
        jax.ShapeDtypeStruct((bsz, seq, LANES), F32),
        jax.ShapeDtypeStruct((bsz, seq, LANES), _BF),
        jax.ShapeDtypeStruct((bsz, SUBLANES, seq), F32),
        jax.ShapeDtypeStruct((bsz, seq, RET_QK_PAD), F32),
        jax.ShapeDtypeStruct((bsz, seq, RET_QK_PAD), F32),
        jax.ShapeDtypeStruct((bsz, seq, RET_WIDTH), F32),
        jax.ShapeDtypeStruct((bsz, seq, RET_WIDTH), F32),
    ]
    out_specs = [row(SSM_WIDTH), row(DSA_WIDTH), row(LANES), colT(LANES), row(LANES), row(LANES),
                 colT(SUBLANES), row(RET_QK_PAD), row(RET_QK_PAD), row(RET_WIDTH), row(RET_WIDTH)]
    return pl.pallas_call(
        _inproj_kernel, grid=grid, in_specs=in_specs, out_specs=out_specs, out_shape=out_shape,
        compiler_params=_params(("parallel", "parallel")), name="inproj",
    )(x, g, *tabs, *ws)


def _s5_kernel(u_ref, bbd_ref, cbd_ref, lam_ref, lam64_ref, pow_ref, d_ref, gw_ref, gb_ref,
               o_ref, st_ref, xm_ref, carry_ref, up_ref, yp_ref):
    n = NSTATE

    @pl.when(pl.program_id(1) == 0)
    def _():
        carry_ref[...] = jnp.zeros_like(carry_ref)

    nhalf = SSM_WIDTH // LANES
    for i in range(SUBLANES):
        for c in range(nhalf):
            up_ref[c, pl.ds(i, S5_SEG, stride=SUBLANES), :] = u_ref[i * S5_SEG:(i + 1) * S5_SEG,
                                                                    c * LANES:(c + 1) * LANES]
    u = jnp.concatenate([up_ref[c] for c in range(nhalf)], axis=1)
    st_ref[...] = _dot(u.astype(_BF), bbd_ref[...])

    lr = jnp.broadcast_to(lam_ref[:, :n], (SUBLANES, n))
    li = jnp.broadcast_to(lam_ref[:, n:], (SUBLANES, n))

    def step(j, c):
        xr, xi = c
        off = pl.multiple_of(j * SUBLANES, SUBLANES)
        ar = st_ref[pl.ds(off, SUBLANES), :n]
        ai = st_ref[pl.ds(off, SUBLANES), n:]
        nr = lr * xr - li * xi + ar
        ni = lr * xi + li * xr + ai
        st_ref[pl.ds(off, SUBLANES), :n] = nr
        st_ref[pl.ds(off, SUBLANES), n:] = ni
        return nr, ni

    zero = jnp.zeros((SUBLANES, n), F32)
    xr, xi = lax.fori_loop(0, S5_SEG, step, (zero, zero))

    l64r, l64i = lam64_ref[:, :n], lam64_ref[:, n:]
    cr, ci = carry_ref[:, :n], carry_ref[:, n:]
    for i in range(SUBLANES):
        xm_ref[i:i + 1, :n] = cr
        xm_ref[i:i + 1, n:] = ci
        er, ei = xr[i:i + 1, :], xi[i:i + 1, :]
        cr, ci = er + l64r * cr - l64i * ci, ei + l64r * ci + l64i * cr
    carry_ref[:, :n] = cr
    carry_ref[:, n:] = ci

    hr, hi = xm_ref[:, :n], xm_ref[:, n:]

    def fix(j, _):
        off = pl.multiple_of(j * SUBLANES, SUBLANES)
        pr = pow_ref[pl.ds(j, 1), :n]
        pi = pow_ref[pl.ds(j, 1), n:]
        st_ref[pl.ds(off, SUBLANES), :n] = st_ref[pl.ds(off, SUBLANES), :n] + pr * hr - pi * hi
        st_ref[pl.ds(off, SUBLANES), n:] = st_ref[pl.ds(off, SUBLANES), n:] + pr * hi + pi * hr
        return 0

    lax.fori_loop(0, S5_SEG, fix, 0)

    y = d_ref[...] * u
    kc = 512
    for k in range(2 * n // kc):
        y = y + _dot(st_ref[:, k * kc:(k + 1) * kc].astype(_BF), cbd_ref[k * kc:(k + 1) * kc, :])
    y = jax.nn.gelu(y)
    y = y * jax.nn.sigmoid(_dot(y.astype(_BF), gw_ref[...]) + gb_ref[...])
    for c in range(nhalf):
        yp_ref[c] = y[:, c * LANES:(c + 1) * LANES]
    for i in range(SUBLANES):
        for c in range(nhalf):
            o_ref[i * S5_SEG:(i + 1) * S5_SEG, c * LANES:(c + 1) * LANES] = yp_ref[
                c, pl.ds(i, S5_SEG, stride=SUBLANES), :]


def _s5(u_perm, prm):
    bsz, seq, _ = u_perm.shape
    grid = (bsz, seq // S5_TILE)
    row = pl.BlockSpec((None, S5_TILE, SSM_WIDTH), lambda b, i: (b, i, 0))
    return pl.pallas_call(
        _s5_kernel, grid=grid,
        in_specs=[row] + [_const_spec(p.shape) for p in prm],
        out_specs=row,
        out_shape=jax.ShapeDtypeStruct((bsz, seq, SSM_WIDTH), F32),
        scratch_shapes=[pltpu.VMEM((S5_TILE, 2 * NSTATE), F32),
                        pltpu.VMEM((SUBLANES, 2 * NSTATE), F32),
                        pltpu.VMEM((1, 2 * NSTATE), F32),
                        pltpu.VMEM((SSM_WIDTH // LANES, S5_TILE, LANES), F32),
                        pltpu.VMEM((SSM_WIDTH // LANES, S5_TILE, LANES), F32)],
        compiler_params=_params(("arbitrary", "arbitrary")), name="s5",
    )(u_perm, *prm)


def _s5_params(lam_re, lam_im, log_step, b_re, b_im, c_re, c_im, d_skip, glu_w, glu_b):
    step = jnp.exp(log_step.astype(F32))[:, None]
    ere, eim = lam_re * step, lam_im * step
    mag = jnp.exp(ere)
    lb_re, lb_im = mag * jnp.cos(eim), mag * jnp.sin(eim)
    den = lam_re * lam_re + lam_im * lam_im
    f_re = ((lb_re - 1.0) * lam_re + lb_im * lam_im) / den
    f_im = (lb_im * lam_re - (lb_re - 1.0) * lam_im) / den
    bb_re = f_re[..., None] * b_re - f_im[..., None] * b_im
    bb_im = f_re[..., None] * b_im + f_im[..., None] * b_re
    eye = jnp.eye(SSM_GROUPS, dtype=F32)
    bbd = jnp.concatenate(
        [jnp.einsum('gpc,gh->gchp', bb, eye).reshape(SSM_WIDTH, NSTATE) for bb in (bb_re, bb_im)], axis=1)
    cbd = jnp.concatenate(
        [jnp.einsum('gcp,gh->hpgc', cc, eye).reshape(NSTATE, SSM_WIDTH) for cc in (c_re, -c_im)], axis=0)
    flat = lambda a: a.reshape(1, NSTATE)
    lam = jnp.concatenate([flat(lb_re), flat(lb_im)], axis=1)

    def power(k):
        k = jnp.asarray(k, F32).reshape(-1, 1)
        m = jnp.exp(k * flat(ere))
        return jnp.concatenate([m * jnp.cos(k * flat(eim)), m * jnp.sin(k * flat(eim))], axis=1)

    return (bbd.astype(_BF), cbd.astype(_BF), lam, power(float(S5_SEG)),
            power(jnp.arange(1, S5_SEG + 1)), d_skip.reshape(1, SSM_WIDTH),
            glu_w.astype(_BF), glu_b.reshape(1, SSM_WIDTH))


def _dsa_kernel(iq_ref, iwt_ref, dq_ref, ik_ref, dk_ref, dvt_ref, o_ref,
                planes_ref, cand_ref, pick_ref, bias_ref, acc_ref,
                lg0_ref, lg1_ref, s0_ref, s1_ref, p0_ref, p1_ref, *, seq, topk):
    qi = pl.program_id(1)
    nk = (qi * QB) // KT + 1
    lane = lax.broadcasted_iota(jnp.int32, (1, QB), 1)
    vis_end = qi * QB + jnp.where(lane < CHUNK, CHUNK, 2 * CHUNK)

    iq = iq_ref[...]
    col = lax.broadcasted_iota(jnp.int32, (QB, LANES), 1)
    qs = jnp.concatenate([jnp.where(col // IDX_DIM == h, iq, 0.0) for h in range(IDX_HEADS)],
                         axis=0).astype(_BF)
    w = iwt_ref[...]

    nkp = nk + lax.rem(nk, 2)
    last_tile = nkp - 1

    def key_tile(ref, kt):
        return ref[pl.ds(pl.multiple_of(kt * KT, KT), KT), :]

    def tile_planes(src_ref):
        lg = src_ref[...]
        sc = jnp.zeros((KT, QB), F32)
        for h in range(IDX_HEADS):
            sc = sc + jnp.maximum(lg[:, h * QB:(h + 1) * QB], 0.0) * w[h:h + 1, :]
        sc = jnp.where(sc == 0.0, 0.0, sc)
        ukey = _to_ukey(sc)
        words = []
        for g in range(KT // GROUP):
            words += _transpose32([ukey[g * GROUP + m * SUBLANES:g * GROUP + (m + 1) * SUBLANES, :]
                                   for m in range(32)])
        return lax.bitcast_convert_type(jnp.concatenate(words, axis=0), jnp.int32)

    def score_pair(i, c):
        ik1 = key_tile(ik_ref, 2 * i + 1)
        ik2 = key_tile(ik_ref, jnp.minimum(2 * i + 2, last_tile))
        lg1_ref[...] = _dot_nt(ik1, qs)
        planes0 = tile_planes(lg0_ref)
        lg0_ref[...] = _dot_nt(ik2, qs)
        planes1 = tile_planes(lg1_ref)
        planes_ref[pl.ds(pl.multiple_of(i * (2 * KT), 2 * KT), 2 * KT), :] = jnp.concatenate(
            [planes0, planes1], axis=0)
        return c

    lg0_ref[...] = _dot_nt(key_tile(ik_ref, 0), qs)
    lax.fori_loop(0, nkp // 2, score_pair, 0)

    ngrp = nkp * (KT // GROUP)
    sub = lax.broadcasted_iota(jnp.int32, (SUBLANES, QB), 0)
    zero8 = jnp.zeros((SUBLANES, QB), jnp.int32)

    def rows_below(limit, g):
        nm = lax.shift_right_arithmetic(limit - g * GROUP + (SUBLANES - 1), 3)
        top = lax.shift_right_arithmetic(jnp.full((SUBLANES, QB), INT_MIN, jnp.int32),
                                         jnp.clip(nm, 1, 32) - 1)
        return jnp.where(nm <= 0, 0, top)

    def word(ref, g):
        return ref[pl.ds(pl.multiple_of(g * SUBLANES, SUBLANES), SUBLANES), :]

    def put_word(ref, g, v):
        ref[pl.ds(pl.multiple_of(g * SUBLANES, SUBLANES), SUBLANES), :] = v

    def plane(g, i):
        return planes_ref[pl.ds(pl.multiple_of(g * GROUP + i * SUBLANES, SUBLANES), SUBLANES), :]

    def sweep(step):
        def body(s, accs):
            return tuple(a + step(s * GROUPS_PER_STEP + t) for t, a in enumerate(accs))
        accs = lax.fori_loop(0, ngrp // GROUPS_PER_STEP, body, (zero8,) * GROUPS_PER_STEP)
        return functools.reduce(lambda a, b: a + b, accs).sum(axis=0, keepdims=True)

    def init_words(g):
        put_word(cand_ref, g, rows_below(vis_end - sub, g))
        put_word(pick_ref, g, zero8)
        return zero8

    sweep(init_words)

    def count_ones(i):
        return sweep(lambda g: lax.population_count(word(cand_ref, g) & plane(g, i)))

    def decide(c1, rem):
        take = c1 >= rem
        return jnp.where(take, 1, 0), jnp.where(take, rem, rem - c1)

    def narrow(g, i, take):
        take8 = jnp.broadcast_to(take, (SUBLANES, QB)) != 0
        c = word(cand_ref, g)
        ones = c & plane(g, i)
        put_word(pick_ref, g, word(pick_ref, g) | jnp.where(take8, 0, ones))
        c = jnp.where(take8, ones, c ^ ones)
        put_word(cand_ref, g, c)
        return c

    def radix_step(i, carry):
        take8, rem = carry

        def step(g):
            return lax.population_count(narrow(g, i - 1, take8) & plane(g, i))
        return decide(sweep(step), rem)

    take8, rem = decide(count_ones(0), jnp.full((1, QB), topk, jnp.int32))
    take8, rem = lax.fori_loop(1, 32, radix_step, (take8, rem))

    def last_narrow(g):
        narrow(g, 31, take8)
        return zero8

    sweep(last_narrow)

    def tie_step(i, edge):
        trial = edge + jnp.left_shift(jnp.int32(1), (seq.bit_length() - 1) - i)
        limit = trial - sub
        below = sweep(lambda g: lax.population_count(word(cand_ref, g) & rows_below(limit, g)))
        return jnp.where(below < rem, trial, edge)

    edge = lax.fori_loop(0, seq.bit_length(), tie_step, jnp.zeros((1, QB), jnp.int32))
    keep_limit = edge + 1 - sub

    def bias_group(g):
        chosen = word(pick_ref, g) | (word(cand_ref, g) & rows_below(keep_limit, g))
        base = pl.multiple_of(g * GROUP, GROUP)
        bias_ref[pl.ds(base, GROUP), :] = jnp.concatenate(
            [jnp.where(lax.shift_left(chosen, jnp.full_like(chosen, m)) < 0, 0.0, NEG_INF_SCORE)
             for m in range(32)], axis=0)
        return zero8

    sweep(bias_group)

    q = dq_ref[...]
    half = (col // DSA_HEAD_DIM)
    qst = jnp.concatenate(
        [jnp.where(half == (h % 2), q[:, (h // 2) * LANES:(h // 2 + 1) * LANES], jnp.zeros((), q.dtype))
         for h in range(DSA_HEADS)], axis=0)
    npair = DSA_HEADS // 2
    hq = DSA_HEADS * QB
    acc_ref[...] = jnp.zeros_like(acc_ref)
    p1_ref[...] = jnp.zeros_like(p1_ref)

    def qk(k_tile, b_tile, dst_ref):
        dst_ref[...] = _dot_nt(k_tile, qst) + jnp.concatenate([b_tile] * DSA_HEADS, axis=1)

    def numer(src_ref, dst_ref, m):
        m_new = []
        for hd in range(DSA_HEADS):
            cols = slice(hd * QB, (hd + 1) * QB)
            mh = jnp.maximum(m[:, cols], jnp.max(src_ref[:, cols], axis=0, keepdims=True))
            dst_ref[:, cols] = jnp.exp2(src_ref[:, cols] - mh).astype(dst_ref.dtype)
            m_new.append(mh)
        m_new = jnp.concatenate(m_new, axis=1)
        return m_new, jnp.exp2(m - m_new)

    def v_tile(kt):
        return dvt_ref[0:V_ROWS, pl.ds(pl.multiple_of(kt * KT, KT), KT)]

    def pv(vt, src_ref, alpha):
        acc_ref[...] = alpha * acc_ref[...] + _dot(vt, src_ref[...])

    def att_pair(i, c):
        m, alpha = c
        nxt = jnp.minimum(2 * i + 2, last_tile)
        k1, b1 = key_tile(dk_ref, 2 * i + 1), key_tile(bias_ref, 2 * i + 1)
        k2, b2 = key_tile(dk_ref, nxt), key_tile(bias_ref, nxt)
        v_prev, v_cur = v_tile(jnp.maximum(2 * i - 1, 0)), v_tile(2 * i)
        qk(k1, b1, s1_ref)
        pv(v_prev, p1_ref, alpha)
        m, alpha = numer(s0_ref, p0_ref, m)
        qk(k2, b2, s0_ref)
        pv(v_cur, p0_ref, alpha)
        m, alpha = numer(s1_ref, p1_ref, m)
        return m, alpha

    qk(key_tile(dk_ref, 0), key_tile(bias_ref, 0), s0_ref)
    c0 = (jnp.full((1, hq), -jnp.inf, F32), jnp.ones((1, hq), F32))
    _, alpha = lax.fori_loop(0, nkp // 2, att_pair, c0)
    pv(v_tile(last_tile), p1_ref, alpha)
    acc = acc_ref[...]
    o = acc[0:DSA_HEAD_DIM, :] / acc[DSA_HEAD_DIM:DSA_HEAD_DIM + 1, :]
    for j in range(npair):
        o_ref[:, j * LANES:(j + 1) * LANES] = jnp.concatenate(
            [o[:, (2 * j) * QB:(2 * j + 1) * QB], o[:, (2 * j + 1) * QB:(2 * j + 2) * QB]], axis=0).T


def _dsa(iq, iwt, dq, ik, dk, dvt):
    bsz, seq, _ = iq.shape
    topk = min(DSA_TOPK, seq // 4)
    grid = (bsz, seq // QB)
    qrow = lambda w: pl.BlockSpec((None, QB, w), lambda b, i: (b, i, 0))
    full = lambda r, c: pl.BlockSpec((None, r, c), lambda b, i: (b, 0, 0))
    return pl.pallas_call(
        functools.partial(_dsa_kernel, seq=seq, topk=topk), grid=grid,
        in_specs=[qrow(LANES), pl.BlockSpec((None, SUBLANES, QB), lambda b, i: (b, 0, i)), qrow(DSA_WIDTH),
                  full(seq, LANES), full(seq, LANES), full(LANES, seq)],
        out_specs=qrow(DSA_WIDTH),
        out_shape=jax.ShapeDtypeStruct((bsz, seq, DSA_WIDTH), F32),
        scratch_shapes=[pltpu.VMEM((seq, QB), jnp.int32),
                        pltpu.VMEM((seq // GROUP * SUBLANES, QB), jnp.int32),
                        pltpu.VMEM((seq // GROUP * SUBLANES, QB), jnp.int32),
                        pltpu.VMEM((seq, QB), F32),
                        pltpu.VMEM((V_ROWS, DSA_HEADS * QB), F32),
                        pltpu.VMEM((KT, IDX_HEADS * QB), F32), pltpu.VMEM((KT, IDX_HEADS * QB), F32),
                        pltpu.VMEM((KT, DSA_HEADS * QB), F32), pltpu.VMEM((KT, DSA_HEADS * QB), F32),
                        pltpu.VMEM((KT, DSA_HEADS * QB), _BF), pltpu.VMEM((KT, DSA_HEADS * QB), _BF)],
        compiler_params=_params(("parallel", "arbitrary")), name="dsa",
    )(iq, iwt, dq, ik, dk, dvt)


def _ret_kernel(rq_ref, rk_ref, rv_ref, rg_ref, qdec_ref, kdec_ref, dmask_ref, cdec_ref, blk_ref,
                hmq_ref, hmv_ref, ones_ref, ng_ref, o_ref, s_ref):
    @pl.when(pl.program_id(0) == 0)
    def _():
        s_ref[...] = jnp.zeros_like(s_ref)

    ones = ones_ref[...]

    def head_mean(z):
        return _dot(z.astype(_BF), ones) * (1.0 / RET_V_DIM)

    for b in range(rq_ref.shape[0]):
        q, k, v = rq_ref[b], rk_ref[b], rv_ref[b]
        state = s_ref[b]
        y = _dot((q * qdec_ref[...]).astype(_BF), state.astype(_BF))
        kd = (k * kdec_ref[...]).T.astype(_BF)
        vb = v.astype(_BF)
        s_ref[b] = state * cdec_ref[...] + _dot(kd, vb) * blk_ref[...]
        kb = k.astype(_BF)
        a = jnp.concatenate([(_dot_nt((q * hmq_ref[h:h + 1, :]).astype(_BF), kb) * dmask_ref[h]).astype(_BF)
                             for h in range(RET_HEADS)], axis=1)
        vh = jnp.concatenate([(v * hmv_ref[h:h + 1, :]).astype(_BF) for h in range(RET_HEADS)], axis=0)
        y = y + _dot(a, vh)
        d = y - head_mean(y)
        yn = d * lax.rsqrt(head_mean(d * d) + EPS) * ng_ref[...]
        o_ref[b] = jax.nn.silu(rg_ref[b]) * yn


def _ret_consts():
    c = RET_C
    log_g = jnp.log1p(-jnp.exp2(-5.0 - jnp.arange(RET_HEADS, dtype=F32)))
    pos = jnp.arange(c, dtype=F32)
    diff = pos[:, None] - pos[None, :]
    dmask = jnp.where(diff >= 0, jnp.exp(log_g[:, None, None] * jnp.maximum(diff, 0.0)), 0.0)
    lane_q = jnp.arange(RET_QK_PAD)
    head_q = jnp.where((lane_q % LANES) < RET_HEADS * (RET_QK_DIM // 2), (lane_q % LANES) // (RET_QK_DIM // 2), -1)
    head_v = jnp.arange(RET_WIDTH) // RET_V_DIM
    hmq = (head_q[None, :] == jnp.arange(RET_HEADS)[:, None]).astype(F32)
    hmv = (head_v[None, :] == jnp.arange(RET_HEADS)[:, None]).astype(F32)
    lg_q = hmq.T @ log_g
    qdec = jnp.exp(lg_q[None, :] * (pos[:, None] + 1.0))
    kdec = jnp.exp(lg_q[None, :] * (c - 1.0 - pos[:, None]))
    blk = hmq.T @ hmv
    cdec = blk * jnp.exp(lg_q * c)[:, None]
    ones = (hmv.T @ hmv).astype(_BF)
    return qdec, kdec, dmask, cdec, blk, hmq, hmv, ones


def _retention(rq, rk, rv, rg, norm_g):
    bsz, seq, _ = rq.shape
    consts = _ret_consts()
    grid = (seq // RET_C,)
    row = lambda w: pl.BlockSpec((bsz, RET_C, w), lambda i: (0, i, 0))
    ng = norm_g.reshape(1, RET_WIDTH)
    return pl.pallas_call(
        _ret_kernel, grid=grid,
        in_specs=[row(RET_QK_PAD), row(RET_QK_PAD), row(RET_WIDTH), row(RET_WIDTH)]
        + [_const_spec(a.shape) for a in consts] + [_const_spec(ng.shape)],
        out_specs=row(RET_WIDTH),
        out_shape=jax.ShapeDtypeStruct((bsz, seq, RET_WIDTH), F32),
        scratch_shapes=[pltpu.VMEM((bsz, RET_QK_PAD, RET_WIDTH), F32)],
        compiler_params=_params(("arbitrary",)), name="retention",
    )(rq, rk, rv, rg, *consts, ng)


def _merge_kernel(x_ref, g_ref, ya_ref, yb_ref, yc_ref, wg_ref, wa_ref, wb_ref, wc_ref, wo_ref, o_ref):
    x = x_ref[...]
    h = _rms(x, g_ref[...]).astype(_BF)
    merged = jnp.zeros(x.shape, F32)
    for n, (y_ref, w_ref) in enumerate(((ya_ref, wa_ref), (yb_ref, wb_ref), (yc_ref, wc_ref))):
        gate = jax.nn.sigmoid(_dot_nt(h, wg_ref[n * D_MODEL:(n + 1) * D_MODEL, :]))
        merged = merged + gate * _dot(y_ref[...].astype(_BF), w_ref[...])
    o_ref[...] = x + _dot(merged.astype(_BF), wo_ref[...])


def _merge(x, g, ya, yb, yc, wg, wa, wb, wc, wo):
    bsz, seq, _ = x.shape
    grid = (bsz, seq // TM_OUT)
    row = lambda w: pl.BlockSpec((None, TM_OUT, w), lambda b, i: (b, i, 0))
    return pl.pallas_call(
        _merge_kernel, grid=grid,
        in_specs=[row(D_MODEL), _const_spec(g.shape), row(SSM_WIDTH), row(DSA_WIDTH), row(RET_WIDTH)]
        + [_const_spec(w.shape) for w in (wg, wa, wb, wc, wo)],
        out_specs=row(D_MODEL),
        out_shape=jax.ShapeDtypeStruct(x.shape, F32),
        compiler_params=_params(("parallel", "parallel")), name="merge",
    )(x, g, ya, yb, yc, wg, wa, wb, wc, wo)


def _mlp_kernel(x_ref, g_ref, w1_ref, w2_ref, fg_ref, o_ref, *, final_norm):
    x = x_ref[...]
    h = _rms(x, g_ref[...]).astype(_BF)
    acc = x
    for f in range(D_FF // FF_CHUNK):
        t = jnp.maximum(_dot(h, w1_ref[:, f * FF_CHUNK:(f + 1) * FF_CHUNK]), 0.0)
        acc = acc + _dot((t * t).astype(_BF), w2_ref[f * FF_CHUNK:(f + 1) * FF_CHUNK, :])
    o_ref[...] = _rms(acc, fg_ref[...]) if final_norm else acc


def _mlp(x, g, w1, w2, fg, final_norm):
    bsz, seq, _ = x.shape
    grid = (bsz, seq // TM_OUT)
    row = pl.BlockSpec((None, TM_OUT, D_MODEL), lambda b, i: (b, i, 0))
    return pl.pallas_call(
        functools.partial(_mlp_kernel, final_norm=final_norm), grid=grid,
        in_specs=[row, _const_spec(g.shape), _const_spec(w1.shape), _const_spec(w2.shape), _const_spec(fg.shape)],
        out_specs=row,
        out_shape=jax.ShapeDtypeStruct(x.shape, F32),
        compiler_params=_params(("parallel", "parallel")), name="mlp",
    )(x, g, w1, w2, fg)


def _rope_tables(positions):
    pos = positions.astype(F32)[..., None]
    lane = jnp.arange(LANES)

    def angles(rot_dim, theta):
        half = rot_dim // 2
        inv = jnp.exp(-math.log(theta) * jnp.arange(half, dtype=F32) * (2.0 / rot_dim))
        return pos * inv

    def roll_tables(head_dim, rot_dim, theta):
        half = rot_dim // 2
        ang = angles(rot_dim, theta)
        ln = lane % head_dim
        rot = ln < rot_dim
        a = jnp.take(ang, ln % half, axis=-1)
        cos = jnp.where(rot, jnp.cos(a), 1.0)
        sin = jnp.where(rot, jnp.sin(a), 0.0)
        sgn = jnp.stack([jnp.where(ln < half, -1.0, 0.0),
                         jnp.where((ln >= half) & rot, 1.0, 0.0)]).astype(F32)
        return cos, sin, sgn

    cd, sd, sgd = roll_tables(DSA_HEAD_DIM, DSA_HEAD_DIM // 4, ROPE_THETA)
    ci, si, sgi = roll_tables(IDX_DIM, IDX_DIM // 4, ROPE_THETA)
    half = RET_QK_DIM // 2
    ang = angles(RET_QK_DIM, RET_THETA)
    a = jnp.take(ang, lane % half, axis=-1)
    used = lane < RET_HEADS * half
    cr = jnp.where(used, jnp.cos(a), 1.0)
    sr = jnp.where(used, jnp.sin(a), 0.0)
    return cd, sd, ci, si, cr, sr, sgd, sgi


def _inproj_weights(wt):
    sizes = (SSM_WIDTH, DSA_WIDTH, DSA_HEAD_DIM, DSA_HEAD_DIM, IDX_HEADS * IDX_DIM, IDX_DIM, IDX_HEADS,
             RET_HEADS * RET_QK_DIM, RET_HEADS * RET_QK_DIM, RET_WIDTH, RET_WIDTH, 3 * D_MODEL)
    parts, off = [], 0
    for n in sizes:
        parts.append(wt[off:off + n, :])
        off += n
    wu, wdq, wdk, wdv, wiq, wik, wiw, wrq, wrk, wrv, wrg, wgt = parts
    half = RET_QK_DIM // 2
    zrows = lambda n: jnp.zeros((n, D_MODEL), wt.dtype)

    def ret_split(a):
        a = a.reshape(RET_HEADS, 2, half, D_MODEL)
        pad = zrows(LANES - RET_HEADS * half)
        return jnp.concatenate([a[:, 0].reshape(-1, D_MODEL), pad, a[:, 1].reshape(-1, D_MODEL), pad], axis=0)

    ws = (wu, wdq, jnp.concatenate([wdk, wdk], axis=0), jnp.concatenate([wdv, zrows(LANES - DSA_HEAD_DIM)], axis=0),
          wiq, jnp.concatenate([wik] * IDX_HEADS, axis=0), jnp.concatenate([wiw, zrows(SUBLANES - IDX_HEADS)], axis=0),
          ret_split(wrq), ret_split(wrk), wrv, wrg)
    return tuple(a.astype(_BF) for a in ws), wgt.astype(_BF)


def kernel(x, positions, norm1_g, w_in, ssm_lambda_re, ssm_lambda_im, ssm_log_step, ssm_b_re, ssm_b_im,
           ssm_c_re, ssm_c_im, ssm_d, ssm_glu_w, ssm_glu_b, ret_norm_g, w_proj_a, w_proj_b, w_proj_c,
           w_out, norm2_g, w_ff1, w_ff2, final_norm_g):
    depth = w_in.shape[0]
    tabs = _rope_tables(positions)
    fg = final_norm_g.reshape(1, D_MODEL)
    w_in_t = jnp.transpose(w_in, (2, 0, 1))
    for l in range(depth):
        g1 = norm1_g[l].reshape(1, D_MODEL)
        ws, wgt = _inproj_weights(w_in_t[:, l, :])
        u, dq, dk, dvt, iq, ik, iwt, rq, rk, rv, rg = _inproj(x, g1, tabs, ws)

        s5p = _s5_params(ssm_lambda_re[l], ssm_lambda_im[l], ssm_log_step[l], ssm_b_re[l], ssm_b_im[l],
                         ssm_c_re[l], ssm_c_im[l], ssm_d[l], ssm_glu_w[l], ssm_glu_b[l])
        ya = _s5(u, s5p)
        yb = _dsa(iq, iwt, dq, ik, dk, dvt)
        yc = _retention(rq, rk, rv, rg, ret_norm_g[l])

        x = _merge(x, g1, ya, yb, yc, wgt, w_proj_a[l].astype(_BF), w_proj_b[l].astype(_BF),
                   w_proj_c[l].astype(_BF), w_out[l].astype(_BF))
        x = _mlp(x, norm2_g[l].reshape(1, D_MODEL), w_ff1[l].astype(_BF), w_ff2[l].astype(_BF), fg,
                 final_norm=(l == depth - 1))
    return x
```

```python
import functools
import math

import jax
import jax.numpy as jnp
from jax import lax
from jax.experimental import pallas as pl
from jax.experimental.pallas import tpu as pltpu

F32 = jnp.float32
_BF = jnp.bfloat16
LANES = 128
SUBLANES = 8
VMEM_LIMIT = 56 * 1024 * 1024

D_MODEL = 1024
CHUNK = 64
EPS = 1e-6
NEG_INF_SCORE = -1e30
INT_MIN = -2 ** 31
GROUP = 32 * SUBLANES
GROUPS_PER_STEP = 4
V_ROWS = 80

SSM_WIDTH = 256
SSM_GROUP = 16
SSM_GROUPS = 16
SSM_STATE = 64
NSTATE = SSM_GROUPS * SSM_STATE

DSA_HEADS = 6
DSA_HEAD_DIM = 64
DSA_WIDTH = 384
IDX_HEADS = 4
IDX_DIM = 32
DSA_TOPK = 256
ROPE_THETA = 500000.0

RET_HEADS = 4
RET_QK_DIM = 48
RET_V_DIM = 96
RET_WIDTH = 384
RET_THETA = 10000.0
RET_QK_PAD = 256

D_FF = 4096

TM_IN = 512
S5_TILE = 512
S5_SEG = S5_TILE // SUBLANES
QB = 128
KT = 512
RET_C = 128
TM_OUT = 512
FF_CHUNK = 1024


def _rms(x, g):
    return x * lax.rsqrt(jnp.mean(x * x, axis=-1, keepdims=True) + EPS) * g


def _dot(a, b):
    return jnp.dot(a, b, preferred_element_type=F32)


def _dot_nt(a, b):
    return lax.dot_general(a, b, (((1,), (1,)), ((), ())), preferred_element_type=F32)


def _to_ukey(v):
    bits = lax.bitcast_convert_type(v, jnp.int32)
    return lax.bitcast_convert_type(bits ^ ((bits >> 31) | INT_MIN), jnp.uint32)


def _transpose32(a):
    a = list(a)
    j, m = 16, 0x0000FFFF
    while j:
        k = 0
        while k < 32:
            t = (a[k] ^ (a[k + j] >> j)) & m
            a[k] = a[k] ^ t
            a[k + j] = a[k + j] ^ (t << j)
            k = (k + j + 1) & ~j
        j >>= 1
        m = (m ^ (m << j)) & 0xFFFFFFFF
    return a


def _const_spec(shape):
    nd = len(shape)
    return pl.BlockSpec(shape, lambda *_: (0,) * nd)


def _params(sem):
    return pltpu.CompilerParams(dimension_semantics=sem, vmem_limit_bytes=VMEM_LIMIT)


def _rope_roll(z, cos, sin, sgn, shift):
    outs = []
    for c in range(z.shape[1] // LANES):
        zc = z[:, c * LANES:(c + 1) * LANES]
        rot = (pltpu.roll(zc, LANES - shift, 1) * sgn[0:1, :]
               + pltpu.roll(zc, shift, 1) * sgn[1:2, :])
        outs.append(zc * cos + rot * sin)
    return outs[0] if len(outs) == 1 else jnp.concatenate(outs, axis=1)


def _inproj_kernel(x_ref, g_ref, cd_ref, sd_ref, ci_ref, si_ref, cr_ref, sr_ref, sgd_ref, sgi_ref,
                   wu_ref, wdq_ref, wdk_ref, wdvt_ref, wiq_ref, wik_ref, wiwt_ref,
                   wrq_ref, wrk_ref, wrv_ref, wrg_ref,
                   u_ref, dq_ref, dk_ref, dvt_ref, iq_ref, ik_ref, iwt_ref,
                   rq_ref, rk_ref, rv_ref, rg_ref):
    h = _rms(x_ref[...], g_ref[...]).astype(_BF)

    def proj(w_ref):
        return _dot_nt(h, w_ref[...])

    u_ref[...] = proj(wu_ref)

    cd, sd, sgd = cd_ref[...], sd_ref[...], sgd_ref[...]
    dq = _rope_roll(proj(wdq_ref), cd, sd, sgd, 8)
    dq_ref[...] = (dq * (DSA_HEAD_DIM ** -0.5 * math.log2(math.e))).astype(dq_ref.dtype)
    dk_ref[...] = _rope_roll(proj(wdk_ref), cd, sd, sgd, 8).astype(dk_ref.dtype)
    vt = _dot_nt(wdvt_ref[...], h)
    ones_row = lax.broadcasted_iota(jnp.int32, vt.shape, 0) == DSA_HEAD_DIM
    dvt_ref[...] = jnp.where(ones_row, 1.0, vt).astype(dvt_ref.dtype)

    ci, si, sgi = ci_ref[...], si_ref[...], sgi_ref[...]
    iq_ref[...] = _rope_roll(proj(wiq_ref), ci, si, sgi, 4) * (IDX_DIM ** -0.5)
    ik_ref[...] = _rope_roll(proj(wik_ref), ci, si, sgi, 4).astype(ik_ref.dtype)
    iwt_ref[...] = _dot_nt(wiwt_ref[...], h) * (IDX_HEADS ** -0.5)

    cr, sr = cr_ref[...], sr_ref[...]

    def rope_split(z):
        x1, x2 = z[:, :LANES], z[:, LANES:]
        return jnp.concatenate([x1 * cr - x2 * sr, x2 * cr + x1 * sr], axis=1)

    rq_ref[...] = rope_split(proj(wrq_ref))
    rk_ref[...] = rope_split(proj(wrk_ref)) * (RET_QK_DIM ** -0.5)
    rv_ref[...] = proj(wrv_ref)
    rg_ref[...] = proj(wrg_ref)


def _inproj(x, g, tabs, ws):
    bsz, seq, _ = x.shape
    tm = TM_IN
    grid = (bsz, seq // tm)
    row = lambda w: pl.BlockSpec((None, tm, w), lambda b, i: (b, i, 0))
    colT = lambda r: pl.BlockSpec((None, r, tm), lambda b, i: (b, 0, i))
    in_specs = ([row(D_MODEL), _const_spec(g.shape)] + [row(LANES)] * 6
                + [_const_spec(t.shape) for t in tabs[6:]] + [_const_spec(w.shape) for w in ws])
    out_shape = [
        jax.ShapeDtypeStruct((bsz, seq, SSM_WIDTH), F32),
        jax.ShapeDtypeStruct((bsz, seq, DSA_WIDTH), _BF),
        jax.ShapeDtypeStruct((bsz, seq, LANES), _BF),
        jax.ShapeDtypeStruct((bsz, LANES, seq), _BF),
        jax.ShapeDtypeStruct((bsz, seq, LANES), F32),
        jax.ShapeDtypeStruct((bsz, seq, LANES), _BF),
        jax.ShapeDtypeStruct((bsz, SUBLANES, seq), F32),
        jax.ShapeDtypeStruct((bsz, seq, RET_QK_PAD), F32),
        jax.ShapeDtypeStruct((bsz, seq, RET_QK_PAD), F32),
        jax.ShapeDtypeStruct((bsz, seq, RET_WIDTH), F32),
        jax.ShapeDtypeStruct((bsz, seq, RET_WIDTH), F32),
    ]
    out_specs = [row(SSM_WIDTH), row(DSA_WIDTH), row(LANES), colT(LANES), row(LANES), row(LANES),
                 colT(SUBLANES), row(RET_QK_PAD), row(RET_QK_PAD), row(RET_WIDTH), row(RET_WIDTH)]
    return pl.pallas_call(
        _inproj_kernel, grid=grid, in_specs=in_specs, out_specs=out_specs, out_shape=out_shape,
        compiler_params=_params(("parallel", "parallel")), name="inproj",
    )(x, g, *tabs, *ws)


def _s5_kernel(u_ref, bbd_ref, cbd_ref, lam_ref, lam64_ref, pow_ref, d_ref, gw_ref, gb_ref,
               o_ref, st_ref, xm_ref, carry_ref, up_ref, yp_ref):
    n = NSTATE

    @pl.when(pl.program_id(1) == 0)
    def _():
        carry_ref[...] = jnp.zeros_like(carry_ref)

    nhalf = SSM_WIDTH // LANES
    for i in range(SUBLANES):
        for c in range(nhalf):
            up_ref[c, pl.ds(i, S5_SEG, stride=SUBLANES), :] = u_ref[i * S5_SEG:(i + 1) * S5_SEG,
                                                                    c * LANES:(c + 1) * LANES]
    u = jnp.concatenate([up_ref[c] for c in range(nhalf)], axis=1)
    st_ref[...] = _dot(u.astype(_BF), bbd_ref[...])

    lr = jnp.broadcast_to(lam_ref[:, :n], (SUBLANES, n))
    li = jnp.broadcast_to(lam_ref[:, n:], (SUBLANES, n))

    def step(j, c):
        xr, xi = c
        off = pl.multiple_of(j * SUBLANES, SUBLANES)
        ar = st_ref[pl.ds(off, SUBLANES), :n]
        ai = st_ref[pl.ds(off, SUBLANES), n:]
        nr = lr * xr - li * xi + ar
        ni = lr * xi + li * xr + ai
        st_ref[pl.ds(off, SUBLANES), :n] = nr
        st_ref[pl.ds(off, SUBLANES), n:] = ni
        return nr, ni

    zero = jnp.zeros((SUBLANES, n), F32)
    xr, xi = lax.fori_loop(0, S5_SEG, step, (zero, zero))

    l64r, l64i = lam64_ref[:, :n], lam64_ref[:, n:]
    cr, ci = carry_ref[:, :n], carry_ref[:, n:]
    for i in range(SUBLANES):
        xm_ref[i:i + 1, :n] = cr
        xm_ref[i:i + 1, n:] = ci
        er, ei = xr[i:i + 1, :], xi[i:i + 1, :]
        cr, ci = er + l64r * cr - l64i * ci, ei + l64r * ci + l64i * cr
    carry_ref[:, :n] = cr
    carry_ref[:, n:] = ci

    hr, hi = xm_ref[:, :n], xm_ref[:, n:]

    def fix(j, _):
        off = pl.multiple_of(j * SUBLANES, SUBLANES)
        pr = pow_ref[pl.ds(j, 1), :n]
        pi = pow_ref[pl.ds(j, 1), n:]
        st_ref[pl.ds(off, SUBLANES), :n] = st_ref[pl.ds(off, SUBLANES), :n] + pr * hr - pi * hi
        st_ref[pl.ds(off, SUBLANES), n:] = st_ref[pl.ds(off, SUBLANES), n:] + pr * hi + pi * hr
        return 0

    lax.fori_loop(0, S5_SEG, fix, 0)

    y = d_ref[...] * u
    kc = 512
    for k in range(2 * n // kc):
        y = y + _dot(st_ref[:, k * kc:(k + 1) * kc].astype(_BF), cbd_ref[k * kc:(k + 1) * kc, :])
    y = jax.nn.gelu(y)
    y = y * jax.nn.sigmoid(_dot(y.astype(_BF), gw_ref[...]) + gb_ref[...])
    for c in range(nhalf):
        yp_ref[c] = y[:, c * LANES:(c + 1) * LANES]
    for i in range(SUBLANES):
        for c in range(nhalf):
            o_ref[i * S5_SEG:(i + 1) * S5_SEG, c * LANES:(c + 1) * LANES] = yp_ref[
                c, pl.ds(i, S5_SEG, stride=SUBLANES), :]


def _s5(u_perm, prm):
    bsz, seq, _ = u_perm.shape
    grid = (bsz, seq // S5_TILE)
    row = pl.BlockSpec((None, S5_TILE, SSM_WIDTH), lambda b, i: (b, i, 0))
    return pl.pallas_call(
        _s5_kernel, grid=grid,
        in_specs=[row] + [_const_spec(p.shape) for p in prm],
        out_specs=row,
        out_shape=jax.ShapeDtypeStruct((bsz, seq, SSM_WIDTH), F32),
        scratch_shapes=[pltpu.VMEM((S5_TILE, 2 * NSTATE), F32),
                        pltpu.VMEM((SUBLANES, 2 * NSTATE), F32),
                        pltpu.VMEM((1, 2 * NSTATE), F32),
                        pltpu.VMEM((SSM_WIDTH // LANES, S5_TILE, LANES), F32),
                        pltpu.VMEM((SSM_WIDTH // LANES, S5_TILE, LANES), F32)],
        compiler_params=_params(("arbitrary", "arbitrary")), name="s5",
    )(u_perm, *prm)


def _s5_params(lam_re, lam_im, log_step, b_re, b_im, c_re, c_im, d_skip, glu_w, glu_b):
    step = jnp.exp(log_step.astype(F32))[:, None]
    ere, eim = lam_re * step, lam_im * step
    mag = jnp.exp(ere)
    lb_re, lb_im = mag * jnp.cos(eim), mag * jnp.sin(eim)
    den = lam_re * lam_re + lam_im * lam_im
    f_re = ((lb_re - 1.0) * lam_re + lb_im * lam_im) / den
    f_im = (lb_im * lam_re - (lb_re - 1.0) * lam_im) / den
    bb_re = f_re[..., None] * b_re - f_im[..., None] * b_im
    bb_im = f_re[..., None] * b_im + f_im[..., None] * b_re
    eye = jnp.eye(SSM_GROUPS, dtype=F32)
    bbd = jnp.concatenate(
        [jnp.einsum('gpc,gh->gchp', bb, eye).reshape(SSM_WIDTH, NSTATE) for bb in (bb_re, bb_im)], axis=1)
    cbd = jnp.concatenate(
        [jnp.einsum('gcp,gh->hpgc', cc, eye).reshape(NSTATE, SSM_WIDTH) for cc in (c_re, -c_im)], axis=0)
    flat = lambda a: a.reshape(1, NSTATE)
    lam = jnp.concatenate([flat(lb_re), flat(lb_im)], axis=1)

    def power(k):
        k = jnp.asarray(k, F32).reshape(-1, 1)
        m = jnp.exp(k * flat(ere))
        return jnp.concatenate([m * jnp.cos(k * flat(eim)), m * jnp.sin(k * flat(eim))], axis=1)

    return (bbd.astype(_BF), cbd.astype(_BF), lam, power(float(S5_SEG)),
            power(jnp.arange(1, S5_SEG + 1)), d_skip.reshape(1, SSM_WIDTH),
            glu_w.astype(_BF), glu_b.reshape(1, SSM_WIDTH))


def _dsa_kernel(iq_ref, iwt_ref, dq_ref, ik_ref, dk_ref, dvt_ref, o_ref,
                planes_ref, cand_ref, pick_ref, bias_ref, acc_ref,
                lg0_ref, lg1_ref, s0_ref, s1_ref, p0_ref, p1_ref, *, seq, topk):
    qi = pl.program_id(1)
    nk = (qi * QB) // KT + 1
    lane = lax.broadcasted_iota(jnp.int32, (1, QB), 1)
    vis_end = qi * QB + jnp.where(lane < CHUNK, CHUNK, 2 * CHUNK)

    iq = iq_ref[...]
    col = lax.broadcasted_iota(jnp.int32, (QB, LANES), 1)
    qs = jnp.concatenate([jnp.where(col // IDX_DIM == h, iq, 0.0) for h in range(IDX_HEADS)],
                         axis=0).astype(_BF)
    w = iwt_ref[...]

    nkp = nk + lax.rem(nk, 2)
    last_tile = nkp - 1

    def key_tile(ref, kt):
        return ref[pl.ds(pl.multiple_of(kt * KT, KT), KT), :]

    def tile_planes(src_ref):
        lg = src_ref[...]
        sc = jnp.zeros((KT, QB), F32)
        for h in range(IDX_HEADS):
            sc = sc + jnp.maximum(lg[:, h * QB:(h + 1) * QB], 0.0) * w[h:h + 1, :]
        sc = jnp.where(sc == 0.0, 0.0, sc)
        ukey = _to_ukey(sc)
        words = []
        for g in range(KT // GROUP):
            words += _transpose32([ukey[g * GROUP + m * SUBLANES:g * GROUP + (m + 1) * SUBLANES, :]
                                   for m in range(32)])
        return lax.bitcast_convert_type(jnp.concatenate(words, axis=0), jnp.int32)

    def score_pair(i, c):
        ik1 = key_tile(ik_ref, 2 * i + 1)
        ik2 = key_tile(ik_ref, jnp.minimum(2 * i + 2, last_tile))
        lg1_ref[...] = _dot_nt(ik1, qs)
        planes0 = tile_planes(lg0_ref)
        lg0_ref[...] = _dot_nt(ik2, qs)
        planes1 = tile_planes(lg1_ref)
        planes_ref[pl.ds(pl.multiple_of(i * (2 * KT), 2 * KT), 2 * KT), :] = jnp.concatenate(
            [planes0, planes1], axis=0)
        return c

    lg0_ref[...] = _dot_nt(key_tile(ik_ref, 0), qs)
    lax.fori_loop(0, nkp // 2, score_pair, 0)

    ngrp = nkp * (KT // GROUP)
    sub = lax.broadcasted_iota(jnp.int32, (SUBLANES, QB), 0)
    zero8 = jnp.zeros((SUBLANES, QB), jnp.int32)

    def rows_below(limit, g):
        nm = lax.shift_right_arithmetic(limit - g * GROUP + (SUBLANES - 1), 3)
        top = lax.shift_right_arithmetic(jnp.full((SUBLANES, QB), INT_MIN, jnp.int32),
                                         jnp.clip(nm, 1, 32) - 1)
        return jnp.where(nm <= 0, 0, top)

    def word(ref, g):
        return ref[pl.ds(pl.multiple_of(g * SUBLANES, SUBLANES), SUBLANES), :]

    def put_word(ref, g, v):
        ref[pl.ds(pl.multiple_of(g * SUBLANES, SUBLANES), SUBLANES), :] = v

    def plane(g, i):
        return planes_ref[pl.ds(pl.multiple_of(g * GROUP + i * SUBLANES, SUBLANES), SUBLANES), :]

    def sweep_words(step):
        def body(s, accs):
            return tuple(a + step(s * GROUPS_PER_STEP + t) for t, a in enumerate(accs))
        accs = lax.fori_loop(0, ngrp // GROUPS_PER_STEP, body, (zero8,) * GROUPS_PER_STEP)
        return functools.reduce(lambda a, b: a + b, accs)

    def sweep(step):
        return sweep_words(step).sum(axis=0, keepdims=True)

    def init_words(g):
        put_word(cand_ref, g, rows_below(vis_end - sub, g))
        put_word(pick_ref, g, zero8)
        return zero8

    sweep(init_words)

    def count_ones(i):
        return sweep(lambda g: lax.population_count(word(cand_ref, g) & plane(g, i)))

    def decide(c1, rem):
        take = c1 >= rem
        return jnp.where(take, 1, 0), jnp.where(take, rem, rem - c1)

    def narrow(g, i, take):
        take8 = jnp.broadcast_to(take, (SUBLANES, QB)) != 0
        c = word(cand_ref, g)
        ones = c & plane(g, i)
        put_word(pick_ref, g, word(pick_ref, g) | jnp.where(take8, 0, ones))
        c = jnp.where(take8, ones, c ^ ones)
        put_word(cand_ref, g, c)
        return c

    def radix_step(i, carry):
        take8, rem = carry

        def step(g):
            return lax.population_count(narrow(g, i - 1, take8) & plane(g, i))
        return decide(sweep(step), rem)

    take8, rem = decide(count_ones(0), jnp.full((1, QB), topk, jnp.int32))
    take8, rem = lax.fori_loop(1, 32, radix_step, (take8, rem))

    def last_narrow(g):
        narrow(g, 31, take8)
        return zero8

    sweep(last_narrow)

    def group_step(i, carry):
        grp, before = carry
        trial = grp + jnp.left_shift(jnp.int32(1), ((seq // GROUP).bit_length() - 2) - i)
        below = sweep(lambda g: jnp.where(jnp.broadcast_to(g < trial, (SUBLANES, QB)),
                                          lax.population_count(word(cand_ref, g)), 0))
        ok = below < rem
        return jnp.where(ok, trial, grp), jnp.where(ok, below, before)

    zrow = jnp.zeros((1, QB), jnp.int32)
    grp, before = lax.fori_loop(0, (seq // GROUP).bit_length() - 1, group_step, (zrow, zrow))
    tied = sweep_words(lambda g: jnp.where(jnp.broadcast_to(g == grp, (SUBLANES, QB)), word(cand_ref, g), 0))

    def row_step(i, edge):
        trial = edge + jnp.left_shift(jnp.int32(1), (GROUP.bit_length() - 2) - i)
        below = lax.population_count(tied & rows_below(trial - sub, 0)).sum(axis=0, keepdims=True)
        return jnp.where(before + below < rem, trial, edge)

    edge = lax.fori_loop(0, GROUP.bit_length() - 1, row_step, zrow)
    keep_limit = grp * GROUP + edge + 1 - sub

    def bias_group(g):
        chosen = word(pick_ref, g) | (word(cand_ref, g) & rows_below(keep_limit, g))
        base = pl.multiple_of(g * GROUP, GROUP)
        bias_ref[pl.ds(base, GROUP), :] = jnp.concatenate(
            [jnp.where(lax.shift_left(chosen, jnp.full_like(chosen, m)) < 0, 0.0, NEG_INF_SCORE)
             for m in range(32)], axis=0)
        return zero8

    sweep(bias_group)

    q = dq_ref[...]
    half = (col // DSA_HEAD_DIM)
    qst = jnp.concatenate(
        [jnp.where(half == (h % 2), q[:, (h // 2) * LANES:(h // 2 + 1) * LANES], jnp.zeros((), q.dtype))
         for h in range(DSA_HEADS)], axis=0)
    npair = DSA_HEADS // 2
    hq = DSA_HEADS * QB
    acc_ref[...] = jnp.zeros_like(acc_ref)
    p1_ref[...] = jnp.zeros_like(p1_ref)

    def qk(k_tile, b_tile, dst_ref):
        dst_ref[...] = _dot_nt(k_tile, qst) + jnp.concatenate([b_tile] * DSA_HEADS, axis=1)

    def numer(src_ref, dst_ref, m):
        m_new = []
        for hd in range(DSA_HEADS):
            cols = slice(hd * QB, (hd + 1) * QB)
            mh = jnp.maximum(m[:, cols], jnp.max(src_ref[:, cols], axis=0, keepdims=True))
            dst_ref[:, cols] = jnp.exp2(src_ref[:, cols] - mh).astype(dst_ref.dtype)
            m_new.append(mh)
        m_new = jnp.concatenate(m_new, axis=1)
        return m_new, jnp.exp2(m - m_new)

    def v_tile(kt):
        return dvt_ref[0:V_ROWS, pl.ds(pl.multiple_of(kt * KT, KT), KT)]

    def pv(vt, src_ref, alpha):
        acc_ref[...] = alpha * acc_ref[...] + _dot(vt, src_ref[...])

    def att_pair(i, c):
        m, alpha = c
        nxt = jnp.minimum(2 * i + 2, last_tile)
        k1, b1 = key_tile(dk_ref, 2 * i + 1), key_tile(bias_ref, 2 * i + 1)
        k2, b2 = key_tile(dk_ref, nxt), key_tile(bias_ref, nxt)
        v_prev, v_cur = v_tile(jnp.maximum(2 * i - 1, 0)), v_tile(2 * i)
        qk(k1, b1, s1_ref)
        pv(v_prev, p1_ref, alpha)
        m, alpha = numer(s0_ref, p0_ref, m)
        qk(k2, b2, s0_ref)
        pv(v_cur, p0_ref, alpha)
        m, alpha = numer(s1_ref, p1_ref, m)
        return m, alpha

    qk(key_tile(dk_ref, 0), key_tile(bias_ref, 0), s0_ref)
    c0 = (jnp.full((1, hq), -jnp.inf, F32), jnp.ones((1, hq), F32))
    _, alpha = lax.fori_loop(0, nkp // 2, att_pair, c0)
    pv(v_tile(last_tile), p1_ref, alpha)
    acc = acc_ref[...]
    o = acc[0:DSA_HEAD_DIM, :] / acc[DSA_HEAD_DIM:DSA_HEAD_DIM + 1, :]
    for j in range(npair):
        o_ref[:, j * LANES:(j + 1) * LANES] = jnp.concatenate(
            [o[:, (2 * j) * QB:(2 * j + 1) * QB], o[:, (2 * j + 1) * QB:(2 * j + 2) * QB]], axis=0).T


def _dsa(iq, iwt, dq, ik, dk, dvt):
    bsz, seq, _ = iq.shape
    topk = min(DSA_TOPK, seq // 4)
    grid = (bsz, seq // QB)
    qrow = lambda w: pl.BlockSpec((None, QB, w), lambda b, i: (b, i, 0))
    full = lambda r, c: pl.BlockSpec((None, r, c), lambda b, i: (b, 0, 0))
    return pl.pallas_call(
        functools.partial(_dsa_kernel, seq=seq, topk=topk), grid=grid,
        in_specs=[qrow(LANES), pl.BlockSpec((None, SUBLANES, QB), lambda b, i: (b, 0, i)), qrow(DSA_WIDTH),
                  full(seq, LANES), full(seq, LANES), full(LANES, seq)],
        out_specs=qrow(DSA_WIDTH),
        out_shape=jax.ShapeDtypeStruct((bsz, seq, DSA_WIDTH), F32),
        scratch_shapes=[pltpu.VMEM((seq, QB), jnp.int32),
                        pltpu.VMEM((seq // GROUP * SUBLANES, QB), jnp.int32),
                        pltpu.VMEM((seq // GROUP * SUBLANES, QB), jnp.int32),
                        pltpu.VMEM((seq, QB), F32),
                        pltpu.VMEM((V_ROWS, DSA_HEADS * QB), F32),
                        pltpu.VMEM((KT, IDX_HEADS * QB), F32), pltpu.VMEM((KT, IDX_HEADS * QB), F32),
                        pltpu.VMEM((KT, DSA_HEADS * QB), F32), pltpu.VMEM((KT, DSA_HEADS * QB), F32),
                        pltpu.VMEM((KT, DSA_HEADS * QB), _BF), pltpu.VMEM((KT, DSA_HEADS * QB), _BF)],
        compiler_params=_params(("parallel", "arbitrary")), name="dsa",
    )(iq, iwt, dq, ik, dk, dvt)


def _ret_kernel(rq_ref, rk_ref, rv_ref, rg_ref, qdec_ref, kdec_ref, dmask_ref, cdec_ref, blk_ref,
                hmq_ref, hmv_ref, ones_ref, ng_ref, o_ref, s_ref):
    @pl.when(pl.program_id(0) == 0)
    def _():
        s_ref[...] = jnp.zeros_like(s_ref)

    ones = ones_ref[...]

    def head_mean(z):
        return _dot(z.astype(_BF), ones) * (1.0 / RET_V_DIM)

    for b in range(rq_ref.shape[0]):
        q, k, v = rq_ref[b], rk_ref[b], rv_ref[b]
        state = s_ref[b]
        y = _dot((q * qdec_ref[...]).astype(_BF), state.astype(_BF))
        kd = (k * kdec_ref[...]).T.astype(_BF)
        vb = v.astype(_BF)
        s_ref[b] = state * cdec_ref[...] + _dot(kd, vb) * blk_ref[...]
        kb = k.astype(_BF)
        a = jnp.concatenate([(_dot_nt((q * hmq_ref[h:h + 1, :]).astype(_BF), kb) * dmask_ref[h]).astype(_BF)
                             for h in range(RET_HEADS)], axis=1)
        vh = jnp.concatenate([(v * hmv_ref[h:h + 1, :]).astype(_BF) for h in range(RET_HEADS)], axis=0)
        y = y + _dot(a, vh)
        d = y - head_mean(y)
        yn = d * lax.rsqrt(head_mean(d * d) + EPS) * ng_ref[...]
        o_ref[b] = jax.nn.silu(rg_ref[b]) * yn


def _ret_consts():
    c = RET_C
    log_g = jnp.log1p(-jnp.exp2(-5.0 - jnp.arange(RET_HEADS, dtype=F32)))
    pos = jnp.arange(c, dtype=F32)
    diff = pos[:, None] - pos[None, :]
    dmask = jnp.where(diff >= 0, jnp.exp(log_g[:, None, None] * jnp.maximum(diff, 0.0)), 0.0)
    lane_q = jnp.arange(RET_QK_PAD)
    head_q = jnp.where((lane_q % LANES) < RET_HEADS * (RET_QK_DIM // 2), (lane_q % LANES) // (RET_QK_DIM // 2), -1)
    head_v = jnp.arange(RET_WIDTH) // RET_V_DIM
    hmq = (head_q[None, :] == jnp.arange(RET_HEADS)[:, None]).astype(F32)
    hmv = (head_v[None, :] == jnp.arange(RET_HEADS)[:, None]).astype(F32)
    lg_q = hmq.T @ log_g
    qdec = jnp.exp(lg_q[None, :] * (pos[:, None] + 1.0))
    kdec = jnp.exp(lg_q[None, :] * (c - 1.0 - pos[:, None]))
    blk = hmq.T @ hmv
    cdec = blk * jnp.exp(lg_q * c)[:, None]
    ones = (hmv.T @ hmv).astype(_BF)
    return qdec, kdec, dmask, cdec, blk, hmq, hmv, ones


def _retention(rq, rk, rv, rg, norm_g):
    bsz, seq, _ = rq.shape
    consts = _ret_consts()
    grid = (seq // RET_C,)
    row = lambda w: pl.BlockSpec((bsz, RET_C, w), lambda i: (0, i, 0))
    ng = norm_g.reshape(1, RET_WIDTH)
    return pl.pallas_call(
        _ret_kernel, grid=grid,
        in_specs=[row(RET_QK_PAD), row(RET_QK_PAD), row(RET_WIDTH), row(RET_WIDTH)]
        + [_const_spec(a.shape) for a in consts] + [_const_spec(ng.shape)],
        out_specs=row(RET_WIDTH),
        out_shape=jax.ShapeDtypeStruct((bsz, seq, RET_WIDTH), F32),
        scratch_shapes=[pltpu.VMEM((bsz, RET_QK_PAD, RET_WIDTH), F32)],
        compiler_params=_params(("arbitrary",)), name="retention",
    )(rq, rk, rv, rg, *consts, ng)


def _merge_kernel(x_ref, g_ref, ya_ref, yb_ref, yc_ref, wg_ref, wa_ref, wb_ref, wc_ref, wo_ref, o_ref):
    x = x_ref[...]
    h = _rms(x, g_ref[...]).astype(_BF)
    merged = jnp.zeros(x.shape, F32)
    for n, (y_ref, w_ref) in enumerate(((ya_ref, wa_ref), (yb_ref, wb_ref), (yc_ref, wc_ref))):
        gate = jax.nn.sigmoid(_dot_nt(h, wg_ref[n * D_MODEL:(n + 1) * D_MODEL, :]))
        merged = merged + gate * _dot(y_ref[...].astype(_BF), w_ref[...])
    o_ref[...] = x + _dot(merged.astype(_BF), wo_ref[...])


def _merge(x, g, ya, yb, yc, wg, wa, wb, wc, wo):
    bsz, seq, _ = x.shape
    grid = (bsz, seq // TM_OUT)
    row = lambda w: pl.BlockSpec((None, TM_OUT, w), lambda b, i: (b, i, 0))
    return pl.pallas_call(
        _merge_kernel, grid=grid,
        in_specs=[row(D_MODEL), _const_spec(g.shape), row(SSM_WIDTH), row(DSA_WIDTH), row(RET_WIDTH)]
        + [_const_spec(w.shape) for w in (wg, wa, wb, wc, wo)],
        out_specs=row(D_MODEL),
        out_shape=jax.ShapeDtypeStruct(x.shape, F32),
        compiler_params=_params(("parallel", "parallel")), name="merge",
    )(x, g, ya, yb, yc, wg, wa, wb, wc, wo)


def _mlp_kernel(x_ref, g_ref, w1_ref, w2_ref, fg_ref, o_ref, *, final_norm):
    x = x_ref[...]
    h = _rms(x, g_ref[...]).astype(_BF)
    acc = x
    for f in range(D_FF // FF_CHUNK):
        t = jnp.maximum(_dot(h, w1_ref[:, f * FF_CHUNK:(f + 1) * FF_CHUNK]), 0.0)
        acc = acc + _dot((t * t).astype(_BF), w2_ref[f * FF_CHUNK:(f + 1) * FF_CHUNK, :])
    o_ref[...] = _rms(acc, fg_ref[...]) if final_norm else acc


def _mlp(x, g, w1, w2, fg, final_norm):
    bsz, seq, _ = x.shape
    grid = (bsz, seq // TM_OUT)
    row = pl.BlockSpec((None, TM_OUT, D_MODEL), lambda b, i: (b, i, 0))
    return pl.pallas_call(
        functools.partial(_mlp_kernel, final_norm=final_norm), grid=grid,
        in_specs=[row, _const_spec(g.shape), _const_spec(w1.shape), _const_spec(w2.shape), _const_spec(fg.shape)],
        out_specs=row,
        out_shape=jax.ShapeDtypeStruct(x.shape, F32),
        compiler_params=_params(("parallel", "parallel")), name="mlp",
    )(x, g, w1, w2, fg)


def _rope_tables(positions):
    pos = positions.astype(F32)[..., None]
    lane = jnp.arange(LANES)

    def angles(rot_dim, theta):
        half = rot_dim // 2
        inv = jnp.exp(-math.log(theta) * jnp.arange(half, dtype=F32) * (2.0 / rot_dim))
        return pos * inv

    def roll_tables(head_dim, rot_dim, theta):
        half = rot_dim // 2
        ang = angles(rot_dim, theta)
        ln = lane % head_dim
        rot = ln < rot_dim
        a = jnp.take(ang, ln % half, axis=-1)
        cos = jnp.where(rot, jnp.cos(a), 1.0)
        sin = jnp.where(rot, jnp.sin(a), 0.0)
        sgn = jnp.stack([jnp.where(ln < half, -1.0, 0.0),
                         jnp.where((ln >= half) & rot, 1.0, 0.0)]).astype(F32)
        return cos, sin, sgn

    cd, sd, sgd = roll_tables(DSA_HEAD_DIM, DSA_HEAD_DIM // 4, ROPE_THETA)
    ci, si, sgi = roll_tables(IDX_DIM, IDX_DIM // 4, ROPE_THETA)
    half = RET_QK_DIM // 2
    ang = angles(RET_QK_DIM, RET_THETA)
    a = jnp.take(ang, lane % half, axis=-1)
    used = lane < RET_HEADS * half
    cr = jnp.where(used, jnp.cos(a), 1.0)
    sr = jnp.where(used, jnp.sin(a), 0.0)
    return cd, sd, ci, si, cr, sr, sgd, sgi


def _inproj_weights(wt):
    sizes = (SSM_WIDTH, DSA_WIDTH, DSA_HEAD_DIM, DSA_HEAD_DIM, IDX_HEADS * IDX_DIM, IDX_DIM, IDX_HEADS,
             RET_HEADS * RET_QK_DIM, RET_HEADS * RET_QK_DIM, RET_WIDTH, RET_WIDTH, 3 * D_MODEL)
    parts, off = [], 0
    for n in sizes:
        parts.append(wt[off:off + n, :])
        off += n
    wu, wdq, wdk, wdv, wiq, wik, wiw, wrq, wrk, wrv, wrg, wgt = parts
    half = RET_QK_DIM // 2
    zrows = lambda n: jnp.zeros((n, D_MODEL), wt.dtype)

    def ret_split(a):
        a = a.reshape(RET_HEADS, 2, half, D_MODEL)
        pad = zrows(LANES - RET_HEADS * half)
        return jnp.concatenate([a[:, 0].reshape(-1, D_MODEL), pad, a[:, 1].reshape(-1, D_MODEL), pad], axis=0)

    ws = (wu, wdq, jnp.concatenate([wdk, wdk], axis=0), jnp.concatenate([wdv, zrows(LANES - DSA_HEAD_DIM)], axis=0),
          wiq, jnp.concatenate([wik] * IDX_HEADS, axis=0), jnp.concatenate([wiw, zrows(SUBLANES - IDX_HEADS)], axis=0),
          ret_split(wrq), ret_split(wrk), wrv, wrg)
    return tuple(a.astype(_BF) for a in ws), wgt.astype(_BF)


def kernel(x, positions, norm1_g, w_in, ssm_lambda_re, ssm_lambda_im, ssm_log_step, ssm_b_re, ssm_b_im,
           ssm_c_re, ssm_c_im, ssm_d, ssm_glu_w, ssm_glu_b, ret_norm_g, w_proj_a, w_proj_b, w_proj_c,
           w_out, norm2_g, w_ff1, w_ff2, final_norm_g):
    depth = w_in.shape[0]
    tabs = _rope_tables(positions)
    fg = final_norm_g.reshape(1, D_MODEL)
    w_in_t = jnp.transpose(w_in, (2, 0, 1))
    for l in range(depth):
        g1 = norm1_g[l].reshape(1, D_MODEL)
        ws, wgt = _inproj_weights(w_in_t[:, l, :])
        u, dq, dk, dvt, iq, ik, iwt, rq, rk, rv, rg = _inproj(x, g1, tabs, ws)

        s5p = _s5_params(ssm_lambda_re[l], ssm_lambda_im[l], ssm_log_step[l], ssm_b_re[l], ssm_b_im[l],
                         ssm_c_re[l], ssm_c_im[l], ssm_d[l], ssm_glu_w[l], ssm_glu_b[l])
        ya = _s5(u, s5p)
        yb = _dsa(iq, iwt, dq, ik, dk, dvt)
        yc = _retention(rq, rk, rv, rg, ret_norm_g[l])

        x = _merge(x, g1, ya, yb, yc, wgt, w_proj_a[l].astype(_BF), w_proj_b[l].astype(_BF),
                   w_proj_c[l].astype(_BF), w_out[l].astype(_BF))
        x = _mlp(x, norm2_g[l].reshape(1, D_MODEL), w_ff1[l].astype(_BF), w_ff2[l].astype(_BF), fg,
                 final_norm=(l == depth - 1))
    return x
```

```python
import functools
import math

import jax
import jax.numpy as jnp
from jax import lax
from jax.experimental import pallas as pl
from jax.experimental.pallas import tpu as pltpu

F32 = jnp.float32
_BF = jnp.bfloat16
LANES = 128
SUBLANES = 8
VMEM_LIMIT = 56 * 1024 * 1024

D_MODEL = 1024
CHUNK = 64
EPS = 1e-6
NEG_INF_SCORE = -1e30
INT_MIN = -2 ** 31
GROUP = 32 * SUBLANES
GROUPS_PER_STEP = 4
V_ROWS = 80

SSM_WIDTH = 256
SSM_GROUP = 16
SSM_GROUPS = 16
SSM_STATE = 64
NSTATE = SSM_GROUPS * SSM_STATE

DSA_HEADS = 6
DSA_HEAD_DIM = 64
DSA_WIDTH = 384
IDX_HEADS = 4
IDX_DIM = 32
DSA_TOPK = 256
ROPE_THETA = 500000.0

RET_HEADS = 4
RET_QK_DIM = 48
RET_V_DIM = 96
RET_WIDTH = 384
RET_THETA = 10000.0
RET_QK_PAD = 256

D_FF = 4096

TM_IN = 512
S5_TILE = 512
S5_SEG = S5_TILE // SUBLANES
QB = 128
KT = 512
RET_C = 128
TM_OUT = 512
FF_CHUNK = 1024


def _rms(x, g):
    return x * lax.rsqrt(jnp.mean(x * x, axis=-1, keepdims=True) + EPS) * g


def _dot(a, b):
    return jnp.dot(a, b, preferred_element_type=F32)


def _dot_nt(a, b):
    return lax.dot_general(a, b, (((1,), (1,)), ((), ())), preferred_element_type=F32)


def _to_ukey(v):
    bits = lax.bitcast_convert_type(v, jnp.int32)
    return lax.bitcast_convert_type(bits ^ ((bits >> 31) | INT_MIN), jnp.uint32)


def _transpose32(a):
    a = list(a)
    j, m = 16, 0x0000FFFF
    while j:
        k = 0
        while k < 32:
            t = (a[k] ^ (a[k + j] >> j)) & m
            a[k] = a[k] ^ t
            a[k + j] = a[k + j] ^ (t << j)
            k = (k + j + 1) & ~j
        j >>= 1
        m = (m ^ (m << j)) & 0xFFFFFFFF
    return a


def _const_spec(shape):
    nd = len(shape)
    return pl.BlockSpec(shape, lambda *_: (0,) * nd)


def _params(sem):
    return pltpu.CompilerParams(dimension_semantics=sem, vmem_limit_bytes=VMEM_LIMIT)


def _rope_roll(z, cos, sin, sgn, shift):
    outs = []
    for c in range(z.shape[1] // LANES):
        zc = z[:, c * LANES:(c + 1) * LANES]
        rot = (pltpu.roll(zc, LANES - shift, 1) * sgn[0:1, :]
               + pltpu.roll(zc, shift, 1) * sgn[1:2, :])
        outs.append(zc * cos + rot * sin)
    return outs[0] if len(outs) == 1 else jnp.concatenate(outs, axis=1)


def _inproj_kernel(x_ref, g_ref, cd_ref, sd_ref, ci_ref, si_ref, cr_ref, sr_ref, sgd_ref, sgi_ref,
                   wu_ref, wdq_ref, wdk_ref, wdvt_ref, wiq_ref, wik_ref, wiwt_ref,
                   wrq_ref, wrk_ref, wrv_ref, wrg_ref,
                   u_ref, dq_ref, dk_ref, dvt_ref, iq_ref, ik_ref, iwt_ref,
                   rq_ref, rk_ref, rv_ref, rg_ref):
    h = _rms(x_ref[...], g_ref[...]).astype(_BF)

    def proj(w_ref):
        return _dot_nt(h, w_ref[...])

    u_ref[...] = proj(wu_ref)

    cd, sd, sgd = cd_ref[...], sd_ref[...], sgd_ref[...]
    dq = _rope_roll(proj(wdq_ref), cd, sd, sgd, 8)
    dq_ref[...] = (dq * (DSA_HEAD_DIM ** -0.5 * math.log2(math.e))).astype(dq_ref.dtype)
    dk_ref[...] = _rope_roll(proj(wdk_ref), cd, sd, sgd, 8).astype(dk_ref.dtype)
    vt = _dot_nt(wdvt_ref[...], h)
    ones_row = lax.broadcasted_iota(jnp.int32, vt.shape, 0) == DSA_HEAD_DIM
    dvt_ref[...] = jnp.where(ones_row, 1.0, vt).astype(dvt_ref.dtype)

    ci, si, sgi = ci_ref[...], si_ref[...], sgi_ref[...]
    iq_ref[...] = _rope_roll(proj(wiq_ref), ci, si, sgi, 4) * (IDX_DIM ** -0.5)
    ik_ref[...] = _rope_roll(proj(wik_ref), ci, si, sgi, 4).astype(ik_ref.dtype)
    iwt_ref[...] = _dot_nt(wiwt_ref[...], h) * (IDX_HEADS ** -0.5)

    cr, sr = cr_ref[...], sr_ref[...]

    def rope_split(z):
        x1, x2 = z[:, :LANES], z[:, LANES:]
        return jnp.concatenate([x1 * cr - x2 * sr, x2 * cr + x1 * sr], axis=1)

    rq_ref[...] = rope_split(proj(wrq_ref))
    rk_ref[...] = rope_split(proj(wrk_ref)) * (RET_QK_DIM ** -0.5)
    rv_ref[...] = proj(wrv_ref)
    rg_ref[...] = proj(wrg_ref)


def _inproj(x, g, tabs, ws):
    bsz, seq, _ = x.shape
    tm = TM_IN
    grid = (bsz, seq // tm)
    row = lambda w: pl.BlockSpec((None, tm, w), lambda b, i: (b, i, 0))
    colT = lambda r: pl.BlockSpec((None, r, tm), lambda b, i: (b, 0, i))
    in_specs = ([row(D_MODEL), _const_spec(g.shape)] + [row(LANES)] * 6
                + [_const_spec(t.shape) for t in tabs[6:]] + [_const_spec(w.shape) for w in ws])
    out_shape = [
        jax.ShapeDtypeStruct((bsz, seq, SSM_WIDTH), F32),
        jax.ShapeDtypeStruct((bsz, seq, DSA_WIDTH), _BF),
        jax.ShapeDtypeStruct((bsz, seq, LANES), _BF),
        jax.ShapeDtypeStruct((bsz, LANES, seq), _BF),
        jax.ShapeDtypeStruct((bsz, seq, LANES), F32),
        jax.ShapeDtypeStruct((bsz, seq, LANES), _BF),
        jax.ShapeDtypeStruct((bsz, SUBLANES, seq), F32),
        jax.ShapeDtypeStruct((bsz, seq, RET_QK_PAD), F32),
        jax.ShapeDtypeStruct((bsz, seq, RET_QK_PAD), F32),
        jax.ShapeDtypeStruct((bsz, seq, RET_WIDTH), F32),
        jax.ShapeDtypeStruct((bsz, seq, RET_WIDTH), F32),
    ]
    out_specs = [row(SSM_WIDTH), row(DSA_WIDTH), row(LANES), colT(LANES), row(LANES), row(LANES),
                 colT(SUBLANES), row(RET_QK_PAD), row(RET_QK_PAD), row(RET_WIDTH), row(RET_WIDTH)]
    return pl.pallas_call(
        _inproj_kernel, grid=grid, in_specs=in_specs, out_specs=out_specs, out_shape=out_shape,
        compiler_params=_params(("parallel", "parallel")), name="inproj",
    )(x, g, *tabs, *ws)


def _s5_kernel(u_ref, bbd_ref, cbd_ref, lam_ref, lam64_ref, pow_ref, d_ref, gw_ref, gb_ref,
               o_ref, st_ref, xm_ref, carry_ref, up_ref, yp_ref):
    n = NSTATE

    @pl.when(pl.program_id(1) == 0)
    def _():
        carry_ref[...] = jnp.zeros_like(carry_ref)

    nhalf = SSM_WIDTH // LANES
    for i in range(SUBLANES):
        for c in range(nhalf):
            up_ref[c, pl.ds(i, S5_SEG, stride=SUBLANES), :] = u_ref[i * S5_SEG:(i + 1) * S5_SEG,
                                                                    c * LANES:(c + 1) * LANES]
    u = jnp.concatenate([up_ref[c] for c in range(nhalf)], axis=1)
    st_ref[...] = _dot(u.astype(_BF), bbd_ref[...])

    lr = jnp.broadcast_to(lam_ref[:, :n], (SUBLANES, n))
    li = jnp.broadcast_to(lam_ref[:, n:], (SUBLANES, n))

    def step(j, c):
        xr, xi = c
        off = pl.multiple_of(j * SUBLANES, SUBLANES)
        ar = st_ref[pl.ds(off, SUBLANES), :n]
        ai = st_ref[pl.ds(off, SUBLANES), n:]
        nr = lr * xr - li * xi + ar
        ni = lr * xi + li * xr + ai
        st_ref[pl.ds(off, SUBLANES), :n] = nr
        st_ref[pl.ds(off, SUBLANES), n:] = ni
        return nr, ni

    zero = jnp.zeros((SUBLANES, n), F32)
    xr, xi = lax.fori_loop(0, S5_SEG, step, (zero, zero))

    l64r, l64i = lam64_ref[:, :n], lam64_ref[:, n:]
    cr, ci = carry_ref[:, :n], carry_ref[:, n:]
    for i in range(SUBLANES):
        xm_ref[i:i + 1, :n] = cr
        xm_ref[i:i + 1, n:] = ci
        er, ei = xr[i:i + 1, :], xi[i:i + 1, :]
        cr, ci = er + l64r * cr - l64i * ci, ei + l64r * ci + l64i * cr
    carry_ref[:, :n] = cr
    carry_ref[:, n:] = ci

    hr, hi = xm_ref[:, :n], xm_ref[:, n:]

    def fix(j, _):
        off = pl.multiple_of(j * SUBLANES, SUBLANES)
        pr = pow_ref[pl.ds(j, 1), :n]
        pi = pow_ref[pl.ds(j, 1), n:]
        st_ref[pl.ds(off, SUBLANES), :n] = st_ref[pl.ds(off, SUBLANES), :n] + pr * hr - pi * hi
        st_ref[pl.ds(off, SUBLANES), n:] = st_ref[pl.ds(off, SUBLANES), n:] + pr * hi + pi * hr
        return 0

    lax.fori_loop(0, S5_SEG, fix, 0)

    y = d_ref[...] * u
    kc = 512
    for k in range(2 * n // kc):
        y = y + _dot(st_ref[:, k * kc:(k + 1) * kc].astype(_BF), cbd_ref[k * kc:(k + 1) * kc, :])
    y = jax.nn.gelu(y)
    y = y * jax.nn.sigmoid(_dot(y.astype(_BF), gw_ref[...]) + gb_ref[...])
    for c in range(nhalf):
        yp_ref[c] = y[:, c * LANES:(c + 1) * LANES]
    for i in range(SUBLANES):
        for c in range(nhalf):
            o_ref[i * S5_SEG:(i + 1) * S5_SEG, c * LANES:(c + 1) * LANES] = yp_ref[
                c, pl.ds(i, S5_SEG, stride=SUBLANES), :]


def _s5(u_perm, prm):
    bsz, seq, _ = u_perm.shape
    grid = (bsz, seq // S5_TILE)
    row = pl.BlockSpec((None, S5_TILE, SSM_WIDTH), lambda b, i: (b, i, 0))
    return pl.pallas_call(
        _s5_kernel, grid=grid,
        in_specs=[row] + [_const_spec(p.shape) for p in prm],
        out_specs=row,
        out_shape=jax.ShapeDtypeStruct((bsz, seq, SSM_WIDTH), F32),
        scratch_shapes=[pltpu.VMEM((S5_TILE, 2 * NSTATE), F32),
                        pltpu.VMEM((SUBLANES, 2 * NSTATE), F32),
                        pltpu.VMEM((1, 2 * NSTATE), F32),
                        pltpu.VMEM((SSM_WIDTH // LANES, S5_TILE, LANES), F32),
                        pltpu.VMEM((SSM_WIDTH // LANES, S5_TILE, LANES), F32)],
        compiler_params=_params(("arbitrary", "arbitrary")), name="s5",
    )(u_perm, *prm)


def _s5_params(lam_re, lam_im, log_step, b_re, b_im, c_re, c_im, d_skip, glu_w, glu_b):
    step = jnp.exp(log_step.astype(F32))[:, None]
    ere, eim = lam_re * step, lam_im * step
    mag = jnp.exp(ere)
    lb_re, lb_im = mag * jnp.cos(eim), mag * jnp.sin(eim)
    den = lam_re * lam_re + lam_im * lam_im
    f_re = ((lb_re - 1.0) * lam_re + lb_im * lam_im) / den
    f_im = (lb_im * lam_re - (lb_re - 1.0) * lam_im) / den
    bb_re = f_re[..., None] * b_re - f_im[..., None] * b_im
    bb_im = f_re[..., None] * b_im + f_im[..., None] * b_re
    eye = jnp.eye(SSM_GROUPS, dtype=F32)
    bbd = jnp.concatenate(
        [jnp.einsum('gpc,gh->gchp', bb, eye).reshape(SSM_WIDTH, NSTATE) for bb in (bb_re, bb_im)], axis=1)
    cbd = jnp.concatenate(
        [jnp.einsum('gcp,gh->hpgc', cc, eye).reshape(NSTATE, SSM_WIDTH) for cc in (c_re, -c_im)], axis=0)
    flat = lambda a: a.reshape(1, NSTATE)
    lam = jnp.concatenate([flat(lb_re), flat(lb_im)], axis=1)

    def power(k):
        k = jnp.asarray(k, F32).reshape(-1, 1)
        m = jnp.exp(k * flat(ere))
        return jnp.concatenate([m * jnp.cos(k * flat(eim)), m * jnp.sin(k * flat(eim))], axis=1)

    return (bbd.astype(_BF), cbd.astype(_BF), lam, power(float(S5_SEG)),
            power(jnp.arange(1, S5_SEG + 1)), d_skip.reshape(1, SSM_WIDTH),
            glu_w.astype(_BF), glu_b.reshape(1, SSM_WIDTH))


def _dsa_kernel(iq_ref, iwt_ref, dq_ref, ik_ref, dk_ref, dvt_ref, o_ref,
                planes_ref, cand_ref, pick_ref, bias_ref, acc_ref,
                lg0_ref, lg1_ref, s0_ref, s1_ref, p0_ref, p1_ref, *, seq, topk):
    qi = pl.program_id(1)
    nk = (qi * QB) // KT + 1
    lane = lax.broadcasted_iota(jnp.int32, (1, QB), 1)
    vis_end = qi * QB + jnp.where(lane < CHUNK, CHUNK, 2 * CHUNK)

    iq = iq_ref[...]
    col = lax.broadcasted_iota(jnp.int32, (QB, LANES), 1)
    qs = jnp.concatenate([jnp.where(col // IDX_DIM == h, iq, 0.0) for h in range(IDX_HEADS)],
                         axis=0).astype(_BF)
    w = iwt_ref[...]

    nkp = nk + lax.rem(nk, 2)
    last_tile = nkp - 1

    def key_tile(ref, kt):
        return ref[pl.ds(pl.multiple_of(kt * KT, KT), KT), :]

    def tile_planes(src_ref):
        lg = src_ref[...]
        sc = jnp.zeros((KT, QB), F32)
        for h in range(IDX_HEADS):
            sc = sc + jnp.maximum(lg[:, h * QB:(h + 1) * QB], 0.0) * w[h:h + 1, :]
        sc = jnp.where(sc == 0.0, 0.0, sc)
        ukey = _to_ukey(sc)
        words = []
        for g in range(KT // GROUP):
            words += _transpose32([ukey[g * GROUP + m * SUBLANES:g * GROUP + (m + 1) * SUBLANES, :]
                                   for m in range(32)])
        return lax.bitcast_convert_type(jnp.concatenate(words, axis=0), jnp.int32)

    def score_pair(i, c):
        ik1 = key_tile(ik_ref, 2 * i + 1)
        ik2 = key_tile(ik_ref, jnp.minimum(2 * i + 2, last_tile))
        lg1_ref[...] = _dot_nt(ik1, qs)
        planes0 = tile_planes(lg0_ref)
        lg0_ref[...] = _dot_nt(ik2, qs)
        planes1 = tile_planes(lg1_ref)
        planes_ref[pl.ds(pl.multiple_of(i * (2 * KT), 2 * KT), 2 * KT), :] = jnp.concatenate(
            [planes0, planes1], axis=0)
        return c

    lg0_ref[...] = _dot_nt(key_tile(ik_ref, 0), qs)
    lax.fori_loop(0, nkp // 2, score_pair, 0)

    ngrp = nkp * (KT // GROUP)
    sub = lax.broadcasted_iota(jnp.int32, (SUBLANES, QB), 0)
    zero8 = jnp.zeros((SUBLANES, QB), jnp.int32)

    def rows_below(limit, g):
        nm = lax.shift_right_arithmetic(limit - g * GROUP + (SUBLANES - 1), 3)
        top = lax.shift_right_arithmetic(jnp.full((SUBLANES, QB), INT_MIN, jnp.int32),
                                         jnp.clip(nm, 1, 32) - 1)
        return jnp.where(nm <= 0, 0, top)

    def word(ref, g):
        return ref[pl.ds(pl.multiple_of(g * SUBLANES, SUBLANES), SUBLANES), :]

    def put_word(ref, g, v):
        ref[pl.ds(pl.multiple_of(g * SUBLANES, SUBLANES), SUBLANES), :] = v

    def plane(g, i):
        return planes_ref[pl.ds(pl.multiple_of(g * GROUP + i * SUBLANES, SUBLANES), SUBLANES), :]

    def sweep_words(step):
        def body(s, accs):
            return tuple(a + step(s * GROUPS_PER_STEP + t) for t, a in enumerate(accs))
        accs = lax.fori_loop(0, ngrp // GROUPS_PER_STEP, body, (zero8,) * GROUPS_PER_STEP)
        return functools.reduce(lambda a, b: a + b, accs)

    def sweep(step):
        return sweep_words(step).sum(axis=0, keepdims=True)

    def init_words(g):
        put_word(cand_ref, g, rows_below(vis_end - sub, g))
        put_word(pick_ref, g, zero8)
        return zero8

    sweep(init_words)

    def count_ones(i):
        return sweep(lambda g: lax.population_count(word(cand_ref, g) & plane(g, i)))

    def decide(c1, rem):
        take = c1 >= rem
        return jnp.where(take, 1, 0), jnp.where(take, rem, rem - c1)

    def narrow(g, i, take):
        take8 = jnp.broadcast_to(take, (SUBLANES, QB)) != 0
        c = word(cand_ref, g)
        ones = c & plane(g, i)
        put_word(pick_ref, g, word(pick_ref, g) | jnp.where(take8, 0, ones))
        c = jnp.where(take8, ones, c ^ ones)
        put_word(cand_ref, g, c)
        return c

    def radix_step(i, carry):
        take8, rem = carry

        def step(g):
            return lax.population_count(narrow(g, i - 1, take8) & plane(g, i))
        return decide(sweep(step), rem)

    take8, rem = decide(count_ones(0), jnp.full((1, QB), topk, jnp.int32))
    take8, rem = lax.fori_loop(1, 32, radix_step, (take8, rem))

    def last_narrow(g):
        narrow(g, 31, take8)
        return zero8

    sweep(last_narrow)

    def group_step(i, carry):
        grp, before = carry
        trial = grp + jnp.left_shift(jnp.int32(1), ((seq // GROUP).bit_length() - 2) - i)
        below = sweep(lambda g: jnp.where(jnp.broadcast_to(g < trial, (SUBLANES, QB)),
                                          lax.population_count(word(cand_ref, g)), 0))
        ok = below < rem
        return jnp.where(ok, trial, grp), jnp.where(ok, below, before)

    zrow = jnp.zeros((1, QB), jnp.int32)
    grp, before = lax.fori_loop(0, (seq // GROUP).bit_length() - 1, group_step, (zrow, zrow))
    tied = sweep_words(lambda g: jnp.where(jnp.broadcast_to(g == grp, (SUBLANES, QB)), word(cand_ref, g), 0))

    def row_step(i, edge):
        trial = edge + jnp.left_shift(jnp.int32(1), (GROUP.bit_length() - 2) - i)
        below = lax.population_count(tied & rows_below(trial - sub, 0)).sum(axis=0, keepdims=True)
        return jnp.where(before + below < rem, trial, edge)

    edge = lax.fori_loop(0, GROUP.bit_length() - 1, row_step, zrow)
    keep_limit = grp * GROUP + edge + 1 - sub

    def bias_group(g):
        chosen = word(pick_ref, g) | (word(cand_ref, g) & rows_below(keep_limit, g))
        base = pl.multiple_of(g * GROUP, GROUP)
        bias_ref[pl.ds(base, GROUP), :] = jnp.concatenate(
            [jnp.where(lax.shift_left(chosen, jnp.full_like(chosen, m)) < 0, 0.0, NEG_INF_SCORE)
             for m in range(32)], axis=0).astype(bias_ref.dtype)
        return zero8

    sweep(bias_group)

    q = dq_ref[...]
    half = (col // DSA_HEAD_DIM)
    qst = jnp.concatenate(
        [jnp.where(half == (h % 2), q[:, (h // 2) * LANES:(h // 2 + 1) * LANES], jnp.zeros((), q.dtype))
         for h in range(DSA_HEADS)], axis=0)
    unit = jnp.where(lax.broadcasted_iota(jnp.int32, (QB, QB), 0) == lax.broadcasted_iota(jnp.int32, (QB, QB), 1),
                     1.0, 0.0).astype(q.dtype)
    qst = jnp.concatenate([qst, jnp.concatenate([unit] * DSA_HEADS, axis=0)], axis=1)
    npair = DSA_HEADS // 2
    hq = DSA_HEADS * QB
    acc_ref[...] = jnp.zeros_like(acc_ref)
    p1_ref[...] = jnp.zeros_like(p1_ref)

    def qk(k_tile, b_tile, dst_ref):
        dst_ref[...] = _dot_nt(jnp.concatenate([k_tile, b_tile], axis=1), qst)

    def numer(src_ref, dst_ref, m):
        m_new = []
        for hd in range(DSA_HEADS):
            cols = slice(hd * QB, (hd + 1) * QB)
            mh = jnp.maximum(m[:, cols], jnp.max(src_ref[:, cols], axis=0, keepdims=True))
            dst_ref[:, cols] = jnp.exp2(src_ref[:, cols] - mh).astype(dst_ref.dtype)
            m_new.append(mh)
        m_new = jnp.concatenate(m_new, axis=1)
        return m_new, jnp.exp2(m - m_new)

    def v_tile(kt):
        return dvt_ref[0:V_ROWS, pl.ds(pl.multiple_of(kt * KT, KT), KT)]

    def pv(vt, src_ref, alpha):
        acc_ref[...] = alpha * acc_ref[...] + _dot(vt, src_ref[...])

    def att_pair(i, c):
        m, alpha = c
        nxt = jnp.minimum(2 * i + 2, last_tile)
        k1, b1 = key_tile(dk_ref, 2 * i + 1), key_tile(bias_ref, 2 * i + 1)
        k2, b2 = key_tile(dk_ref, nxt), key_tile(bias_ref, nxt)
        v_prev, v_cur = v_tile(jnp.maximum(2 * i - 1, 0)), v_tile(2 * i)
        qk(k1, b1, s1_ref)
        pv(v_prev, p1_ref, alpha)
        m, alpha = numer(s0_ref, p0_ref, m)
        qk(k2, b2, s0_ref)
        pv(v_cur, p0_ref, alpha)
        m, alpha = numer(s1_ref, p1_ref, m)
        return m, alpha

    qk(key_tile(dk_ref, 0), key_tile(bias_ref, 0), s0_ref)
    c0 = (jnp.full((1, hq), -jnp.inf, F32), jnp.ones((1, hq), F32))
    _, alpha = lax.fori_loop(0, nkp // 2, att_pair, c0)
    pv(v_tile(last_tile), p1_ref, alpha)
    acc = acc_ref[...]
    o = acc[0:DSA_HEAD_DIM, :] / acc[DSA_HEAD_DIM:DSA_HEAD_DIM + 1, :]
    for j in range(npair):
        o_ref[:, j * LANES:(j + 1) * LANES] = jnp.concatenate(
            [o[:, (2 * j) * QB:(2 * j + 1) * QB], o[:, (2 * j + 1) * QB:(2 * j + 2) * QB]], axis=0).T


def _dsa(iq, iwt, dq, ik, dk, dvt):
    bsz, seq, _ = iq.shape
    topk = min(DSA_TOPK, seq // 4)
    grid = (bsz, seq // QB)
    qrow = lambda w: pl.BlockSpec((None, QB, w), lambda b, i: (b, i, 0))
    full = lambda r, c: pl.BlockSpec((None, r, c), lambda b, i: (b, 0, 0))
    return pl.pallas_call(
        functools.partial(_dsa_kernel, seq=seq, topk=topk), grid=grid,
        in_specs=[qrow(LANES), pl.BlockSpec((None, SUBLANES, QB), lambda b, i: (b, 0, i)), qrow(DSA_WIDTH),
                  full(seq, LANES), full(seq, LANES), full(LANES, seq)],
        out_specs=qrow(DSA_WIDTH),
        out_shape=jax.ShapeDtypeStruct((bsz, seq, DSA_WIDTH), F32),
        scratch_shapes=[pltpu.VMEM((seq, QB), jnp.int32),
                        pltpu.VMEM((seq // GROUP * SUBLANES, QB), jnp.int32),
                        pltpu.VMEM((seq // GROUP * SUBLANES, QB), jnp.int32),
                        pltpu.VMEM((seq, QB), _BF),
                        pltpu.VMEM((V_ROWS, DSA_HEADS * QB), F32),
                        pltpu.VMEM((KT, IDX_HEADS * QB), F32), pltpu.VMEM((KT, IDX_HEADS * QB), F32),
                        pltpu.VMEM((KT, DSA_HEADS * QB), F32), pltpu.VMEM((KT, DSA_HEADS * QB), F32),
                        pltpu.VMEM((KT, DSA_HEADS * QB), _BF), pltpu.VMEM((KT, DSA_HEADS * QB), _BF)],
        compiler_params=_params(("parallel", "arbitrary")), name="dsa",
    )(iq, iwt, dq, ik, dk, dvt)


def _ret_kernel(rq_ref, rk_ref, rv_ref, rg_ref, qdec_ref, kdec_ref, dmask_ref, cdec_ref, blk_ref,
                hmq_ref, hmv_ref, ones_ref, ng_ref, o_ref, s_ref):
    @pl.when(pl.program_id(0) == 0)
    def _():
        s_ref[...] = jnp.zeros_like(s_ref)

    ones = ones_ref[...]

    def head_mean(z):
        return _dot(z.astype(_BF), ones) * (1.0 / RET_V_DIM)

    for b in range(rq_ref.shape[0]):
        q, k, v = rq_ref[b], rk_ref[b], rv_ref[b]
        state = s_ref[b]
        y = _dot((q * qdec_ref[...]).astype(_BF), state.astype(_BF))
        kd = (k * kdec_ref[...]).T.astype(_BF)
        vb = v.astype(_BF)
        s_ref[b] = state * cdec_ref[...] + _dot(kd, vb) * blk_ref[...]
        kb = k.astype(_BF)
        a = jnp.concatenate([(_dot_nt((q * hmq_ref[h:h + 1, :]).astype(_BF), kb) * dmask_ref[h]).astype(_BF)
                             for h in range(RET_HEADS)], axis=1)
        vh = jnp.concatenate([(v * hmv_ref[h:h + 1, :]).astype(_BF) for h in range(RET_HEADS)], axis=0)
        y = y + _dot(a, vh)
        d = y - head_mean(y)
        yn = d * lax.rsqrt(head_mean(d * d) + EPS) * ng_ref[...]
        o_ref[b] = jax.nn.silu(rg_ref[b]) * yn


def _ret_consts():
    c = RET_C
    log_g = jnp.log1p(-jnp.exp2(-5.0 - jnp.arange(RET_HEADS, dtype=F32)))
    pos = jnp.arange(c, dtype=F32)
    diff = pos[:, None] - pos[None, :]
    dmask = jnp.where(diff >= 0, jnp.exp(log_g[:, None, None] * jnp.maximum(diff, 0.0)), 0.0)
    lane_q = jnp.arange(RET_QK_PAD)
    head_q = jnp.where((lane_q % LANES) < RET_HEADS * (RET_QK_DIM // 2), (lane_q % LANES) // (RET_QK_DIM // 2), -1)
    head_v = jnp.arange(RET_WIDTH) // RET_V_DIM
    hmq = (head_q[None, :] == jnp.arange(RET_HEADS)[:, None]).astype(F32)
    hmv = (head_v[None, :] == jnp.arange(RET_HEADS)[:, None]).astype(F32)
    lg_q = hmq.T @ log_g
    qdec = jnp.exp(lg_q[None, :] * (pos[:, None] + 1.0))
    kdec = jnp.exp(lg_q[None, :] * (c - 1.0 - pos[:, None]))
    blk = hmq.T @ hmv
    cdec = blk * jnp.exp(lg_q * c)[:, None]
    ones = (hmv.T @ hmv).astype(_BF)
    return qdec, kdec, dmask, cdec, blk, hmq, hmv, ones


def _retention(rq, rk, rv, rg, norm_g):
    bsz, seq, _ = rq.shape
    consts = _ret_consts()
    grid = (seq // RET_C,)
    row = lambda w: pl.BlockSpec((bsz, RET_C, w), lambda i: (0, i, 0))
    ng = norm_g.reshape(1, RET_WIDTH)
    return pl.pallas_call(
        _ret_kernel, grid=grid,
        in_specs=[row(RET_QK_PAD), row(RET_QK_PAD), row(RET_WIDTH), row(RET_WIDTH)]
        + [_const_spec(a.shape) for a in consts] + [_const_spec(ng.shape)],
        out_specs=row(RET_WIDTH),
        out_shape=jax.ShapeDtypeStruct((bsz, seq, RET_WIDTH), F32),
        scratch_shapes=[pltpu.VMEM((bsz, RET_QK_PAD, RET_WIDTH), F32)],
        compiler_params=_params(("arbitrary",)), name="retention",
    )(rq, rk, rv, rg, *consts, ng)


def _merge_kernel(x_ref, g_ref, ya_ref, yb_ref, yc_ref, wg_ref, wa_ref, wb_ref, wc_ref, wo_ref, o_ref):
    x = x_ref[...]
    h = _rms(x, g_ref[...]).astype(_BF)
    merged = jnp.zeros(x.shape, F32)
    for n, (y_ref, w_ref) in enumerate(((ya_ref, wa_ref), (yb_ref, wb_ref), (yc_ref, wc_ref))):
        gate = jax.nn.sigmoid(_dot_nt(h, wg_ref[n * D_MODEL:(n + 1) * D_MODEL, :]))
        merged = merged + gate * _dot(y_ref[...].astype(_BF), w_ref[...])
    o_ref[...] = x + _dot(merged.astype(_BF), wo_ref[...])


def _merge(x, g, ya, yb, yc, wg, wa, wb, wc, wo):
    bsz, seq, _ = x.shape
    grid = (bsz, seq // TM_OUT)
    row = lambda w: pl.BlockSpec((None, TM_OUT, w), lambda b, i: (b, i, 0))
    return pl.pallas_call(
        _merge_kernel, grid=grid,
        in_specs=[row(D_MODEL), _const_spec(g.shape), row(SSM_WIDTH), row(DSA_WIDTH), row(RET_WIDTH)]
        + [_const_spec(w.shape) for w in (wg, wa, wb, wc, wo)],
        out_specs=row(D_MODEL),
        out_shape=jax.ShapeDtypeStruct(x.shape, F32),
        compiler_params=_params(("parallel", "parallel")), name="merge",
    )(x, g, ya, yb, yc, wg, wa, wb, wc, wo)


def _mlp_kernel(x_ref, g_ref, w1_ref, w2_ref, fg_ref, o_ref, *, final_norm):
    x = x_ref[...]
    h = _rms(x, g_ref[...]).astype(_BF)
    acc = x
    for f in range(D_FF // FF_CHUNK):
        t = jnp.maximum(_dot(h, w1_ref[:, f * FF_CHUNK:(f + 1) * FF_CHUNK]), 0.0)
        acc = acc + _dot((t * t).astype(_BF), w2_ref[f * FF_CHUNK:(f + 1) * FF_CHUNK, :])
    o_ref[...] = _rms(acc, fg_ref[...]) if final_norm else acc


def _mlp(x, g, w1, w2, fg, final_norm):
    bsz, seq, _ = x.shape
    grid = (bsz, seq // TM_OUT)
    row = pl.BlockSpec((None, TM_OUT, D_MODEL), lambda b, i: (b, i, 0))
    return pl.pallas_call(
        functools.partial(_mlp_kernel, final_norm=final_norm), grid=grid,
        in_specs=[row, _const_spec(g.shape), _const_spec(w1.shape), _const_spec(w2.shape), _const_spec(fg.shape)],
        out_specs=row,
        out_shape=jax.ShapeDtypeStruct(x.shape, F32),
        compiler_params=_params(("parallel", "parallel")), name="mlp",
    )(x, g, w1, w2, fg)


def _rope_tables(positions):
    pos = positions.astype(F32)[..., None]
    lane = jnp.arange(LANES)

    def angles(rot_dim, theta):
        half = rot_dim // 2
        inv = jnp.exp(-math.log(theta) * jnp.arange(half, dtype=F32) * (2.0 / rot_dim))
        return pos * inv

    def roll_tables(head_dim, rot_dim, theta):
        half = rot_dim // 2
        ang = angles(rot_dim, theta)
        ln = lane % head_dim
        rot = ln < rot_dim
        cos = jnp.where(rot, jnp.take(jnp.cos(ang), ln % half, axis=-1), 1.0)
        sin = jnp.where(rot, jnp.take(jnp.sin(ang), ln % half, axis=-1), 0.0)
        sgn = jnp.stack([jnp.where(ln < half, -1.0, 0.0),
                         jnp.where((ln >= half) & rot, 1.0, 0.0)]).astype(F32)
        return cos, sin, sgn

    cd, sd, sgd = roll_tables(DSA_HEAD_DIM, DSA_HEAD_DIM // 4, ROPE_THETA)
    ci, si, sgi = roll_tables(IDX_DIM, IDX_DIM // 4, ROPE_THETA)
    half = RET_QK_DIM // 2
    ang = angles(RET_QK_DIM, RET_THETA)
    used = lane < RET_HEADS * half
    cr = jnp.where(used, jnp.take(jnp.cos(ang), lane % half, axis=-1), 1.0)
    sr = jnp.where(used, jnp.take(jnp.sin(ang), lane % half, axis=-1), 0.0)
    return cd, sd, ci, si, cr, sr, sgd, sgi


def _inproj_weights(wt):
    sizes = (SSM_WIDTH, DSA_WIDTH, DSA_HEAD_DIM, DSA_HEAD_DIM, IDX_HEADS * IDX_DIM, IDX_DIM, IDX_HEADS,
             RET_HEADS * RET_QK_DIM, RET_HEADS * RET_QK_DIM, RET_WIDTH, RET_WIDTH, 3 * D_MODEL)
    parts, off = [], 0
    for n in sizes:
        parts.append(wt[off:off + n, :])
        off += n
    wu, wdq, wdk, wdv, wiq, wik, wiw, wrq, wrk, wrv, wrg, wgt = parts
    half = RET_QK_DIM // 2
    zrows = lambda n: jnp.zeros((n, D_MODEL), wt.dtype)

    def ret_split(a):
        a = a.reshape(RET_HEADS, 2, half, D_MODEL)
        pad = zrows(LANES - RET_HEADS * half)
        return jnp.concatenate([a[:, 0].reshape(-1, D_MODEL), pad, a[:, 1].reshape(-1, D_MODEL), pad], axis=0)

    ws = (wu, wdq, jnp.concatenate([wdk, wdk], axis=0), jnp.concatenate([wdv, zrows(LANES - DSA_HEAD_DIM)], axis=0),
          wiq, jnp.concatenate([wik] * IDX_HEADS, axis=0), jnp.concatenate([wiw, zrows(SUBLANES - IDX_HEADS)], axis=0),
          ret_split(wrq), ret_split(wrk), wrv, wrg)
    return tuple(a.astype(_BF) for a in ws), wgt.astype(_BF)


def kernel(x, positions, norm1_g, w_in, ssm_lambda_re, ssm_lambda_im, ssm_log_step, ssm_b_re, ssm_b_im,
           ssm_c_re, ssm_c_im, ssm_d, ssm_glu_w, ssm_glu_b, ret_norm_g, w_proj_a, w_proj_b, w_proj_c,
           w_out, norm2_g, w_ff1, w_ff2, final_norm_g):
    depth = w_in.shape[0]
    tabs = _rope_tables(positions)
    fg = final_norm_g.reshape(1, D_MODEL)
    w_in_t = jnp.transpose(w_in, (2, 0, 1))
    for l in range(depth):
        g1 = norm1_g[l].reshape(1, D_MODEL)
        ws, wgt = _inproj_weights(w_in_t[:, l, :])
        u, dq, dk, dvt, iq, ik, iwt, rq, rk, rv, rg = _inproj(x, g1, tabs, ws)

        s5p = _s5_params(ssm_lambda_re[l], ssm_lambda_im[l], ssm_log_step[l], ssm_b_re[l], ssm_b_im[l],
                         ssm_c_re[l], ssm_c_im[l], ssm_d[l], ssm_glu_w[l], ssm_glu_b[l])
        ya = _s5(u, s5p)
        yb = _dsa(iq, iwt, dq, ik, dk, dvt)
        yc = _retention(rq, rk, rv, rg, ret_norm_g[l])

        x = _merge(x, g1, ya, yb, yc, wgt, w_proj_a[l].astype(_BF), w_proj_b[l].astype(_BF),
                   w_proj_c[l].astype(_BF), w_out[l].astype(_BF))
        x = _mlp(x, norm2_g[l].reshape(1, D_MODEL), w_ff1[l].astype(_BF), w_ff2[l].astype(_BF), fg,
                 final_norm=(l == depth - 1))
    return x
```

```python
import functools
import math

import jax
import jax.numpy as jnp
from jax import lax
from jax.experimental import pallas as pl
from jax.experimental.pallas import tpu as pltpu

F32 = jnp.float32
_BF = jnp.bfloat16
LANES = 128
SUBLANES = 8
VMEM_LIMIT = 56 * 1024 * 1024

D_MODEL = 1024
CHUNK = 64
EPS = 1e-6
NEG_INF_SCORE = -1e30
INT_MIN = -2 ** 31
GROUP = 32 * SUBLANES
GROUPS_PER_STEP = 4
PAIR_UNROLL = (4, 2, 1)
V_ROWS = 80

SSM_WIDTH = 256
SSM_GROUP = 16
SSM_GROUPS = 16
SSM_STATE = 64
NSTATE = SSM_GROUPS * SSM_STATE

DSA_HEADS = 6
DSA_HEAD_DIM = 64
DSA_WIDTH = 384
IDX_HEADS = 4
IDX_DIM = 32
DSA_TOPK = 256
ROPE_THETA = 500000.0

RET_HEADS = 4
RET_QK_DIM = 48
RET_V_DIM = 96
RET_WIDTH = 384
RET_THETA = 10000.0
RET_QK_PAD = 256

D_FF = 4096

TM_IN = 512
S5_TILE = 512
S5_SEG = S5_TILE // SUBLANES
QB = 128
KT = 512
RET_C = 128
TM_OUT = 512
FF_CHUNK = 1024


def _rms(x, g):
    return x * lax.rsqrt(jnp.mean(x * x, axis=-1, keepdims=True) + EPS) * g


def _dot(a, b):
    return jnp.dot(a, b, preferred_element_type=F32)


def _dot_nt(a, b):
    return lax.dot_general(a, b, (((1,), (1,)), ((), ())), preferred_element_type=F32)


def _to_ukey(v):
    bits = lax.bitcast_convert_type(v, jnp.int32)
    return lax.bitcast_convert_type(bits ^ ((bits >> 31) | INT_MIN), jnp.uint32)


def _transpose32(a):
    a = list(a)
    j, m = 16, 0x0000FFFF
    while j:
        k = 0
        while k < 32:
            t = (a[k] ^ (a[k + j] >> j)) & m
            a[k] = a[k] ^ t
            a[k + j] = a[k + j] ^ (t << j)
            k = (k + j + 1) & ~j
        j >>= 1
        m = (m ^ (m << j)) & 0xFFFFFFFF
    return a


def _const_spec(shape):
    nd = len(shape)
    return pl.BlockSpec(shape, lambda *_: (0,) * nd)


def _params(sem):
    return pltpu.CompilerParams(dimension_semantics=sem, vmem_limit_bytes=VMEM_LIMIT)


def _rope_roll(z, cos, sin, sgn, shift):
    outs = []
    for c in range(z.shape[1] // LANES):
        zc = z[:, c * LANES:(c + 1) * LANES]
        rot = (pltpu.roll(zc, LANES - shift, 1) * sgn[0:1, :]
               + pltpu.roll(zc, shift, 1) * sgn[1:2, :])
        outs.append(zc * cos + rot * sin)
    return outs[0] if len(outs) == 1 else jnp.concatenate(outs, axis=1)


def _inproj_kernel(x_ref, g_ref, cd_ref, sd_ref, ci_ref, si_ref, cr_ref, sr_ref, sgd_ref, sgi_ref,
                   wu_ref, wdq_ref, wdk_ref, wdvt_ref, wiq_ref, wik_ref, wiwt_ref,
                   wrq_ref, wrk_ref, wrv_ref, wrg_ref,
                   u_ref, dq_ref, dk_ref, dvt_ref, iq_ref, ik_ref, iwt_ref,
                   rq_ref, rk_ref, rv_ref, rg_ref):
    h = _rms(x_ref[...], g_ref[...]).astype(_BF)

    def proj(w_ref):
        return _dot_nt(h, w_ref[...])

    u_ref[...] = proj(wu_ref)

    cd, sd, sgd = cd_ref[...], sd_ref[...], sgd_ref[...]
    dq = _rope_roll(proj(wdq_ref), cd, sd, sgd, 8)
    dq_ref[...] = (dq * (DSA_HEAD_DIM ** -0.5 * math.log2(math.e))).astype(dq_ref.dtype)
    dk_ref[...] = _rope_roll(proj(wdk_ref), cd, sd, sgd, 8).astype(dk_ref.dtype)
    vt = _dot_nt(wdvt_ref[...], h)
    ones_row = lax.broadcasted_iota(jnp.int32, vt.shape, 0) == DSA_HEAD_DIM
    dvt_ref[...] = jnp.where(ones_row, 1.0, vt).astype(dvt_ref.dtype)

    ci, si, sgi = ci_ref[...], si_ref[...], sgi_ref[...]
    iq_ref[...] = _rope_roll(proj(wiq_ref), ci, si, sgi, 4) * (IDX_DIM ** -0.5)
    ik_ref[...] = _rope_roll(proj(wik_ref), ci, si, sgi, 4).astype(ik_ref.dtype)
    iwt_ref[...] = _dot_nt(wiwt_ref[...], h) * (IDX_HEADS ** -0.5)

    cr, sr = cr_ref[...], sr_ref[...]

    def rope_split(z):
        x1, x2 = z[:, :LANES], z[:, LANES:]
        return jnp.concatenate([x1 * cr - x2 * sr, x2 * cr + x1 * sr], axis=1)

    rq_ref[...] = rope_split(proj(wrq_ref))
    rk_ref[...] = rope_split(proj(wrk_ref)) * (RET_QK_DIM ** -0.5)
    rv_ref[...] = proj(wrv_ref)
    rg_ref[...] = proj(wrg_ref)


def _inproj(x, g, tabs, ws):
    bsz, seq, _ = x.shape
    tm = TM_IN
    grid = (bsz, seq // tm)
    row = lambda w: pl.BlockSpec((None, tm, w), lambda b, i: (b, i, 0))
    colT = lambda r: pl.BlockSpec((None, r, tm), lambda b, i: (b, 0, i))
    in_specs = ([row(D_MODEL), _const_spec(g.shape)] + [row(LANES)] * 6
                + [_const_spec(t.shape) for t in tabs[6:]] + [_const_spec(w.shape) for w in ws])
    out_shape = [
        jax.ShapeDtypeStruct((bsz, seq, SSM_WIDTH), F32),
        jax.ShapeDtypeStruct((bsz, seq, DSA_WIDTH), _BF),
        jax.ShapeDtypeStruct((bsz, seq, LANES), _BF),
        jax.ShapeDtypeStruct((bsz, LANES, seq), _BF),
        jax.ShapeDtypeStruct((bsz, seq, LANES), F32),
        jax.ShapeDtypeStruct((bsz, seq, LANES), _BF),
        jax.ShapeDtypeStruct((bsz, SUBLANES, seq), F32),
        jax.ShapeDtypeStruct((bsz, seq, RET_QK_PAD), F32),
        jax.ShapeDtypeStruct((bsz, seq, RET_QK_PAD), F32),
        jax.ShapeDtypeStruct((bsz, seq, RET_WIDTH), F32),
        jax.ShapeDtypeStruct((bsz, seq, RET_WIDTH), F32),
    ]
    out_specs = [row(SSM_WIDTH), row(DSA_WIDTH), row(LANES), colT(LANES), row(LANES), row(LANES),
                 colT(SUBLANES), row(RET_QK_PAD), row(RET_QK_PAD), row(RET_WIDTH), row(RET_WIDTH)]
    return pl.pallas_call(
        _inproj_kernel, grid=grid, in_specs=in_specs, out_specs=out_specs, out_shape=out_shape,
        compiler_params=_params(("parallel", "parallel")), name="inproj",
    )(x, g, *tabs, *ws)


def _s5_kernel(u_ref, bbd_ref, cbd_ref, lam_ref, lam64_ref, pow_ref, d_ref, gw_ref, gb_ref,
               o_ref, st_ref, xm_ref, carry_ref, up_ref, yp_ref):
    n = NSTATE

    @pl.when(pl.program_id(1) == 0)
    def _():
        carry_ref[...] = jnp.zeros_like(carry_ref)

    nhalf = SSM_WIDTH // LANES
    for i in range(SUBLANES):
        for c in range(nhalf):
            up_ref[c, pl.ds(i, S5_SEG, stride=SUBLANES), :] = u_ref[i * S5_SEG:(i + 1) * S5_SEG,
                                                                    c * LANES:(c + 1) * LANES]
    u = jnp.concatenate([up_ref[c] for c in range(nhalf)], axis=1)
    st_ref[...] = _dot(u.astype(_BF), bbd_ref[...])

    lr = jnp.broadcast_to(lam_ref[:, :n], (SUBLANES, n))
    li = jnp.broadcast_to(lam_ref[:, n:], (SUBLANES, n))

    def step(j, c):
        xr, xi = c
        off = pl.multiple_of(j * SUBLANES, SUBLANES)
        ar = st_ref[pl.ds(off, SUBLANES), :n]
        ai = st_ref[pl.ds(off, SUBLANES), n:]
        nr = lr * xr - li * xi + ar
        ni = lr * xi + li * xr + ai
        st_ref[pl.ds(off, SUBLANES), :n] = nr
        st_ref[pl.ds(off, SUBLANES), n:] = ni
        return nr, ni

    zero = jnp.zeros((SUBLANES, n), F32)
    xr, xi = lax.fori_loop(0, S5_SEG, step, (zero, zero), unroll=4)

    l64r, l64i = lam64_ref[:, :n], lam64_ref[:, n:]
    cr, ci = carry_ref[:, :n], carry_ref[:, n:]
    for i in range(SUBLANES):
        xm_ref[i:i + 1, :n] = cr
        xm_ref[i:i + 1, n:] = ci
        er, ei = xr[i:i + 1, :], xi[i:i + 1, :]
        cr, ci = er + l64r * cr - l64i * ci, ei + l64r * ci + l64i * cr
    carry_ref[:, :n] = cr
    carry_ref[:, n:] = ci

    hr, hi = xm_ref[:, :n], xm_ref[:, n:]

    def fix(j, _):
        off = pl.multiple_of(j * SUBLANES, SUBLANES)
        pr = pow_ref[pl.ds(j, 1), :n]
        pi = pow_ref[pl.ds(j, 1), n:]
        st_ref[pl.ds(off, SUBLANES), :n] = st_ref[pl.ds(off, SUBLANES), :n] + pr * hr - pi * hi
        st_ref[pl.ds(off, SUBLANES), n:] = st_ref[pl.ds(off, SUBLANES), n:] + pr * hi + pi * hr
        return 0

    lax.fori_loop(0, S5_SEG, fix, 0, unroll=4)

    y = d_ref[...] * u
    kc = 512
    for k in range(2 * n // kc):
        y = y + _dot(st_ref[:, k * kc:(k + 1) * kc].astype(_BF), cbd_ref[k * kc:(k + 1) * kc, :])
    y = jax.nn.gelu(y)
    y = y * jax.nn.sigmoid(_dot(y.astype(_BF), gw_ref[...]) + gb_ref[...])
    for c in range(nhalf):
        yp_ref[c] = y[:, c * LANES:(c + 1) * LANES]
    for i in range(SUBLANES):
        for c in range(nhalf):
            o_ref[i * S5_SEG:(i + 1) * S5_SEG, c * LANES:(c + 1) * LANES] = yp_ref[
                c, pl.ds(i, S5_SEG, stride=SUBLANES), :]


def _s5(u_perm, prm):
    bsz, seq, _ = u_perm.shape
    grid = (bsz, seq // S5_TILE)
    row = pl.BlockSpec((None, S5_TILE, SSM_WIDTH), lambda b, i: (b, i, 0))
    return pl.pallas_call(
        _s5_kernel, grid=grid,
        in_specs=[row] + [_const_spec(p.shape) for p in prm],
        out_specs=row,
        out_shape=jax.ShapeDtypeStruct((bsz, seq, SSM_WIDTH), F32),
        scratch_shapes=[pltpu.VMEM((S5_TILE, 2 * NSTATE), F32),
                        pltpu.VMEM((SUBLANES, 2 * NSTATE), F32),
                        pltpu.VMEM((1, 2 * NSTATE), F32),
                        pltpu.VMEM((SSM_WIDTH // LANES, S5_TILE, LANES), F32),
                        pltpu.VMEM((SSM_WIDTH // LANES, S5_TILE, LANES), F32)],
        compiler_params=_params(("arbitrary", "arbitrary")), name="s5",
    )(u_perm, *prm)


def _s5_params(lam_re, lam_im, log_step, b_re, b_im, c_re, c_im, d_skip, glu_w, glu_b):
    step = jnp.exp(log_step.astype(F32))[:, None]
    ere, eim = lam_re * step, lam_im * step
    mag = jnp.exp(ere)
    lb_re, lb_im = mag * jnp.cos(eim), mag * jnp.sin(eim)
    den = lam_re * lam_re + lam_im * lam_im
    f_re = ((lb_re - 1.0) * lam_re + lb_im * lam_im) / den
    f_im = (lb_im * lam_re - (lb_re - 1.0) * lam_im) / den
    bb_re = f_re[..., None] * b_re - f_im[..., None] * b_im
    bb_im = f_re[..., None] * b_im + f_im[..., None] * b_re
    eye = jnp.eye(SSM_GROUPS, dtype=F32)
    bbd = jnp.concatenate(
        [jnp.einsum('gpc,gh->gchp', bb, eye).reshape(SSM_WIDTH, NSTATE) for bb in (bb_re, bb_im)], axis=1)
    cbd = jnp.concatenate(
        [jnp.einsum('gcp,gh->hpgc', cc, eye).reshape(NSTATE, SSM_WIDTH) for cc in (c_re, -c_im)], axis=0)
    flat = lambda a: a.reshape(1, NSTATE)
    lam = jnp.concatenate([flat(lb_re), flat(lb_im)], axis=1)

    def power(k):
        k = jnp.asarray(k, F32).reshape(-1, 1)
        m = jnp.exp(k * flat(ere))
        return jnp.concatenate([m * jnp.cos(k * flat(eim)), m * jnp.sin(k * flat(eim))], axis=1)

    return (bbd.astype(_BF), cbd.astype(_BF), lam, power(float(S5_SEG)),
            power(jnp.arange(1, S5_SEG + 1)), d_skip.reshape(1, SSM_WIDTH),
            glu_w.astype(_BF), glu_b.reshape(1, SSM_WIDTH))


def _dsa_kernel(iq_ref, iwt_ref, dq_ref, ik_ref, dk_ref, dvt_ref, o_ref,
                planes_ref, cand_ref, pick_ref, bias_ref, acc_ref,
                lg0_ref, lg1_ref, s0_ref, s1_ref, p0_ref, p1_ref, *, seq, topk):
    qi = pl.program_id(1)
    nk = (qi * QB) // KT + 1
    lane = lax.broadcasted_iota(jnp.int32, (1, QB), 1)
    vis_end = qi * QB + jnp.where(lane < CHUNK, CHUNK, 2 * CHUNK)

    iq = iq_ref[...]
    col = lax.broadcasted_iota(jnp.int32, (QB, LANES), 1)
    qs = jnp.concatenate([jnp.where(col // IDX_DIM == h, iq, 0.0) for h in range(IDX_HEADS)],
                         axis=0).astype(_BF)
    w = iwt_ref[...]

    nkp = nk + lax.rem(nk, 2)
    last_tile = nkp - 1

    def key_tile(ref, kt):
        return ref[pl.ds(pl.multiple_of(kt * KT, KT), KT), :]

    def tile_planes(src_ref):
        lg = src_ref[...]
        sc = jnp.zeros((KT, QB), F32)
        for h in range(IDX_HEADS):
            sc = sc + jnp.maximum(lg[:, h * QB:(h + 1) * QB], 0.0) * w[h:h + 1, :]
        sc = jnp.where(sc == 0.0, 0.0, sc)
        ukey = _to_ukey(sc)
        words = []
        for g in range(KT // GROUP):
            words += _transpose32([ukey[g * GROUP + m * SUBLANES:g * GROUP + (m + 1) * SUBLANES, :]
                                   for m in range(32)])
        return lax.bitcast_convert_type(jnp.concatenate(words, axis=0), jnp.int32)

    def score_pair(i, c):
        ik1 = key_tile(ik_ref, 2 * i + 1)
        ik2 = key_tile(ik_ref, jnp.minimum(2 * i + 2, last_tile))
        lg1_ref[...] = _dot_nt(ik1, qs)
        planes0 = tile_planes(lg0_ref)
        lg0_ref[...] = _dot_nt(ik2, qs)
        planes1 = tile_planes(lg1_ref)
        planes_ref[pl.ds(pl.multiple_of(i * (2 * KT), 2 * KT), 2 * KT), :] = jnp.concatenate(
            [planes0, planes1], axis=0)
        return c

    lg0_ref[...] = _dot_nt(key_tile(ik_ref, 0), qs)
    npairs = nkp // 2

    def pair_loops(pair_fn, carry):
        start = 0
        for u in PAIR_UNROLL:
            def body(j, c, u=u, start=start):
                for t in range(u):
                    c = pair_fn(start + j * u + t, c)
                return c
            n = (npairs - start) // u
            carry = lax.fori_loop(0, n, body, carry)
            start = start + n * u
        return carry

    pair_loops(score_pair, 0)

    ngrp = nkp * (KT // GROUP)
    sub = lax.broadcasted_iota(jnp.int32, (SUBLANES, QB), 0)
    zero8 = jnp.zeros((SUBLANES, QB), jnp.int32)

    def rows_below(limit, g):
        nm = lax.shift_right_arithmetic(limit - g * GROUP + (SUBLANES - 1), 3)
        top = lax.shift_right_arithmetic(jnp.full((SUBLANES, QB), INT_MIN, jnp.int32),
                                         jnp.clip(nm, 1, 32) - 1)
        return jnp.where(nm <= 0, 0, top)

    def word(ref, g):
        return ref[pl.ds(pl.multiple_of(g * SUBLANES, SUBLANES), SUBLANES), :]

    def put_word(ref, g, v):
        ref[pl.ds(pl.multiple_of(g * SUBLANES, SUBLANES), SUBLANES), :] = v

    def plane(g, i):
        return planes_ref[pl.ds(pl.multiple_of(g * GROUP + i * SUBLANES, SUBLANES), SUBLANES), :]

    def sweep_words(step):
        def body(s, accs):
            return tuple(a + step(s * GROUPS_PER_STEP + t) for t, a in enumerate(accs))
        accs = lax.fori_loop(0, ngrp // GROUPS_PER_STEP, body, (zero8,) * GROUPS_PER_STEP)
        return functools.reduce(lambda a, b: a + b, accs)

    def sweep(step):
        return sweep_words(step).sum(axis=0, keepdims=True)

    def init_words(g):
        put_word(cand_ref, g, rows_below(vis_end - sub, g))
        put_word(pick_ref, g, zero8)
        return zero8

    sweep(init_words)

    def count_ones(i):
        return sweep(lambda g: lax.population_count(word(cand_ref, g) & plane(g, i)))

    def decide(c1, rem):
        take = c1 >= rem
        return jnp.where(take, 1, 0), jnp.where(take, rem, rem - c1)

    def narrow(g, i, take):
        take8 = jnp.broadcast_to(take, (SUBLANES, QB)) != 0
        c = word(cand_ref, g)
        ones = c & plane(g, i)
        put_word(pick_ref, g, word(pick_ref, g) | jnp.where(take8, 0, ones))
        c = jnp.where(take8, ones, c ^ ones)
        put_word(cand_ref, g, c)
        return c

    def radix_step(i, carry):
        take8, rem = carry

        def step(g):
            return lax.population_count(narrow(g, i - 1, take8) & plane(g, i))
        return decide(sweep(step), rem)

    take8, rem = decide(count_ones(0), jnp.full((1, QB), topk, jnp.int32))
    take8, rem = lax.fori_loop(1, 32, radix_step, (take8, rem))

    def last_narrow(g):
        narrow(g, 31, take8)
        return zero8

    sweep(last_narrow)

    def group_step(i, carry):
        grp, before = carry
        trial = grp + jnp.left_shift(jnp.int32(1), ((seq // GROUP).bit_length() - 2) - i)
        below = sweep(lambda g: jnp.where(jnp.broadcast_to(g < trial, (SUBLANES, QB)),
                                          lax.population_count(word(cand_ref, g)), 0))
        ok = below < rem
        return jnp.where(ok, trial, grp), jnp.where(ok, below, before)

    zrow = jnp.zeros((1, QB), jnp.int32)
    grp, before = lax.fori_loop(0, (seq // GROUP).bit_length() - 1, group_step, (zrow, zrow))
    tied = sweep_words(lambda g: jnp.where(jnp.broadcast_to(g == grp, (SUBLANES, QB)), word(cand_ref, g), 0))

    def row_step(i, edge):
        trial = edge + jnp.left_shift(jnp.int32(1), (GROUP.bit_length() - 2) - i)
        below = lax.population_count(tied & rows_below(trial - sub, 0)).sum(axis=0, keepdims=True)
        return jnp.where(before + below < rem, trial, edge)

    edge = lax.fori_loop(0, GROUP.bit_length() - 1, row_step, zrow)
    keep_limit = grp * GROUP + edge + 1 - sub

    def bias_group(g):
        chosen = word(pick_ref, g) | (word(cand_ref, g) & rows_below(keep_limit, g))
        base = pl.multiple_of(g * GROUP, GROUP)
        bias_ref[pl.ds(base, GROUP), :] = jnp.concatenate(
            [jnp.where(lax.shift_left(chosen, jnp.full_like(chosen, m)) < 0, 0.0, NEG_INF_SCORE)
             for m in range(32)], axis=0).astype(bias_ref.dtype)
        return zero8

    sweep(bias_group)

    q = dq_ref[...]
    half = (col // DSA_HEAD_DIM)
    qst = jnp.concatenate(
        [jnp.where(half == (h % 2), q[:, (h // 2) * LANES:(h // 2 + 1) * LANES], jnp.zeros((), q.dtype))
         for h in range(DSA_HEADS)], axis=0)
    unit = jnp.where(lax.broadcasted_iota(jnp.int32, (QB, QB), 0) == lax.broadcasted_iota(jnp.int32, (QB, QB), 1),
                     1.0, 0.0).astype(q.dtype)
    qst = jnp.concatenate([qst, jnp.concatenate([unit] * DSA_HEADS, axis=0)], axis=1)
    npair = DSA_HEADS // 2
    hq = DSA_HEADS * QB
    acc_ref[...] = jnp.zeros_like(acc_ref)
    p1_ref[...] = jnp.zeros_like(p1_ref)

    def qk(k_tile, b_tile, dst_ref):
        dst_ref[...] = _dot_nt(jnp.concatenate([k_tile, b_tile], axis=1), qst)

    def numer(src_ref, dst_ref, m):
        m_new = []
        for hd in range(DSA_HEADS):
            cols = slice(hd * QB, (hd + 1) * QB)
            mh = jnp.maximum(m[:, cols], jnp.max(src_ref[:, cols], axis=0, keepdims=True))
            dst_ref[:, cols] = jnp.exp2(src_ref[:, cols] - mh).astype(dst_ref.dtype)
            m_new.append(mh)
        m_new = jnp.concatenate(m_new, axis=1)
        return m_new, jnp.exp2(m - m_new)

    def v_tile(kt):
        return dvt_ref[0:V_ROWS, pl.ds(pl.multiple_of(kt * KT, KT), KT)]

    def pv(vt, src_ref, alpha):
        acc_ref[...] = alpha * acc_ref[...] + _dot(vt, src_ref[...])

    def att_pair(i, c):
        m, alpha = c
        nxt = jnp.minimum(2 * i + 2, last_tile)
        k1, b1 = key_tile(dk_ref, 2 * i + 1), key_tile(bias_ref, 2 * i + 1)
        k2, b2 = key_tile(dk_ref, nxt), key_tile(bias_ref, nxt)
        v_prev, v_cur = v_tile(jnp.maximum(2 * i - 1, 0)), v_tile(2 * i)
        qk(k1, b1, s1_ref)
        pv(v_prev, p1_ref, alpha)
        m, alpha = numer(s0_ref, p0_ref, m)
        qk(k2, b2, s0_ref)
        pv(v_cur, p0_ref, alpha)
        m, alpha = numer(s1_ref, p1_ref, m)
        return m, alpha

    qk(key_tile(dk_ref, 0), key_tile(bias_ref, 0), s0_ref)
    c0 = (jnp.full((1, hq), -jnp.inf, F32), jnp.ones((1, hq), F32))
    _, alpha = pair_loops(att_pair, c0)
    pv(v_tile(last_tile), p1_ref, alpha)
    acc = acc_ref[...]
    o = acc[0:DSA_HEAD_DIM, :] / acc[DSA_HEAD_DIM:DSA_HEAD_DIM + 1, :]
    for j in range(npair):
        o_ref[:, j * LANES:(j + 1) * LANES] = jnp.concatenate(
            [o[:, (2 * j) * QB:(2 * j + 1) * QB], o[:, (2 * j + 1) * QB:(2 * j + 2) * QB]], axis=0).T


def _dsa(iq, iwt, dq, ik, dk, dvt):
    bsz, seq, _ = iq.shape
    topk = min(DSA_TOPK, seq // 4)
    grid = (bsz, seq // QB)
    qrow = lambda w: pl.BlockSpec((None, QB, w), lambda b, i: (b, i, 0))
    full = lambda r, c: pl.BlockSpec((None, r, c), lambda b, i: (b, 0, 0))
    return pl.pallas_call(
        functools.partial(_dsa_kernel, seq=seq, topk=topk), grid=grid,
        in_specs=[qrow(LANES), pl.BlockSpec((None, SUBLANES, QB), lambda b, i: (b, 0, i)), qrow(DSA_WIDTH),
                  full(seq, LANES), full(seq, LANES), full(LANES, seq)],
        out_specs=qrow(DSA_WIDTH),
        out_shape=jax.ShapeDtypeStruct((bsz, seq, DSA_WIDTH), F32),
        scratch_shapes=[pltpu.VMEM((seq, QB), jnp.int32),
                        pltpu.VMEM((seq // GROUP * SUBLANES, QB), jnp.int32),
                        pltpu.VMEM((seq // GROUP * SUBLANES, QB), jnp.int32),
                        pltpu.VMEM((seq, QB), _BF),
                        pltpu.VMEM((V_ROWS, DSA_HEADS * QB), F32),
                        pltpu.VMEM((KT, IDX_HEADS * QB), F32), pltpu.VMEM((KT, IDX_HEADS * QB), F32),
                        pltpu.VMEM((KT, DSA_HEADS * QB), F32), pltpu.VMEM((KT, DSA_HEADS * QB), F32),
                        pltpu.VMEM((KT, DSA_HEADS * QB), _BF), pltpu.VMEM((KT, DSA_HEADS * QB), _BF)],
        compiler_params=_params(("parallel", "arbitrary")), name="dsa",
    )(iq, iwt, dq, ik, dk, dvt)


def _ret_kernel(rq_ref, rk_ref, rv_ref, rg_ref, qdec_ref, kdec_ref, dmask_ref, cdec_ref, blk_ref,
                hmq_ref, hmv_ref, ones_ref, ng_ref, o_ref, s_ref):
    @pl.when(pl.program_id(0) == 0)
    def _():
        s_ref[...] = jnp.zeros_like(s_ref)

    ones = ones_ref[...]

    def head_mean(z):
        return _dot(z.astype(_BF), ones) * (1.0 / RET_V_DIM)

    for b in range(rq_ref.shape[0]):
        q, k, v = rq_ref[b], rk_ref[b], rv_ref[b]
        state = s_ref[b]
        y = _dot((q * qdec_ref[...]).astype(_BF), state.astype(_BF))
        kd = (k * kdec_ref[...]).T.astype(_BF)
        vb = v.astype(_BF)
        s_ref[b] = state * cdec_ref[...] + _dot(kd, vb) * blk_ref[...]
        kb = k.astype(_BF)
        a = jnp.concatenate([(_dot_nt((q * hmq_ref[h:h + 1, :]).astype(_BF), kb) * dmask_ref[h]).astype(_BF)
                             for h in range(RET_HEADS)], axis=1)
        vh = jnp.concatenate([(v * hmv_ref[h:h + 1, :]).astype(_BF) for h in range(RET_HEADS)], axis=0)
        y = y + _dot(a, vh)
        d = y - head_mean(y)
        yn = d * lax.rsqrt(head_mean(d * d) + EPS) * ng_ref[...]
        o_ref[b] = jax.nn.silu(rg_ref[b]) * yn


def _ret_consts():
    c = RET_C
    log_g = jnp.log1p(-jnp.exp2(-5.0 - jnp.arange(RET_HEADS, dtype=F32)))
    pos = jnp.arange(c, dtype=F32)
    diff = pos[:, None] - pos[None, :]
    dmask = jnp.where(diff >= 0, jnp.exp(log_g[:, None, None] * jnp.maximum(diff, 0.0)), 0.0)
    lane_q = jnp.arange(RET_QK_PAD)
    head_q = jnp.where((lane_q % LANES) < RET_HEADS * (RET_QK_DIM // 2), (lane_q % LANES) // (RET_QK_DIM // 2), -1)
    head_v = jnp.arange(RET_WIDTH) // RET_V_DIM
    hmq = (head_q[None, :] == jnp.arange(RET_HEADS)[:, None]).astype(F32)
    hmv = (head_v[None, :] == jnp.arange(RET_HEADS)[:, None]).astype(F32)
    lg_q = hmq.T @ log_g
    qdec = jnp.exp(lg_q[None, :] * (pos[:, None] + 1.0))
    kdec = jnp.exp(lg_q[None, :] * (c - 1.0 - pos[:, None]))
    blk = hmq.T @ hmv
    cdec = blk * jnp.exp(lg_q * c)[:, None]
    ones = (hmv.T @ hmv).astype(_BF)
    return qdec, kdec, dmask, cdec, blk, hmq, hmv, ones


def _retention(rq, rk, rv, rg, norm_g):
    bsz, seq, _ = rq.shape
    consts = _ret_consts()
    grid = (seq // RET_C,)
    row = lambda w: pl.BlockSpec((bsz, RET_C, w), lambda i: (0, i, 0))
    ng = norm_g.reshape(1, RET_WIDTH)
    return pl.pallas_call(
        _ret_kernel, grid=grid,
        in_specs=[row(RET_QK_PAD), row(RET_QK_PAD), row(RET_WIDTH), row(RET_WIDTH)]
        + [_const_spec(a.shape) for a in consts] + [_const_spec(ng.shape)],
        out_specs=row(RET_WIDTH),
        out_shape=jax.ShapeDtypeStruct((bsz, seq, RET_WIDTH), F32),
        scratch_shapes=[pltpu.VMEM((bsz, RET_QK_PAD, RET_WIDTH), F32)],
        compiler_params=_params(("arbitrary",)), name="retention",
    )(rq, rk, rv, rg, *consts, ng)


def _merge_kernel(x_ref, g_ref, ya_ref, yb_ref, yc_ref, wg_ref, wa_ref, wb_ref, wc_ref, wo_ref, o_ref):
    x = x_ref[...]
    h = _rms(x, g_ref[...]).astype(_BF)
    merged = jnp.zeros(x.shape, F32)
    for n, (y_ref, w_ref) in enumerate(((ya_ref, wa_ref), (yb_ref, wb_ref), (yc_ref, wc_ref))):
        gate = jax.nn.sigmoid(_dot_nt(h, wg_ref[n * D_MODEL:(n + 1) * D_MODEL, :]))
        merged = merged + gate * _dot(y_ref[...].astype(_BF), w_ref[...])
    o_ref[...] = x + _dot(merged.astype(_BF), wo_ref[...])


def _merge(x, g, ya, yb, yc, wg, wa, wb, wc, wo):
    bsz, seq, _ = x.shape
    grid = (bsz, seq // TM_OUT)
    row = lambda w: pl.BlockSpec((None, TM_OUT, w), lambda b, i: (b, i, 0))
    return pl.pallas_call(
        _merge_kernel, grid=grid,
        in_specs=[row(D_MODEL), _const_spec(g.shape), row(SSM_WIDTH), row(DSA_WIDTH), row(RET_WIDTH)]
        + [_const_spec(w.shape) for w in (wg, wa, wb, wc, wo)],
        out_specs=row(D_MODEL),
        out_shape=jax.ShapeDtypeStruct(x.shape, F32),
        compiler_params=_params(("parallel", "parallel")), name="merge",
    )(x, g, ya, yb, yc, wg, wa, wb, wc, wo)


def _mlp_kernel(x_ref, g_ref, w1_ref, w2_ref, fg_ref, o_ref, *, final_norm):
    x = x_ref[...]
    h = _rms(x, g_ref[...]).astype(_BF)
    acc = x
    for f in range(D_FF // FF_CHUNK):
        t = jnp.maximum(_dot(h, w1_ref[:, f * FF_CHUNK:(f + 1) * FF_CHUNK]), 0.0)
        acc = acc + _dot((t * t).astype(_BF), w2_ref[f * FF_CHUNK:(f + 1) * FF_CHUNK, :])
    o_ref[...] = _rms(acc, fg_ref[...]) if final_norm else acc


def _mlp(x, g, w1, w2, fg, final_norm):
    bsz, seq, _ = x.shape
    grid = (bsz, seq // TM_OUT)
    row = pl.BlockSpec((None, TM_OUT, D_MODEL), lambda b, i: (b, i, 0))
    return pl.pallas_call(
        functools.partial(_mlp_kernel, final_norm=final_norm), grid=grid,
        in_specs=[row, _const_spec(g.shape), _const_spec(w1.shape), _const_spec(w2.shape), _const_spec(fg.shape)],
        out_specs=row,
        out_shape=jax.ShapeDtypeStruct(x.shape, F32),
        compiler_params=_params(("parallel", "parallel")), name="mlp",
    )(x, g, w1, w2, fg)


def _rope_tables(positions):
    pos = positions.astype(F32)[..., None]
    lane = jnp.arange(LANES)

    def angles(rot_dim, theta):
        half = rot_dim // 2
        inv = jnp.exp(-math.log(theta) * jnp.arange(half, dtype=F32) * (2.0 / rot_dim))
        return pos * inv

    def roll_tables(head_dim, rot_dim, theta):
        half = rot_dim // 2
        ang = angles(rot_dim, theta)
        ln = lane % head_dim
        rot = ln < rot_dim
        cos = jnp.where(rot, jnp.take(jnp.cos(ang), ln % half, axis=-1), 1.0)
        sin = jnp.where(rot, jnp.take(jnp.sin(ang), ln % half, axis=-1), 0.0)
        sgn = jnp.stack([jnp.where(ln < half, -1.0, 0.0),
                         jnp.where((ln >= half) & rot, 1.0, 0.0)]).astype(F32)
        return cos, sin, sgn

    cd, sd, sgd = roll_tables(DSA_HEAD_DIM, DSA_HEAD_DIM // 4, ROPE_THETA)
    ci, si, sgi = roll_tables(IDX_DIM, IDX_DIM // 4, ROPE_THETA)
    half = RET_QK_DIM // 2
    ang = angles(RET_QK_DIM, RET_THETA)
    used = lane < RET_HEADS * half
    cr = jnp.where(used, jnp.take(jnp.cos(ang), lane % half, axis=-1), 1.0)
    sr = jnp.where(used, jnp.take(jnp.sin(ang), lane % half, axis=-1), 0.0)
    return cd, sd, ci, si, cr, sr, sgd, sgi


def _inproj_weights(wt):
    sizes = (SSM_WIDTH, DSA_WIDTH, DSA_HEAD_DIM, DSA_HEAD_DIM, IDX_HEADS * IDX_DIM, IDX_DIM, IDX_HEADS,
             RET_HEADS * RET_QK_DIM, RET_HEADS * RET_QK_DIM, RET_WIDTH, RET_WIDTH, 3 * D_MODEL)
    parts, off = [], 0
    for n in sizes:
        parts.append(wt[off:off + n, :])
        off += n
    wu, wdq, wdk, wdv, wiq, wik, wiw, wrq, wrk, wrv, wrg, wgt = parts
    half = RET_QK_DIM // 2
    zrows = lambda n: jnp.zeros((n, D_MODEL), wt.dtype)

    def ret_split(a):
        a = a.reshape(RET_HEADS, 2, half, D_MODEL)
        pad = zrows(LANES - RET_HEADS * half)
        return jnp.concatenate([a[:, 0].reshape(-1, D_MODEL), pad, a[:, 1].reshape(-1, D_MODEL), pad], axis=0)

    ws = (wu, wdq, jnp.concatenate([wdk, wdk], axis=0), jnp.concatenate([wdv, zrows(LANES - DSA_HEAD_DIM)], axis=0),
          wiq, jnp.concatenate([wik] * IDX_HEADS, axis=0), jnp.concatenate([wiw, zrows(SUBLANES - IDX_HEADS)], axis=0),
          ret_split(wrq), ret_split(wrk), wrv, wrg)
    return tuple(a.astype(_BF) for a in ws), wgt.astype(_BF)


def kernel(x, positions, norm1_g, w_in, ssm_lambda_re, ssm_lambda_im, ssm_log_step, ssm_b_re, ssm_b_im,
           ssm_c_re, ssm_c_im, ssm_d, ssm_glu_w, ssm_glu_b, ret_norm_g, w_proj_a, w_proj_b, w_proj_c,
           w_out, norm2_g, w_ff1, w_ff2, final_norm_g):
    depth = w_in.shape[0]
    tabs = _rope_tables(positions)
    fg = final_norm_g.reshape(1, D_MODEL)
    w_in_t = jnp.transpose(w_in, (2, 0, 1))
    for l in range(depth):
        g1 = norm1_g[l].reshape(1, D_MODEL)
        ws, wgt = _inproj_weights(w_in_t[:, l, :])
        u, dq, dk, dvt, iq, ik, iwt, rq, rk, rv, rg = _inproj(x, g1, tabs, ws)

        s5p = _s5_params(ssm_lambda_re[l], ssm_lambda_im[l], ssm_log_step[l], ssm_b_re[l], ssm_b_im[l],
                         ssm_c_re[l], ssm_c_im[l], ssm_d[l], ssm_glu_w[l], ssm_glu_b[l])
        ya = _s5(u, s5p)
        yb = _dsa(iq, iwt, dq, ik, dk, dvt)
        yc = _retention(rq, rk, rv, rg, ret_norm_g[l])

        x = _merge(x, g1, ya, yb, yc, wgt, w_proj_a[l].astype(_BF), w_proj_b[l].astype(_BF),
                   w_proj_c[l].astype(_BF), w_out[l].astype(_BF))
        x = _mlp(x, norm2_g[l].reshape(1, D_MODEL), w_ff1[l].astype(_BF), w_ff2[l].astype(_BF), fg,
                 final_norm=(l == depth - 1))
    return x
```

```python
import functools
import math

import jax
import jax.numpy as jnp
from jax import lax
from jax.experimental import pallas as pl
from jax.experimental.pallas import tpu as pltpu

F32 = jnp.float32
_BF = jnp.bfloat16
LANES = 128
SUBLANES = 8
VMEM_LIMIT = 56 * 1024 * 1024

D_MODEL = 1024
CHUNK = 64
EPS = 1e-6
NEG_INF_SCORE = -1e30
INT_MIN = -2 ** 31
GROUP = 32 * SUBLANES
GROUP_UNROLL = (16, 8, 4)
PAIR_UNROLL = (4, 2, 1)
V_ROWS = 80

SSM_WIDTH = 256
SSM_GROUP = 16
SSM_GROUPS = 16
SSM_STATE = 64
NSTATE = SSM_GROUPS * SSM_STATE

DSA_HEADS = 6
DSA_HEAD_DIM = 64
DSA_WIDTH = 384
IDX_HEADS = 4
IDX_DIM = 32
DSA_TOPK = 256
ROPE_THETA = 500000.0

RET_HEADS = 4
RET_QK_DIM = 48
RET_V_DIM = 96
RET_WIDTH = 384
RET_THETA = 10000.0
RET_QK_PAD = 256

D_FF = 4096

TM_IN = 512
S5_TILE = 512
S5_SEG = S5_TILE // SUBLANES
QB = 128
KT = 512
RET_C = 128
TM_OUT = 512
FF_CHUNK = 1024


def _rms(x, g):
    return x * lax.rsqrt(jnp.mean(x * x, axis=-1, keepdims=True) + EPS) * g


def _dot(a, b):
    return jnp.dot(a, b, preferred_element_type=F32)


def _dot_nt(a, b):
    return lax.dot_general(a, b, (((1,), (1,)), ((), ())), preferred_element_type=F32)


def _to_ukey(v):
    bits = lax.bitcast_convert_type(v, jnp.int32)
    return lax.bitcast_convert_type(bits ^ ((bits >> 31) | INT_MIN), jnp.uint32)


def _transpose32(a):
    a = list(a)
    j, m = 16, 0x0000FFFF
    while j:
        k = 0
        while k < 32:
            t = (a[k] ^ (a[k + j] >> j)) & m
            a[k] = a[k] ^ t
            a[k + j] = a[k + j] ^ (t << j)
            k = (k + j + 1) & ~j
        j >>= 1
        m = (m ^ (m << j)) & 0xFFFFFFFF
    return a


def _const_spec(shape):
    nd = len(shape)
    return pl.BlockSpec(shape, lambda *_: (0,) * nd)


def _params(sem):
    return pltpu.CompilerParams(dimension_semantics=sem, vmem_limit_bytes=VMEM_LIMIT)


def _rope_roll(z, cos, sin, sgn, shift):
    outs = []
    for c in range(z.shape[1] // LANES):
        zc = z[:, c * LANES:(c + 1) * LANES]
        rot = (pltpu.roll(zc, LANES - shift, 1) * sgn[0:1, :]
               + pltpu.roll(zc, shift, 1) * sgn[1:2, :])
        outs.append(zc * cos + rot * sin)
    return outs[0] if len(outs) == 1 else jnp.concatenate(outs, axis=1)


def _inproj_kernel(x_ref, g_ref, cd_ref, sd_ref, ci_ref, si_ref, cr_ref, sr_ref, sgd_ref, sgi_ref,
                   wu_ref, wdq_ref, wdk_ref, wdvt_ref, wiq_ref, wik_ref, wiwt_ref,
                   wrq_ref, wrk_ref, wrv_ref, wrg_ref,
                   u_ref, dq_ref, dk_ref, dvt_ref, iq_ref, ik_ref, iwt_ref,
                   rq_ref, rk_ref, rv_ref, rg_ref):
    h = _rms(x_ref[...], g_ref[...]).astype(_BF)

    def proj(w_ref):
        return _dot_nt(h, w_ref[...])

    u_ref[...] = proj(wu_ref)

    cd, sd, sgd = cd_ref[...], sd_ref[...], sgd_ref[...]
    dq = _rope_roll(proj(wdq_ref), cd, sd, sgd, 8)
    dq_ref[...] = (dq * (DSA_HEAD_DIM ** -0.5 * math.log2(math.e))).astype(dq_ref.dtype)
    dk_ref[...] = _rope_roll(proj(wdk_ref), cd, sd, sgd, 8).astype(dk_ref.dtype)
    vt = _dot_nt(wdvt_ref[...], h)
    ones_row = lax.broadcasted_iota(jnp.int32, vt.shape, 0) == DSA_HEAD_DIM
    dvt_ref[...] = jnp.where(ones_row, 1.0, vt).astype(dvt_ref.dtype)

    ci, si, sgi = ci_ref[...], si_ref[...], sgi_ref[...]
    iq_ref[...] = _rope_roll(proj(wiq_ref), ci, si, sgi, 4) * (IDX_DIM ** -0.5)
    ik_ref[...] = _rope_roll(proj(wik_ref), ci, si, sgi, 4).astype(ik_ref.dtype)
    iwt_ref[...] = _dot_nt(wiwt_ref[...], h) * (IDX_HEADS ** -0.5)

    cr, sr = cr_ref[...], sr_ref[...]

    def rope_split(z):
        x1, x2 = z[:, :LANES], z[:, LANES:]
        return jnp.concatenate([x1 * cr - x2 * sr, x2 * cr + x1 * sr], axis=1)

    rq_ref[...] = rope_split(proj(wrq_ref))
    rk_ref[...] = rope_split(proj(wrk_ref)) * (RET_QK_DIM ** -0.5)
    rv_ref[...] = proj(wrv_ref)
    rg_ref[...] = proj(wrg_ref)


def _inproj(x, g, tabs, ws):
    bsz, seq, _ = x.shape
    tm = TM_IN
    grid = (bsz, seq // tm)
    row = lambda w: pl.BlockSpec((None, tm, w), lambda b, i: (b, i, 0))
    colT = lambda r: pl.BlockSpec((None, r, tm), lambda b, i: (b, 0, i))
    in_specs = ([row(D_MODEL), _const_spec(g.shape)] + [row(LANES)] * 6
                + [_const_spec(t.shape) for t in tabs[6:]] + [_const_spec(w.shape) for w in ws])
    out_shape = [
        jax.ShapeDtypeStruct((bsz, seq, SSM_WIDTH), F32),
        jax.ShapeDtypeStruct((bsz, seq, DSA_WIDTH), _BF),
        jax.ShapeDtypeStruct((bsz, seq, LANES), _BF),
        jax.ShapeDtypeStruct((bsz, LANES, seq), _BF),
        jax.ShapeDtypeStruct((bsz, seq, LANES), F32),
        jax.ShapeDtypeStruct((bsz, seq, LANES), _BF),
        jax.ShapeDtypeStruct((bsz, SUBLANES, seq), F32),
        jax.ShapeDtypeStruct((bsz, seq, RET_QK_PAD), F32),
        jax.ShapeDtypeStruct((bsz, seq, RET_QK_PAD), F32),
        jax.ShapeDtypeStruct((bsz, seq, RET_WIDTH), F32),
        jax.ShapeDtypeStruct((bsz, seq, RET_WIDTH), F32),
    ]
    out_specs = [row(SSM_WIDTH), row(DSA_WIDTH), row(LANES), colT(LANES), row(LANES), row(LANES),
                 colT(SUBLANES), row(RET_QK_PAD), row(RET_QK_PAD), row(RET_WIDTH), row(RET_WIDTH)]
    return pl.pallas_call(
        _inproj_kernel, grid=grid, in_specs=in_specs, out_specs=out_specs, out_shape=out_shape,
        compiler_params=_params(("parallel", "parallel")), name="inproj",
    )(x, g, *tabs, *ws)


def _s5_kernel(u_ref, bbd_ref, cbd_ref, lam_ref, lam64_ref, pow_ref, d_ref, gw_ref, gb_ref,
               o_ref, st_ref, xm_ref, carry_ref, up_ref, yp_ref):
    n = NSTATE

    @pl.when(pl.program_id(1) == 0)
    def _():
        carry_ref[...] = jnp.zeros_like(carry_ref)

    nhalf = SSM_WIDTH // LANES
    for i in range(SUBLANES):
        for c in range(nhalf):
            up_ref[c, pl.ds(i, S5_SEG, stride=SUBLANES), :] = u_ref[i * S5_SEG:(i + 1) * S5_SEG,
                                                                    c * LANES:(c + 1) * LANES]
    u = jnp.concatenate([up_ref[c] for c in range(nhalf)], axis=1)
    st_ref[...] = _dot(u.astype(_BF), bbd_ref[...])

    lr = jnp.broadcast_to(lam_ref[:, :n], (SUBLANES, n))
    li = jnp.broadcast_to(lam_ref[:, n:], (SUBLANES, n))

    def step(j, c):
        xr, xi = c
        off = pl.multiple_of(j * SUBLANES, SUBLANES)
        ar = st_ref[pl.ds(off, SUBLANES), :n]
        ai = st_ref[pl.ds(off, SUBLANES), n:]
        nr = lr * xr - li * xi + ar
        ni = lr * xi + li * xr + ai
        st_ref[pl.ds(off, SUBLANES), :n] = nr
        st_ref[pl.ds(off, SUBLANES), n:] = ni
        return nr, ni

    zero = jnp.zeros((SUBLANES, n), F32)
    xr, xi = lax.fori_loop(0, S5_SEG, step, (zero, zero), unroll=4)

    l64r, l64i = lam64_ref[:, :n], lam64_ref[:, n:]
    cr, ci = carry_ref[:, :n], carry_ref[:, n:]
    for i in range(SUBLANES):
        xm_ref[i:i + 1, :n] = cr
        xm_ref[i:i + 1, n:] = ci
        er, ei = xr[i:i + 1, :], xi[i:i + 1, :]
        cr, ci = er + l64r * cr - l64i * ci, ei + l64r * ci + l64i * cr
    carry_ref[:, :n] = cr
    carry_ref[:, n:] = ci

    hr, hi = xm_ref[:, :n], xm_ref[:, n:]

    def fix(j, _):
        off = pl.multiple_of(j * SUBLANES, SUBLANES)
        pr = pow_ref[pl.ds(j, 1), :n]
        pi = pow_ref[pl.ds(j, 1), n:]
        st_ref[pl.ds(off, SUBLANES), :n] = st_ref[pl.ds(off, SUBLANES), :n] + pr * hr - pi * hi
        st_ref[pl.ds(off, SUBLANES), n:] = st_ref[pl.ds(off, SUBLANES), n:] + pr * hi + pi * hr
        return 0

    lax.fori_loop(0, S5_SEG, fix, 0, unroll=4)

    y = d_ref[...] * u
    kc = 512
    for k in range(2 * n // kc):
        y = y + _dot(st_ref[:, k * kc:(k + 1) * kc].astype(_BF), cbd_ref[k * kc:(k + 1) * kc, :])
    y = jax.nn.gelu(y)
    y = y * jax.nn.sigmoid(_dot(y.astype(_BF), gw_ref[...]) + gb_ref[...])
    for c in range(nhalf):
        yp_ref[c] = y[:, c * LANES:(c + 1) * LANES]
    for i in range(SUBLANES):
        for c in range(nhalf):
            o_ref[i * S5_SEG:(i + 1) * S5_SEG, c * LANES:(c + 1) * LANES] = yp_ref[
                c, pl.ds(i, S5_SEG, stride=SUBLANES), :]


def _s5(u_perm, prm):
    bsz, seq, _ = u_perm.shape
    grid = (bsz, seq // S5_TILE)
    row = pl.BlockSpec((None, S5_TILE, SSM_WIDTH), lambda b, i: (b, i, 0))
    return pl.pallas_call(
        _s5_kernel, grid=grid,
        in_specs=[row] + [_const_spec(p.shape) for p in prm],
        out_specs=row,
        out_shape=jax.ShapeDtypeStruct((bsz, seq, SSM_WIDTH), F32),
        scratch_shapes=[pltpu.VMEM((S5_TILE, 2 * NSTATE), F32),
                        pltpu.VMEM((SUBLANES, 2 * NSTATE), F32),
                        pltpu.VMEM((1, 2 * NSTATE), F32),
                        pltpu.VMEM((SSM_WIDTH // LANES, S5_TILE, LANES), F32),
                        pltpu.VMEM((SSM_WIDTH // LANES, S5_TILE, LANES), F32)],
        compiler_params=_params(("arbitrary", "arbitrary")), name="s5",
    )(u_perm, *prm)


def _s5_params(lam_re, lam_im, log_step, b_re, b_im, c_re, c_im, d_skip, glu_w, glu_b):
    step = jnp.exp(log_step.astype(F32))[:, None]
    ere, eim = lam_re * step, lam_im * step
    mag = jnp.exp(ere)
    lb_re, lb_im = mag * jnp.cos(eim), mag * jnp.sin(eim)
    den = lam_re * lam_re + lam_im * lam_im
    f_re = ((lb_re - 1.0) * lam_re + lb_im * lam_im) / den
    f_im = (lb_im * lam_re - (lb_re - 1.0) * lam_im) / den
    bb_re = f_re[..., None] * b_re - f_im[..., None] * b_im
    bb_im = f_re[..., None] * b_im + f_im[..., None] * b_re
    eye = jnp.eye(SSM_GROUPS, dtype=F32)
    bbd = jnp.concatenate(
        [jnp.einsum('gpc,gh->gchp', bb, eye).reshape(SSM_WIDTH, NSTATE) for bb in (bb_re, bb_im)], axis=1)
    cbd = jnp.concatenate(
        [jnp.einsum('gcp,gh->hpgc', cc, eye).reshape(NSTATE, SSM_WIDTH) for cc in (c_re, -c_im)], axis=0)
    flat = lambda a: a.reshape(1, NSTATE)
    lam = jnp.concatenate([flat(lb_re), flat(lb_im)], axis=1)

    def power(k):
        k = jnp.asarray(k, F32).reshape(-1, 1)
        m = jnp.exp(k * flat(ere))
        return jnp.concatenate([m * jnp.cos(k * flat(eim)), m * jnp.sin(k * flat(eim))], axis=1)

    return (bbd.astype(_BF), cbd.astype(_BF), lam, power(float(S5_SEG)),
            power(jnp.arange(1, S5_SEG + 1)), d_skip.reshape(1, SSM_WIDTH),
            glu_w.astype(_BF), glu_b.reshape(1, SSM_WIDTH))


def _dsa_kernel(iq_ref, iwt_ref, dq_ref, ik_ref, dk_ref, dvt_ref, o_ref,
                planes_ref, cand_ref, pick_ref, bias_ref, acc_ref,
                lg0_ref, lg1_ref, s0_ref, s1_ref, p0_ref, p1_ref, *, seq, topk):
    qi = pl.program_id(1)
    nk = (qi * QB) // KT + 1
    lane = lax.broadcasted_iota(jnp.int32, (1, QB), 1)
    vis_end = qi * QB + jnp.where(lane < CHUNK, CHUNK, 2 * CHUNK)

    iq = iq_ref[...]
    col = lax.broadcasted_iota(jnp.int32, (QB, LANES), 1)
    qs = jnp.concatenate([jnp.where(col // IDX_DIM == h, iq, 0.0) for h in range(IDX_HEADS)],
                         axis=0).astype(_BF)
    w = iwt_ref[...]

    nkp = nk + lax.rem(nk, 2)
    last_tile = nkp - 1

    def key_tile(ref, kt):
        return ref[pl.ds(pl.multiple_of(kt * KT, KT), KT), :]

    def tile_planes(src_ref):
        lg = src_ref[...]
        sc = jnp.zeros((KT, QB), F32)
        for h in range(IDX_HEADS):
            sc = sc + jnp.maximum(lg[:, h * QB:(h + 1) * QB], 0.0) * w[h:h + 1, :]
        sc = jnp.where(sc == 0.0, 0.0, sc)
        ukey = _to_ukey(sc)
        words = []
        for g in range(KT // GROUP):
            words += _transpose32([ukey[g * GROUP + m * SUBLANES:g * GROUP + (m + 1) * SUBLANES, :]
                                   for m in range(32)])
        return lax.bitcast_convert_type(jnp.concatenate(words, axis=0), jnp.int32)

    def score_pair(i, c):
        ik1 = key_tile(ik_ref, 2 * i + 1)
        ik2 = key_tile(ik_ref, jnp.minimum(2 * i + 2, last_tile))
        lg1_ref[...] = _dot_nt(ik1, qs)
        planes0 = tile_planes(lg0_ref)
        lg0_ref[...] = _dot_nt(ik2, qs)
        planes1 = tile_planes(lg1_ref)
        planes_ref[pl.ds(pl.multiple_of(i * (2 * KT), 2 * KT), 2 * KT), :] = jnp.concatenate(
            [planes0, planes1], axis=0)
        return c

    lg0_ref[...] = _dot_nt(key_tile(ik_ref, 0), qs)
    npairs = nkp // 2

    def pair_loops(pair_fn, carry):
        start = 0
        for u in PAIR_UNROLL:
            def body(j, c, u=u, start=start):
                for t in range(u):
                    c = pair_fn(start + j * u + t, c)
                return c
            n = (npairs - start) // u
            carry = lax.fori_loop(0, n, body, carry)
            start = start + n * u
        return carry

    pair_loops(score_pair, 0)

    ngrp = nkp * (KT // GROUP)
    sub = lax.broadcasted_iota(jnp.int32, (SUBLANES, QB), 0)
    zero8 = jnp.zeros((SUBLANES, QB), jnp.int32)

    def rows_below(limit, g):
        nm = lax.shift_right_arithmetic(limit - g * GROUP + (SUBLANES - 1), 3)
        top = lax.shift_right_arithmetic(jnp.full((SUBLANES, QB), INT_MIN, jnp.int32),
                                         jnp.clip(nm, 1, 32) - 1)
        return jnp.where(nm <= 0, 0, top)

    def word(ref, g):
        return ref[pl.ds(pl.multiple_of(g * SUBLANES, SUBLANES), SUBLANES), :]

    def put_word(ref, g, v):
        ref[pl.ds(pl.multiple_of(g * SUBLANES, SUBLANES), SUBLANES), :] = v

    def plane(g, i):
        return planes_ref[pl.ds(pl.multiple_of(g * GROUP + i * SUBLANES, SUBLANES), SUBLANES), :]

    def sweep_words(step):
        accs, start = (zero8,) * GROUP_UNROLL[-1], 0
        for u in GROUP_UNROLL:
            def body(s, accs, u=u, start=start):
                accs = list(accs)
                for t in range(u):
                    accs[t % len(accs)] = accs[t % len(accs)] + step(start + s * u + t)
                return tuple(accs)
            n = (ngrp - start) // u
            accs = lax.fori_loop(0, n, body, accs)
            start = start + n * u
        return functools.reduce(lambda a, b: a + b, accs)

    def sweep(step):
        return sweep_words(step).sum(axis=0, keepdims=True)

    def init_words(g):
        put_word(cand_ref, g, rows_below(vis_end - sub, g))
        put_word(pick_ref, g, zero8)
        return zero8

    sweep(init_words)

    def count_ones(i):
        return sweep(lambda g: lax.population_count(word(cand_ref, g) & plane(g, i)))

    def decide(c1, rem):
        take = c1 >= rem
        return jnp.where(take, 1, 0), jnp.where(take, rem, rem - c1)

    def narrow(g, i, take):
        take8 = jnp.broadcast_to(take, (SUBLANES, QB)) != 0
        c = word(cand_ref, g)
        ones = c & plane(g, i)
        put_word(pick_ref, g, word(pick_ref, g) | jnp.where(take8, 0, ones))
        c = jnp.where(take8, ones, c ^ ones)
        put_word(cand_ref, g, c)
        return c

    def radix_step(i, carry):
        take8, rem = carry

        def step(g):
            return lax.population_count(narrow(g, i - 1, take8) & plane(g, i))
        return decide(sweep(step), rem)

    take8, rem = decide(count_ones(0), jnp.full((1, QB), topk, jnp.int32))
    take8, rem = lax.fori_loop(1, 32, radix_step, (take8, rem))

    def last_narrow(g):
        narrow(g, 31, take8)
        return zero8

    sweep(last_narrow)

    def group_step(i, carry):
        grp, before = carry
        trial = grp + jnp.left_shift(jnp.int32(1), ((seq // GROUP).bit_length() - 2) - i)
        below = sweep(lambda g: jnp.where(jnp.broadcast_to(g < trial, (SUBLANES, QB)),
                                          lax.population_count(word(cand_ref, g)), 0))
        ok = below < rem
        return jnp.where(ok, trial, grp), jnp.where(ok, below, before)

    zrow = jnp.zeros((1, QB), jnp.int32)
    grp, before = lax.fori_loop(0, (seq // GROUP).bit_length() - 1, group_step, (zrow, zrow))
    tied = sweep_words(lambda g: jnp.where(jnp.broadcast_to(g == grp, (SUBLANES, QB)), word(cand_ref, g), 0))

    def row_step(i, edge):
        trial = edge + jnp.left_shift(jnp.int32(1), (GROUP.bit_length() - 2) - i)
        below = lax.population_count(tied & rows_below(trial - sub, 0)).sum(axis=0, keepdims=True)
        return jnp.where(before + below < rem, trial, edge)

    edge = lax.fori_loop(0, GROUP.bit_length() - 1, row_step, zrow)
    keep_limit = grp * GROUP + edge + 1 - sub

    def bias_group(g):
        chosen = word(pick_ref, g) | (word(cand_ref, g) & rows_below(keep_limit, g))
        base = pl.multiple_of(g * GROUP, GROUP)
        bias_ref[pl.ds(base, GROUP), :] = jnp.concatenate(
            [jnp.where(lax.shift_left(chosen, jnp.full_like(chosen, m)) < 0, 0.0, NEG_INF_SCORE)
             for m in range(32)], axis=0).astype(bias_ref.dtype)
        return zero8

    sweep(bias_group)

    q = dq_ref[...]
    half = (col // DSA_HEAD_DIM)
    qst = jnp.concatenate(
        [jnp.where(half == (h % 2), q[:, (h // 2) * LANES:(h // 2 + 1) * LANES], jnp.zeros((), q.dtype))
         for h in range(DSA_HEADS)], axis=0)
    unit = jnp.where(lax.broadcasted_iota(jnp.int32, (QB, QB), 0) == lax.broadcasted_iota(jnp.int32, (QB, QB), 1),
                     1.0, 0.0).astype(q.dtype)
    qst = jnp.concatenate([qst, jnp.concatenate([unit] * DSA_HEADS, axis=0)], axis=1)
    npair = DSA_HEADS // 2
    hq = DSA_HEADS * QB
    acc_ref[...] = jnp.zeros_like(acc_ref)
    p1_ref[...] = jnp.zeros_like(p1_ref)

    def qk(k_tile, b_tile, dst_ref):
        dst_ref[...] = _dot_nt(jnp.concatenate([k_tile, b_tile], axis=1), qst)

    def numer(src_ref, dst_ref, m):
        m_new = []
        for hd in range(DSA_HEADS):
            cols = slice(hd * QB, (hd + 1) * QB)
            mh = jnp.maximum(m[:, cols], jnp.max(src_ref[:, cols], axis=0, keepdims=True))
            dst_ref[:, cols] = jnp.exp2(src_ref[:, cols] - mh).astype(dst_ref.dtype)
            m_new.append(mh)
        m_new = jnp.concatenate(m_new, axis=1)
        return m_new, jnp.exp2(m - m_new)

    def v_tile(kt):
        return dvt_ref[0:V_ROWS, pl.ds(pl.multiple_of(kt * KT, KT), KT)]

    def pv(vt, src_ref, alpha):
        acc_ref[...] = alpha * acc_ref[...] + _dot(vt, src_ref[...])

    def att_pair(i, c):
        m, alpha = c
        nxt = jnp.minimum(2 * i + 2, last_tile)
        k1, b1 = key_tile(dk_ref, 2 * i + 1), key_tile(bias_ref, 2 * i + 1)
        k2, b2 = key_tile(dk_ref, nxt), key_tile(bias_ref, nxt)
        v_prev, v_cur = v_tile(jnp.maximum(2 * i - 1, 0)), v_tile(2 * i)
        qk(k1, b1, s1_ref)
        pv(v_prev, p1_ref, alpha)
        m, alpha = numer(s0_ref, p0_ref, m)
        qk(k2, b2, s0_ref)
        pv(v_cur, p0_ref, alpha)
        m, alpha = numer(s1_ref, p1_ref, m)
        return m, alpha

    qk(key_tile(dk_ref, 0), key_tile(bias_ref, 0), s0_ref)
    c0 = (jnp.full((1, hq), -jnp.inf, F32), jnp.ones((1, hq), F32))
    _, alpha = pair_loops(att_pair, c0)
    pv(v_tile(last_tile), p1_ref, alpha)
    acc = acc_ref[...]
    o = acc[0:DSA_HEAD_DIM, :] / acc[DSA_HEAD_DIM:DSA_HEAD_DIM + 1, :]
    for j in range(npair):
        o_ref[:, j * LANES:(j + 1) * LANES] = jnp.concatenate(
            [o[:, (2 * j) * QB:(2 * j + 1) * QB], o[:, (2 * j + 1) * QB:(2 * j + 2) * QB]], axis=0).T


def _dsa(iq, iwt, dq, ik, dk, dvt):
    bsz, seq, _ = iq.shape
    topk = min(DSA_TOPK, seq // 4)
    grid = (bsz, seq // QB)
    qrow = lambda w: pl.BlockSpec((None, QB, w), lambda b, i: (b, i, 0))
    full = lambda r, c: pl.BlockSpec((None, r, c), lambda b, i: (b, 0, 0))
    return pl.pallas_call(
        functools.partial(_dsa_kernel, seq=seq, topk=topk), grid=grid,
        in_specs=[qrow(LANES), pl.BlockSpec((None, SUBLANES, QB), lambda b, i: (b, 0, i)), qrow(DSA_WIDTH),
                  full(seq, LANES), full(seq, LANES), full(LANES, seq)],
        out_specs=qrow(DSA_WIDTH),
        out_shape=jax.ShapeDtypeStruct((bsz, seq, DSA_WIDTH), F32),
        scratch_shapes=[pltpu.VMEM((seq, QB), jnp.int32),
                        pltpu.VMEM((seq // GROUP * SUBLANES, QB), jnp.int32),
                        pltpu.VMEM((seq // GROUP * SUBLANES, QB), jnp.int32),
                        pltpu.VMEM((seq, QB), _BF),
                        pltpu.VMEM((V_ROWS, DSA_HEADS * QB), F32),
                        pltpu.VMEM((KT, IDX_HEADS * QB), F32), pltpu.VMEM((KT, IDX_HEADS * QB), F32),
                        pltpu.VMEM((KT, DSA_HEADS * QB), F32), pltpu.VMEM((KT, DSA_HEADS * QB), F32),
                        pltpu.VMEM((KT, DSA_HEADS * QB), _BF), pltpu.VMEM((KT, DSA_HEADS * QB), _BF)],
        compiler_params=_params(("parallel", "arbitrary")), name="dsa",
    )(iq, iwt, dq, ik, dk, dvt)


def _ret_kernel(rq_ref, rk_ref, rv_ref, rg_ref, qdec_ref, kdec_ref, dmask_ref, cdec_ref, blk_ref,
                hmq_ref, hmv_ref, ones_ref, ng_ref, o_ref, s_ref):
    @pl.when(pl.program_id(0) == 0)
    def _():
        s_ref[...] = jnp.zeros_like(s_ref)

    ones = ones_ref[...]

    def head_mean(z):
        return _dot(z.astype(_BF), ones) * (1.0 / RET_V_DIM)

    for b in range(rq_ref.shape[0]):
        q, k, v = rq_ref[b], rk_ref[b], rv_ref[b]
        state = s_ref[b]
        y = _dot((q * qdec_ref[...]).astype(_BF), state.astype(_BF))
        kd = (k * kdec_ref[...]).T.astype(_BF)
        vb = v.astype(_BF)
        s_ref[b] = state * cdec_ref[...] + _dot(kd, vb) * blk_ref[...]
        kb = k.astype(_BF)
        a = jnp.concatenate([(_dot_nt((q * hmq_ref[h:h + 1, :]).astype(_BF), kb) * dmask_ref[h]).astype(_BF)
                             for h in range(RET_HEADS)], axis=1)
        vh = jnp.concatenate([(v * hmv_ref[h:h + 1, :]).astype(_BF) for h in range(RET_HEADS)], axis=0)
        y = y + _dot(a, vh)
        d = y - head_mean(y)
        yn = d * lax.rsqrt(head_mean(d * d) + EPS) * ng_ref[...]
        o_ref[b] = jax.nn.silu(rg_ref[b]) * yn


def _ret_consts():
    c = RET_C
    log_g = jnp.log1p(-jnp.exp2(-5.0 - jnp.arange(RET_HEADS, dtype=F32)))
    pos = jnp.arange(c, dtype=F32)
    diff = pos[:, None] - pos[None, :]
    dmask = jnp.where(diff >= 0, jnp.exp(log_g[:, None, None] * jnp.maximum(diff, 0.0)), 0.0)
    lane_q = jnp.arange(RET_QK_PAD)
    head_q = jnp.where((lane_q % LANES) < RET_HEADS * (RET_QK_DIM // 2), (lane_q % LANES) // (RET_QK_DIM // 2), -1)
    head_v = jnp.arange(RET_WIDTH) // RET_V_DIM
    hmq = (head_q[None, :] == jnp.arange(RET_HEADS)[:, None]).astype(F32)
    hmv = (head_v[None, :] == jnp.arange(RET_HEADS)[:, None]).astype(F32)
    lg_q = hmq.T @ log_g
    qdec = jnp.exp(lg_q[None, :] * (pos[:, None] + 1.0))
    kdec = jnp.exp(lg_q[None, :] * (c - 1.0 - pos[:, None]))
    blk = hmq.T @ hmv
    cdec = blk * jnp.exp(lg_q * c)[:, None]
    ones = (hmv.T @ hmv).astype(_BF)
    return qdec, kdec, dmask, cdec, blk, hmq, hmv, ones


def _retention(rq, rk, rv, rg, norm_g):
    bsz, seq, _ = rq.shape
    consts = _ret_consts()
    grid = (seq // RET_C,)
    row = lambda w: pl.BlockSpec((bsz, RET_C, w), lambda i: (0, i, 0))
    ng = norm_g.reshape(1, RET_WIDTH)
    return pl.pallas_call(
        _ret_kernel, grid=grid,
        in_specs=[row(RET_QK_PAD), row(RET_QK_PAD), row(RET_WIDTH), row(RET_WIDTH)]
        + [_const_spec(a.shape) for a in consts] + [_const_spec(ng.shape)],
        out_specs=row(RET_WIDTH),
        out_shape=jax.ShapeDtypeStruct((bsz, seq, RET_WIDTH), F32),
        scratch_shapes=[pltpu.VMEM((bsz, RET_QK_PAD, RET_WIDTH), F32)],
        compiler_params=_params(("arbitrary",)), name="retention",
    )(rq, rk, rv, rg, *consts, ng)


def _merge_kernel(x_ref, g_ref, ya_ref, yb_ref, yc_ref, wg_ref, wa_ref, wb_ref, wc_ref, wo_ref, o_ref):
    x = x_ref[...]
    h = _rms(x, g_ref[...]).astype(_BF)
    merged = jnp.zeros(x.shape, F32)
    for n, (y_ref, w_ref) in enumerate(((ya_ref, wa_ref), (yb_ref, wb_ref), (yc_ref, wc_ref))):
        gate = jax.nn.sigmoid(_dot_nt(h, wg_ref[n * D_MODEL:(n + 1) * D_MODEL, :]))
        merged = merged + gate * _dot(y_ref[...].astype(_BF), w_ref[...])
    o_ref[...] = x + _dot(merged.astype(_BF), wo_ref[...])


def _merge(x, g, ya, yb, yc, wg, wa, wb, wc, wo):
    bsz, seq, _ = x.shape
    grid = (bsz, seq // TM_OUT)
    row = lambda w: pl.BlockSpec((None, TM_OUT, w), lambda b, i: (b, i, 0))
    return pl.pallas_call(
        _merge_kernel, grid=grid,
        in_specs=[row(D_MODEL), _const_spec(g.shape), row(SSM_WIDTH), row(DSA_WIDTH), row(RET_WIDTH)]
        + [_const_spec(w.shape) for w in (wg, wa, wb, wc, wo)],
        out_specs=row(D_MODEL),
        out_shape=jax.ShapeDtypeStruct(x.shape, F32),
        compiler_params=_params(("parallel", "parallel")), name="merge",
    )(x, g, ya, yb, yc, wg, wa, wb, wc, wo)


def _mlp_kernel(x_ref, g_ref, w1_ref, w2_ref, fg_ref, o_ref, *, final_norm):
    x = x_ref[...]
    h = _rms(x, g_ref[...]).astype(_BF)
    acc = x
    for f in range(D_FF // FF_CHUNK):
        t = jnp.maximum(_dot(h, w1_ref[:, f * FF_CHUNK:(f + 1) * FF_CHUNK]), 0.0)
        acc = acc + _dot((t * t).astype(_BF), w2_ref[f * FF_CHUNK:(f + 1) * FF_CHUNK, :])
    o_ref[...] = _rms(acc, fg_ref[...]) if final_norm else acc


def _mlp(x, g, w1, w2, fg, final_norm):
    bsz, seq, _ = x.shape
    grid = (bsz, seq // TM_OUT)
    row = pl.BlockSpec((None, TM_OUT, D_MODEL), lambda b, i: (b, i, 0))
    return pl.pallas_call(
        functools.partial(_mlp_kernel, final_norm=final_norm), grid=grid,
        in_specs=[row, _const_spec(g.shape), _const_spec(w1.shape), _const_spec(w2.shape), _const_spec(fg.shape)],
        out_specs=row,
        out_shape=jax.ShapeDtypeStruct(x.shape, F32),
        compiler_params=_params(("parallel", "parallel")), name="mlp",
    )(x, g, w1, w2, fg)


def _rope_tables(positions):
    pos = positions.astype(F32)[..., None]
    lane = jnp.arange(LANES)

    def angles(rot_dim, theta):
        half = rot_dim // 2
        inv = jnp.exp(-math.log(theta) * jnp.arange(half, dtype=F32) * (2.0 / rot_dim))
        return pos * inv

    def roll_tables(head_dim, rot_dim, theta):
        half = rot_dim // 2
        ang = angles(rot_dim, theta)
        ln = lane % head_dim
        rot = ln < rot_dim
        cos = jnp.where(rot, jnp.take(jnp.cos(ang), ln % half, axis=-1), 1.0)
        sin = jnp.where(rot, jnp.take(jnp.sin(ang), ln % half, axis=-1), 0.0)
        sgn = jnp.stack([jnp.where(ln < half, -1.0, 0.0),
                         jnp.where((ln >= half) & rot, 1.0, 0.0)]).astype(F32)
        return cos, sin, sgn

    cd, sd, sgd = roll_tables(DSA_HEAD_DIM, DSA_HEAD_DIM // 4, ROPE_THETA)
    ci, si, sgi = roll_tables(IDX_DIM, IDX_DIM // 4, ROPE_THETA)
    half = RET_QK_DIM // 2
    ang = angles(RET_QK_DIM, RET_THETA)
    used = lane < RET_HEADS * half
    cr = jnp.where(used, jnp.take(jnp.cos(ang), lane % half, axis=-1), 1.0)
    sr = jnp.where(used, jnp.take(jnp.sin(ang), lane % half, axis=-1), 0.0)
    return cd, sd, ci, si, cr, sr, sgd, sgi


def _inproj_weights(wt):
    sizes = (SSM_WIDTH, DSA_WIDTH, DSA_HEAD_DIM, DSA_HEAD_DIM, IDX_HEADS * IDX_DIM, IDX_DIM, IDX_HEADS,
             RET_HEADS * RET_QK_DIM, RET_HEADS * RET_QK_DIM, RET_WIDTH, RET_WIDTH, 3 * D_MODEL)
    parts, off = [], 0
    for n in sizes:
        parts.append(wt[off:off + n, :])
        off += n
    wu, wdq, wdk, wdv, wiq, wik, wiw, wrq, wrk, wrv, wrg, wgt = parts
    half = RET_QK_DIM // 2
    zrows = lambda n: jnp.zeros((n, D_MODEL), wt.dtype)

    def ret_split(a):
        a = a.reshape(RET_HEADS, 2, half, D_MODEL)
        pad = zrows(LANES - RET_HEADS * half)
        return jnp.concatenate([a[:, 0].reshape(-1, D_MODEL), pad, a[:, 1].reshape(-1, D_MODEL), pad], axis=0)

    ws = (wu, wdq, jnp.concatenate([wdk, wdk], axis=0), jnp.concatenate([wdv, zrows(LANES - DSA_HEAD_DIM)], axis=0),
          wiq, jnp.concatenate([wik] * IDX_HEADS, axis=0), jnp.concatenate([wiw, zrows(SUBLANES - IDX_HEADS)], axis=0),
          ret_split(wrq), ret_split(wrk), wrv, wrg)
    return tuple(a.astype(_BF) for a in ws), wgt.astype(_BF)


def kernel(x, positions, norm1_g, w_in, ssm_lambda_re, ssm_lambda_im, ssm_log_step, ssm_b_re, ssm_b_im,
           ssm_c_re, ssm_c_im, ssm_d, ssm_glu_w, ssm_glu_b, ret_norm_g, w_proj_a, w_proj_b, w_proj_c,
           w_out, norm2_g, w_ff1, w_ff2, final_norm_g):
    depth = w_in.shape[0]
    tabs = _rope_tables(positions)
    fg = final_norm_g.reshape(1, D_MODEL)
    w_in_t = jnp.transpose(w_in, (2, 0, 1))
    for l in range(depth):
        g1 = norm1_g[l].reshape(1, D_MODEL)
        ws, wgt = _inproj_weights(w_in_t[:, l, :])
        u, dq, dk, dvt, iq, ik, iwt, rq, rk, rv, rg = _inproj(x, g1, tabs, ws)

        s5p = _s5_params(ssm_lambda_re[l], ssm_lambda_im[l], ssm_log_step[l], ssm_b_re[l], ssm_b_im[l],
                         ssm_c_re[l], ssm_c_im[l], ssm_d[l], ssm_glu_w[l], ssm_glu_b[l])
        ya = _s5(u, s5p)
        yb = _dsa(iq, iwt, dq, ik, dk, dvt)
        yc = _retention(rq, rk, rv, rg, ret_norm_g[l])

        x = _merge(x, g1, ya, yb, yc, wgt, w_proj_a[l].astype(_BF), w_proj_b[l].astype(_BF),
                   w_proj_c[l].astype(_BF), w_out[l].astype(_BF))
        x = _mlp(x, norm2_g[l].reshape(1, D_MODEL), w_ff1[l].astype(_BF), w_ff2[l].astype(_BF), fg,
                 final_norm=(l == depth - 1))
    return x
```

```python
import functools
import math

import jax
import jax.numpy as jnp
from jax import lax
from jax.experimental import pallas as pl
from jax.experimental.pallas import tpu as pltpu

F32 = jnp.float32
_BF = jnp.bfloat16
LANES = 128
SUBLANES = 8
VMEM_LIMIT = 56 * 1024 * 1024

D_MODEL = 1024
CHUNK = 64
EPS = 1e-6
NEG_INF_SCORE = -1e30
INT_MIN = -2 ** 31
GROUP = 32 * SUBLANES
GROUP_UNROLL = (16, 8, 4)
PAIR_UNROLL = (4, 2, 1)
V_ROWS = 80

SSM_WIDTH = 256
SSM_GROUP = 16
SSM_GROUPS = 16
SSM_STATE = 64
NSTATE = SSM_GROUPS * SSM_STATE

DSA_HEADS = 6
DSA_HEAD_DIM = 64
DSA_WIDTH = 384
IDX_HEADS = 4
IDX_DIM = 32
DSA_TOPK = 256
ROPE_THETA = 500000.0

RET_HEADS = 4
RET_QK_DIM = 48
RET_V_DIM = 96
RET_WIDTH = 384
RET_THETA = 10000.0
RET_QK_PAD = 256

D_FF = 4096

TM_IN = 512
S5_TILE = 512
S5_SEG = S5_TILE // SUBLANES
QB = 128
KT = 512
RET_C = 128
TM_OUT = 512
FF_CHUNK = 1024


def _rms(x, g):
    return x * lax.rsqrt(jnp.mean(x * x, axis=-1, keepdims=True) + EPS) * g


def _dot(a, b):
    return jnp.dot(a, b, preferred_element_type=F32)


def _dot_nt(a, b):
    return lax.dot_general(a, b, (((1,), (1,)), ((), ())), preferred_element_type=F32)


def _to_ukey(v):
    bits = lax.bitcast_convert_type(v, jnp.int32)
    return lax.bitcast_convert_type(bits ^ ((bits >> 31) | INT_MIN), jnp.uint32)


def _transpose32(a):
    a = list(a)
    j, m = 16, 0x0000FFFF
    while j:
        k = 0
        while k < 32:
            t = (a[k] ^ (a[k + j] >> j)) & m
            a[k] = a[k] ^ t
            a[k + j] = a[k + j] ^ (t << j)
            k = (k + j + 1) & ~j
        j >>= 1
        m = (m ^ (m << j)) & 0xFFFFFFFF
    return a


def _const_spec(shape):
    nd = len(shape)
    return pl.BlockSpec(shape, lambda *_: (0,) * nd)


def _params(sem):
    return pltpu.CompilerParams(dimension_semantics=sem, vmem_limit_bytes=VMEM_LIMIT)


def _rope_roll(z, cos, sin, sgn, shift):
    outs = []
    for c in range(z.shape[1] // LANES):
        zc = z[:, c * LANES:(c + 1) * LANES]
        rot = (pltpu.roll(zc, LANES - shift, 1) * sgn[0:1, :]
               + pltpu.roll(zc, shift, 1) * sgn[1:2, :])
        outs.append(zc * cos + rot * sin)
    return outs[0] if len(outs) == 1 else jnp.concatenate(outs, axis=1)


def _inproj_kernel(x_ref, g_ref, cd_ref, sd_ref, ci_ref, si_ref, cr_ref, sr_ref, sgd_ref, sgi_ref,
                   wu_ref, wdq_ref, wdk_ref, wdvt_ref, wiq_ref, wik_ref, wiwt_ref,
                   wrq_ref, wrk_ref, wrv_ref, wrg_ref,
                   u_ref, dq_ref, dk_ref, dvt_ref, iq_ref, ik_ref, iwt_ref,
                   rq_ref, rk_ref, rv_ref, rg_ref):
    h = _rms(x_ref[...], g_ref[...]).astype(_BF)

    def proj(w_ref):
        return _dot_nt(h, w_ref[...])

    u_ref[...] = proj(wu_ref)

    cd, sd, sgd = cd_ref[...], sd_ref[...], sgd_ref[...]
    dq = _rope_roll(proj(wdq_ref), cd, sd, sgd, 8)
    dq_ref[...] = (dq * (DSA_HEAD_DIM ** -0.5 * math.log2(math.e))).astype(dq_ref.dtype)
    dk_ref[...] = _rope_roll(proj(wdk_ref), cd, sd, sgd, 8).astype(dk_ref.dtype)
    vt = _dot_nt(wdvt_ref[...], h)
    ones_row = lax.broadcasted_iota(jnp.int32, vt.shape, 0) == DSA_HEAD_DIM
    dvt_ref[...] = jnp.where(ones_row, 1.0, vt).astype(dvt_ref.dtype)

    ci, si, sgi = ci_ref[...], si_ref[...], sgi_ref[...]
    iq_ref[...] = _rope_roll(proj(wiq_ref), ci, si, sgi, 4) * (IDX_DIM ** -0.5)
    ik_ref[...] = _rope_roll(proj(wik_ref), ci, si, sgi, 4).astype(ik_ref.dtype)
    iwt_ref[...] = _dot_nt(wiwt_ref[...], h) * (IDX_HEADS ** -0.5)

    cr, sr = cr_ref[...], sr_ref[...]

    def rope_split(z):
        x1, x2 = z[:, :LANES], z[:, LANES:]
        return jnp.concatenate([x1 * cr - x2 * sr, x2 * cr + x1 * sr], axis=1)

    rq_ref[...] = rope_split(proj(wrq_ref))
    rk_ref[...] = rope_split(proj(wrk_ref)) * (RET_QK_DIM ** -0.5)
    rv_ref[...] = proj(wrv_ref)
    rg_ref[...] = proj(wrg_ref)


def _inproj(x, g, tabs, ws):
    bsz, seq, _ = x.shape
    tm = TM_IN
    grid = (bsz, seq // tm)
    row = lambda w: pl.BlockSpec((None, tm, w), lambda b, i: (b, i, 0))
    colT = lambda r: pl.BlockSpec((None, r, tm), lambda b, i: (b, 0, i))
    in_specs = ([row(D_MODEL), _const_spec(g.shape)] + [row(LANES)] * 6
                + [_const_spec(t.shape) for t in tabs[6:]] + [_const_spec(w.shape) for w in ws])
    out_shape = [
        jax.ShapeDtypeStruct((bsz, seq, SSM_WIDTH), F32),
        jax.ShapeDtypeStruct((bsz, seq, DSA_WIDTH), _BF),
        jax.ShapeDtypeStruct((bsz, seq, LANES), _BF),
        jax.ShapeDtypeStruct((bsz, LANES, seq), _BF),
        jax.ShapeDtypeStruct((bsz, seq, LANES), F32),
        jax.ShapeDtypeStruct((bsz, seq, LANES), _BF),
        jax.ShapeDtypeStruct((bsz, SUBLANES, seq), F32),
        jax.ShapeDtypeStruct((bsz, seq, RET_QK_PAD), F32),
        jax.ShapeDtypeStruct((bsz, seq, RET_QK_PAD), F32),
        jax.ShapeDtypeStruct((bsz, seq, RET_WIDTH), F32),
        jax.ShapeDtypeStruct((bsz, seq, RET_WIDTH), F32),
    ]
    out_specs = [row(SSM_WIDTH), row(DSA_WIDTH), row(LANES), colT(LANES), row(LANES), row(LANES),
                 colT(SUBLANES), row(RET_QK_PAD), row(RET_QK_PAD), row(RET_WIDTH), row(RET_WIDTH)]
    return pl.pallas_call(
        _inproj_kernel, grid=grid, in_specs=in_specs, out_specs=out_specs, out_shape=out_shape,
        compiler_params=_params(("parallel", "parallel")), name="inproj",
    )(x, g, *tabs, *ws)


def _s5_kernel(u_ref, bbd_ref, cbd_ref, lam_ref, lam64_ref, pow_ref, d_ref, gw_ref, gb_ref,
               o_ref, st_ref, xm_ref, carry_ref, up_ref, yp_ref):
    n = NSTATE

    @pl.when(pl.program_id(1) == 0)
    def _():
        carry_ref[...] = jnp.zeros_like(carry_ref)

    nhalf = SSM_WIDTH // LANES
    for i in range(SUBLANES):
        for c in range(nhalf):
            up_ref[c, pl.ds(i, S5_SEG, stride=SUBLANES), :] = u_ref[i * S5_SEG:(i + 1) * S5_SEG,
                                                                    c * LANES:(c + 1) * LANES]
    u = jnp.concatenate([up_ref[c] for c in range(nhalf)], axis=1)
    st_ref[...] = _dot(u.astype(_BF), bbd_ref[...])

    lr = jnp.broadcast_to(lam_ref[:, :n], (SUBLANES, n))
    li = jnp.broadcast_to(lam_ref[:, n:], (SUBLANES, n))

    def step(j, c):
        xr, xi = c
        off = pl.multiple_of(j * SUBLANES, SUBLANES)
        ar = st_ref[pl.ds(off, SUBLANES), :n]
        ai = st_ref[pl.ds(off, SUBLANES), n:]
        nr = lr * xr - li * xi + ar
        ni = lr * xi + li * xr + ai
        st_ref[pl.ds(off, SUBLANES), :n] = nr
        st_ref[pl.ds(off, SUBLANES), n:] = ni
        return nr, ni

    zero = jnp.zeros((SUBLANES, n), F32)
    xr, xi = lax.fori_loop(0, S5_SEG, step, (zero, zero), unroll=4)

    l64r, l64i = lam64_ref[:, :n], lam64_ref[:, n:]
    cr, ci = carry_ref[:, :n], carry_ref[:, n:]
    for i in range(SUBLANES):
        xm_ref[i:i + 1, :n] = cr
        xm_ref[i:i + 1, n:] = ci
        er, ei = xr[i:i + 1, :], xi[i:i + 1, :]
        cr, ci = er + l64r * cr - l64i * ci, ei + l64r * ci + l64i * cr
    carry_ref[:, :n] = cr
    carry_ref[:, n:] = ci

    hr, hi = xm_ref[:, :n], xm_ref[:, n:]

    def fix(j, _):
        off = pl.multiple_of(j * SUBLANES, SUBLANES)
        pr = pow_ref[pl.ds(j, 1), :n]
        pi = pow_ref[pl.ds(j, 1), n:]
        st_ref[pl.ds(off, SUBLANES), :n] = st_ref[pl.ds(off, SUBLANES), :n] + pr * hr - pi * hi
        st_ref[pl.ds(off, SUBLANES), n:] = st_ref[pl.ds(off, SUBLANES), n:] + pr * hi + pi * hr
        return 0

    lax.fori_loop(0, S5_SEG, fix, 0, unroll=4)

    y = d_ref[...] * u
    kc = 512
    for k in range(2 * n // kc):
        y = y + _dot(st_ref[:, k * kc:(k + 1) * kc].astype(_BF), cbd_ref[k * kc:(k + 1) * kc, :])
    y = jax.nn.gelu(y)
    y = y * jax.nn.sigmoid(_dot(y.astype(_BF), gw_ref[...]) + gb_ref[...])
    for c in range(nhalf):
        yp_ref[c] = y[:, c * LANES:(c + 1) * LANES]
    for i in range(SUBLANES):
        for c in range(nhalf):
            o_ref[i * S5_SEG:(i + 1) * S5_SEG, c * LANES:(c + 1) * LANES] = yp_ref[
                c, pl.ds(i, S5_SEG, stride=SUBLANES), :]


def _s5(u_perm, prm):
    bsz, seq, _ = u_perm.shape
    grid = (bsz, seq // S5_TILE)
    row = pl.BlockSpec((None, S5_TILE, SSM_WIDTH), lambda b, i: (b, i, 0))
    return pl.pallas_call(
        _s5_kernel, grid=grid,
        in_specs=[row] + [_const_spec(p.shape) for p in prm],
        out_specs=row,
        out_shape=jax.ShapeDtypeStruct((bsz, seq, SSM_WIDTH), F32),
        scratch_shapes=[pltpu.VMEM((S5_TILE, 2 * NSTATE), F32),
                        pltpu.VMEM((SUBLANES, 2 * NSTATE), F32),
                        pltpu.VMEM((1, 2 * NSTATE), F32),
                        pltpu.VMEM((SSM_WIDTH // LANES, S5_TILE, LANES), F32),
                        pltpu.VMEM((SSM_WIDTH // LANES, S5_TILE, LANES), F32)],
        compiler_params=_params(("arbitrary", "arbitrary")), name="s5",
    )(u_perm, *prm)


def _s5_params(lam_re, lam_im, log_step, b_re, b_im, c_re, c_im, d_skip, glu_w, glu_b):
    step = jnp.exp(log_step.astype(F32))[:, None]
    ere, eim = lam_re * step, lam_im * step
    mag = jnp.exp(ere)
    lb_re, lb_im = mag * jnp.cos(eim), mag * jnp.sin(eim)
    den = lam_re * lam_re + lam_im * lam_im
    f_re = ((lb_re - 1.0) * lam_re + lb_im * lam_im) / den
    f_im = (lb_im * lam_re - (lb_re - 1.0) * lam_im) / den
    bb_re = f_re[..., None] * b_re - f_im[..., None] * b_im
    bb_im = f_re[..., None] * b_im + f_im[..., None] * b_re
    eye = jnp.eye(SSM_GROUPS, dtype=F32)
    bbd = jnp.concatenate(
        [jnp.einsum('gpc,gh->gchp', bb, eye).reshape(SSM_WIDTH, NSTATE) for bb in (bb_re, bb_im)], axis=1)
    cbd = jnp.concatenate(
        [jnp.einsum('gcp,gh->hpgc', cc, eye).reshape(NSTATE, SSM_WIDTH) for cc in (c_re, -c_im)], axis=0)
    flat = lambda a: a.reshape(1, NSTATE)
    lam = jnp.concatenate([flat(lb_re), flat(lb_im)], axis=1)

    def power(k):
        k = jnp.asarray(k, F32).reshape(-1, 1)
        m = jnp.exp(k * flat(ere))
        return jnp.concatenate([m * jnp.cos(k * flat(eim)), m * jnp.sin(k * flat(eim))], axis=1)

    return (bbd.astype(_BF), cbd.astype(_BF), lam, power(float(S5_SEG)),
            power(jnp.arange(1, S5_SEG + 1)), d_skip.reshape(1, SSM_WIDTH),
            glu_w.astype(_BF), glu_b.reshape(1, SSM_WIDTH))


def _dsa_kernel(iq_ref, iwt_ref, dq_ref, ik_ref, dk_ref, dvt_ref, o_ref,
                planes_ref, cand_ref, pick_ref, bias_ref, acc_ref,
                lg0_ref, lg1_ref, s0_ref, s1_ref, p0_ref, p1_ref, *, seq, topk):
    qi = pl.program_id(1)
    nk = (qi * QB) // KT + 1
    lane = lax.broadcasted_iota(jnp.int32, (1, QB), 1)
    vis_end = qi * QB + jnp.where(lane < CHUNK, CHUNK, 2 * CHUNK)

    iq = iq_ref[...]
    col = lax.broadcasted_iota(jnp.int32, (QB, LANES), 1)
    qs_t = jnp.concatenate([jnp.where(col // IDX_DIM == h, iq, 0.0) for h in range(IDX_HEADS)],
                           axis=0).T.astype(_BF)
    w = iwt_ref[...]

    nkp = nk + lax.rem(nk, 2)
    last_tile = nkp - 1

    def key_tile(ref, kt):
        return ref[pl.ds(pl.multiple_of(kt * KT, KT), KT), :]

    def tile_planes(src_ref):
        lg = src_ref[...]
        sc = jnp.zeros((KT, QB), F32)
        for h in range(IDX_HEADS):
            sc = sc + jnp.maximum(lg[:, h * QB:(h + 1) * QB], 0.0) * w[h:h + 1, :]
        sc = jnp.where(sc == 0.0, 0.0, sc)
        ukey = _to_ukey(sc)
        words = []
        for g in range(KT // GROUP):
            words += _transpose32([ukey[g * GROUP + m * SUBLANES:g * GROUP + (m + 1) * SUBLANES, :]
                                   for m in range(32)])
        return lax.bitcast_convert_type(jnp.concatenate(words, axis=0), jnp.int32)

    def score_pair(i, c):
        ik1 = key_tile(ik_ref, 2 * i + 1)
        ik2 = key_tile(ik_ref, jnp.minimum(2 * i + 2, last_tile))
        lg1_ref[...] = _dot(ik1, qs_t)
        planes0 = tile_planes(lg0_ref)
        lg0_ref[...] = _dot(ik2, qs_t)
        planes1 = tile_planes(lg1_ref)
        planes_ref[pl.ds(pl.multiple_of(i * (2 * KT), 2 * KT), 2 * KT), :] = jnp.concatenate(
            [planes0, planes1], axis=0)
        return c

    lg0_ref[...] = _dot(key_tile(ik_ref, 0), qs_t)
    npairs = nkp // 2

    def pair_loops(pair_fn, carry):
        start = 0
        for u in PAIR_UNROLL:
            def body(j, c, u=u, start=start):
                for t in range(u):
                    c = pair_fn(start + j * u + t, c)
                return c
            n = (npairs - start) // u
            carry = lax.fori_loop(0, n, body, carry)
            start = start + n * u
        return carry

    pair_loops(score_pair, 0)

    ngrp = nkp * (KT // GROUP)
    sub = lax.broadcasted_iota(jnp.int32, (SUBLANES, QB), 0)
    zero8 = jnp.zeros((SUBLANES, QB), jnp.int32)

    def rows_below(limit, g):
        nm = lax.shift_right_arithmetic(limit - g * GROUP + (SUBLANES - 1), 3)
        top = lax.shift_right_arithmetic(jnp.full((SUBLANES, QB), INT_MIN, jnp.int32),
                                         jnp.clip(nm, 1, 32) - 1)
        return jnp.where(nm <= 0, 0, top)

    def word(ref, g):
        return ref[pl.ds(pl.multiple_of(g * SUBLANES, SUBLANES), SUBLANES), :]

    def put_word(ref, g, v):
        ref[pl.ds(pl.multiple_of(g * SUBLANES, SUBLANES), SUBLANES), :] = v

    def plane(g, i):
        return planes_ref[pl.ds(pl.multiple_of(g * GROUP + i * SUBLANES, SUBLANES), SUBLANES), :]

    def sweep_words(step):
        accs, start = (zero8,) * GROUP_UNROLL[-1], 0
        for u in GROUP_UNROLL:
            def body(s, accs, u=u, start=start):
                accs = list(accs)
                for t in range(u):
                    accs[t % len(accs)] = accs[t % len(accs)] + step(start + s * u + t)
                return tuple(accs)
            n = (ngrp - start) // u
            accs = lax.fori_loop(0, n, body, accs)
            start = start + n * u
        return functools.reduce(lambda a, b: a + b, accs)

    def sweep(step):
        return sweep_words(step).sum(axis=0, keepdims=True)

    def init_words(g):
        put_word(cand_ref, g, rows_below(vis_end - sub, g))
        put_word(pick_ref, g, zero8)
        return zero8

    sweep(init_words)

    def count_ones(i):
        return sweep(lambda g: lax.population_count(word(cand_ref, g) & plane(g, i)))

    def decide(c1, rem):
        take = c1 >= rem
        return jnp.where(take, 1, 0), jnp.where(take, rem, rem - c1)

    def narrow(g, i, take):
        take8 = jnp.broadcast_to(take, (SUBLANES, QB)) != 0
        c = word(cand_ref, g)
        ones = c & plane(g, i)
        put_word(pick_ref, g, word(pick_ref, g) | jnp.where(take8, 0, ones))
        c = jnp.where(take8, ones, c ^ ones)
        put_word(cand_ref, g, c)
        return c

    def radix_step(i, carry):
        take8, rem = carry

        def step(g):
            return lax.population_count(narrow(g, i - 1, take8) & plane(g, i))
        return decide(sweep(step), rem)

    take8, rem = decide(count_ones(0), jnp.full((1, QB), topk, jnp.int32))
    take8, rem = lax.fori_loop(1, 32, radix_step, (take8, rem))

    def last_narrow(g):
        narrow(g, 31, take8)
        return zero8

    sweep(last_narrow)

    def group_step(i, carry):
        grp, before = carry
        trial = grp + jnp.left_shift(jnp.int32(1), ((seq // GROUP).bit_length() - 2) - i)
        below = sweep(lambda g: jnp.where(jnp.broadcast_to(g < trial, (SUBLANES, QB)),
                                          lax.population_count(word(cand_ref, g)), 0))
        ok = below < rem
        return jnp.where(ok, trial, grp), jnp.where(ok, below, before)

    zrow = jnp.zeros((1, QB), jnp.int32)
    grp, before = lax.fori_loop(0, (seq // GROUP).bit_length() - 1, group_step, (zrow, zrow))
    tied = sweep_words(lambda g: jnp.where(jnp.broadcast_to(g == grp, (SUBLANES, QB)), word(cand_ref, g), 0))

    def row_step(i, edge):
        trial = edge + jnp.left_shift(jnp.int32(1), (GROUP.bit_length() - 2) - i)
        below = lax.population_count(tied & rows_below(trial - sub, 0)).sum(axis=0, keepdims=True)
        return jnp.where(before + below < rem, trial, edge)

    edge = lax.fori_loop(0, GROUP.bit_length() - 1, row_step, zrow)
    keep_limit = grp * GROUP + edge + 1 - sub

    def bias_group(g):
        chosen = word(pick_ref, g) | (word(cand_ref, g) & rows_below(keep_limit, g))
        base = pl.multiple_of(g * GROUP, GROUP)
        bias_ref[pl.ds(base, GROUP), :] = jnp.concatenate(
            [jnp.where(lax.shift_left(chosen, jnp.full_like(chosen, m)) < 0, 0.0, NEG_INF_SCORE)
             for m in range(32)], axis=0).astype(bias_ref.dtype)
        return zero8

    sweep(bias_group)

    q = dq_ref[...]
    half = (col // DSA_HEAD_DIM)
    qst = jnp.concatenate(
        [jnp.where(half == (h % 2), q[:, (h // 2) * LANES:(h // 2 + 1) * LANES], jnp.zeros((), q.dtype))
         for h in range(DSA_HEADS)], axis=0)
    unit = jnp.where(lax.broadcasted_iota(jnp.int32, (QB, QB), 0) == lax.broadcasted_iota(jnp.int32, (QB, QB), 1),
                     1.0, 0.0).astype(q.dtype)
    qst = jnp.concatenate([qst, jnp.concatenate([unit] * DSA_HEADS, axis=0)], axis=1)
    qst_t = qst.astype(F32).T.astype(q.dtype)
    npair = DSA_HEADS // 2
    hq = DSA_HEADS * QB
    acc_ref[...] = jnp.zeros_like(acc_ref)
    p1_ref[...] = jnp.zeros_like(p1_ref)

    def qk(k_tile, b_tile, dst_ref):
        dst_ref[...] = _dot(jnp.concatenate([k_tile, b_tile], axis=1), qst_t)

    def numer(src_ref, dst_ref, m):
        m_new = []
        for hd in range(DSA_HEADS):
            cols = slice(hd * QB, (hd + 1) * QB)
            mh = jnp.maximum(m[:, cols], jnp.max(src_ref[:, cols], axis=0, keepdims=True))
            dst_ref[:, cols] = jnp.exp2(src_ref[:, cols] - mh).astype(dst_ref.dtype)
            m_new.append(mh)
        m_new = jnp.concatenate(m_new, axis=1)
        return m_new, jnp.exp2(m - m_new)

    def v_tile(kt):
        return dvt_ref[0:V_ROWS, pl.ds(pl.multiple_of(kt * KT, KT), KT)]

    def pv(vt, src_ref, alpha):
        acc_ref[...] = alpha * acc_ref[...] + _dot(vt, src_ref[...])

    def att_pair(i, c):
        m, alpha = c
        nxt = jnp.minimum(2 * i + 2, last_tile)
        k1, b1 = key_tile(dk_ref, 2 * i + 1), key_tile(bias_ref, 2 * i + 1)
        k2, b2 = key_tile(dk_ref, nxt), key_tile(bias_ref, nxt)
        v_prev, v_cur = v_tile(jnp.maximum(2 * i - 1, 0)), v_tile(2 * i)
        qk(k1, b1, s1_ref)
        pv(v_prev, p1_ref, alpha)
        m, alpha = numer(s0_ref, p0_ref, m)
        qk(k2, b2, s0_ref)
        pv(v_cur, p0_ref, alpha)
        m, alpha = numer(s1_ref, p1_ref, m)
        return m, alpha

    qk(key_tile(dk_ref, 0), key_tile(bias_ref, 0), s0_ref)
    c0 = (jnp.full((1, hq), -jnp.inf, F32), jnp.ones((1, hq), F32))
    _, alpha = pair_loops(att_pair, c0)
    pv(v_tile(last_tile), p1_ref, alpha)
    acc = acc_ref[...]
    o = acc[0:DSA_HEAD_DIM, :] / acc[DSA_HEAD_DIM:DSA_HEAD_DIM + 1, :]
    for j in range(npair):
        o_ref[:, j * LANES:(j + 1) * LANES] = jnp.concatenate(
            [o[:, (2 * j) * QB:(2 * j + 1) * QB], o[:, (2 * j + 1) * QB:(2 * j + 2) * QB]], axis=0).T


def _dsa(iq, iwt, dq, ik, dk, dvt):
    bsz, seq, _ = iq.shape
    topk = min(DSA_TOPK, seq // 4)
    grid = (bsz, seq // QB)
    qrow = lambda w: pl.BlockSpec((None, QB, w), lambda b, i: (b, i, 0))
    full = lambda r, c: pl.BlockSpec((None, r, c), lambda b, i: (b, 0, 0))
    return pl.pallas_call(
        functools.partial(_dsa_kernel, seq=seq, topk=topk), grid=grid,
        in_specs=[qrow(LANES), pl.BlockSpec((None, SUBLANES, QB), lambda b, i: (b, 0, i)), qrow(DSA_WIDTH),
                  full(seq, LANES), full(seq, LANES), full(LANES, seq)],
        out_specs=qrow(DSA_WIDTH),
        out_shape=jax.ShapeDtypeStruct((bsz, seq, DSA_WIDTH), F32),
        scratch_shapes=[pltpu.VMEM((seq, QB), jnp.int32),
                        pltpu.VMEM((seq // GROUP * SUBLANES, QB), jnp.int32),
                        pltpu.VMEM((seq // GROUP * SUBLANES, QB), jnp.int32),
                        pltpu.VMEM((seq, QB), _BF),
                        pltpu.VMEM((V_ROWS, DSA_HEADS * QB), F32),
                        pltpu.VMEM((KT, IDX_HEADS * QB), F32), pltpu.VMEM((KT, IDX_HEADS * QB), F32),
                        pltpu.VMEM((KT, DSA_HEADS * QB), F32), pltpu.VMEM((KT, DSA_HEADS * QB), F32),
                        pltpu.VMEM((KT, DSA_HEADS * QB), _BF), pltpu.VMEM((KT, DSA_HEADS * QB), _BF)],
        compiler_params=_params(("parallel", "arbitrary")), name="dsa",
    )(iq, iwt, dq, ik, dk, dvt)


def _ret_kernel(rq_ref, rk_ref, rv_ref, rg_ref, qdec_ref, kdec_ref, dmask_ref, cdec_ref, blk_ref,
                hmq_ref, hmv_ref, ones_ref, ng_ref, o_ref, s_ref):
    @pl.when(pl.program_id(0) == 0)
    def _():
        s_ref[...] = jnp.zeros_like(s_ref)

    ones = ones_ref[...]

    def head_mean(z):
        return _dot(z.astype(_BF), ones) * (1.0 / RET_V_DIM)

    for b in range(rq_ref.shape[0]):
        q, k, v = rq_ref[b], rk_ref[b], rv_ref[b]
        state = s_ref[b]
        y = _dot((q * qdec_ref[...]).astype(_BF), state.astype(_BF))
        kd = (k * kdec_ref[...]).T.astype(_BF)
        vb = v.astype(_BF)
        s_ref[b] = state * cdec_ref[...] + _dot(kd, vb) * blk_ref[...]
        kb = k.astype(_BF)
        a = jnp.concatenate([(_dot_nt((q * hmq_ref[h:h + 1, :]).astype(_BF), kb) * dmask_ref[h]).astype(_BF)
                             for h in range(RET_HEADS)], axis=1)
        vh = jnp.concatenate([(v * hmv_ref[h:h + 1, :]).astype(_BF) for h in range(RET_HEADS)], axis=0)
        y = y + _dot(a, vh)
        d = y - head_mean(y)
        yn = d * lax.rsqrt(head_mean(d * d) + EPS) * ng_ref[...]
        o_ref[b] = jax.nn.silu(rg_ref[b]) * yn


def _ret_consts():
    c = RET_C
    log_g = jnp.log1p(-jnp.exp2(-5.0 - jnp.arange(RET_HEADS, dtype=F32)))
    pos = jnp.arange(c, dtype=F32)
    diff = pos[:, None] - pos[None, :]
    dmask = jnp.where(diff >= 0, jnp.exp(log_g[:, None, None] * jnp.maximum(diff, 0.0)), 0.0)
    lane_q = jnp.arange(RET_QK_PAD)
    head_q = jnp.where((lane_q % LANES) < RET_HEADS * (RET_QK_DIM // 2), (lane_q % LANES) // (RET_QK_DIM // 2), -1)
    head_v = jnp.arange(RET_WIDTH) // RET_V_DIM
    hmq = (head_q[None, :] == jnp.arange(RET_HEADS)[:, None]).astype(F32)
    hmv = (head_v[None, :] == jnp.arange(RET_HEADS)[:, None]).astype(F32)
    lg_q = hmq.T @ log_g
    qdec = jnp.exp(lg_q[None, :] * (pos[:, None] + 1.0))
    kdec = jnp.exp(lg_q[None, :] * (c - 1.0 - pos[:, None]))
    blk = hmq.T @ hmv
    cdec = blk * jnp.exp(lg_q * c)[:, None]
    ones = (hmv.T @ hmv).astype(_BF)
    return qdec, kdec, dmask, cdec, blk, hmq, hmv, ones


def _retention(rq, rk, rv, rg, norm_g):
    bsz, seq, _ = rq.shape
    consts = _ret_consts()
    grid = (seq // RET_C,)
    row = lambda w: pl.BlockSpec((bsz, RET_C, w), lambda i: (0, i, 0))
    ng = norm_g.reshape(1, RET_WIDTH)
    return pl.pallas_call(
        _ret_kernel, grid=grid,
        in_specs=[row(RET_QK_PAD), row(RET_QK_PAD), row(RET_WIDTH), row(RET_WIDTH)]
        + [_const_spec(a.shape) for a in consts] + [_const_spec(ng.shape)],
        out_specs=row(RET_WIDTH),
        out_shape=jax.ShapeDtypeStruct((bsz, seq, RET_WIDTH), F32),
        scratch_shapes=[pltpu.VMEM((bsz, RET_QK_PAD, RET_WIDTH), F32)],
        compiler_params=_params(("arbitrary",)), name="retention",
    )(rq, rk, rv, rg, *consts, ng)


def _merge_kernel(x_ref, g_ref, ya_ref, yb_ref, yc_ref, wg_ref, wa_ref, wb_ref, wc_ref, wo_ref, o_ref):
    x = x_ref[...]
    h = _rms(x, g_ref[...]).astype(_BF)
    merged = jnp.zeros(x.shape, F32)
    for n, (y_ref, w_ref) in enumerate(((ya_ref, wa_ref), (yb_ref, wb_ref), (yc_ref, wc_ref))):
        gate = jax.nn.sigmoid(_dot_nt(h, wg_ref[n * D_MODEL:(n + 1) * D_MODEL, :]))
        merged = merged + gate * _dot(y_ref[...].astype(_BF), w_ref[...])
    o_ref[...] = x + _dot(merged.astype(_BF), wo_ref[...])


def _merge(x, g, ya, yb, yc, wg, wa, wb, wc, wo):
    bsz, seq, _ = x.shape
    grid = (bsz, seq // TM_OUT)
    row = lambda w: pl.BlockSpec((None, TM_OUT, w), lambda b, i: (b, i, 0))
    return pl.pallas_call(
        _merge_kernel, grid=grid,
        in_specs=[row(D_MODEL), _const_spec(g.shape), row(SSM_WIDTH), row(DSA_WIDTH), row(RET_WIDTH)]
        + [_const_spec(w.shape) for w in (wg, wa, wb, wc, wo)],
        out_specs=row(D_MODEL),
        out_shape=jax.ShapeDtypeStruct(x.shape, F32),
        compiler_params=_params(("parallel", "parallel")), name="merge",
    )(x, g, ya, yb, yc, wg, wa, wb, wc, wo)


def _mlp_kernel(x_ref, g_ref, w1_ref, w2_ref, fg_ref, o_ref, *, final_norm):
    x = x_ref[...]
    h = _rms(x, g_ref[...]).astype(_BF)
    acc = x
    for f in range(D_FF // FF_CHUNK):
        t = jnp.maximum(_dot(h, w1_ref[:, f * FF_CHUNK:(f + 1) * FF_CHUNK]), 0.0)
        acc = acc + _dot((t * t).astype(_BF), w2_ref[f * FF_CHUNK:(f + 1) * FF_CHUNK, :])
    o_ref[...] = _rms(acc, fg_ref[...]) if final_norm else acc


def _mlp(x, g, w1, w2, fg, final_norm):
    bsz, seq, _ = x.shape
    grid = (bsz, seq // TM_OUT)
    row = pl.BlockSpec((None, TM_OUT, D_MODEL), lambda b, i: (b, i, 0))
    return pl.pallas_call(
        functools.partial(_mlp_kernel, final_norm=final_norm), grid=grid,
        in_specs=[row, _const_spec(g.shape), _const_spec(w1.shape), _const_spec(w2.shape), _const_spec(fg.shape)],
        out_specs=row,
        out_shape=jax.ShapeDtypeStruct(x.shape, F32),
        compiler_params=_params(("parallel", "parallel")), name="mlp",
    )(x, g, w1, w2, fg)


def _rope_tables(positions):
    pos = positions.astype(F32)[..., None]
    lane = jnp.arange(LANES)

    def angles(rot_dim, theta):
        half = rot_dim // 2
        inv = jnp.exp(-math.log(theta) * jnp.arange(half, dtype=F32) * (2.0 / rot_dim))
        return pos * inv

    def roll_tables(head_dim, rot_dim, theta):
        half = rot_dim // 2
        ang = angles(rot_dim, theta)
        ln = lane % head_dim
        rot = ln < rot_dim
        cos = jnp.where(rot, jnp.take(jnp.cos(ang), ln % half, axis=-1), 1.0)
        sin = jnp.where(rot, jnp.take(jnp.sin(ang), ln % half, axis=-1), 0.0)
        sgn = jnp.stack([jnp.where(ln < half, -1.0, 0.0),
                         jnp.where((ln >= half) & rot, 1.0, 0.0)]).astype(F32)
        return cos, sin, sgn

    cd, sd, sgd = roll_tables(DSA_HEAD_DIM, DSA_HEAD_DIM // 4, ROPE_THETA)
    ci, si, sgi = roll_tables(IDX_DIM, IDX_DIM // 4, ROPE_THETA)
    half = RET_QK_DIM // 2
    ang = angles(RET_QK_DIM, RET_THETA)
    used = lane < RET_HEADS * half
    cr = jnp.where(used, jnp.take(jnp.cos(ang), lane % half, axis=-1), 1.0)
    sr = jnp.where(used, jnp.take(jnp.sin(ang), lane % half, axis=-1), 0.0)
    return cd, sd, ci, si, cr, sr, sgd, sgi


def _inproj_weights(wt):
    sizes = (SSM_WIDTH, DSA_WIDTH, DSA_HEAD_DIM, DSA_HEAD_DIM, IDX_HEADS * IDX_DIM, IDX_DIM, IDX_HEADS,
             RET_HEADS * RET_QK_DIM, RET_HEADS * RET_QK_DIM, RET_WIDTH, RET_WIDTH, 3 * D_MODEL)
    parts, off = [], 0
    for n in sizes:
        parts.append(wt[off:off + n, :])
        off += n
    wu, wdq, wdk, wdv, wiq, wik, wiw, wrq, wrk, wrv, wrg, wgt = parts
    half = RET_QK_DIM // 2
    zrows = lambda n: jnp.zeros((n, D_MODEL), wt.dtype)

    def ret_split(a):
        a = a.reshape(RET_HEADS, 2, half, D_MODEL)
        pad = zrows(LANES - RET_HEADS * half)
        return jnp.concatenate([a[:, 0].reshape(-1, D_MODEL), pad, a[:, 1].reshape(-1, D_MODEL), pad], axis=0)

    ws = (wu, wdq, jnp.concatenate([wdk, wdk], axis=0), jnp.concatenate([wdv, zrows(LANES - DSA_HEAD_DIM)], axis=0),
          wiq, jnp.concatenate([wik] * IDX_HEADS, axis=0), jnp.concatenate([wiw, zrows(SUBLANES - IDX_HEADS)], axis=0),
          ret_split(wrq), ret_split(wrk), wrv, wrg)
    return tuple(a.astype(_BF) for a in ws), wgt.astype(_BF)


def kernel(x, positions, norm1_g, w_in, ssm_lambda_re, ssm_lambda_im, ssm_log_step, ssm_b_re, ssm_b_im,
           ssm_c_re, ssm_c_im, ssm_d, ssm_glu_w, ssm_glu_b, ret_norm_g, w_proj_a, w_proj_b, w_proj_c,
           w_out, norm2_g, w_ff1, w_ff2, final_norm_g):
    depth = w_in.shape[0]
    tabs = _rope_tables(positions)
    fg = final_norm_g.reshape(1, D_MODEL)
    w_in_t = jnp.transpose(w_in, (2, 0, 1))
    for l in range(depth):
        g1 = norm1_g[l].reshape(1, D_MODEL)
        ws, wgt = _inproj_weights(w_in_t[:, l, :])
        u, dq, dk, dvt, iq, ik, iwt, rq, rk, rv, rg = _inproj(x, g1, tabs, ws)

        s5p = _s5_params(ssm_lambda_re[l], ssm_lambda_im[l], ssm_log_step[l], ssm_b_re[l], ssm_b_im[l],
                         ssm_c_re[l], ssm_c_im[l], ssm_d[l], ssm_glu_w[l], ssm_glu_b[l])
        ya = _s5(u, s5p)
        yb = _dsa(iq, iwt, dq, ik, dk, dvt)
        yc = _retention(rq, rk, rv, rg, ret_norm_g[l])

        x = _merge(x, g1, ya, yb, yc, wgt, w_proj_a[l].astype(_BF), w_proj_b[l].astype(_BF),
                   w_proj_c[l].astype(_BF), w_out[l].astype(_BF))
        x = _mlp(x, norm2_g[l].reshape(1, D_MODEL), w_ff1[l].astype(_BF), w_ff2[l].astype(_BF), fg,
                 final_norm=(l == depth - 1))
    return x
```

```python
import functools
import math

import jax
import jax.numpy as jnp
from jax import lax
from jax.experimental import pallas as pl
from jax.experimental.pallas import tpu as pltpu

F32 = jnp.float32
_BF = jnp.bfloat16
LANES = 128
SUBLANES = 8
VMEM_LIMIT = 56 * 1024 * 1024

D_MODEL = 1024
CHUNK = 64
EPS = 1e-6
NEG_INF_SCORE = -1e30
INT_MIN = -2 ** 31
GROUP = 32 * SUBLANES
GROUP_UNROLL = (16, 8, 4)
PAIR_UNROLL = (4, 2, 1)
V_ROWS = 80

SSM_WIDTH = 256
SSM_GROUP = 16
SSM_GROUPS = 16
SSM_STATE = 64
NSTATE = SSM_GROUPS * SSM_STATE

DSA_HEADS = 6
DSA_HEAD_DIM = 64
DSA_WIDTH = 384
IDX_HEADS = 4
IDX_DIM = 32
DSA_TOPK = 256
ROPE_THETA = 500000.0

RET_HEADS = 4
RET_QK_DIM = 48
RET_V_DIM = 96
RET_WIDTH = 384
RET_THETA = 10000.0
RET_QK_PAD = 256

D_FF = 4096

TM_IN = 512
S5_TILE = 512
S5_SEG = S5_TILE // SUBLANES
QB = 128
KT = 512
RET_C = 128
TM_OUT = 512
FF_CHUNK = 1024


def _rms(x, g):
    return x * lax.rsqrt(jnp.mean(x * x, axis=-1, keepdims=True) + EPS) * g


def _dot(a, b):
    return jnp.dot(a, b, preferred_element_type=F32)


def _dot_nt(a, b):
    return lax.dot_general(a, b, (((1,), (1,)), ((), ())), preferred_element_type=F32)


def _to_ukey(v):
    bits = lax.bitcast_convert_type(v, jnp.int32)
    return lax.bitcast_convert_type(bits ^ ((bits >> 31) | INT_MIN), jnp.uint32)


def _transpose32(a):
    a = list(a)
    j, m = 16, 0x0000FFFF
    while j:
        k = 0
        while k < 32:
            t = (a[k] ^ (a[k + j] >> j)) & m
            a[k] = a[k] ^ t
            a[k + j] = a[k + j] ^ (t << j)
            k = (k + j + 1) & ~j
        j >>= 1
        m = (m ^ (m << j)) & 0xFFFFFFFF
    return a


def _const_spec(shape):
    nd = len(shape)
    return pl.BlockSpec(shape, lambda *_: (0,) * nd)


def _params(sem):
    return pltpu.CompilerParams(dimension_semantics=sem, vmem_limit_bytes=VMEM_LIMIT)


def _rope_roll(z, cos, sin, sgn, shift):
    outs = []
    for c in range(z.shape[1] // LANES):
        zc = z[:, c * LANES:(c + 1) * LANES]
        rot = (pltpu.roll(zc, LANES - shift, 1) * sgn[0:1, :]
               + pltpu.roll(zc, shift, 1) * sgn[1:2, :])
        outs.append(zc * cos + rot * sin)
    return outs[0] if len(outs) == 1 else jnp.concatenate(outs, axis=1)


def _inproj_kernel(x_ref, g_ref, cd_ref, sd_ref, ci_ref, si_ref, cr_ref, sr_ref, sgd_ref, sgi_ref,
                   wu_ref, wdq_ref, wdk_ref, wdvt_ref, wiq_ref, wik_ref, wiwt_ref,
                   wrq_ref, wrk_ref, wrv_ref, wrg_ref,
                   u_ref, dq_ref, dk_ref, dvt_ref, iq_ref, ik_ref, iwt_ref,
                   rq_ref, rk_ref, rv_ref, rg_ref):
    h = _rms(x_ref[...], g_ref[...]).astype(_BF)

    def proj(w_ref):
        return _dot_nt(h, w_ref[...])

    u_ref[...] = proj(wu_ref)

    cd, sd, sgd = cd_ref[...], sd_ref[...], sgd_ref[...]
    dq = _rope_roll(proj(wdq_ref), cd, sd, sgd, 8)
    dq_ref[...] = (dq * (DSA_HEAD_DIM ** -0.5 * math.log2(math.e))).astype(dq_ref.dtype)
    dk_ref[...] = _rope_roll(proj(wdk_ref), cd, sd, sgd, 8).astype(dk_ref.dtype)
    vt = _dot_nt(wdvt_ref[...], h)
    ones_row = lax.broadcasted_iota(jnp.int32, vt.shape, 0) == DSA_HEAD_DIM
    dvt_ref[...] = jnp.where(ones_row, 1.0, vt).astype(dvt_ref.dtype)

    ci, si, sgi = ci_ref[...], si_ref[...], sgi_ref[...]
    iq_ref[...] = _rope_roll(proj(wiq_ref), ci, si, sgi, 4) * (IDX_DIM ** -0.5)
    ik_ref[...] = _rope_roll(proj(wik_ref), ci, si, sgi, 4).astype(ik_ref.dtype)
    iwt_ref[...] = _dot_nt(wiwt_ref[...], h) * (IDX_HEADS ** -0.5)

    cr, sr = cr_ref[...], sr_ref[...]

    def rope_split(z):
        x1, x2 = z[:, :LANES], z[:, LANES:]
        return jnp.concatenate([x1 * cr - x2 * sr, x2 * cr + x1 * sr], axis=1)

    rq_ref[...] = rope_split(proj(wrq_ref))
    rk_ref[...] = rope_split(proj(wrk_ref)) * (RET_QK_DIM ** -0.5)
    rv_ref[...] = proj(wrv_ref)
    rg_ref[...] = proj(wrg_ref)


def _inproj(x, g, tabs, ws):
    bsz, seq, _ = x.shape
    tm = TM_IN
    grid = (bsz, seq // tm)
    row = lambda w: pl.BlockSpec((None, tm, w), lambda b, i: (b, i, 0))
    colT = lambda r: pl.BlockSpec((None, r, tm), lambda b, i: (b, 0, i))
    in_specs = ([row(D_MODEL), _const_spec(g.shape)] + [row(LANES)] * 6
                + [_const_spec(t.shape) for t in tabs[6:]] + [_const_spec(w.shape) for w in ws])
    out_shape = [
        jax.ShapeDtypeStruct((bsz, seq, SSM_WIDTH), F32),
        jax.ShapeDtypeStruct((bsz, seq, DSA_WIDTH), _BF),
        jax.ShapeDtypeStruct((bsz, seq, LANES), _BF),
        jax.ShapeDtypeStruct((bsz, LANES, seq), _BF),
        jax.ShapeDtypeStruct((bsz, seq, LANES), F32),
        jax.ShapeDtypeStruct((bsz, seq, LANES), _BF),
        jax.ShapeDtypeStruct((bsz, SUBLANES, seq), F32),
        jax.ShapeDtypeStruct((bsz, seq, RET_QK_PAD), F32),
        jax.ShapeDtypeStruct((bsz, seq, RET_QK_PAD), F32),
        jax.ShapeDtypeStruct((bsz, seq, RET_WIDTH), F32),
        jax.ShapeDtypeStruct((bsz, seq, RET_WIDTH), F32),
    ]
    out_specs = [row(SSM_WIDTH), row(DSA_WIDTH), row(LANES), colT(LANES), row(LANES), row(LANES),
                 colT(SUBLANES), row(RET_QK_PAD), row(RET_QK_PAD), row(RET_WIDTH), row(RET_WIDTH)]
    return pl.pallas_call(
        _inproj_kernel, grid=grid, in_specs=in_specs, out_specs=out_specs, out_shape=out_shape,
        compiler_params=_params(("parallel", "parallel")), name="inproj",
    )(x, g, *tabs, *ws)


def _s5_kernel(u_ref, bbd_ref, cbd_ref, lam_ref, lam64_ref, pow_ref, d_ref, gw_ref, gb_ref,
               o_ref, st_ref, xm_ref, carry_ref, up_ref, yp_ref):
    n = NSTATE

    @pl.when(pl.program_id(1) == 0)
    def _():
        carry_ref[...] = jnp.zeros_like(carry_ref)

    nhalf = SSM_WIDTH // LANES
    for i in range(SUBLANES):
        for c in range(nhalf):
            up_ref[c, pl.ds(i, S5_SEG, stride=SUBLANES), :] = u_ref[i * S5_SEG:(i + 1) * S5_SEG,
                                                                    c * LANES:(c + 1) * LANES]
    u = jnp.concatenate([up_ref[c] for c in range(nhalf)], axis=1)
    st_ref[...] = _dot(u.astype(_BF), bbd_ref[...])

    lr = jnp.broadcast_to(lam_ref[:, :n], (SUBLANES, n))
    li = jnp.broadcast_to(lam_ref[:, n:], (SUBLANES, n))

    def step(j, c):
        xr, xi = c
        off = pl.multiple_of(j * SUBLANES, SUBLANES)
        ar = st_ref[pl.ds(off, SUBLANES), :n]
        ai = st_ref[pl.ds(off, SUBLANES), n:]
        nr = lr * xr - li * xi + ar
        ni = lr * xi + li * xr + ai
        st_ref[pl.ds(off, SUBLANES), :n] = nr
        st_ref[pl.ds(off, SUBLANES), n:] = ni
        return nr, ni

    zero = jnp.zeros((SUBLANES, n), F32)
    xr, xi = lax.fori_loop(0, S5_SEG, step, (zero, zero), unroll=4)

    l64r, l64i = lam64_ref[:, :n], lam64_ref[:, n:]
    cr, ci = carry_ref[:, :n], carry_ref[:, n:]
    for i in range(SUBLANES):
        xm_ref[i:i + 1, :n] = cr
        xm_ref[i:i + 1, n:] = ci
        er, ei = xr[i:i + 1, :], xi[i:i + 1, :]
        cr, ci = er + l64r * cr - l64i * ci, ei + l64r * ci + l64i * cr
    carry_ref[:, :n] = cr
    carry_ref[:, n:] = ci

    hr, hi = xm_ref[:, :n], xm_ref[:, n:]

    def fix(j, _):
        off = pl.multiple_of(j * SUBLANES, SUBLANES)
        pr = pow_ref[pl.ds(j, 1), :n]
        pi = pow_ref[pl.ds(j, 1), n:]
        st_ref[pl.ds(off, SUBLANES), :n] = st_ref[pl.ds(off, SUBLANES), :n] + pr * hr - pi * hi
        st_ref[pl.ds(off, SUBLANES), n:] = st_ref[pl.ds(off, SUBLANES), n:] + pr * hi + pi * hr
        return 0

    lax.fori_loop(0, S5_SEG, fix, 0, unroll=4)

    y = d_ref[...] * u
    kc = 512
    for k in range(2 * n // kc):
        y = y + _dot(st_ref[:, k * kc:(k + 1) * kc].astype(_BF), cbd_ref[k * kc:(k + 1) * kc, :])
    y = jax.nn.gelu(y)
    y = y * jax.nn.sigmoid(_dot(y.astype(_BF), gw_ref[...]) + gb_ref[...])
    for c in range(nhalf):
        yp_ref[c] = y[:, c * LANES:(c + 1) * LANES]
    for i in range(SUBLANES):
        for c in range(nhalf):
            o_ref[i * S5_SEG:(i + 1) * S5_SEG, c * LANES:(c + 1) * LANES] = yp_ref[
                c, pl.ds(i, S5_SEG, stride=SUBLANES), :]


def _s5(u_perm, prm):
    bsz, seq, _ = u_perm.shape
    grid = (bsz, seq // S5_TILE)
    row = pl.BlockSpec((None, S5_TILE, SSM_WIDTH), lambda b, i: (b, i, 0))
    return pl.pallas_call(
        _s5_kernel, grid=grid,
        in_specs=[row] + [_const_spec(p.shape) for p in prm],
        out_specs=row,
        out_shape=jax.ShapeDtypeStruct((bsz, seq, SSM_WIDTH), F32),
        scratch_shapes=[pltpu.VMEM((S5_TILE, 2 * NSTATE), F32),
                        pltpu.VMEM((SUBLANES, 2 * NSTATE), F32),
                        pltpu.VMEM((1, 2 * NSTATE), F32),
                        pltpu.VMEM((SSM_WIDTH // LANES, S5_TILE, LANES), F32),
                        pltpu.VMEM((SSM_WIDTH // LANES, S5_TILE, LANES), F32)],
        compiler_params=_params(("arbitrary", "arbitrary")), name="s5",
    )(u_perm, *prm)


def _s5_params(lam_re, lam_im, log_step, b_re, b_im, c_re, c_im, d_skip, glu_w, glu_b):
    step = jnp.exp(log_step.astype(F32))[:, None]
    ere, eim = lam_re * step, lam_im * step
    mag = jnp.exp(ere)
    lb_re, lb_im = mag * jnp.cos(eim), mag * jnp.sin(eim)
    den = lam_re * lam_re + lam_im * lam_im
    f_re = ((lb_re - 1.0) * lam_re + lb_im * lam_im) / den
    f_im = (lb_im * lam_re - (lb_re - 1.0) * lam_im) / den
    bb_re = f_re[..., None] * b_re - f_im[..., None] * b_im
    bb_im = f_re[..., None] * b_im + f_im[..., None] * b_re
    eye = jnp.eye(SSM_GROUPS, dtype=F32)
    bbd = jnp.concatenate(
        [jnp.einsum('gpc,gh->gchp', bb, eye).reshape(SSM_WIDTH, NSTATE) for bb in (bb_re, bb_im)], axis=1)
    cbd = jnp.concatenate(
        [jnp.einsum('gcp,gh->hpgc', cc, eye).reshape(NSTATE, SSM_WIDTH) for cc in (c_re, -c_im)], axis=0)
    flat = lambda a: a.reshape(1, NSTATE)
    lam = jnp.concatenate([flat(lb_re), flat(lb_im)], axis=1)

    def power(k):
        k = jnp.asarray(k, F32).reshape(-1, 1)
        m = jnp.exp(k * flat(ere))
        return jnp.concatenate([m * jnp.cos(k * flat(eim)), m * jnp.sin(k * flat(eim))], axis=1)

    return (bbd.astype(_BF), cbd.astype(_BF), lam, power(float(S5_SEG)),
            power(jnp.arange(1, S5_SEG + 1)), d_skip.reshape(1, SSM_WIDTH),
            glu_w.astype(_BF), glu_b.reshape(1, SSM_WIDTH))


def _dsa_kernel(iq_ref, iwt_ref, dq_ref, ik_ref, dk_ref, dvt_ref, o_ref,
                planes_ref, cand_ref, pick_ref, bias_ref, acc_ref,
                lg0_ref, lg1_ref, s0_ref, s1_ref, p0_ref, p1_ref, *, seq, topk):
    qi = pl.program_id(1)
    nk = (qi * QB) // KT + 1
    lane = lax.broadcasted_iota(jnp.int32, (1, QB), 1)
    vis_end = qi * QB + jnp.where(lane < CHUNK, CHUNK, 2 * CHUNK)

    iq = iq_ref[...]
    col = lax.broadcasted_iota(jnp.int32, (QB, LANES), 1)
    qs_t = jnp.concatenate([jnp.where(col // IDX_DIM == h, iq, 0.0) for h in range(IDX_HEADS)],
                           axis=0).T.astype(_BF)
    w = iwt_ref[...]

    nkp = nk + lax.rem(nk, 2)
    last_tile = nkp - 1

    def key_tile(ref, kt):
        return ref[pl.ds(pl.multiple_of(kt * KT, KT), KT), :]

    def tile_planes(src_ref):
        lg = src_ref[...]
        sc = jnp.zeros((KT, QB), F32)
        for h in range(IDX_HEADS):
            sc = sc + jnp.maximum(lg[:, h * QB:(h + 1) * QB], 0.0) * w[h:h + 1, :]
        sc = jnp.where(sc == 0.0, 0.0, sc)
        ukey = _to_ukey(sc)
        words = []
        for g in range(KT // GROUP):
            words += _transpose32([ukey[g * GROUP + m * SUBLANES:g * GROUP + (m + 1) * SUBLANES, :]
                                   for m in range(32)])
        return lax.bitcast_convert_type(jnp.concatenate(words, axis=0), jnp.int32)

    def score_pair(i, c):
        ik1 = key_tile(ik_ref, 2 * i + 1)
        ik2 = key_tile(ik_ref, jnp.minimum(2 * i + 2, last_tile))
        lg1_ref[...] = _dot(ik1, qs_t)
        planes0 = tile_planes(lg0_ref)
        lg0_ref[...] = _dot(ik2, qs_t)
        planes1 = tile_planes(lg1_ref)
        planes_ref[pl.ds(pl.multiple_of(i * (2 * KT), 2 * KT), 2 * KT), :] = jnp.concatenate(
            [planes0, planes1], axis=0)
        return c

    lg0_ref[...] = _dot(key_tile(ik_ref, 0), qs_t)
    npairs = nkp // 2

    def pair_loops(pair_fn, carry):
        start = 0
        for u in PAIR_UNROLL:
            def body(j, c, u=u, start=start):
                for t in range(u):
                    c = pair_fn(start + j * u + t, c)
                return c
            n = (npairs - start) // u
            carry = lax.fori_loop(0, n, body, carry)
            start = start + n * u
        return carry

    pair_loops(score_pair, 0)

    ngrp = nkp * (KT // GROUP)
    sub = lax.broadcasted_iota(jnp.int32, (SUBLANES, QB), 0)
    zero8 = jnp.zeros((SUBLANES, QB), jnp.int32)

    def rows_below(limit, g):
        nm = lax.shift_right_arithmetic(limit - g * GROUP + (SUBLANES - 1), 3)
        top = lax.shift_right_arithmetic(jnp.full((SUBLANES, QB), INT_MIN, jnp.int32),
                                         jnp.clip(nm, 1, 32) - 1)
        return jnp.where(nm <= 0, 0, top)

    def word(ref, g):
        return ref[pl.ds(pl.multiple_of(g * SUBLANES, SUBLANES), SUBLANES), :]

    def put_word(ref, g, v):
        ref[pl.ds(pl.multiple_of(g * SUBLANES, SUBLANES), SUBLANES), :] = v

    def plane(g, i):
        return planes_ref[pl.ds(pl.multiple_of(g * GROUP + i * SUBLANES, SUBLANES), SUBLANES), :]

    def sweep_words(step):
        accs, start = (zero8,) * GROUP_UNROLL[-1], 0
        for u in GROUP_UNROLL:
            def body(s, accs, u=u, start=start):
                accs = list(accs)
                for t in range(u):
                    accs[t % len(accs)] = accs[t % len(accs)] + step(start + s * u + t)
                return tuple(accs)
            n = (ngrp - start) // u
            accs = lax.fori_loop(0, n, body, accs)
            start = start + n * u
        return functools.reduce(lambda a, b: a + b, accs)

    def sweep(step):
        return sweep_words(step).sum(axis=0, keepdims=True)

    def init_words(g):
        put_word(cand_ref, g, rows_below(vis_end - sub, g))
        put_word(pick_ref, g, zero8)
        return zero8

    sweep(init_words)

    def count_ones(i):
        return sweep(lambda g: lax.population_count(word(cand_ref, g) & plane(g, i)))

    def decide(c1, rem):
        take = c1 >= rem
        return jnp.where(take, 1, 0), jnp.where(take, rem, rem - c1)

    def narrow(g, i, take):
        take8 = jnp.broadcast_to(take, (SUBLANES, QB)) != 0
        c = word(cand_ref, g)
        ones = c & plane(g, i)
        put_word(pick_ref, g, word(pick_ref, g) | jnp.where(take8, 0, ones))
        c = jnp.where(take8, ones, c ^ ones)
        put_word(cand_ref, g, c)
        return c

    def radix_step(i, carry):
        take8, rem = carry

        def step(g):
            return lax.population_count(narrow(g, i - 1, take8) & plane(g, i))
        return decide(sweep(step), rem)

    take8, rem = decide(count_ones(0), jnp.full((1, QB), topk, jnp.int32))
    take8, rem = lax.fori_loop(1, 32, radix_step, (take8, rem))

    def last_narrow(g):
        narrow(g, 31, take8)
        return zero8

    sweep(last_narrow)

    def group_step(i, carry):
        grp, before = carry
        trial = grp + jnp.left_shift(jnp.int32(1), ((seq // GROUP).bit_length() - 2) - i)
        below = sweep(lambda g: jnp.where(jnp.broadcast_to(g < trial, (SUBLANES, QB)),
                                          lax.population_count(word(cand_ref, g)), 0))
        ok = below < rem
        return jnp.where(ok, trial, grp), jnp.where(ok, below, before)

    zrow = jnp.zeros((1, QB), jnp.int32)
    grp, before = lax.fori_loop(0, (seq // GROUP).bit_length() - 1, group_step, (zrow, zrow))
    tied = sweep_words(lambda g: jnp.where(jnp.broadcast_to(g == grp, (SUBLANES, QB)), word(cand_ref, g), 0))

    def row_step(i, edge):
        trial = edge + jnp.left_shift(jnp.int32(1), (GROUP.bit_length() - 2) - i)
        below = lax.population_count(tied & rows_below(trial - sub, 0)).sum(axis=0, keepdims=True)
        return jnp.where(before + below < rem, trial, edge)

    edge = lax.fori_loop(0, GROUP.bit_length() - 1, row_step, zrow)
    keep_limit = grp * GROUP + edge + 1 - sub

    def bias_group(g):
        chosen = word(pick_ref, g) | (word(cand_ref, g) & rows_below(keep_limit, g))
        base = pl.multiple_of(g * GROUP, GROUP)
        bias_ref[pl.ds(base, GROUP), :] = jnp.concatenate(
            [jnp.where(lax.shift_left(chosen, jnp.full_like(chosen, m)) < 0, 0.0, NEG_INF_SCORE)
             for m in range(32)], axis=0).astype(bias_ref.dtype)
        return zero8

    sweep(bias_group)

    q = dq_ref[...]
    half = (col // DSA_HEAD_DIM)
    qst = jnp.concatenate(
        [jnp.where(half == (h % 2), q[:, (h // 2) * LANES:(h // 2 + 1) * LANES], jnp.zeros((), q.dtype))
         for h in range(DSA_HEADS)], axis=0)
    unit = jnp.where(lax.broadcasted_iota(jnp.int32, (QB, QB), 0) == lax.broadcasted_iota(jnp.int32, (QB, QB), 1),
                     1.0, 0.0).astype(q.dtype)
    qst = jnp.concatenate([qst, jnp.concatenate([unit] * DSA_HEADS, axis=0)], axis=1)
    qst_t = qst.astype(F32).T.astype(q.dtype)
    npair = DSA_HEADS // 2
    hq = DSA_HEADS * QB
    acc_ref[...] = jnp.zeros_like(acc_ref)
    p1_ref[...] = jnp.zeros_like(p1_ref)

    def qk(k_tile, b_tile, dst_ref):
        dst_ref[...] = _dot(jnp.concatenate([k_tile, b_tile], axis=1), qst_t)

    def numer(src_ref, dst_ref, m):
        m_new = []
        for hd in range(DSA_HEADS):
            cols = slice(hd * QB, (hd + 1) * QB)
            mh = jnp.maximum(m[:, cols], jnp.max(src_ref[:, cols], axis=0, keepdims=True))
            dst_ref[:, cols] = jnp.exp2(src_ref[:, cols] - mh).astype(dst_ref.dtype)
            m_new.append(mh)
        m_new = jnp.concatenate(m_new, axis=1)
        return m_new, jnp.exp2(m - m_new)

    def v_tile(kt):
        return dvt_ref[0:V_ROWS, pl.ds(pl.multiple_of(kt * KT, KT), KT)]

    def pv(vt, src_ref, alpha):
        acc_ref[...] = alpha * acc_ref[...] + _dot(vt, src_ref[...])

    def att_pair(i, c):
        m, alpha = c
        nxt = jnp.minimum(2 * i + 2, last_tile)
        k1, b1 = key_tile(dk_ref, 2 * i + 1), key_tile(bias_ref, 2 * i + 1)
        k2, b2 = key_tile(dk_ref, nxt), key_tile(bias_ref, nxt)
        v_prev, v_cur = v_tile(jnp.maximum(2 * i - 1, 0)), v_tile(2 * i)
        qk(k1, b1, s1_ref)
        pv(v_prev, p1_ref, alpha)
        m, alpha = numer(s0_ref, p0_ref, m)
        qk(k2, b2, s0_ref)
        pv(v_cur, p0_ref, alpha)
        m, alpha = numer(s1_ref, p1_ref, m)
        return m, alpha

    qk(key_tile(dk_ref, 0), key_tile(bias_ref, 0), s0_ref)
    c0 = (jnp.full((1, hq), -jnp.inf, F32), jnp.ones((1, hq), F32))
    _, alpha = pair_loops(att_pair, c0)
    pv(v_tile(last_tile), p1_ref, alpha)
    acc = acc_ref[...]
    o = acc[0:DSA_HEAD_DIM, :] / acc[DSA_HEAD_DIM:DSA_HEAD_DIM + 1, :]
    for j in range(npair):
        o_ref[:, j * LANES:(j + 1) * LANES] = jnp.concatenate(
            [o[:, (2 * j) * QB:(2 * j + 1) * QB], o[:, (2 * j + 1) * QB:(2 * j + 2) * QB]], axis=0).T


def _dsa(iq, iwt, dq, ik, dk, dvt):
    bsz, seq, _ = iq.shape
    topk = min(DSA_TOPK, seq // 4)
    grid = (bsz, seq // QB)
    qrow = lambda w: pl.BlockSpec((None, QB, w), lambda b, i: (b, i, 0))
    full = lambda r, c: pl.BlockSpec((None, r, c), lambda b, i: (b, 0, 0))
    return pl.pallas_call(
        functools.partial(_dsa_kernel, seq=seq, topk=topk), grid=grid,
        in_specs=[qrow(LANES), pl.BlockSpec((None, SUBLANES, QB), lambda b, i: (b, 0, i)), qrow(DSA_WIDTH),
                  full(seq, LANES), full(seq, LANES), full(LANES, seq)],
        out_specs=qrow(DSA_WIDTH),
        out_shape=jax.ShapeDtypeStruct((bsz, seq, DSA_WIDTH), F32),
        scratch_shapes=[pltpu.VMEM((seq, QB), jnp.int32),
                        pltpu.VMEM((seq // GROUP * SUBLANES, QB), jnp.int32),
                        pltpu.VMEM((seq // GROUP * SUBLANES, QB), jnp.int32),
                        pltpu.VMEM((seq, QB), _BF),
                        pltpu.VMEM((V_ROWS, DSA_HEADS * QB), F32),
                        pltpu.VMEM((KT, IDX_HEADS * QB), F32), pltpu.VMEM((KT, IDX_HEADS * QB), F32),
                        pltpu.VMEM((KT, DSA_HEADS * QB), F32), pltpu.VMEM((KT, DSA_HEADS * QB), F32),
                        pltpu.VMEM((KT, DSA_HEADS * QB), _BF), pltpu.VMEM((KT, DSA_HEADS * QB), _BF)],
        compiler_params=_params(("parallel", "arbitrary")), name="dsa",
    )(iq, iwt, dq, ik, dk, dvt)


def _ret_kernel(rq_ref, rk_ref, rv_ref, rg_ref, qdec_ref, kdec_ref, dmask_ref, cdec_ref, blk_ref,
                hmq_ref, hmv_ref, ones_ref, ng_ref, o_ref, s_ref):
    @pl.when(pl.program_id(0) == 0)
    def _():
        s_ref[...] = jnp.zeros_like(s_ref)

    ones = ones_ref[...]

    def head_mean(z):
        return _dot(z.astype(_BF), ones) * (1.0 / RET_V_DIM)

    nseq = rq_ref.shape[0]
    ys = []
    for b in range(nseq):
        q, k, v = rq_ref[b], rk_ref[b], rv_ref[b]
        state = s_ref[b]
        y = _dot((q * qdec_ref[...]).astype(_BF), state.astype(_BF))
        kd = (k * kdec_ref[...]).T.astype(_BF)
        vb = v.astype(_BF)
        s_ref[b] = state * cdec_ref[...] + _dot(kd, vb) * blk_ref[...]
        kb = k.astype(_BF)
        a = jnp.concatenate([(_dot_nt((q * hmq_ref[h:h + 1, :]).astype(_BF), kb) * dmask_ref[h]).astype(_BF)
                             for h in range(RET_HEADS)], axis=1)
        vh = jnp.concatenate([(v * hmv_ref[h:h + 1, :]).astype(_BF) for h in range(RET_HEADS)], axis=0)
        ys.append(y + _dot(a, vh))
    y = jnp.concatenate(ys, axis=0)
    d = y - head_mean(y)
    yn = d * lax.rsqrt(head_mean(d * d) + EPS) * ng_ref[...]
    for b in range(nseq):
        o_ref[b] = jax.nn.silu(rg_ref[b]) * yn[b * RET_C:(b + 1) * RET_C, :]


def _ret_consts():
    c = RET_C
    log_g = jnp.log1p(-jnp.exp2(-5.0 - jnp.arange(RET_HEADS, dtype=F32)))
    pos = jnp.arange(c, dtype=F32)
    diff = pos[:, None] - pos[None, :]
    dmask = jnp.where(diff >= 0, jnp.exp(log_g[:, None, None] * jnp.maximum(diff, 0.0)), 0.0)
    lane_q = jnp.arange(RET_QK_PAD)
    head_q = jnp.where((lane_q % LANES) < RET_HEADS * (RET_QK_DIM // 2), (lane_q % LANES) // (RET_QK_DIM // 2), -1)
    head_v = jnp.arange(RET_WIDTH) // RET_V_DIM
    hmq = (head_q[None, :] == jnp.arange(RET_HEADS)[:, None]).astype(F32)
    hmv = (head_v[None, :] == jnp.arange(RET_HEADS)[:, None]).astype(F32)
    lg_q = hmq.T @ log_g
    qdec = jnp.exp(lg_q[None, :] * (pos[:, None] + 1.0))
    kdec = jnp.exp(lg_q[None, :] * (c - 1.0 - pos[:, None]))
    blk = hmq.T @ hmv
    cdec = blk * jnp.exp(lg_q * c)[:, None]
    ones = (hmv.T @ hmv).astype(_BF)
    return qdec, kdec, dmask, cdec, blk, hmq, hmv, ones


def _retention(rq, rk, rv, rg, norm_g):
    bsz, seq, _ = rq.shape
    consts = _ret_consts()
    grid = (seq // RET_C,)
    row = lambda w: pl.BlockSpec((bsz, RET_C, w), lambda i: (0, i, 0))
    ng = norm_g.reshape(1, RET_WIDTH)
    return pl.pallas_call(
        _ret_kernel, grid=grid,
        in_specs=[row(RET_QK_PAD), row(RET_QK_PAD), row(RET_WIDTH), row(RET_WIDTH)]
        + [_const_spec(a.shape) for a in consts] + [_const_spec(ng.shape)],
        out_specs=row(RET_WIDTH),
        out_shape=jax.ShapeDtypeStruct((bsz, seq, RET_WIDTH), F32),
        scratch_shapes=[pltpu.VMEM((bsz, RET_QK_PAD, RET_WIDTH), F32)],
        compiler_params=_params(("arbitrary",)), name="retention",
    )(rq, rk, rv, rg, *consts, ng)


def _merge_kernel(x_ref, g_ref, ya_ref, yb_ref, yc_ref, wg_ref, wa_ref, wb_ref, wc_ref, wo_ref, o_ref):
    x = x_ref[...]
    h = _rms(x, g_ref[...]).astype(_BF)
    merged = jnp.zeros(x.shape, F32)
    for n, (y_ref, w_ref) in enumerate(((ya_ref, wa_ref), (yb_ref, wb_ref), (yc_ref, wc_ref))):
        gate = jax.nn.sigmoid(_dot_nt(h, wg_ref[n * D_MODEL:(n + 1) * D_MODEL, :]))
        merged = merged + gate * _dot(y_ref[...].astype(_BF), w_ref[...])
    o_ref[...] = x + _dot(merged.astype(_BF), wo_ref[...])


def _merge(x, g, ya, yb, yc, wg, wa, wb, wc, wo):
    bsz, seq, _ = x.shape
    grid = (bsz, seq // TM_OUT)
    row = lambda w: pl.BlockSpec((None, TM_OUT, w), lambda b, i: (b, i, 0))
    return pl.pallas_call(
        _merge_kernel, grid=grid,
        in_specs=[row(D_MODEL), _const_spec(g.shape), row(SSM_WIDTH), row(DSA_WIDTH), row(RET_WIDTH)]
        + [_const_spec(w.shape) for w in (wg, wa, wb, wc, wo)],
        out_specs=row(D_MODEL),
        out_shape=jax.ShapeDtypeStruct(x.shape, F32),
        compiler_params=_params(("parallel", "parallel")), name="merge",
    )(x, g, ya, yb, yc, wg, wa, wb, wc, wo)


def _mlp_kernel(x_ref, g_ref, w1_ref, w2_ref, fg_ref, o_ref, *, final_norm):
    x = x_ref[...]
    h = _rms(x, g_ref[...]).astype(_BF)
    acc = x
    for f in range(D_FF // FF_CHUNK):
        t = jnp.maximum(_dot(h, w1_ref[:, f * FF_CHUNK:(f + 1) * FF_CHUNK]), 0.0)
        acc = acc + _dot((t * t).astype(_BF), w2_ref[f * FF_CHUNK:(f + 1) * FF_CHUNK, :])
    o_ref[...] = _rms(acc, fg_ref[...]) if final_norm else acc


def _mlp(x, g, w1, w2, fg, final_norm):
    bsz, seq, _ = x.shape
    grid = (bsz, seq // TM_OUT)
    row = pl.BlockSpec((None, TM_OUT, D_MODEL), lambda b, i: (b, i, 0))
    return pl.pallas_call(
        functools.partial(_mlp_kernel, final_norm=final_norm), grid=grid,
        in_specs=[row, _const_spec(g.shape), _const_spec(w1.shape), _const_spec(w2.shape), _const_spec(fg.shape)],
        out_specs=row,
        out_shape=jax.ShapeDtypeStruct(x.shape, F32),
        compiler_params=_params(("parallel", "parallel")), name="mlp",
    )(x, g, w1, w2, fg)


def _rope_tables(positions):
    pos = positions.astype(F32)[..., None]
    lane = jnp.arange(LANES)

    def angles(rot_dim, theta):
        half = rot_dim // 2
        inv = jnp.exp(-math.log(theta) * jnp.arange(half, dtype=F32) * (2.0 / rot_dim))
        return pos * inv

    def roll_tables(head_dim, rot_dim, theta):
        half = rot_dim // 2
        ang = angles(rot_dim, theta)
        ln = lane % head_dim
        rot = ln < rot_dim
        cos = jnp.where(rot, jnp.take(jnp.cos(ang), ln % half, axis=-1), 1.0)
        sin = jnp.where(rot, jnp.take(jnp.sin(ang), ln % half, axis=-1), 0.0)
        sgn = jnp.stack([jnp.where(ln < half, -1.0, 0.0),
                         jnp.where((ln >= half) & rot, 1.0, 0.0)]).astype(F32)
        return cos, sin, sgn

    cd, sd, sgd = roll_tables(DSA_HEAD_DIM, DSA_HEAD_DIM // 4, ROPE_THETA)
    ci, si, sgi = roll_tables(IDX_DIM, IDX_DIM // 4, ROPE_THETA)
    half = RET_QK_DIM // 2
    ang = angles(RET_QK_DIM, RET_THETA)
    used = lane < RET_HEADS * half
    cr = jnp.where(used, jnp.take(jnp.cos(ang), lane % half, axis=-1), 1.0)
    sr = jnp.where(used, jnp.take(jnp.sin(ang), lane % half, axis=-1), 0.0)
    return cd, sd, ci, si, cr, sr, sgd, sgi


def _inproj_weights(wt):
    sizes = (SSM_WIDTH, DSA_WIDTH, DSA_HEAD_DIM, DSA_HEAD_DIM, IDX_HEADS * IDX_DIM, IDX_DIM, IDX_HEADS,
             RET_HEADS * RET_QK_DIM, RET_HEADS * RET_QK_DIM, RET_WIDTH, RET_WIDTH, 3 * D_MODEL)
    parts, off = [], 0
    for n in sizes:
        parts.append(wt[off:off + n, :])
        off += n
    wu, wdq, wdk, wdv, wiq, wik, wiw, wrq, wrk, wrv, wrg, wgt = parts
    half = RET_QK_DIM // 2
    zrows = lambda n: jnp.zeros((n, D_MODEL), wt.dtype)

    def ret_split(a):
        a = a.reshape(RET_HEADS, 2, half, D_MODEL)
        pad = zrows(LANES - RET_HEADS * half)
        return jnp.concatenate([a[:, 0].reshape(-1, D_MODEL), pad, a[:, 1].reshape(-1, D_MODEL), pad], axis=0)

    ws = (wu, wdq, jnp.concatenate([wdk, wdk], axis=0), jnp.concatenate([wdv, zrows(LANES - DSA_HEAD_DIM)], axis=0),
          wiq, jnp.concatenate([wik] * IDX_HEADS, axis=0), jnp.concatenate([wiw, zrows(SUBLANES - IDX_HEADS)], axis=0),
          ret_split(wrq), ret_split(wrk), wrv, wrg)
    return tuple(a.astype(_BF) for a in ws), wgt.astype(_BF)


def kernel(x, positions, norm1_g, w_in, ssm_lambda_re, ssm_lambda_im, ssm_log_step, ssm_b_re, ssm_b_im,
           ssm_c_re, ssm_c_im, ssm_d, ssm_glu_w, ssm_glu_b, ret_norm_g, w_proj_a, w_proj_b, w_proj_c,
           w_out, norm2_g, w_ff1, w_ff2, final_norm_g):
    depth = w_in.shape[0]
    tabs = _rope_tables(positions)
    fg = final_norm_g.reshape(1, D_MODEL)
    w_in_t = jnp.transpose(w_in, (2, 0, 1))
    for l in range(depth):
        g1 = norm1_g[l].reshape(1, D_MODEL)
        ws, wgt = _inproj_weights(w_in_t[:, l, :])
        u, dq, dk, dvt, iq, ik, iwt, rq, rk, rv, rg = _inproj(x, g1, tabs, ws)

        s5p = _s5_params(ssm_lambda_re[l], ssm_lambda_im[l], ssm_log_step[l], ssm_b_re[l], ssm_b_im[l],
                         ssm_c_re[l], ssm_c_im[l], ssm_d[l], ssm_glu_w[l], ssm_glu_b[l])
        ya = _s5(u, s5p)
        yb = _dsa(iq, iwt, dq, ik, dk, dvt)
        yc = _retention(rq, rk, rv, rg, ret_norm_g[l])

        x = _merge(x, g1, ya, yb, yc, wgt, w_proj_a[l].astype(_BF), w_proj_b[l].astype(_BF),
                   w_proj_c[l].astype(_BF), w_out[l].astype(_BF))
        x = _mlp(x, norm2_g[l].reshape(1, D_MODEL), w_ff1[l].astype(_BF), w_ff2[l].astype(_BF), fg,
                 final_norm=(l == depth - 1))
    return x
```

```python
import functools
import math

import jax
import jax.numpy as jnp
from jax import lax
from jax.experimental import pallas as pl
from jax.experimental.pallas import tpu as pltpu

F32 = jnp.float32
_BF = jnp.bfloat16
LANES = 128
SUBLANES = 8
VMEM_LIMIT = 56 * 1024 * 1024

D_MODEL = 1024
CHUNK = 64
EPS = 1e-6
NEG_INF_SCORE = -1e30
INT_MIN = -2 ** 31
GROUP = 32 * SUBLANES
GROUP_UNROLL = (16, 8, 4)
PAIR_UNROLL = (4, 2, 1)
V_ROWS = 80

SSM_WIDTH = 256
SSM_GROUP = 16
SSM_GROUPS = 16
SSM_STATE = 64
NSTATE = SSM_GROUPS * SSM_STATE

DSA_HEADS = 6
DSA_HEAD_DIM = 64
DSA_WIDTH = 384
IDX_HEADS = 4
IDX_DIM = 32
DSA_TOPK = 256
ROPE_THETA = 500000.0

RET_HEADS = 4
RET_QK_DIM = 48
RET_V_DIM = 96
RET_WIDTH = 384
RET_THETA = 10000.0
RET_QK_PAD = 256

D_FF = 4096

TM_IN = 512
S5_TILE = 512
S5_SEG = S5_TILE // SUBLANES
QB = 128
KT = 512
RET_C = 128
TM_OUT = 512
FF_CHUNK = 1024


def _rms(x, g):
    return x * lax.rsqrt(jnp.mean(x * x, axis=-1, keepdims=True) + EPS) * g


def _dot(a, b):
    return jnp.dot(a, b, preferred_element_type=F32)


def _dot_nt(a, b):
    return lax.dot_general(a, b, (((1,), (1,)), ((), ())), preferred_element_type=F32)


def _to_ukey(v):
    bits = lax.bitcast_convert_type(v, jnp.int32)
    return lax.bitcast_convert_type(bits ^ ((bits >> 31) | INT_MIN), jnp.uint32)


def _transpose32(a):
    a = list(a)
    j, m = 16, 0x0000FFFF
    while j:
        k = 0
        while k < 32:
            t = (a[k] ^ (a[k + j] >> j)) & m
            a[k] = a[k] ^ t
            a[k + j] = a[k + j] ^ (t << j)
            k = (k + j + 1) & ~j
        j >>= 1
        m = (m ^ (m << j)) & 0xFFFFFFFF
    return a


def _const_spec(shape):
    nd = len(shape)
    return pl.BlockSpec(shape, lambda *_: (0,) * nd)


def _params(sem):
    return pltpu.CompilerParams(dimension_semantics=sem, vmem_limit_bytes=VMEM_LIMIT)


def _rope_roll(z, cos, sin, sgn, shift):
    outs = []
    for c in range(z.shape[1] // LANES):
        zc = z[:, c * LANES:(c + 1) * LANES]
        rot = (pltpu.roll(zc, LANES - shift, 1) * sgn[0:1, :]
               + pltpu.roll(zc, shift, 1) * sgn[1:2, :])
        outs.append(zc * cos + rot * sin)
    return outs[0] if len(outs) == 1 else jnp.concatenate(outs, axis=1)


def _inproj_kernel(x_ref, g_ref, cd_ref, sd_ref, ci_ref, si_ref, cr_ref, sr_ref, sgd_ref, sgi_ref,
                   wu_ref, wdq_ref, wdk_ref, wdvt_ref, wiq_ref, wik_ref, wiwt_ref,
                   wrq_ref, wrk_ref, wrv_ref, wrg_ref,
                   u_ref, dq_ref, dk_ref, dvt_ref, iq_ref, ik_ref, iwt_ref,
                   rq_ref, rk_ref, rv_ref, rg_ref):
    h = _rms(x_ref[...], g_ref[...]).astype(_BF)

    def proj(w_ref):
        return _dot_nt(h, w_ref[...])

    u_ref[...] = proj(wu_ref)

    cd, sd, sgd = cd_ref[...], sd_ref[...], sgd_ref[...]
    dq = _rope_roll(proj(wdq_ref), cd, sd, sgd, 8)
    dq_ref[...] = (dq * (DSA_HEAD_DIM ** -0.5 * math.log2(math.e))).astype(dq_ref.dtype)
    dk_ref[...] = _rope_roll(proj(wdk_ref), cd, sd, sgd, 8).astype(dk_ref.dtype)
    vt = _dot_nt(wdvt_ref[...], h)
    ones_row = lax.broadcasted_iota(jnp.int32, vt.shape, 0) == DSA_HEAD_DIM
    dvt_ref[...] = jnp.where(ones_row, 1.0, vt).astype(dvt_ref.dtype)

    ci, si, sgi = ci_ref[...], si_ref[...], sgi_ref[...]
    iq_ref[...] = _rope_roll(proj(wiq_ref), ci, si, sgi, 4) * (IDX_DIM ** -0.5)
    ik_ref[...] = _rope_roll(proj(wik_ref), ci, si, sgi, 4).astype(ik_ref.dtype)
    iwt_ref[...] = _dot_nt(wiwt_ref[...], h) * (IDX_HEADS ** -0.5)

    cr, sr = cr_ref[...], sr_ref[...]

    def rope_split(z):
        x1, x2 = z[:, :LANES], z[:, LANES:]
        return jnp.concatenate([x1 * cr - x2 * sr, x2 * cr + x1 * sr], axis=1)

    rq_ref[...] = rope_split(proj(wrq_ref))
    rk_ref[...] = rope_split(proj(wrk_ref)) * (RET_QK_DIM ** -0.5)
    rv_ref[...] = proj(wrv_ref)
    rg_ref[...] = proj(wrg_ref)


def _inproj(x, g, tabs, ws):
    bsz, seq, _ = x.shape
    tm = TM_IN
    grid = (bsz, seq // tm)
    row = lambda w: pl.BlockSpec((None, tm, w), lambda b, i: (b, i, 0))
    colT = lambda r: pl.BlockSpec((None, r, tm), lambda b, i: (b, 0, i))
    in_specs = ([row(D_MODEL), _const_spec(g.shape)] + [row(LANES)] * 6
                + [_const_spec(t.shape) for t in tabs[6:]] + [_const_spec(w.shape) for w in ws])
    out_shape = [
        jax.ShapeDtypeStruct((bsz, seq, SSM_WIDTH), F32),
        jax.ShapeDtypeStruct((bsz, seq, DSA_WIDTH), _BF),
        jax.ShapeDtypeStruct((bsz, seq, LANES), _BF),
        jax.ShapeDtypeStruct((bsz, LANES, seq), _BF),
        jax.ShapeDtypeStruct((bsz, seq, LANES), F32),
        jax.ShapeDtypeStruct((bsz, seq, LANES), _BF),
        jax.ShapeDtypeStruct((bsz, SUBLANES, seq), F32),
        jax.ShapeDtypeStruct((bsz, seq, RET_QK_PAD), F32),
        jax.ShapeDtypeStruct((bsz, seq, RET_QK_PAD), F32),
        jax.ShapeDtypeStruct((bsz, seq, RET_WIDTH), F32),
        jax.ShapeDtypeStruct((bsz, seq, RET_WIDTH), F32),
    ]
    out_specs = [row(SSM_WIDTH), row(DSA_WIDTH), row(LANES), colT(LANES), row(LANES), row(LANES),
                 colT(SUBLANES), row(RET_QK_PAD), row(RET_QK_PAD), row(RET_WIDTH), row(RET_WIDTH)]
    return pl.pallas_call(
        _inproj_kernel, grid=grid, in_specs=in_specs, out_specs=out_specs, out_shape=out_shape,
        compiler_params=_params(("parallel", "parallel")), name="inproj",
    )(x, g, *tabs, *ws)


def _s5_kernel(u_ref, bbd_ref, cbd_ref, lam_ref, lam64_ref, pow_ref, d_ref, gw_ref, gb_ref,
               o_ref, st_ref, xm_ref, carry_ref, up_ref, yp_ref):
    n = NSTATE

    @pl.when(pl.program_id(1) == 0)
    def _():
        carry_ref[...] = jnp.zeros_like(carry_ref)

    nhalf = SSM_WIDTH // LANES
    for i in range(SUBLANES):
        for c in range(nhalf):
            up_ref[c, pl.ds(i, S5_SEG, stride=SUBLANES), :] = u_ref[i * S5_SEG:(i + 1) * S5_SEG,
                                                                    c * LANES:(c + 1) * LANES]
    u = jnp.concatenate([up_ref[c] for c in range(nhalf)], axis=1)
    st_ref[...] = _dot(u.astype(_BF), bbd_ref[...])

    lr = jnp.broadcast_to(lam_ref[:, :n], (SUBLANES, n))
    li = jnp.broadcast_to(lam_ref[:, n:], (SUBLANES, n))

    def step(j, c):
        xr, xi = c
        off = pl.multiple_of(j * SUBLANES, SUBLANES)
        ar = st_ref[pl.ds(off, SUBLANES), :n]
        ai = st_ref[pl.ds(off, SUBLANES), n:]
        nr = lr * xr - li * xi + ar
        ni = lr * xi + li * xr + ai
        st_ref[pl.ds(off, SUBLANES), :n] = nr
        st_ref[pl.ds(off, SUBLANES), n:] = ni
        return nr, ni

    zero = jnp.zeros((SUBLANES, n), F32)
    xr, xi = lax.fori_loop(0, S5_SEG, step, (zero, zero), unroll=4)

    l64r, l64i = lam64_ref[:, :n], lam64_ref[:, n:]
    cr, ci = carry_ref[:, :n], carry_ref[:, n:]
    for i in range(SUBLANES):
        xm_ref[i:i + 1, :n] = cr
        xm_ref[i:i + 1, n:] = ci
        er, ei = xr[i:i + 1, :], xi[i:i + 1, :]
        cr, ci = er + l64r * cr - l64i * ci, ei + l64r * ci + l64i * cr
    carry_ref[:, :n] = cr
    carry_ref[:, n:] = ci

    hr, hi = xm_ref[:, :n], xm_ref[:, n:]

    def fix(j, _):
        off = pl.multiple_of(j * SUBLANES, SUBLANES)
        pr = pow_ref[pl.ds(j, 1), :n]
        pi = pow_ref[pl.ds(j, 1), n:]
        st_ref[pl.ds(off, SUBLANES), :n] = st_ref[pl.ds(off, SUBLANES), :n] + pr * hr - pi * hi
        st_ref[pl.ds(off, SUBLANES), n:] = st_ref[pl.ds(off, SUBLANES), n:] + pr * hi + pi * hr
        return 0

    lax.fori_loop(0, S5_SEG, fix, 0, unroll=4)

    y = d_ref[...] * u
    kc = 512
    for k in range(2 * n // kc):
        y = y + _dot(st_ref[:, k * kc:(k + 1) * kc].astype(_BF), cbd_ref[k * kc:(k + 1) * kc, :])
    y = jax.nn.gelu(y)
    y = y * jax.nn.sigmoid(_dot(y.astype(_BF), gw_ref[...]) + gb_ref[...])
    for c in range(nhalf):
        yp_ref[c] = y[:, c * LANES:(c + 1) * LANES]
    for i in range(SUBLANES):
        for c in range(nhalf):
            o_ref[i * S5_SEG:(i + 1) * S5_SEG, c * LANES:(c + 1) * LANES] = yp_ref[
                c, pl.ds(i, S5_SEG, stride=SUBLANES), :]


def _s5(u_perm, prm):
    bsz, seq, _ = u_perm.shape
    grid = (bsz, seq // S5_TILE)
    row = pl.BlockSpec((None, S5_TILE, SSM_WIDTH), lambda b, i: (b, i, 0))
    return pl.pallas_call(
        _s5_kernel, grid=grid,
        in_specs=[row] + [_const_spec(p.shape) for p in prm],
        out_specs=row,
        out_shape=jax.ShapeDtypeStruct((bsz, seq, SSM_WIDTH), F32),
        scratch_shapes=[pltpu.VMEM((S5_TILE, 2 * NSTATE), F32),
                        pltpu.VMEM((SUBLANES, 2 * NSTATE), F32),
                        pltpu.VMEM((1, 2 * NSTATE), F32),
                        pltpu.VMEM((SSM_WIDTH // LANES, S5_TILE, LANES), F32),
                        pltpu.VMEM((SSM_WIDTH // LANES, S5_TILE, LANES), F32)],
        compiler_params=_params(("arbitrary", "arbitrary")), name="s5",
    )(u_perm, *prm)


def _s5_params(lam_re, lam_im, log_step, b_re, b_im, c_re, c_im, d_skip, glu_w, glu_b):
    step = jnp.exp(log_step.astype(F32))[:, None]
    ere, eim = lam_re * step, lam_im * step
    mag = jnp.exp(ere)
    lb_re, lb_im = mag * jnp.cos(eim), mag * jnp.sin(eim)
    den = lam_re * lam_re + lam_im * lam_im
    f_re = ((lb_re - 1.0) * lam_re + lb_im * lam_im) / den
    f_im = (lb_im * lam_re - (lb_re - 1.0) * lam_im) / den
    bb_re = f_re[..., None] * b_re - f_im[..., None] * b_im
    bb_im = f_re[..., None] * b_im + f_im[..., None] * b_re
    eye = jnp.eye(SSM_GROUPS, dtype=F32)
    bbd = jnp.concatenate(
        [jnp.einsum('gpc,gh->gchp', bb, eye).reshape(SSM_WIDTH, NSTATE) for bb in (bb_re, bb_im)], axis=1)
    cbd = jnp.concatenate(
        [jnp.einsum('gcp,gh->hpgc', cc, eye).reshape(NSTATE, SSM_WIDTH) for cc in (c_re, -c_im)], axis=0)
    flat = lambda a: a.reshape(1, NSTATE)
    lam = jnp.concatenate([flat(lb_re), flat(lb_im)], axis=1)

    def power(k):
        k = jnp.asarray(k, F32).reshape(-1, 1)
        m = jnp.exp(k * flat(ere))
        return jnp.concatenate([m * jnp.cos(k * flat(eim)), m * jnp.sin(k * flat(eim))], axis=1)

    return (bbd.astype(_BF), cbd.astype(_BF), lam, power(float(S5_SEG)),
            power(jnp.arange(1, S5_SEG + 1)), d_skip.reshape(1, SSM_WIDTH),
            glu_w.astype(_BF), glu_b.reshape(1, SSM_WIDTH))


def _dsa_kernel(iq_ref, iwt_ref, dq_ref, ik_ref, dk_ref, dvt_ref, o_ref,
                planes_ref, cand_ref, pick_ref, bias_ref, acc_ref,
                lg0_ref, lg1_ref, s0_ref, s1_ref, p0_ref, p1_ref, *, seq, topk):
    t = pl.program_id(1)
    nq = seq // QB
    has_sel = t < nq
    has_att = t >= 1
    qi = jnp.minimum(t, nq - 1)
    qa = jnp.maximum(t - 1, 0)
    nk = (qi * QB) // KT + 1
    lane = lax.broadcasted_iota(jnp.int32, (1, QB), 1)
    vis_end = qi * QB + jnp.where(lane < CHUNK, CHUNK, 2 * CHUNK)

    iq = iq_ref[...]
    col = lax.broadcasted_iota(jnp.int32, (QB, LANES), 1)
    qs_t = jnp.concatenate([jnp.where(col // IDX_DIM == h, iq, 0.0) for h in range(IDX_HEADS)],
                           axis=0).T.astype(_BF)
    w = iwt_ref[...]

    nkp = nk + lax.rem(nk, 2)
    last_tile = nkp - 1
    nk_a = (qa * QB) // KT + 1
    nkp_a = nk_a + lax.rem(nk_a, 2)
    last_tile_a = nkp_a - 1

    def key_tile(ref, kt):
        return ref[pl.ds(pl.multiple_of(kt * KT, KT), KT), :]

    def tile_planes(src_ref):
        lg = src_ref[...]
        sc = jnp.zeros((KT, QB), F32)
        for h in range(IDX_HEADS):
            sc = sc + jnp.maximum(lg[:, h * QB:(h + 1) * QB], 0.0) * w[h:h + 1, :]
        sc = jnp.where(sc == 0.0, 0.0, sc)
        ukey = _to_ukey(sc)
        words = []
        for g in range(KT // GROUP):
            words += _transpose32([ukey[g * GROUP + m * SUBLANES:g * GROUP + (m + 1) * SUBLANES, :]
                                   for m in range(32)])
        return lax.bitcast_convert_type(jnp.concatenate(words, axis=0), jnp.int32)

    def score_pair(i, c):
        ik1 = key_tile(ik_ref, 2 * i + 1)
        ik2 = key_tile(ik_ref, jnp.minimum(2 * i + 2, last_tile))
        lg1_ref[...] = _dot(ik1, qs_t)
        planes0 = tile_planes(lg0_ref)
        lg0_ref[...] = _dot(ik2, qs_t)
        planes1 = tile_planes(lg1_ref)
        planes_ref[pl.ds(pl.multiple_of(i * (2 * KT), 2 * KT), 2 * KT), :] = jnp.concatenate(
            [planes0, planes1], axis=0)
        return c

    @pl.when(has_sel)
    def _():
        lg0_ref[...] = _dot(key_tile(ik_ref, 0), qs_t)

    npairs = jnp.where(has_sel, nkp // 2, 0)
    npairs_a = jnp.where(has_att, nkp_a // 2, 0)

    def pair_loops(pair_fn, carry, first, end):
        start = first
        for u in PAIR_UNROLL:
            def body(j, c, u=u, start=start):
                for k in range(u):
                    c = pair_fn(start + j * u + k, c)
                return c
            n = (end - start) // u
            carry = lax.fori_loop(0, n, body, carry)
            start = start + n * u
        return carry

    q = dq_ref[...]
    half = (col // DSA_HEAD_DIM)
    qst = jnp.concatenate(
        [jnp.where(half == (h % 2), q[:, (h // 2) * LANES:(h // 2 + 1) * LANES], jnp.zeros((), q.dtype))
         for h in range(DSA_HEADS)], axis=0)
    unit = jnp.where(lax.broadcasted_iota(jnp.int32, (QB, QB), 0) == lax.broadcasted_iota(jnp.int32, (QB, QB), 1),
                     1.0, 0.0).astype(q.dtype)
    qst = jnp.concatenate([qst, jnp.concatenate([unit] * DSA_HEADS, axis=0)], axis=1)
    qst_t = qst.astype(F32).T.astype(q.dtype)
    npair = DSA_HEADS // 2
    hq = DSA_HEADS * QB
    acc_ref[...] = jnp.zeros_like(acc_ref)
    p1_ref[...] = jnp.zeros_like(p1_ref)

    def qk(k_tile, b_tile, dst_ref):
        dst_ref[...] = _dot(jnp.concatenate([k_tile, b_tile], axis=1), qst_t)

    def numer(src_ref, dst_ref, m):
        m_new = []
        for hd in range(DSA_HEADS):
            cols = slice(hd * QB, (hd + 1) * QB)
            mh = jnp.maximum(m[:, cols], jnp.max(src_ref[:, cols], axis=0, keepdims=True))
            dst_ref[:, cols] = jnp.exp2(src_ref[:, cols] - mh).astype(dst_ref.dtype)
            m_new.append(mh)
        m_new = jnp.concatenate(m_new, axis=1)
        return m_new, jnp.exp2(m - m_new)

    def v_tile(kt):
        return dvt_ref[0:V_ROWS, pl.ds(pl.multiple_of(kt * KT, KT), KT)]

    def pv(vt, src_ref, alpha):
        acc_ref[...] = alpha * acc_ref[...] + _dot(vt, src_ref[...])

    def att_pair(i, c):
        m, alpha = c
        nxt = jnp.minimum(2 * i + 2, last_tile_a)
        k1, b1 = key_tile(dk_ref, 2 * i + 1), key_tile(bias_ref, 2 * i + 1)
        k2, b2 = key_tile(dk_ref, nxt), key_tile(bias_ref, nxt)
        v_prev, v_cur = v_tile(jnp.maximum(2 * i - 1, 0)), v_tile(2 * i)
        qk(k1, b1, s1_ref)
        pv(v_prev, p1_ref, alpha)
        m, alpha = numer(s0_ref, p0_ref, m)
        qk(k2, b2, s0_ref)
        pv(v_cur, p0_ref, alpha)
        m, alpha = numer(s1_ref, p1_ref, m)
        return m, alpha

    @pl.when(has_att)
    def _():
        qk(key_tile(dk_ref, 0), key_tile(bias_ref, 0), s0_ref)

    c0 = (jnp.full((1, hq), -jnp.inf, F32), jnp.ones((1, hq), F32))

    def fused_pair(i, c):
        c = att_pair(i, c)
        score_pair(i, 0)
        return c

    n_fused = jnp.minimum(npairs, npairs_a)
    c1 = pair_loops(fused_pair, c0, 0, n_fused)
    pair_loops(score_pair, 0, n_fused, npairs)
    _, alpha = pair_loops(att_pair, c1, n_fused, npairs_a)

    @pl.when(has_att)
    def _():
        pv(v_tile(last_tile_a), p1_ref, alpha)
        acc = acc_ref[...]
        o = acc[0:DSA_HEAD_DIM, :] / acc[DSA_HEAD_DIM:DSA_HEAD_DIM + 1, :]
        for j in range(npair):
            o_ref[:, j * LANES:(j + 1) * LANES] = jnp.concatenate(
                [o[:, (2 * j) * QB:(2 * j + 1) * QB], o[:, (2 * j + 1) * QB:(2 * j + 2) * QB]], axis=0).T

    ngrp = nkp * (KT // GROUP)
    sub = lax.broadcasted_iota(jnp.int32, (SUBLANES, QB), 0)
    zero8 = jnp.zeros((SUBLANES, QB), jnp.int32)

    def rows_below(limit, g):
        nm = lax.shift_right_arithmetic(limit - g * GROUP + (SUBLANES - 1), 3)
        top = lax.shift_right_arithmetic(jnp.full((SUBLANES, QB), INT_MIN, jnp.int32),
                                         jnp.clip(nm, 1, 32) - 1)
        return jnp.where(nm <= 0, 0, top)

    def word(ref, g):
        return ref[pl.ds(pl.multiple_of(g * SUBLANES, SUBLANES), SUBLANES), :]

    def put_word(ref, g, v):
        ref[pl.ds(pl.multiple_of(g * SUBLANES, SUBLANES), SUBLANES), :] = v

    def plane(g, i):
        return planes_ref[pl.ds(pl.multiple_of(g * GROUP + i * SUBLANES, SUBLANES), SUBLANES), :]

    def sweep_words(step):
        accs, start = (zero8,) * GROUP_UNROLL[-1], 0
        for u in GROUP_UNROLL:
            def body(s, accs, u=u, start=start):
                accs = list(accs)
                for t in range(u):
                    accs[t % len(accs)] = accs[t % len(accs)] + step(start + s * u + t)
                return tuple(accs)
            n = (ngrp - start) // u
            accs = lax.fori_loop(0, n, body, accs)
            start = start + n * u
        return functools.reduce(lambda a, b: a + b, accs)

    def sweep(step):
        return sweep_words(step).sum(axis=0, keepdims=True)

    def init_words(g):
        put_word(cand_ref, g, rows_below(vis_end - sub, g))
        put_word(pick_ref, g, zero8)
        return zero8

    sweep(init_words)

    def count_ones(i):
        return sweep(lambda g: lax.population_count(word(cand_ref, g) & plane(g, i)))

    def decide(c1, rem):
        take = c1 >= rem
        return jnp.where(take, 1, 0), jnp.where(take, rem, rem - c1)

    def narrow(g, i, take):
        take8 = jnp.broadcast_to(take, (SUBLANES, QB)) != 0
        c = word(cand_ref, g)
        ones = c & plane(g, i)
        put_word(pick_ref, g, word(pick_ref, g) | jnp.where(take8, 0, ones))
        c = jnp.where(take8, ones, c ^ ones)
        put_word(cand_ref, g, c)
        return c

    def radix_step(i, carry):
        take8, rem = carry

        def step(g):
            return lax.population_count(narrow(g, i - 1, take8) & plane(g, i))
        return decide(sweep(step), rem)

    take8, rem = decide(count_ones(0), jnp.full((1, QB), topk, jnp.int32))
    take8, rem = lax.fori_loop(1, 32, radix_step, (take8, rem))

    def last_narrow(g):
        narrow(g, 31, take8)
        return zero8

    sweep(last_narrow)

    def group_step(i, carry):
        grp, before = carry
        trial = grp + jnp.left_shift(jnp.int32(1), ((seq // GROUP).bit_length() - 2) - i)
        below = sweep(lambda g: jnp.where(jnp.broadcast_to(g < trial, (SUBLANES, QB)),
                                          lax.population_count(word(cand_ref, g)), 0))
        ok = below < rem
        return jnp.where(ok, trial, grp), jnp.where(ok, below, before)

    zrow = jnp.zeros((1, QB), jnp.int32)
    grp, before = lax.fori_loop(0, (seq // GROUP).bit_length() - 1, group_step, (zrow, zrow))
    tied = sweep_words(lambda g: jnp.where(jnp.broadcast_to(g == grp, (SUBLANES, QB)), word(cand_ref, g), 0))

    def row_step(i, edge):
        trial = edge + jnp.left_shift(jnp.int32(1), (GROUP.bit_length() - 2) - i)
        below = lax.population_count(tied & rows_below(trial - sub, 0)).sum(axis=0, keepdims=True)
        return jnp.where(before + below < rem, trial, edge)

    edge = lax.fori_loop(0, GROUP.bit_length() - 1, row_step, zrow)
    keep_limit = grp * GROUP + edge + 1 - sub

    def bias_group(g):
        chosen = word(pick_ref, g) | (word(cand_ref, g) & rows_below(keep_limit, g))
        base = pl.multiple_of(g * GROUP, GROUP)
        bias_ref[pl.ds(base, GROUP), :] = jnp.concatenate(
            [jnp.where(lax.shift_left(chosen, jnp.full_like(chosen, m)) < 0, 0.0, NEG_INF_SCORE)
             for m in range(32)], axis=0).astype(bias_ref.dtype)
        return zero8

    sweep(bias_group)


def _dsa(iq, iwt, dq, ik, dk, dvt):
    bsz, seq, _ = iq.shape
    topk = min(DSA_TOPK, seq // 4)
    nq = seq // QB
    grid = (bsz, nq + 1)
    sel_row = lambda w: pl.BlockSpec((None, QB, w), lambda b, i: (b, jnp.minimum(i, nq - 1), 0))
    att_row = lambda w: pl.BlockSpec((None, QB, w), lambda b, i: (b, jnp.maximum(i - 1, 0), 0))
    full = lambda r, c: pl.BlockSpec((None, r, c), lambda b, i: (b, 0, 0))
    return pl.pallas_call(
        functools.partial(_dsa_kernel, seq=seq, topk=topk), grid=grid,
        in_specs=[sel_row(LANES), pl.BlockSpec((None, SUBLANES, QB), lambda b, i: (b, 0, jnp.minimum(i, nq - 1))),
                  att_row(DSA_WIDTH),
                  full(seq, LANES), full(seq, LANES), full(LANES, seq)],
        out_specs=att_row(DSA_WIDTH),
        out_shape=jax.ShapeDtypeStruct((bsz, seq, DSA_WIDTH), F32),
        scratch_shapes=[pltpu.VMEM((seq, QB), jnp.int32),
                        pltpu.VMEM((seq // GROUP * SUBLANES, QB), jnp.int32),
                        pltpu.VMEM((seq // GROUP * SUBLANES, QB), jnp.int32),
                        pltpu.VMEM((seq, QB), _BF),
                        pltpu.VMEM((V_ROWS, DSA_HEADS * QB), F32),
                        pltpu.VMEM((KT, IDX_HEADS * QB), F32), pltpu.VMEM((KT, IDX_HEADS * QB), F32),
                        pltpu.VMEM((KT, DSA_HEADS * QB), F32), pltpu.VMEM((KT, DSA_HEADS * QB), F32),
                        pltpu.VMEM((KT, DSA_HEADS * QB), _BF), pltpu.VMEM((KT, DSA_HEADS * QB), _BF)],
        compiler_params=_params(("parallel", "arbitrary")), name="dsa",
    )(iq, iwt, dq, ik, dk, dvt)


def _ret_kernel(rq_ref, rk_ref, rv_ref, rg_ref, qdec_ref, kdec_ref, dmask_ref, cdec_ref, blk_ref,
                hmq_ref, hmv_ref, ones_ref, ng_ref, o_ref, s_ref):
    @pl.when(pl.program_id(0) == 0)
    def _():
        s_ref[...] = jnp.zeros_like(s_ref)

    ones = ones_ref[...]

    def head_mean(z):
        return _dot(z.astype(_BF), ones) * (1.0 / RET_V_DIM)

    nseq = rq_ref.shape[0]
    ys = []
    for b in range(nseq):
        q, k, v = rq_ref[b], rk_ref[b], rv_ref[b]
        state = s_ref[b]
        y = _dot((q * qdec_ref[...]).astype(_BF), state.astype(_BF))
        kd = (k * kdec_ref[...]).T.astype(_BF)
        vb = v.astype(_BF)
        s_ref[b] = state * cdec_ref[...] + _dot(kd, vb) * blk_ref[...]
        kb = k.astype(_BF)
        a = jnp.concatenate([(_dot_nt((q * hmq_ref[h:h + 1, :]).astype(_BF), kb) * dmask_ref[h]).astype(_BF)
                             for h in range(RET_HEADS)], axis=1)
        vh = jnp.concatenate([(v * hmv_ref[h:h + 1, :]).astype(_BF) for h in range(RET_HEADS)], axis=0)
        ys.append(y + _dot(a, vh))
    y = jnp.concatenate(ys, axis=0)
    d = y - head_mean(y)
    yn = d * lax.rsqrt(head_mean(d * d) + EPS) * ng_ref[...]
    for b in range(nseq):
        o_ref[b] = jax.nn.silu(rg_ref[b]) * yn[b * RET_C:(b + 1) * RET_C, :]


def _ret_consts():
    c = RET_C
    log_g = jnp.log1p(-jnp.exp2(-5.0 - jnp.arange(RET_HEADS, dtype=F32)))
    pos = jnp.arange(c, dtype=F32)
    diff = pos[:, None] - pos[None, :]
    dmask = jnp.where(diff >= 0, jnp.exp(log_g[:, None, None] * jnp.maximum(diff, 0.0)), 0.0)
    lane_q = jnp.arange(RET_QK_PAD)
    head_q = jnp.where((lane_q % LANES) < RET_HEADS * (RET_QK_DIM // 2), (lane_q % LANES) // (RET_QK_DIM // 2), -1)
    head_v = jnp.arange(RET_WIDTH) // RET_V_DIM
    hmq = (head_q[None, :] == jnp.arange(RET_HEADS)[:, None]).astype(F32)
    hmv = (head_v[None, :] == jnp.arange(RET_HEADS)[:, None]).astype(F32)
    lg_q = hmq.T @ log_g
    qdec = jnp.exp(lg_q[None, :] * (pos[:, None] + 1.0))
    kdec = jnp.exp(lg_q[None, :] * (c - 1.0 - pos[:, None]))
    blk = hmq.T @ hmv
    cdec = blk * jnp.exp(lg_q * c)[:, None]
    ones = (hmv.T @ hmv).astype(_BF)
    return qdec, kdec, dmask, cdec, blk, hmq, hmv, ones


def _retention(rq, rk, rv, rg, norm_g):
    bsz, seq, _ = rq.shape
    consts = _ret_consts()
    grid = (seq // RET_C,)
    row = lambda w: pl.BlockSpec((bsz, RET_C, w), lambda i: (0, i, 0))
    ng = norm_g.reshape(1, RET_WIDTH)
    return pl.pallas_call(
        _ret_kernel, grid=grid,
        in_specs=[row(RET_QK_PAD), row(RET_QK_PAD), row(RET_WIDTH), row(RET_WIDTH)]
        + [_const_spec(a.shape) for a in consts] + [_const_spec(ng.shape)],
        out_specs=row(RET_WIDTH),
        out_shape=jax.ShapeDtypeStruct((bsz, seq, RET_WIDTH), F32),
        scratch_shapes=[pltpu.VMEM((bsz, RET_QK_PAD, RET_WIDTH), F32)],
        compiler_params=_params(("arbitrary",)), name="retention",
    )(rq, rk, rv, rg, *consts, ng)


def _merge_kernel(x_ref, g_ref, ya_ref, yb_ref, yc_ref, wg_ref, wa_ref, wb_ref, wc_ref, wo_ref, o_ref):
    x = x_ref[...]
    h = _rms(x, g_ref[...]).astype(_BF)
    merged = jnp.zeros(x.shape, F32)
    for n, (y_ref, w_ref) in enumerate(((ya_ref, wa_ref), (yb_ref, wb_ref), (yc_ref, wc_ref))):
        gate = jax.nn.sigmoid(_dot_nt(h, wg_ref[n * D_MODEL:(n + 1) * D_MODEL, :]))
        merged = merged + gate * _dot(y_ref[...].astype(_BF), w_ref[...])
    o_ref[...] = x + _dot(merged.astype(_BF), wo_ref[...])


def _merge(x, g, ya, yb, yc, wg, wa, wb, wc, wo):
    bsz, seq, _ = x.shape
    grid = (bsz, seq // TM_OUT)
    row = lambda w: pl.BlockSpec((None, TM_OUT, w), lambda b, i: (b, i, 0))
    return pl.pallas_call(
        _merge_kernel, grid=grid,
        in_specs=[row(D_MODEL), _const_spec(g.shape), row(SSM_WIDTH), row(DSA_WIDTH), row(RET_WIDTH)]
        + [_const_spec(w.shape) for w in (wg, wa, wb, wc, wo)],
        out_specs=row(D_MODEL),
        out_shape=jax.ShapeDtypeStruct(x.shape, F32),
        compiler_params=_params(("parallel", "parallel")), name="merge",
    )(x, g, ya, yb, yc, wg, wa, wb, wc, wo)


def _mlp_kernel(x_ref, g_ref, w1_ref, w2_ref, fg_ref, o_ref, *, final_norm):
    x = x_ref[...]
    h = _rms(x, g_ref[...]).astype(_BF)
    acc = x
    for f in range(D_FF // FF_CHUNK):
        t = jnp.maximum(_dot(h, w1_ref[:, f * FF_CHUNK:(f + 1) * FF_CHUNK]), 0.0)
        acc = acc + _dot((t * t).astype(_BF), w2_ref[f * FF_CHUNK:(f + 1) * FF_CHUNK, :])
    o_ref[...] = _rms(acc, fg_ref[...]) if final_norm else acc


def _mlp(x, g, w1, w2, fg, final_norm):
    bsz, seq, _ = x.shape
    grid = (bsz, seq // TM_OUT)
    row = pl.BlockSpec((None, TM_OUT, D_MODEL), lambda b, i: (b, i, 0))
    return pl.pallas_call(
        functools.partial(_mlp_kernel, final_norm=final_norm), grid=grid,
        in_specs=[row, _const_spec(g.shape), _const_spec(w1.shape), _const_spec(w2.shape), _const_spec(fg.shape)],
        out_specs=row,
        out_shape=jax.ShapeDtypeStruct(x.shape, F32),
        compiler_params=_params(("parallel", "parallel")), name="mlp",
    )(x, g, w1, w2, fg)


def _rope_tables(positions):
    pos = positions.astype(F32)[..., None]
    lane = jnp.arange(LANES)

    def angles(rot_dim, theta):
        half = rot_dim // 2
        inv = jnp.exp(-math.log(theta) * jnp.arange(half, dtype=F32) * (2.0 / rot_dim))
        return pos * inv

    def roll_tables(head_dim, rot_dim, theta):
        half = rot_dim // 2
        ang = angles(rot_dim, theta)
        ln = lane % head_dim
        rot = ln < rot_dim
        cos = jnp.where(rot, jnp.take(jnp.cos(ang), ln % half, axis=-1), 1.0)
        sin = jnp.where(rot, jnp.take(jnp.sin(ang), ln % half, axis=-1), 0.0)
        sgn = jnp.stack([jnp.where(ln < half, -1.0, 0.0),
                         jnp.where((ln >= half) & rot, 1.0, 0.0)]).astype(F32)
        return cos, sin, sgn

    cd, sd, sgd = roll_tables(DSA_HEAD_DIM, DSA_HEAD_DIM // 4, ROPE_THETA)
    ci, si, sgi = roll_tables(IDX_DIM, IDX_DIM // 4, ROPE_THETA)
    half = RET_QK_DIM // 2
    ang = angles(RET_QK_DIM, RET_THETA)
    used = lane < RET_HEADS * half
    cr = jnp.where(used, jnp.take(jnp.cos(ang), lane % half, axis=-1), 1.0)
    sr = jnp.where(used, jnp.take(jnp.sin(ang), lane % half, axis=-1), 0.0)
    return cd, sd, ci, si, cr, sr, sgd, sgi


def _inproj_weights(wt):
    sizes = (SSM_WIDTH, DSA_WIDTH, DSA_HEAD_DIM, DSA_HEAD_DIM, IDX_HEADS * IDX_DIM, IDX_DIM, IDX_HEADS,
             RET_HEADS * RET_QK_DIM, RET_HEADS * RET_QK_DIM, RET_WIDTH, RET_WIDTH, 3 * D_MODEL)
    parts, off = [], 0
    for n in sizes:
        parts.append(wt[off:off + n, :])
        off += n
    wu, wdq, wdk, wdv, wiq, wik, wiw, wrq, wrk, wrv, wrg, wgt = parts
    half = RET_QK_DIM // 2
    zrows = lambda n: jnp.zeros((n, D_MODEL), wt.dtype)

    def ret_split(a):
        a = a.reshape(RET_HEADS, 2, half, D_MODEL)
        pad = zrows(LANES - RET_HEADS * half)
        return jnp.concatenate([a[:, 0].reshape(-1, D_MODEL), pad, a[:, 1].reshape(-1, D_MODEL), pad], axis=0)

    ws = (wu, wdq, jnp.concatenate([wdk, wdk], axis=0), jnp.concatenate([wdv, zrows(LANES - DSA_HEAD_DIM)], axis=0),
          wiq, jnp.concatenate([wik] * IDX_HEADS, axis=0), jnp.concatenate([wiw, zrows(SUBLANES - IDX_HEADS)], axis=0),
          ret_split(wrq), ret_split(wrk), wrv, wrg)
    return tuple(a.astype(_BF) for a in ws), wgt.astype(_BF)


def kernel(x, positions, norm1_g, w_in, ssm_lambda_re, ssm_lambda_im, ssm_log_step, ssm_b_re, ssm_b_im,
           ssm_c_re, ssm_c_im, ssm_d, ssm_glu_w, ssm_glu_b, ret_norm_g, w_proj_a, w_proj_b, w_proj_c,
           w_out, norm2_g, w_ff1, w_ff2, final_norm_g):
    depth = w_in.shape[0]
    tabs = _rope_tables(positions)
    fg = final_norm_g.reshape(1, D_MODEL)
    w_in_t = jnp.transpose(w_in, (2, 0, 1))
    for l in range(depth):
        g1 = norm1_g[l].reshape(1, D_MODEL)
        ws, wgt = _inproj_weights(w_in_t[:, l, :])
        u, dq, dk, dvt, iq, ik, iwt, rq, rk, rv, rg = _inproj(x, g1, tabs, ws)

        s5p = _s5_params(ssm_lambda_re[l], ssm_lambda_im[l], ssm_log_step[l], ssm_b_re[l], ssm_b_im[l],
                         ssm_c_re[l], ssm_c_im[l], ssm_d[l], ssm_glu_w[l], ssm_glu_b[l])
        ya = _s5(u, s5p)
        yb = _dsa(iq, iwt, dq, ik, dk, dvt)
        yc = _retention(rq, rk, rv, rg, ret_norm_g[l])

        x = _merge(x, g1, ya, yb, yc, wgt, w_proj_a[l].astype(_BF), w_proj_b[l].astype(_BF),
                   w_proj_c[l].astype(_BF), w_out[l].astype(_BF))
        x = _mlp(x, norm2_g[l].reshape(1, D_MODEL), w_ff1[l].astype(_BF), w_ff2[l].astype(_BF), fg,
                 final_norm=(l == depth - 1))
    return x
```

```python
import functools
import math

import jax
import jax.numpy as jnp
from jax import lax
from jax.experimental import pallas as pl
from jax.experimental.pallas import tpu as pltpu

F32 = jnp.float32
_BF = jnp.bfloat16
LANES = 128
SUBLANES = 8
VMEM_LIMIT = 56 * 1024 * 1024

D_MODEL = 1024
CHUNK = 64
EPS = 1e-6
NEG_INF_SCORE = -1e30
INT_MIN = -2 ** 31
GROUP = 32 * SUBLANES
GROUP_UNROLL = (16, 8, 4)
PAIR_UNROLL = (4, 2, 1)
V_ROWS = 80

SSM_WIDTH = 256
SSM_GROUP = 16
SSM_GROUPS = 16
SSM_STATE = 64
NSTATE = SSM_GROUPS * SSM_STATE

DSA_HEADS = 6
DSA_HEAD_DIM = 64
DSA_WIDTH = 384
IDX_HEADS = 4
IDX_DIM = 32
DSA_TOPK = 256
ROPE_THETA = 500000.0

RET_HEADS = 4
RET_QK_DIM = 48
RET_V_DIM = 96
RET_WIDTH = 384
RET_THETA = 10000.0
RET_QK_PAD = 256

D_FF = 4096

TM_IN = 512
S5_TILE = 512
S5_SEG = S5_TILE // SUBLANES
QB = 128
KT = 512
RET_C = 128
TM_OUT = 512
FF_CHUNK = 1024


def _rms(x, g):
    return x * lax.rsqrt(jnp.mean(x * x, axis=-1, keepdims=True) + EPS) * g


def _dot(a, b):
    return jnp.dot(a, b, preferred_element_type=F32)


def _dot_nt(a, b):
    return lax.dot_general(a, b, (((1,), (1,)), ((), ())), preferred_element_type=F32)


def _to_ukey(v):
    bits = lax.bitcast_convert_type(v, jnp.int32)
    return lax.bitcast_convert_type(bits ^ ((bits >> 31) | INT_MIN), jnp.uint32)


def _transpose32(a):
    a = list(a)
    j, m = 16, 0x0000FFFF
    while j:
        k = 0
        while k < 32:
            t = (a[k] ^ (a[k + j] >> j)) & m
            a[k] = a[k] ^ t
            a[k + j] = a[k + j] ^ (t << j)
            k = (k + j + 1) & ~j
        j >>= 1
        m = (m ^ (m << j)) & 0xFFFFFFFF
    return a


def _const_spec(shape):
    nd = len(shape)
    return pl.BlockSpec(shape, lambda *_: (0,) * nd)


def _params(sem):
    return pltpu.CompilerParams(dimension_semantics=sem, vmem_limit_bytes=VMEM_LIMIT)


def _rope_roll(z, cos, sin, sgn, shift):
    outs = []
    for c in range(z.shape[1] // LANES):
        zc = z[:, c * LANES:(c + 1) * LANES]
        rot = (pltpu.roll(zc, LANES - shift, 1) * sgn[0:1, :]
               + pltpu.roll(zc, shift, 1) * sgn[1:2, :])
        outs.append(zc * cos + rot * sin)
    return outs[0] if len(outs) == 1 else jnp.concatenate(outs, axis=1)


def _inproj_kernel(x_ref, g_ref, cd_ref, sd_ref, ci_ref, si_ref, cr_ref, sr_ref, sgd_ref, sgi_ref,
                   wu_ref, wdq_ref, wdk_ref, wdvt_ref, wiq_ref, wik_ref, wiwt_ref,
                   wrq_ref, wrk_ref, wrv_ref, wrg_ref,
                   u_ref, dq_ref, dk_ref, dvt_ref, iq_ref, ik_ref, iwt_ref,
                   rq_ref, rk_ref, rv_ref, rg_ref):
    h = _rms(x_ref[...], g_ref[...]).astype(_BF)

    def proj(w_ref):
        return _dot_nt(h, w_ref[...])

    u_ref[...] = proj(wu_ref)

    cd, sd, sgd = cd_ref[...], sd_ref[...], sgd_ref[...]
    dq = _rope_roll(proj(wdq_ref), cd, sd, sgd, 8)
    dq_ref[...] = (dq * (DSA_HEAD_DIM ** -0.5 * math.log2(math.e))).astype(dq_ref.dtype)
    dk_ref[...] = _rope_roll(proj(wdk_ref), cd, sd, sgd, 8).astype(dk_ref.dtype)
    vt = _dot_nt(wdvt_ref[...], h)
    ones_row = lax.broadcasted_iota(jnp.int32, vt.shape, 0) == DSA_HEAD_DIM
    dvt_ref[...] = jnp.where(ones_row, 1.0, vt).astype(dvt_ref.dtype)

    ci, si, sgi = ci_ref[...], si_ref[...], sgi_ref[...]
    iq_ref[...] = _rope_roll(proj(wiq_ref), ci, si, sgi, 4) * (IDX_DIM ** -0.5)
    ik_ref[...] = _rope_roll(proj(wik_ref), ci, si, sgi, 4).astype(ik_ref.dtype)
    iwt_ref[...] = _dot_nt(wiwt_ref[...], h) * (IDX_HEADS ** -0.5)

    cr, sr = cr_ref[...], sr_ref[...]

    def rope_split(z):
        x1, x2 = z[:, :LANES], z[:, LANES:]
        return jnp.concatenate([x1 * cr - x2 * sr, x2 * cr + x1 * sr], axis=1)

    rq_ref[...] = rope_split(proj(wrq_ref))
    rk_ref[...] = rope_split(proj(wrk_ref)) * (RET_QK_DIM ** -0.5)
    rv_ref[...] = proj(wrv_ref)
    rg_ref[...] = proj(wrg_ref)


def _inproj(x, g, tabs, ws):
    bsz, seq, _ = x.shape
    tm = TM_IN
    grid = (bsz, seq // tm)
    row = lambda w: pl.BlockSpec((None, tm, w), lambda b, i: (b, i, 0))
    colT = lambda r: pl.BlockSpec((None, r, tm), lambda b, i: (b, 0, i))
    in_specs = ([row(D_MODEL), _const_spec(g.shape)] + [row(LANES)] * 6
                + [_const_spec(t.shape) for t in tabs[6:]] + [_const_spec(w.shape) for w in ws])
    out_shape = [
        jax.ShapeDtypeStruct((bsz, seq, SSM_WIDTH), F32),
        jax.ShapeDtypeStruct((bsz, seq, DSA_WIDTH), _BF),
        jax.ShapeDtypeStruct((bsz, seq, LANES), _BF),
        jax.ShapeDtypeStruct((bsz, LANES, seq), _BF),
        jax.ShapeDtypeStruct((bsz, seq, LANES), F32),
        jax.ShapeDtypeStruct((bsz, seq, LANES), _BF),
        jax.ShapeDtypeStruct((bsz, SUBLANES, seq), F32),
        jax.ShapeDtypeStruct((bsz, seq, RET_QK_PAD), F32),
        jax.ShapeDtypeStruct((bsz, seq, RET_QK_PAD), F32),
        jax.ShapeDtypeStruct((bsz, seq, RET_WIDTH), F32),
        jax.ShapeDtypeStruct((bsz, seq, RET_WIDTH), F32),
    ]
    out_specs = [row(SSM_WIDTH), row(DSA_WIDTH), row(LANES), colT(LANES), row(LANES), row(LANES),
                 colT(SUBLANES), row(RET_QK_PAD), row(RET_QK_PAD), row(RET_WIDTH), row(RET_WIDTH)]
    return pl.pallas_call(
        _inproj_kernel, grid=grid, in_specs=in_specs, out_specs=out_specs, out_shape=out_shape,
        compiler_params=_params(("parallel", "parallel")), name="inproj",
    )(x, g, *tabs, *ws)


def _s5_kernel(u_ref, bbd_ref, cbd_ref, lam_ref, lam64_ref, pow_ref, d_ref, gw_ref, gb_ref,
               o_ref, st_ref, xm_ref, carry_ref, up_ref, yp_ref):
    n = NSTATE

    @pl.when(pl.program_id(1) == 0)
    def _():
        carry_ref[...] = jnp.zeros_like(carry_ref)

    nhalf = SSM_WIDTH // LANES
    for i in range(SUBLANES):
        for c in range(nhalf):
            up_ref[c, pl.ds(i, S5_SEG, stride=SUBLANES), :] = u_ref[i * S5_SEG:(i + 1) * S5_SEG,
                                                                    c * LANES:(c + 1) * LANES]
    u = jnp.concatenate([up_ref[c] for c in range(nhalf)], axis=1)
    st_ref[...] = _dot(u.astype(_BF), bbd_ref[...])

    lr = jnp.broadcast_to(lam_ref[:, :n], (SUBLANES, n))
    li = jnp.broadcast_to(lam_ref[:, n:], (SUBLANES, n))

    def step(j, c):
        xr, xi = c
        off = pl.multiple_of(j * SUBLANES, SUBLANES)
        ar = st_ref[pl.ds(off, SUBLANES), :n]
        ai = st_ref[pl.ds(off, SUBLANES), n:]
        nr = lr * xr - li * xi + ar
        ni = lr * xi + li * xr + ai
        st_ref[pl.ds(off, SUBLANES), :n] = nr
        st_ref[pl.ds(off, SUBLANES), n:] = ni
        return nr, ni

    zero = jnp.zeros((SUBLANES, n), F32)
    xr, xi = lax.fori_loop(0, S5_SEG, step, (zero, zero), unroll=4)

    l64r, l64i = lam64_ref[:, :n], lam64_ref[:, n:]
    cr, ci = carry_ref[:, :n], carry_ref[:, n:]
    for i in range(SUBLANES):
        xm_ref[i:i + 1, :n] = cr
        xm_ref[i:i + 1, n:] = ci
        er, ei = xr[i:i + 1, :], xi[i:i + 1, :]
        cr, ci = er + l64r * cr - l64i * ci, ei + l64r * ci + l64i * cr
    carry_ref[:, :n] = cr
    carry_ref[:, n:] = ci

    hr, hi = xm_ref[:, :n], xm_ref[:, n:]

    def fix(j, _):
        off = pl.multiple_of(j * SUBLANES, SUBLANES)
        pr = pow_ref[pl.ds(j, 1), :n]
        pi = pow_ref[pl.ds(j, 1), n:]
        st_ref[pl.ds(off, SUBLANES), :n] = st_ref[pl.ds(off, SUBLANES), :n] + pr * hr - pi * hi
        st_ref[pl.ds(off, SUBLANES), n:] = st_ref[pl.ds(off, SUBLANES), n:] + pr * hi + pi * hr
        return 0

    lax.fori_loop(0, S5_SEG, fix, 0, unroll=4)

    y = d_ref[...] * u
    kc = 512
    for k in range(2 * n // kc):
        y = y + _dot(st_ref[:, k * kc:(k + 1) * kc].astype(_BF), cbd_ref[k * kc:(k + 1) * kc, :])
    y = jax.nn.gelu(y)
    y = y * jax.nn.sigmoid(_dot(y.astype(_BF), gw_ref[...]) + gb_ref[...])
    for c in range(nhalf):
        yp_ref[c] = y[:, c * LANES:(c + 1) * LANES]
    for i in range(SUBLANES):
        for c in range(nhalf):
            o_ref[i * S5_SEG:(i + 1) * S5_SEG, c * LANES:(c + 1) * LANES] = yp_ref[
                c, pl.ds(i, S5_SEG, stride=SUBLANES), :]


def _s5(u_perm, prm):
    bsz, seq, _ = u_perm.shape
    grid = (bsz, seq // S5_TILE)
    row = pl.BlockSpec((None, S5_TILE, SSM_WIDTH), lambda b, i: (b, i, 0))
    return pl.pallas_call(
        _s5_kernel, grid=grid,
        in_specs=[row] + [_const_spec(p.shape) for p in prm],
        out_specs=row,
        out_shape=jax.ShapeDtypeStruct((bsz, seq, SSM_WIDTH), F32),
        scratch_shapes=[pltpu.VMEM((S5_TILE, 2 * NSTATE), F32),
                        pltpu.VMEM((SUBLANES, 2 * NSTATE), F32),
                        pltpu.VMEM((1, 2 * NSTATE), F32),
                        pltpu.VMEM((SSM_WIDTH // LANES, S5_TILE, LANES), F32),
                        pltpu.VMEM((SSM_WIDTH // LANES, S5_TILE, LANES), F32)],
        compiler_params=_params(("arbitrary", "arbitrary")), name="s5",
    )(u_perm, *prm)


def _s5_params(lam_re, lam_im, log_step, b_re, b_im, c_re, c_im, d_skip, glu_w, glu_b):
    step = jnp.exp(log_step.astype(F32))[:, None]
    ere, eim = lam_re * step, lam_im * step
    mag = jnp.exp(ere)
    lb_re, lb_im = mag * jnp.cos(eim), mag * jnp.sin(eim)
    den = lam_re * lam_re + lam_im * lam_im
    f_re = ((lb_re - 1.0) * lam_re + lb_im * lam_im) / den
    f_im = (lb_im * lam_re - (lb_re - 1.0) * lam_im) / den
    bb_re = f_re[..., None] * b_re - f_im[..., None] * b_im
    bb_im = f_re[..., None] * b_im + f_im[..., None] * b_re
    eye = jnp.eye(SSM_GROUPS, dtype=F32)
    bbd = jnp.concatenate(
        [jnp.einsum('gpc,gh->gchp', bb, eye).reshape(SSM_WIDTH, NSTATE) for bb in (bb_re, bb_im)], axis=1)
    cbd = jnp.concatenate(
        [jnp.einsum('gcp,gh->hpgc', cc, eye).reshape(NSTATE, SSM_WIDTH) for cc in (c_re, -c_im)], axis=0)
    flat = lambda a: a.reshape(1, NSTATE)
    lam = jnp.concatenate([flat(lb_re), flat(lb_im)], axis=1)

    def power(k):
        k = jnp.asarray(k, F32).reshape(-1, 1)
        m = jnp.exp(k * flat(ere))
        return jnp.concatenate([m * jnp.cos(k * flat(eim)), m * jnp.sin(k * flat(eim))], axis=1)

    return (bbd.astype(_BF), cbd.astype(_BF), lam, power(float(S5_SEG)),
            power(jnp.arange(1, S5_SEG + 1)), d_skip.reshape(1, SSM_WIDTH),
            glu_w.astype(_BF), glu_b.reshape(1, SSM_WIDTH))


def _dsa_kernel(iq_ref, iwt_ref, dq_ref, ik_ref, dk_ref, dvt_ref, o_ref,
                planes_ref, cand_ref, pick_ref, bias_ref, acc_ref,
                lg0_ref, lg1_ref, s0_ref, s1_ref, p0_ref, p1_ref, *, seq, topk):
    t = pl.program_id(1)
    nq = seq // QB
    has_sel = t < nq
    has_att = t >= 1
    qi = jnp.minimum(t, nq - 1)
    qa = jnp.maximum(t - 1, 0)
    nk = (qi * QB) // KT + 1
    lane = lax.broadcasted_iota(jnp.int32, (1, QB), 1)
    vis_end = qi * QB + jnp.where(lane < CHUNK, CHUNK, 2 * CHUNK)

    iq = iq_ref[...]
    col = lax.broadcasted_iota(jnp.int32, (QB, LANES), 1)
    qs_t = jnp.concatenate([jnp.where(col // IDX_DIM == h, iq, 0.0) for h in range(IDX_HEADS)],
                           axis=0).T.astype(_BF)
    w = iwt_ref[...]

    nkp = nk + lax.rem(nk, 2)
    last_tile = nkp - 1
    nk_a = (qa * QB) // KT + 1
    nkp_a = nk_a + lax.rem(nk_a, 2)
    last_tile_a = nkp_a - 1

    def key_tile(ref, kt):
        return ref[pl.ds(pl.multiple_of(kt * KT, KT), KT), :]

    def tile_planes(src_ref):
        lg = src_ref[...]
        sc = jnp.zeros((KT, QB), F32)
        for h in range(IDX_HEADS):
            sc = sc + jnp.maximum(lg[:, h * QB:(h + 1) * QB], 0.0) * w[h:h + 1, :]
        sc = jnp.where(sc == 0.0, 0.0, sc)
        ukey = _to_ukey(sc)
        words = []
        for g in range(KT // GROUP):
            words += _transpose32([ukey[g * GROUP + m * SUBLANES:g * GROUP + (m + 1) * SUBLANES, :]
                                   for m in range(32)])
        return lax.bitcast_convert_type(jnp.concatenate(words, axis=0), jnp.int32)

    def score_pair(i, c):
        ik1 = key_tile(ik_ref, 2 * i + 1)
        ik2 = key_tile(ik_ref, jnp.minimum(2 * i + 2, last_tile))
        lg1_ref[...] = _dot(ik1, qs_t)
        planes0 = tile_planes(lg0_ref)
        lg0_ref[...] = _dot(ik2, qs_t)
        planes1 = tile_planes(lg1_ref)
        planes_ref[pl.ds(pl.multiple_of(i * (2 * KT), 2 * KT), 2 * KT), :] = jnp.concatenate(
            [planes0, planes1], axis=0)
        return c

    @pl.when(has_sel)
    def _():
        lg0_ref[...] = _dot(key_tile(ik_ref, 0), qs_t)

    npairs = jnp.where(has_sel, nkp // 2, 0)
    npairs_a = jnp.where(has_att, nkp_a // 2, 0)

    def pair_loops(pair_fn, carry, first, end):
        start = first
        for u in PAIR_UNROLL:
            def body(j, c, u=u, start=start):
                for k in range(u):
                    c = pair_fn(start + j * u + k, c)
                return c
            n = (end - start) // u
            carry = lax.fori_loop(0, n, body, carry)
            start = start + n * u
        return carry

    q = dq_ref[...]
    half = (col // DSA_HEAD_DIM)
    qst = jnp.concatenate(
        [jnp.where(half == (h % 2), q[:, (h // 2) * LANES:(h // 2 + 1) * LANES], jnp.zeros((), q.dtype))
         for h in range(DSA_HEADS)], axis=0)
    unit = jnp.where(lax.broadcasted_iota(jnp.int32, (QB, QB), 0) == lax.broadcasted_iota(jnp.int32, (QB, QB), 1),
                     1.0, 0.0).astype(q.dtype)
    qst = jnp.concatenate([qst, jnp.concatenate([unit] * DSA_HEADS, axis=0)], axis=1)
    qst_t = qst.astype(F32).T.astype(q.dtype)
    npair = DSA_HEADS // 2
    hq = DSA_HEADS * QB
    acc_ref[...] = jnp.zeros_like(acc_ref)
    p1_ref[...] = jnp.zeros_like(p1_ref)

    def qk(k_tile, b_tile, dst_ref):
        dst_ref[...] = _dot(jnp.concatenate([k_tile, b_tile], axis=1), qst_t)

    def numer(src_ref, dst_ref, m):
        m_new = []
        for hd in range(DSA_HEADS):
            cols = slice(hd * QB, (hd + 1) * QB)
            mh = jnp.maximum(m[:, cols], jnp.max(src_ref[:, cols], axis=0, keepdims=True))
            dst_ref[:, cols] = jnp.exp2(src_ref[:, cols] - mh).astype(dst_ref.dtype)
            m_new.append(mh)
        m_new = jnp.concatenate(m_new, axis=1)
        return m_new, jnp.exp2(m - m_new)

    def v_tile(kt):
        return dvt_ref[0:V_ROWS, pl.ds(pl.multiple_of(kt * KT, KT), KT)]

    def pv(vt, src_ref, alpha):
        acc_ref[...] = alpha * acc_ref[...] + _dot(vt, src_ref[...])

    def att_pair(i, c):
        m, alpha = c
        nxt = jnp.minimum(2 * i + 2, last_tile_a)
        k1, b1 = key_tile(dk_ref, 2 * i + 1), key_tile(bias_ref, 2 * i + 1)
        k2, b2 = key_tile(dk_ref, nxt), key_tile(bias_ref, nxt)
        v_prev, v_cur = v_tile(jnp.maximum(2 * i - 1, 0)), v_tile(2 * i)
        qk(k1, b1, s1_ref)
        pv(v_prev, p1_ref, alpha)
        m, alpha = numer(s0_ref, p0_ref, m)
        qk(k2, b2, s0_ref)
        pv(v_cur, p0_ref, alpha)
        m, alpha = numer(s1_ref, p1_ref, m)
        return m, alpha

    @pl.when(has_att)
    def _():
        qk(key_tile(dk_ref, 0), key_tile(bias_ref, 0), s0_ref)

    c0 = (jnp.full((1, hq), -jnp.inf, F32), jnp.ones((1, hq), F32))

    def fused_pair(i, c):
        c = att_pair(i, c)
        score_pair(i, 0)
        return c

    n_fused = jnp.minimum(npairs, npairs_a)
    c1 = pair_loops(fused_pair, c0, 0, n_fused)
    pair_loops(score_pair, 0, n_fused, npairs)
    _, alpha = pair_loops(att_pair, c1, n_fused, npairs_a)

    @pl.when(jnp.logical_not(has_att))
    def _():
        o_ref[...] = jnp.zeros_like(o_ref)

    @pl.when(has_att)
    def _():
        pv(v_tile(last_tile_a), p1_ref, alpha)
        acc = acc_ref[...]
        o = acc[0:DSA_HEAD_DIM, :] / acc[DSA_HEAD_DIM:DSA_HEAD_DIM + 1, :]
        for j in range(npair):
            o_ref[:, j * LANES:(j + 1) * LANES] = jnp.concatenate(
                [o[:, (2 * j) * QB:(2 * j + 1) * QB], o[:, (2 * j + 1) * QB:(2 * j + 2) * QB]], axis=0).T

    ngrp = nkp * (KT // GROUP)
    sub = lax.broadcasted_iota(jnp.int32, (SUBLANES, QB), 0)
    zero8 = jnp.zeros((SUBLANES, QB), jnp.int32)

    def rows_below(limit, g):
        nm = lax.shift_right_arithmetic(limit - g * GROUP + (SUBLANES - 1), 3)
        top = lax.shift_right_arithmetic(jnp.full((SUBLANES, QB), INT_MIN, jnp.int32),
                                         jnp.clip(nm, 1, 32) - 1)
        return jnp.where(nm <= 0, 0, top)

    def word(ref, g):
        return ref[pl.ds(pl.multiple_of(g * SUBLANES, SUBLANES), SUBLANES), :]

    def put_word(ref, g, v):
        ref[pl.ds(pl.multiple_of(g * SUBLANES, SUBLANES), SUBLANES), :] = v

    def plane(g, i):
        return planes_ref[pl.ds(pl.multiple_of(g * GROUP + i * SUBLANES, SUBLANES), SUBLANES), :]

    def sweep_words(step):
        accs, start = (zero8,) * GROUP_UNROLL[-1], 0
        for u in GROUP_UNROLL:
            def body(s, accs, u=u, start=start):
                accs = list(accs)
                for t in range(u):
                    accs[t % len(accs)] = accs[t % len(accs)] + step(start + s * u + t)
                return tuple(accs)
            n = (ngrp - start) // u
            accs = lax.fori_loop(0, n, body, accs)
            start = start + n * u
        return functools.reduce(lambda a, b: a + b, accs)

    def sweep(step):
        return sweep_words(step).sum(axis=0, keepdims=True)

    def init_words(g):
        put_word(cand_ref, g, rows_below(vis_end - sub, g))
        put_word(pick_ref, g, zero8)
        return zero8

    sweep(init_words)

    def count_ones(i):
        return sweep(lambda g: lax.population_count(word(cand_ref, g) & plane(g, i)))

    def decide(c1, rem):
        take = c1 >= rem
        return jnp.where(take, 1, 0), jnp.where(take, rem, rem - c1)

    def narrow(g, i, take):
        take8 = jnp.broadcast_to(take, (SUBLANES, QB)) != 0
        c = word(cand_ref, g)
        ones = c & plane(g, i)
        put_word(pick_ref, g, word(pick_ref, g) | jnp.where(take8, 0, ones))
        c = jnp.where(take8, ones, c ^ ones)
        put_word(cand_ref, g, c)
        return c

    def radix_step(i, carry):
        take8, rem = carry

        def step(g):
            return lax.population_count(narrow(g, i - 1, take8) & plane(g, i))
        return decide(sweep(step), rem)

    take8, rem = decide(count_ones(0), jnp.full((1, QB), topk, jnp.int32))
    take8, rem = lax.fori_loop(1, 32, radix_step, (take8, rem))

    def last_narrow(g):
        narrow(g, 31, take8)
        return zero8

    sweep(last_narrow)

    def group_step(i, carry):
        grp, before = carry
        trial = grp + jnp.left_shift(jnp.int32(1), ((seq // GROUP).bit_length() - 2) - i)
        below = sweep(lambda g: jnp.where(jnp.broadcast_to(g < trial, (SUBLANES, QB)),
                                          lax.population_count(word(cand_ref, g)), 0))
        ok = below < rem
        return jnp.where(ok, trial, grp), jnp.where(ok, below, before)

    zrow = jnp.zeros((1, QB), jnp.int32)
    grp, before = lax.fori_loop(0, (seq // GROUP).bit_length() - 1, group_step, (zrow, zrow))
    tied = sweep_words(lambda g: jnp.where(jnp.broadcast_to(g == grp, (SUBLANES, QB)), word(cand_ref, g), 0))

    def row_step(i, edge):
        trial = edge + jnp.left_shift(jnp.int32(1), (GROUP.bit_length() - 2) - i)
        below = lax.population_count(tied & rows_below(trial - sub, 0)).sum(axis=0, keepdims=True)
        return jnp.where(before + below < rem, trial, edge)

    edge = lax.fori_loop(0, GROUP.bit_length() - 1, row_step, zrow)
    keep_limit = grp * GROUP + edge + 1 - sub

    def bias_group(g):
        chosen = word(pick_ref, g) | (word(cand_ref, g) & rows_below(keep_limit, g))
        base = pl.multiple_of(g * GROUP, GROUP)
        bias_ref[pl.ds(base, GROUP), :] = jnp.concatenate(
            [jnp.where(lax.shift_left(chosen, jnp.full_like(chosen, m)) < 0, 0.0, NEG_INF_SCORE)
             for m in range(32)], axis=0).astype(bias_ref.dtype)
        return zero8

    sweep(bias_group)


def _dsa(iq, iwt, dq, ik, dk, dvt):
    bsz, seq, _ = iq.shape
    topk = min(DSA_TOPK, seq // 4)
    nq = seq // QB
    grid = (bsz, nq + 1)
    sel_row = lambda w: pl.BlockSpec((None, QB, w), lambda b, i: (b, jnp.minimum(i, nq - 1), 0))
    att_row = lambda w: pl.BlockSpec((None, QB, w), lambda b, i: (b, jnp.maximum(i - 1, 0), 0))
    full = lambda r, c: pl.BlockSpec((None, r, c), lambda b, i: (b, 0, 0))
    return pl.pallas_call(
        functools.partial(_dsa_kernel, seq=seq, topk=topk), grid=grid,
        in_specs=[sel_row(LANES), pl.BlockSpec((None, SUBLANES, QB), lambda b, i: (b, 0, jnp.minimum(i, nq - 1))),
                  att_row(DSA_WIDTH),
                  full(seq, LANES), full(seq, LANES), full(LANES, seq)],
        out_specs=att_row(DSA_WIDTH),
        out_shape=jax.ShapeDtypeStruct((bsz, seq, DSA_WIDTH), F32),
        scratch_shapes=[pltpu.VMEM((seq, QB), jnp.int32),
                        pltpu.VMEM((seq // GROUP * SUBLANES, QB), jnp.int32),
                        pltpu.VMEM((seq // GROUP * SUBLANES, QB), jnp.int32),
                        pltpu.VMEM((seq, QB), _BF),
                        pltpu.VMEM((V_ROWS, DSA_HEADS * QB), F32),
                        pltpu.VMEM((KT, IDX_HEADS * QB), F32), pltpu.VMEM((KT, IDX_HEADS * QB), F32),
                        pltpu.VMEM((KT, DSA_HEADS * QB), F32), pltpu.VMEM((KT, DSA_HEADS * QB), F32),
                        pltpu.VMEM((KT, DSA_HEADS * QB), _BF), pltpu.VMEM((KT, DSA_HEADS * QB), _BF)],
        compiler_params=_params(("parallel", "arbitrary")), name="dsa",
    )(iq, iwt, dq, ik, dk, dvt)


def _ret_kernel(rq_ref, rk_ref, rv_ref, rg_ref, qdec_ref, kdec_ref, dmask_ref, cdec_ref, blk_ref,
                hmq_ref, hmv_ref, ones_ref, ng_ref, o_ref, s_ref):
    @pl.when(pl.program_id(0) == 0)
    def _():
        s_ref[...] = jnp.zeros_like(s_ref)

    ones = ones_ref[...]

    def head_mean(z):
        return _dot(z.astype(_BF), ones) * (1.0 / RET_V_DIM)

    nseq = rq_ref.shape[0]
    ys = []
    for b in range(nseq):
        q, k, v = rq_ref[b], rk_ref[b], rv_ref[b]
        state = s_ref[b]
        y = _dot((q * qdec_ref[...]).astype(_BF), state.astype(_BF))
        kd = (k * kdec_ref[...]).T.astype(_BF)
        vb = v.astype(_BF)
        s_ref[b] = state * cdec_ref[...] + _dot(kd, vb) * blk_ref[...]
        kb = k.astype(_BF)
        a = jnp.concatenate([(_dot_nt((q * hmq_ref[h:h + 1, :]).astype(_BF), kb) * dmask_ref[h]).astype(_BF)
                             for h in range(RET_HEADS)], axis=1)
        vh = jnp.concatenate([(v * hmv_ref[h:h + 1, :]).astype(_BF) for h in range(RET_HEADS)], axis=0)
        ys.append(y + _dot(a, vh))
    y = jnp.concatenate(ys, axis=0)
    d = y - head_mean(y)
    yn = d * lax.rsqrt(head_mean(d * d) + EPS) * ng_ref[...]
    for b in range(nseq):
        o_ref[b] = jax.nn.silu(rg_ref[b]) * yn[b * RET_C:(b + 1) * RET_C, :]


def _ret_consts():
    c = RET_C
    log_g = jnp.log1p(-jnp.exp2(-5.0 - jnp.arange(RET_HEADS, dtype=F32)))
    pos = jnp.arange(c, dtype=F32)
    diff = pos[:, None] - pos[None, :]
    dmask = jnp.where(diff >= 0, jnp.exp(log_g[:, None, None] * jnp.maximum(diff, 0.0)), 0.0)
    lane_q = jnp.arange(RET_QK_PAD)
    head_q = jnp.where((lane_q % LANES) < RET_HEADS * (RET_QK_DIM // 2), (lane_q % LANES) // (RET_QK_DIM // 2), -1)
    head_v = jnp.arange(RET_WIDTH) // RET_V_DIM
    hmq = (head_q[None, :] == jnp.arange(RET_HEADS)[:, None]).astype(F32)
    hmv = (head_v[None, :] == jnp.arange(RET_HEADS)[:, None]).astype(F32)
    lg_q = hmq.T @ log_g
    qdec = jnp.exp(lg_q[None, :] * (pos[:, None] + 1.0))
    kdec = jnp.exp(lg_q[None, :] * (c - 1.0 - pos[:, None]))
    blk = hmq.T @ hmv
    cdec = blk * jnp.exp(lg_q * c)[:, None]
    ones = (hmv.T @ hmv).astype(_BF)
    return qdec, kdec, dmask, cdec, blk, hmq, hmv, ones


def _retention(rq, rk, rv, rg, norm_g):
    bsz, seq, _ = rq.shape
    consts = _ret_consts()
    grid = (seq // RET_C,)
    row = lambda w: pl.BlockSpec((bsz, RET_C, w), lambda i: (0, i, 0))
    ng = norm_g.reshape(1, RET_WIDTH)
    return pl.pallas_call(
        _ret_kernel, grid=grid,
        in_specs=[row(RET_QK_PAD), row(RET_QK_PAD), row(RET_WIDTH), row(RET_WIDTH)]
        + [_const_spec(a.shape) for a in consts] + [_const_spec(ng.shape)],
        out_specs=row(RET_WIDTH),
        out_shape=jax.ShapeDtypeStruct((bsz, seq, RET_WIDTH), F32),
        scratch_shapes=[pltpu.VMEM((bsz, RET_QK_PAD, RET_WIDTH), F32)],
        compiler_params=_params(("arbitrary",)), name="retention",
    )(rq, rk, rv, rg, *consts, ng)


def _merge_kernel(x_ref, g_ref, ya_ref, yb_ref, yc_ref, wg_ref, wa_ref, wb_ref, wc_ref, wo_ref, o_ref):
    x = x_ref[...]
    h = _rms(x, g_ref[...]).astype(_BF)
    merged = jnp.zeros(x.shape, F32)
    for n, (y_ref, w_ref) in enumerate(((ya_ref, wa_ref), (yb_ref, wb_ref), (yc_ref, wc_ref))):
        gate = jax.nn.sigmoid(_dot_nt(h, wg_ref[n * D_MODEL:(n + 1) * D_MODEL, :]))
        merged = merged + gate * _dot(y_ref[...].astype(_BF), w_ref[...])
    o_ref[...] = x + _dot(merged.astype(_BF), wo_ref[...])


def _merge(x, g, ya, yb, yc, wg, wa, wb, wc, wo):
    bsz, seq, _ = x.shape
    grid = (bsz, seq // TM_OUT)
    row = lambda w: pl.BlockSpec((None, TM_OUT, w), lambda b, i: (b, i, 0))
    return pl.pallas_call(
        _merge_kernel, grid=grid,
        in_specs=[row(D_MODEL), _const_spec(g.shape), row(SSM_WIDTH), row(DSA_WIDTH), row(RET_WIDTH)]
        + [_const_spec(w.shape) for w in (wg, wa, wb, wc, wo)],
        out_specs=row(D_MODEL),
        out_shape=jax.ShapeDtypeStruct(x.shape, F32),
        compiler_params=_params(("parallel", "parallel")), name="merge",
    )(x, g, ya, yb, yc, wg, wa, wb, wc, wo)


def _mlp_kernel(x_ref, g_ref, w1_ref, w2_ref, fg_ref, o_ref, *, final_norm):
    x = x_ref[...]
    h = _rms(x, g_ref[...]).astype(_BF)
    acc = x
    for f in range(D_FF // FF_CHUNK):
        t = jnp.maximum(_dot(h, w1_ref[:, f * FF_CHUNK:(f + 1) * FF_CHUNK]), 0.0)
        acc = acc + _dot((t * t).astype(_BF), w2_ref[f * FF_CHUNK:(f + 1) * FF_CHUNK, :])
    o_ref[...] = _rms(acc, fg_ref[...]) if final_norm else acc


def _mlp(x, g, w1, w2, fg, final_norm):
    bsz, seq, _ = x.shape
    grid = (bsz, seq // TM_OUT)
    row = pl.BlockSpec((None, TM_OUT, D_MODEL), lambda b, i: (b, i, 0))
    return pl.pallas_call(
        functools.partial(_mlp_kernel, final_norm=final_norm), grid=grid,
        in_specs=[row, _const_spec(g.shape), _const_spec(w1.shape), _const_spec(w2.shape), _const_spec(fg.shape)],
        out_specs=row,
        out_shape=jax.ShapeDtypeStruct(x.shape, F32),
        compiler_params=_params(("parallel", "parallel")), name="mlp",
    )(x, g, w1, w2, fg)


def _rope_tables(positions):
    pos = positions.astype(F32)[..., None]
    lane = jnp.arange(LANES)

    def angles(rot_dim, theta):
        half = rot_dim // 2
        inv = jnp.exp(-math.log(theta) * jnp.arange(half, dtype=F32) * (2.0 / rot_dim))
        return pos * inv

    def roll_tables(head_dim, rot_dim, theta):
        half = rot_dim // 2
        ang = angles(rot_dim, theta)
        ln = lane % head_dim
        rot = ln < rot_dim
        cos = jnp.where(rot, jnp.take(jnp.cos(ang), ln % half, axis=-1), 1.0)
        sin = jnp.where(rot, jnp.take(jnp.sin(ang), ln % half, axis=-1), 0.0)
        sgn = jnp.stack([jnp.where(ln < half, -1.0, 0.0),
                         jnp.where((ln >= half) & rot, 1.0, 0.0)]).astype(F32)
        return cos, sin, sgn

    cd, sd, sgd = roll_tables(DSA_HEAD_DIM, DSA_HEAD_DIM // 4, ROPE_THETA)
    ci, si, sgi = roll_tables(IDX_DIM, IDX_DIM // 4, ROPE_THETA)
    half = RET_QK_DIM // 2
    ang = angles(RET_QK_DIM, RET_THETA)
    used = lane < RET_HEADS * half
    cr = jnp.where(used, jnp.take(jnp.cos(ang), lane % half, axis=-1), 1.0)
    sr = jnp.where(used, jnp.take(jnp.sin(ang), lane % half, axis=-1), 0.0)
    return cd, sd, ci, si, cr, sr, sgd, sgi


def _inproj_weights(wt):
    sizes = (SSM_WIDTH, DSA_WIDTH, DSA_HEAD_DIM, DSA_HEAD_DIM, IDX_HEADS * IDX_DIM, IDX_DIM, IDX_HEADS,
             RET_HEADS * RET_QK_DIM, RET_HEADS * RET_QK_DIM, RET_WIDTH, RET_WIDTH, 3 * D_MODEL)
    parts, off = [], 0
    for n in sizes:
        parts.append(wt[off:off + n, :])
        off += n
    wu, wdq, wdk, wdv, wiq, wik, wiw, wrq, wrk, wrv, wrg, wgt = parts
    half = RET_QK_DIM // 2
    zrows = lambda n: jnp.zeros((n, D_MODEL), wt.dtype)

    def ret_split(a):
        a = a.reshape(RET_HEADS, 2, half, D_MODEL)
        pad = zrows(LANES - RET_HEADS * half)
        return jnp.concatenate([a[:, 0].reshape(-1, D_MODEL), pad, a[:, 1].reshape(-1, D_MODEL), pad], axis=0)

    ws = (wu, wdq, jnp.concatenate([wdk, wdk], axis=0), jnp.concatenate([wdv, zrows(LANES - DSA_HEAD_DIM)], axis=0),
          wiq, jnp.concatenate([wik] * IDX_HEADS, axis=0), jnp.concatenate([wiw, zrows(SUBLANES - IDX_HEADS)], axis=0),
          ret_split(wrq), ret_split(wrk), wrv, wrg)
    return tuple(a.astype(_BF) for a in ws), wgt.astype(_BF)


def kernel(x, positions, norm1_g, w_in, ssm_lambda_re, ssm_lambda_im, ssm_log_step, ssm_b_re, ssm_b_im,
           ssm_c_re, ssm_c_im, ssm_d, ssm_glu_w, ssm_glu_b, ret_norm_g, w_proj_a, w_proj_b, w_proj_c,
           w_out, norm2_g, w_ff1, w_ff2, final_norm_g):
    depth = w_in.shape[0]
    tabs = _rope_tables(positions)
    fg = final_norm_g.reshape(1, D_MODEL)
    w_in_t = jnp.transpose(w_in, (2, 0, 1))
    for l in range(depth):
        g1 = norm1_g[l].reshape(1, D_MODEL)
        ws, wgt = _inproj_weights(w_in_t[:, l, :])
        u, dq, dk, dvt, iq, ik, iwt, rq, rk, rv, rg = _inproj(x, g1, tabs, ws)

        s5p = _s5_params(ssm_lambda_re[l], ssm_lambda_im[l], ssm_log_step[l], ssm_b_re[l], ssm_b_im[l],
                         ssm_c_re[l], ssm_c_im[l], ssm_d[l], ssm_glu_w[l], ssm_glu_b[l])
        ya = _s5(u, s5p)
        yb = _dsa(iq, iwt, dq, ik, dk, dvt)
        yc = _retention(rq, rk, rv, rg, ret_norm_g[l])

        x = _merge(x, g1, ya, yb, yc, wgt, w_proj_a[l].astype(_BF), w_proj_b[l].astype(_BF),
                   w_proj_c[l].astype(_BF), w_out[l].astype(_BF))
        x = _mlp(x, norm2_g[l].reshape(1, D_MODEL), w_ff1[l].astype(_BF), w_ff2[l].astype(_BF), fg,
                 final_norm=(l == depth - 1))
    return x
```

```python
import functools
import math

import jax
import jax.numpy as jnp
from jax import lax
from jax.experimental import pallas as pl
from jax.experimental.pallas import tpu as pltpu

F32 = jnp.float32
_BF = jnp.bfloat16
LANES = 128
SUBLANES = 8
VMEM_LIMIT = 56 * 1024 * 1024

D_MODEL = 1024
CHUNK = 64
EPS = 1e-6
NEG_INF_SCORE = -1e30
INT_MIN = -2 ** 31
GROUP = 32 * SUBLANES
GROUP_UNROLL = (16, 8, 4)
PAIR_UNROLL = (4, 2, 1)
V_ROWS = 80

SSM_WIDTH = 256
SSM_GROUP = 16
SSM_GROUPS = 16
SSM_STATE = 64
NSTATE = SSM_GROUPS * SSM_STATE

DSA_HEADS = 6
DSA_HEAD_DIM = 64
DSA_WIDTH = 384
IDX_HEADS = 4
IDX_DIM = 32
DSA_TOPK = 256
ROPE_THETA = 500000.0

RET_HEADS = 4
RET_QK_DIM = 48
RET_V_DIM = 96
RET_WIDTH = 384
RET_THETA = 10000.0
RET_QK_PAD = 256

D_FF = 4096

TM_IN = 512
S5_TILE = 512
S5_SEG = S5_TILE // SUBLANES
QB = 128
KT = 512
RET_C = 128
TM_OUT = 512
FF_CHUNK = 1024


def _rms(x, g):
    return x * lax.rsqrt(jnp.mean(x * x, axis=-1, keepdims=True) + EPS) * g


def _dot(a, b):
    return jnp.dot(a, b, preferred_element_type=F32)


def _dot_nt(a, b):
    return lax.dot_general(a, b, (((1,), (1,)), ((), ())), preferred_element_type=F32)


def _to_ukey(v):
    bits = lax.bitcast_convert_type(v, jnp.int32)
    return lax.bitcast_convert_type(bits ^ ((bits >> 31) | INT_MIN), jnp.uint32)


def _transpose32(a):
    a = list(a)
    j, m = 16, 0x0000FFFF
    while j:
        k = 0
        while k < 32:
            t = (a[k] ^ (a[k + j] >> j)) & m
            a[k] = a[k] ^ t
            a[k + j] = a[k + j] ^ (t << j)
            k = (k + j + 1) & ~j
        j >>= 1
        m = (m ^ (m << j)) & 0xFFFFFFFF
    return a


def _const_spec(shape):
    nd = len(shape)
    return pl.BlockSpec(shape, lambda *_: (0,) * nd)


def _params(sem):
    return pltpu.CompilerParams(dimension_semantics=sem, vmem_limit_bytes=VMEM_LIMIT)


def _rope_roll(z, cos, sin, sgn, shift):
    outs = []
    for c in range(z.shape[1] // LANES):
        zc = z[:, c * LANES:(c + 1) * LANES]
        rot = (pltpu.roll(zc, LANES - shift, 1) * sgn[0:1, :]
               + pltpu.roll(zc, shift, 1) * sgn[1:2, :])
        outs.append(zc * cos + rot * sin)
    return outs[0] if len(outs) == 1 else jnp.concatenate(outs, axis=1)


def _inproj_kernel(x_ref, g_ref, cd_ref, sd_ref, ci_ref, si_ref, cr_ref, sr_ref, sgd_ref, sgi_ref,
                   wu_ref, wdq_ref, wdk_ref, wdvt_ref, wiq_ref, wik_ref, wiwt_ref,
                   wrq_ref, wrk_ref, wrv_ref, wrg_ref,
                   u_ref, dq_ref, dk_ref, dvt_ref, iq_ref, ik_ref, iwt_ref,
                   rq_ref, rk_ref, rv_ref, rg_ref):
    h = _rms(x_ref[...], g_ref[...]).astype(_BF)

    def proj(w_ref):
        return _dot_nt(h, w_ref[...])

    u_ref[...] = proj(wu_ref)

    cd, sd, sgd = cd_ref[...], sd_ref[...], sgd_ref[...]
    dq = _rope_roll(proj(wdq_ref), cd, sd, sgd, 8)
    dq_ref[...] = (dq * (DSA_HEAD_DIM ** -0.5 * math.log2(math.e))).astype(dq_ref.dtype)
    dk_ref[...] = _rope_roll(proj(wdk_ref), cd, sd, sgd, 8).astype(dk_ref.dtype)
    vt = _dot_nt(wdvt_ref[...], h)
    ones_row = lax.broadcasted_iota(jnp.int32, vt.shape, 0) == DSA_HEAD_DIM
    dvt_ref[...] = jnp.where(ones_row, 1.0, vt).astype(dvt_ref.dtype)

    ci, si, sgi = ci_ref[...], si_ref[...], sgi_ref[...]
    iq_ref[...] = _rope_roll(proj(wiq_ref), ci, si, sgi, 4) * (IDX_DIM ** -0.5)
    ik_ref[...] = _rope_roll(proj(wik_ref), ci, si, sgi, 4).astype(ik_ref.dtype)
    iwt_ref[...] = _dot_nt(wiwt_ref[...], h) * (IDX_HEADS ** -0.5)

    cr, sr = cr_ref[...], sr_ref[...]

    def rope_split(z):
        x1, x2 = z[:, :LANES], z[:, LANES:]
        return jnp.concatenate([x1 * cr - x2 * sr, x2 * cr + x1 * sr], axis=1)

    rq_ref[...] = rope_split(proj(wrq_ref))
    rk_ref[...] = rope_split(proj(wrk_ref)) * (RET_QK_DIM ** -0.5)
    rv_ref[...] = proj(wrv_ref)
    rg_ref[...] = proj(wrg_ref)


def _inproj(x, g, tabs, ws):
    bsz, seq, _ = x.shape
    tm = TM_IN
    grid = (bsz, seq // tm)
    row = lambda w: pl.BlockSpec((None, tm, w), lambda b, i: (b, i, 0))
    colT = lambda r: pl.BlockSpec((None, r, tm), lambda b, i: (b, 0, i))
    in_specs = ([row(D_MODEL), _const_spec(g.shape)] + [row(LANES)] * 6
                + [_const_spec(t.shape) for t in tabs[6:]] + [_const_spec(w.shape) for w in ws])
    out_shape = [
        jax.ShapeDtypeStruct((bsz, seq, SSM_WIDTH), F32),
        jax.ShapeDtypeStruct((bsz, seq, DSA_WIDTH), _BF),
        jax.ShapeDtypeStruct((bsz, seq, LANES), _BF),
        jax.ShapeDtypeStruct((bsz, LANES, seq), _BF),
        jax.ShapeDtypeStruct((bsz, seq, LANES), F32),
        jax.ShapeDtypeStruct((bsz, seq, LANES), _BF),
        jax.ShapeDtypeStruct((bsz, SUBLANES, seq), F32),
        jax.ShapeDtypeStruct((bsz, seq, RET_QK_PAD), F32),
        jax.ShapeDtypeStruct((bsz, seq, RET_QK_PAD), F32),
        jax.ShapeDtypeStruct((bsz, seq, RET_WIDTH), F32),
        jax.ShapeDtypeStruct((bsz, seq, RET_WIDTH), F32),
    ]
    out_specs = [row(SSM_WIDTH), row(DSA_WIDTH), row(LANES), colT(LANES), row(LANES), row(LANES),
                 colT(SUBLANES), row(RET_QK_PAD), row(RET_QK_PAD), row(RET_WIDTH), row(RET_WIDTH)]
    return pl.pallas_call(
        _inproj_kernel, grid=grid, in_specs=in_specs, out_specs=out_specs, out_shape=out_shape,
        compiler_params=_params(("parallel", "parallel")), name="inproj",
    )(x, g, *tabs, *ws)


def _s5_kernel(u_ref, bbd_ref, cbd_ref, lam_ref, lam64_ref, pow_ref, d_ref, gw_ref, gb_ref,
               o_ref, st_ref, xm_ref, carry_ref, up_ref, yp_ref):
    n = NSTATE

    @pl.when(pl.program_id(1) == 0)
    def _():
        carry_ref[...] = jnp.zeros_like(carry_ref)

    nhalf = SSM_WIDTH // LANES
    for i in range(SUBLANES):
        for c in range(nhalf):
            up_ref[c, pl.ds(i, S5_SEG, stride=SUBLANES), :] = u_ref[i * S5_SEG:(i + 1) * S5_SEG,
                                                                    c * LANES:(c + 1) * LANES]
    u = jnp.concatenate([up_ref[c] for c in range(nhalf)], axis=1)
    st_ref[...] = _dot(u.astype(_BF), bbd_ref[...])

    lr = jnp.broadcast_to(lam_ref[:, :n], (SUBLANES, n))
    li = jnp.broadcast_to(lam_ref[:, n:], (SUBLANES, n))

    def step(j, c):
        xr, xi = c
        off = pl.multiple_of(j * SUBLANES, SUBLANES)
        ar = st_ref[pl.ds(off, SUBLANES), :n]
        ai = st_ref[pl.ds(off, SUBLANES), n:]
        nr = lr * xr - li * xi + ar
        ni = lr * xi + li * xr + ai
        st_ref[pl.ds(off, SUBLANES), :n] = nr
        st_ref[pl.ds(off, SUBLANES), n:] = ni
        return nr, ni

    zero = jnp.zeros((SUBLANES, n), F32)
    xr, xi = lax.fori_loop(0, S5_SEG, step, (zero, zero), unroll=4)

    l64r, l64i = lam64_ref[:, :n], lam64_ref[:, n:]
    cr, ci = carry_ref[:, :n], carry_ref[:, n:]
    for i in range(SUBLANES):
        xm_ref[i:i + 1, :n] = cr
        xm_ref[i:i + 1, n:] = ci
        er, ei = xr[i:i + 1, :], xi[i:i + 1, :]
        cr, ci = er + l64r * cr - l64i * ci, ei + l64r * ci + l64i * cr
    carry_ref[:, :n] = cr
    carry_ref[:, n:] = ci

    hr, hi = xm_ref[:, :n], xm_ref[:, n:]

    def fix(j, _):
        off = pl.multiple_of(j * SUBLANES, SUBLANES)
        pr = pow_ref[pl.ds(j, 1), :n]
        pi = pow_ref[pl.ds(j, 1), n:]
        st_ref[pl.ds(off, SUBLANES), :n] = st_ref[pl.ds(off, SUBLANES), :n] + pr * hr - pi * hi
        st_ref[pl.ds(off, SUBLANES), n:] = st_ref[pl.ds(off, SUBLANES), n:] + pr * hi + pi * hr
        return 0

    lax.fori_loop(0, S5_SEG, fix, 0, unroll=4)

    y = d_ref[...] * u
    kc = 512
    for k in range(2 * n // kc):
        y = y + _dot(st_ref[:, k * kc:(k + 1) * kc].astype(_BF), cbd_ref[k * kc:(k + 1) * kc, :])
    y = jax.nn.gelu(y)
    y = y * jax.nn.sigmoid(_dot(y.astype(_BF), gw_ref[...]) + gb_ref[...])
    for c in range(nhalf):
        yp_ref[c] = y[:, c * LANES:(c + 1) * LANES]
    for i in range(SUBLANES):
        for c in range(nhalf):
            o_ref[i * S5_SEG:(i + 1) * S5_SEG, c * LANES:(c + 1) * LANES] = yp_ref[
                c, pl.ds(i, S5_SEG, stride=SUBLANES), :]


def _s5(u_perm, prm):
    bsz, seq, _ = u_perm.shape
    grid = (bsz, seq // S5_TILE)
    row = pl.BlockSpec((None, S5_TILE, SSM_WIDTH), lambda b, i: (b, i, 0))
    return pl.pallas_call(
        _s5_kernel, grid=grid,
        in_specs=[row] + [_const_spec(p.shape) for p in prm],
        out_specs=row,
        out_shape=jax.ShapeDtypeStruct((bsz, seq, SSM_WIDTH), F32),
        scratch_shapes=[pltpu.VMEM((S5_TILE, 2 * NSTATE), F32),
                        pltpu.VMEM((SUBLANES, 2 * NSTATE), F32),
                        pltpu.VMEM((1, 2 * NSTATE), F32),
                        pltpu.VMEM((SSM_WIDTH // LANES, S5_TILE, LANES), F32),
                        pltpu.VMEM((SSM_WIDTH // LANES, S5_TILE, LANES), F32)],
        compiler_params=_params(("arbitrary", "arbitrary")), name="s5",
    )(u_perm, *prm)


def _s5_params(lam_re, lam_im, log_step, b_re, b_im, c_re, c_im, d_skip, glu_w, glu_b):
    step = jnp.exp(log_step.astype(F32))[:, None]
    ere, eim = lam_re * step, lam_im * step
    mag = jnp.exp(ere)
    lb_re, lb_im = mag * jnp.cos(eim), mag * jnp.sin(eim)
    den = lam_re * lam_re + lam_im * lam_im
    f_re = ((lb_re - 1.0) * lam_re + lb_im * lam_im) / den
    f_im = (lb_im * lam_re - (lb_re - 1.0) * lam_im) / den
    bb_re = f_re[..., None] * b_re - f_im[..., None] * b_im
    bb_im = f_re[..., None] * b_im + f_im[..., None] * b_re
    eye = jnp.eye(SSM_GROUPS, dtype=F32)
    bbd = jnp.concatenate(
        [jnp.einsum('gpc,gh->gchp', bb, eye).reshape(SSM_WIDTH, NSTATE) for bb in (bb_re, bb_im)], axis=1)
    cbd = jnp.concatenate(
        [jnp.einsum('gcp,gh->hpgc', cc, eye).reshape(NSTATE, SSM_WIDTH) for cc in (c_re, -c_im)], axis=0)
    flat = lambda a: a.reshape(1, NSTATE)
    lam = jnp.concatenate([flat(lb_re), flat(lb_im)], axis=1)

    def power(k):
        k = jnp.asarray(k, F32).reshape(-1, 1)
        m = jnp.exp(k * flat(ere))
        return jnp.concatenate([m * jnp.cos(k * flat(eim)), m * jnp.sin(k * flat(eim))], axis=1)

    return (bbd.astype(_BF), cbd.astype(_BF), lam, power(float(S5_SEG)),
            power(jnp.arange(1, S5_SEG + 1)), d_skip.reshape(1, SSM_WIDTH),
            glu_w.astype(_BF), glu_b.reshape(1, SSM_WIDTH))


def _dsa_kernel(iq_ref, iwt_ref, dq_ref, ik_ref, dk_ref, dvt_ref, o_ref,
                planes_ref, cand_ref, pick_ref, bias_ref, acc_ref,
                lg0_ref, lg1_ref, s0_ref, s1_ref, p0_ref, p1_ref, *, seq, topk):
    t = pl.program_id(1)
    nq = seq // QB
    has_sel = t < nq
    has_att = t >= 1
    qi = jnp.minimum(t, nq - 1)
    qa = jnp.maximum(t - 1, 0)
    nk = (qi * QB) // KT + 1
    lane = lax.broadcasted_iota(jnp.int32, (1, QB), 1)
    vis_end = qi * QB + jnp.where(lane < CHUNK, CHUNK, 2 * CHUNK)

    iq = iq_ref[...]
    col = lax.broadcasted_iota(jnp.int32, (QB, LANES), 1)
    qs_t = jnp.concatenate([jnp.where(col // IDX_DIM == h, iq, 0.0) for h in range(IDX_HEADS)],
                           axis=0).T.astype(_BF)
    w = iwt_ref[...]

    nkp = nk + lax.rem(nk, 2)
    last_tile = nkp - 1
    nk_a = (qa * QB) // KT + 1
    nkp_a = nk_a + lax.rem(nk_a, 2)
    last_tile_a = nkp_a - 1

    def key_tile(ref, kt):
        return ref[pl.ds(pl.multiple_of(kt * KT, KT), KT), :]

    def tile_planes(src_ref):
        lg = src_ref[...]
        sc = jnp.zeros((KT, QB), F32)
        for h in range(IDX_HEADS):
            sc = sc + jnp.maximum(lg[:, h * QB:(h + 1) * QB], 0.0) * w[h:h + 1, :]
        sc = jnp.where(sc == 0.0, 0.0, sc)
        ukey = _to_ukey(sc)
        words = []
        for g in range(KT // GROUP):
            words += _transpose32([ukey[g * GROUP + m * SUBLANES:g * GROUP + (m + 1) * SUBLANES, :]
                                   for m in range(32)])
        return lax.bitcast_convert_type(jnp.concatenate(words, axis=0), jnp.int32)

    def score_pair(i, c):
        ik1 = key_tile(ik_ref, 2 * i + 1)
        ik2 = key_tile(ik_ref, jnp.minimum(2 * i + 2, last_tile))
        lg1_ref[...] = _dot(ik1, qs_t)
        planes0 = tile_planes(lg0_ref)
        lg0_ref[...] = _dot(ik2, qs_t)
        planes1 = tile_planes(lg1_ref)
        planes_ref[pl.ds(pl.multiple_of(i * (2 * KT), 2 * KT), 2 * KT), :] = jnp.concatenate(
            [planes0, planes1], axis=0)
        return c

    @pl.when(has_sel)
    def _():
        lg0_ref[...] = _dot(key_tile(ik_ref, 0), qs_t)

    npairs = jnp.where(has_sel, nkp // 2, 0)
    npairs_a = jnp.where(has_att, nkp_a // 2, 0)

    def pair_loops(pair_fn, carry, first, end):
        start = first
        for u in PAIR_UNROLL:
            def body(j, c, u=u, start=start):
                for k in range(u):
                    c = pair_fn(start + j * u + k, c)
                return c
            n = (end - start) // u
            carry = lax.fori_loop(0, n, body, carry)
            start = start + n * u
        return carry

    q = dq_ref[...]
    half = (col // DSA_HEAD_DIM)
    qst = jnp.concatenate(
        [jnp.where(half == (h % 2), q[:, (h // 2) * LANES:(h // 2 + 1) * LANES], jnp.zeros((), q.dtype))
         for h in range(DSA_HEADS)], axis=0)
    unit = jnp.where(lax.broadcasted_iota(jnp.int32, (QB, QB), 0) == lax.broadcasted_iota(jnp.int32, (QB, QB), 1),
                     1.0, 0.0).astype(q.dtype)
    qst = jnp.concatenate([qst, jnp.concatenate([unit] * DSA_HEADS, axis=0)], axis=1)
    qst_t = qst.astype(F32).T.astype(q.dtype)
    npair = DSA_HEADS // 2
    hq = DSA_HEADS * QB
    acc_ref[...] = jnp.zeros_like(acc_ref)
    p1_ref[...] = jnp.zeros_like(p1_ref)

    def qk(k_tile, b_tile, dst_ref):
        dst_ref[...] = _dot(jnp.concatenate([k_tile, b_tile], axis=1), qst_t)

    def numer(src_ref, dst_ref, m):
        m_new = []
        for hd in range(DSA_HEADS):
            cols = slice(hd * QB, (hd + 1) * QB)
            mh = jnp.maximum(m[:, cols], jnp.max(src_ref[:, cols], axis=0, keepdims=True))
            dst_ref[:, cols] = jnp.exp2(src_ref[:, cols] - mh).astype(dst_ref.dtype)
            m_new.append(mh)
        m_new = jnp.concatenate(m_new, axis=1)
        return m_new, jnp.exp2(m - m_new)

    def v_tile(kt):
        return dvt_ref[0:V_ROWS, pl.ds(pl.multiple_of(kt * KT, KT), KT)]

    def pv(vt, src_ref, alpha):
        acc_ref[...] = alpha * acc_ref[...] + _dot(vt, src_ref[...])

    def att_pair(i, c):
        m, alpha = c
        nxt = jnp.minimum(2 * i + 2, last_tile_a)
        k1, b1 = key_tile(dk_ref, 2 * i + 1), key_tile(bias_ref, 2 * i + 1)
        k2, b2 = key_tile(dk_ref, nxt), key_tile(bias_ref, nxt)
        v_prev, v_cur = v_tile(jnp.maximum(2 * i - 1, 0)), v_tile(2 * i)
        qk(k1, b1, s1_ref)
        pv(v_prev, p1_ref, alpha)
        m, alpha = numer(s0_ref, p0_ref, m)
        qk(k2, b2, s0_ref)
        pv(v_cur, p0_ref, alpha)
        m, alpha = numer(s1_ref, p1_ref, m)
        return m, alpha

    @pl.when(has_att)
    def _():
        qk(key_tile(dk_ref, 0), key_tile(bias_ref, 0), s0_ref)

    c0 = (jnp.full((1, hq), -jnp.inf, F32), jnp.ones((1, hq), F32))

    def fused_pair(i, c):
        c = att_pair(i, c)
        score_pair(i, 0)
        return c

    n_fused = jnp.minimum(npairs, npairs_a)
    c1 = pair_loops(fused_pair, c0, 0, n_fused)
    pair_loops(score_pair, 0, n_fused, npairs)
    _, alpha = pair_loops(att_pair, c1, n_fused, npairs_a)

    @pl.when(jnp.logical_not(has_att))
    def _():
        o_ref[...] = jnp.zeros_like(o_ref)

    @pl.when(has_att)
    def _():
        pv(v_tile(last_tile_a), p1_ref, alpha)
        acc = acc_ref[...]
        o = acc[0:DSA_HEAD_DIM, :] / acc[DSA_HEAD_DIM:DSA_HEAD_DIM + 1, :]
        for j in range(npair):
            o_ref[:, j * LANES:(j + 1) * LANES] = jnp.concatenate(
                [o[:, (2 * j) * QB:(2 * j + 1) * QB], o[:, (2 * j + 1) * QB:(2 * j + 2) * QB]], axis=0).T

    ngrp = nkp * (KT // GROUP)
    sub = lax.broadcasted_iota(jnp.int32, (SUBLANES, QB), 0)
    zero8 = jnp.zeros((SUBLANES, QB), jnp.int32)

    def rows_below(limit, g):
        nm = lax.shift_right_arithmetic(limit - g * GROUP + (SUBLANES - 1), 3)
        top = lax.shift_right_arithmetic(jnp.full((SUBLANES, QB), INT_MIN, jnp.int32),
                                         jnp.clip(nm, 1, 32) - 1)
        return jnp.where(nm <= 0, 0, top)

    def word(ref, g):
        return ref[pl.ds(pl.multiple_of(g * SUBLANES, SUBLANES), SUBLANES), :]

    def put_word(ref, g, v):
        ref[pl.ds(pl.multiple_of(g * SUBLANES, SUBLANES), SUBLANES), :] = v

    def plane(g, i):
        return planes_ref[pl.ds(pl.multiple_of(g * GROUP + i * SUBLANES, SUBLANES), SUBLANES), :]

    def sweep_words(step):
        accs, start = (zero8,) * GROUP_UNROLL[-1], 0
        for u in GROUP_UNROLL:
            def body(s, accs, u=u, start=start):
                accs = list(accs)
                for t in range(u):
                    accs[t % len(accs)] = accs[t % len(accs)] + step(start + s * u + t)
                return tuple(accs)
            n = (ngrp - start) // u
            accs = lax.fori_loop(0, n, body, accs)
            start = start + n * u
        return functools.reduce(lambda a, b: a + b, accs)

    def sweep(step):
        return sweep_words(step).sum(axis=0, keepdims=True)

    def init_words(g):
        put_word(cand_ref, g, rows_below(vis_end - sub, g))
        put_word(pick_ref, g, zero8)
        return zero8

    sweep(init_words)

    def count_ones(i):
        return sweep(lambda g: lax.population_count(word(cand_ref, g) & plane(g, i)))

    def decide(c1, rem):
        take = c1 >= rem
        return jnp.where(take, 1, 0), jnp.where(take, rem, rem - c1)

    def narrow(g, i, take):
        take8 = jnp.broadcast_to(take, (SUBLANES, QB)) != 0
        c = word(cand_ref, g)
        ones = c & plane(g, i)
        put_word(pick_ref, g, word(pick_ref, g) | jnp.where(take8, 0, ones))
        c = jnp.where(take8, ones, c ^ ones)
        put_word(cand_ref, g, c)
        return c

    def radix_step(i, carry):
        take8, rem = carry

        def step(g):
            return lax.population_count(narrow(g, i - 1, take8) & plane(g, i))
        return decide(sweep(step), rem)

    take8, rem = decide(count_ones(0), jnp.full((1, QB), topk, jnp.int32))
    take8, rem = lax.fori_loop(1, 32, radix_step, (take8, rem))

    def last_narrow(g):
        narrow(g, 31, take8)
        return zero8

    sweep(last_narrow)

    def group_step(i, carry):
        grp, before = carry
        trial = grp + jnp.left_shift(jnp.int32(1), ((seq // GROUP).bit_length() - 2) - i)
        below = sweep(lambda g: jnp.where(jnp.broadcast_to(g < trial, (SUBLANES, QB)),
                                          lax.population_count(word(cand_ref, g)), 0))
        ok = below < rem
        return jnp.where(ok, trial, grp), jnp.where(ok, below, before)

    zrow = jnp.zeros((1, QB), jnp.int32)
    grp, before = lax.fori_loop(0, (seq // GROUP).bit_length() - 1, group_step, (zrow, zrow))
    tied = sweep_words(lambda g: jnp.where(jnp.broadcast_to(g == grp, (SUBLANES, QB)), word(cand_ref, g), 0))

    def row_step(i, edge):
        trial = edge + jnp.left_shift(jnp.int32(1), (GROUP.bit_length() - 2) - i)
        below = lax.population_count(tied & rows_below(trial - sub, 0)).sum(axis=0, keepdims=True)
        return jnp.where(before + below < rem, trial, edge)

    edge = lax.fori_loop(0, GROUP.bit_length() - 1, row_step, zrow)
    keep_limit = grp * GROUP + edge + 1 - sub

    def bias_group(g):
        chosen = word(pick_ref, g) | (word(cand_ref, g) & rows_below(keep_limit, g))
        base = pl.multiple_of(g * GROUP, GROUP)
        bias_ref[pl.ds(base, GROUP), :] = jnp.concatenate(
            [jnp.where(lax.shift_left(chosen, jnp.full_like(chosen, m)) < 0, 0.0, NEG_INF_SCORE)
             for m in range(32)], axis=0).astype(bias_ref.dtype)
        return zero8

    sweep(bias_group)


def _dsa(iq, iwt, dq, ik, dk, dvt):
    bsz, seq, _ = iq.shape
    topk = min(DSA_TOPK, seq // 4)
    nq = seq // QB
    grid = (bsz, nq + 1)
    sel_row = lambda w: pl.BlockSpec((None, QB, w), lambda b, i: (b, jnp.minimum(i, nq - 1), 0))
    att_row = lambda w: pl.BlockSpec((None, QB, w), lambda b, i: (b, jnp.maximum(i - 1, 0), 0))
    full = lambda r, c: pl.BlockSpec((None, r, c), lambda b, i: (b, 0, 0))
    return pl.pallas_call(
        functools.partial(_dsa_kernel, seq=seq, topk=topk), grid=grid,
        in_specs=[sel_row(LANES), pl.BlockSpec((None, SUBLANES, QB), lambda b, i: (b, 0, jnp.minimum(i, nq - 1))),
                  att_row(DSA_WIDTH),
                  full(seq, LANES), full(seq, LANES), full(LANES, seq)],
        out_specs=att_row(DSA_WIDTH),
        out_shape=jax.ShapeDtypeStruct((bsz, seq, DSA_WIDTH), F32),
        scratch_shapes=[pltpu.VMEM((seq, QB), jnp.int32),
                        pltpu.VMEM((seq // GROUP * SUBLANES, QB), jnp.int32),
                        pltpu.VMEM((seq // GROUP * SUBLANES, QB), jnp.int32),
                        pltpu.VMEM((seq, QB), _BF),
                        pltpu.VMEM((V_ROWS, DSA_HEADS * QB), F32),
                        pltpu.VMEM((KT, IDX_HEADS * QB), F32), pltpu.VMEM((KT, IDX_HEADS * QB), F32),
                        pltpu.VMEM((KT, DSA_HEADS * QB), F32), pltpu.VMEM((KT, DSA_HEADS * QB), F32),
                        pltpu.VMEM((KT, DSA_HEADS * QB), _BF), pltpu.VMEM((KT, DSA_HEADS * QB), _BF)],
        compiler_params=_params(("parallel", "arbitrary")), name="dsa",
    )(iq, iwt, dq, ik, dk, dvt)


def _ret_kernel(rq_ref, rk_ref, rv_ref, rg_ref, qdec_ref, kdec_ref, dmask_ref, cdec_ref, blk_ref,
                hmq_ref, hmv_ref, ones_ref, ng_ref, o_ref, s_ref):
    @pl.when(pl.program_id(0) == 0)
    def _():
        s_ref[...] = jnp.zeros_like(s_ref)

    ones = ones_ref[...]

    def head_mean(z):
        return _dot(z.astype(_BF), ones) * (1.0 / RET_V_DIM)

    nseq = rq_ref.shape[0]
    ys = []
    for b in range(nseq):
        q, k, v = rq_ref[b], rk_ref[b], rv_ref[b]
        state = s_ref[b]
        y = _dot((q * qdec_ref[...]).astype(_BF), state.astype(_BF))
        kd = (k * kdec_ref[...]).T.astype(_BF)
        vb = v.astype(_BF)
        s_ref[b] = state * cdec_ref[...] + _dot(kd, vb) * blk_ref[...]
        kh = jnp.concatenate([(k * hmq_ref[h:h + 1, :]).astype(_BF) for h in range(RET_HEADS)], axis=0)
        a = (_dot_nt(q.astype(_BF), kh) * dmask_ref[...]).astype(_BF)
        vh = jnp.concatenate([(v * hmv_ref[h:h + 1, :]).astype(_BF) for h in range(RET_HEADS)], axis=0)
        ys.append(y + _dot(a, vh))
    y = jnp.concatenate(ys, axis=0)
    d = y - head_mean(y)
    yn = d * lax.rsqrt(head_mean(d * d) + EPS) * ng_ref[...]
    for b in range(nseq):
        o_ref[b] = jax.nn.silu(rg_ref[b]) * yn[b * RET_C:(b + 1) * RET_C, :]


def _ret_consts():
    c = RET_C
    log_g = jnp.log1p(-jnp.exp2(-5.0 - jnp.arange(RET_HEADS, dtype=F32)))
    pos = jnp.arange(c, dtype=F32)
    diff = pos[:, None] - pos[None, :]
    dmask = jnp.where(diff >= 0, jnp.exp(log_g[:, None, None] * jnp.maximum(diff, 0.0)), 0.0)
    lane_q = jnp.arange(RET_QK_PAD)
    head_q = jnp.where((lane_q % LANES) < RET_HEADS * (RET_QK_DIM // 2), (lane_q % LANES) // (RET_QK_DIM // 2), -1)
    head_v = jnp.arange(RET_WIDTH) // RET_V_DIM
    hmq = (head_q[None, :] == jnp.arange(RET_HEADS)[:, None]).astype(F32)
    hmv = (head_v[None, :] == jnp.arange(RET_HEADS)[:, None]).astype(F32)
    lg_q = hmq.T @ log_g
    qdec = jnp.exp(lg_q[None, :] * (pos[:, None] + 1.0))
    kdec = jnp.exp(lg_q[None, :] * (c - 1.0 - pos[:, None]))
    blk = hmq.T @ hmv
    cdec = blk * jnp.exp(lg_q * c)[:, None]
    ones = (hmv.T @ hmv).astype(_BF)
    dmask = dmask.transpose(1, 0, 2).reshape(c, RET_HEADS * c)
    return qdec, kdec, dmask, cdec, blk, hmq, hmv, ones


def _retention(rq, rk, rv, rg, norm_g):
    bsz, seq, _ = rq.shape
    consts = _ret_consts()
    grid = (seq // RET_C,)
    row = lambda w: pl.BlockSpec((bsz, RET_C, w), lambda i: (0, i, 0))
    ng = norm_g.reshape(1, RET_WIDTH)
    return pl.pallas_call(
        _ret_kernel, grid=grid,
        in_specs=[row(RET_QK_PAD), row(RET_QK_PAD), row(RET_WIDTH), row(RET_WIDTH)]
        + [_const_spec(a.shape) for a in consts] + [_const_spec(ng.shape)],
        out_specs=row(RET_WIDTH),
        out_shape=jax.ShapeDtypeStruct((bsz, seq, RET_WIDTH), F32),
        scratch_shapes=[pltpu.VMEM((bsz, RET_QK_PAD, RET_WIDTH), F32)],
        compiler_params=_params(("arbitrary",)), name="retention",
    )(rq, rk, rv, rg, *consts, ng)


def _merge_kernel(x_ref, g_ref, ya_ref, yb_ref, yc_ref, wg_ref, wa_ref, wb_ref, wc_ref, wo_ref, o_ref):
    x = x_ref[...]
    h = _rms(x, g_ref[...]).astype(_BF)
    merged = jnp.zeros(x.shape, F32)
    for n, (y_ref, w_ref) in enumerate(((ya_ref, wa_ref), (yb_ref, wb_ref), (yc_ref, wc_ref))):
        gate = jax.nn.sigmoid(_dot_nt(h, wg_ref[n * D_MODEL:(n + 1) * D_MODEL, :]))
        merged = merged + gate * _dot(y_ref[...].astype(_BF), w_ref[...])
    o_ref[...] = x + _dot(merged.astype(_BF), wo_ref[...])


def _merge(x, g, ya, yb, yc, wg, wa, wb, wc, wo):
    bsz, seq, _ = x.shape
    grid = (bsz, seq // TM_OUT)
    row = lambda w: pl.BlockSpec((None, TM_OUT, w), lambda b, i: (b, i, 0))
    return pl.pallas_call(
        _merge_kernel, grid=grid,
        in_specs=[row(D_MODEL), _const_spec(g.shape), row(SSM_WIDTH), row(DSA_WIDTH), row(RET_WIDTH)]
        + [_const_spec(w.shape) for w in (wg, wa, wb, wc, wo)],
        out_specs=row(D_MODEL),
        out_shape=jax.ShapeDtypeStruct(x.shape, F32),
        compiler_params=_params(("parallel", "parallel")), name="merge",
    )(x, g, ya, yb, yc, wg, wa, wb, wc, wo)


def _mlp_kernel(x_ref, g_ref, w1_ref, w2_ref, fg_ref, o_ref, *, final_norm):
    x = x_ref[...]
    h = _rms(x, g_ref[...]).astype(_BF)
    acc = x
    for f in range(D_FF // FF_CHUNK):
        t = jnp.maximum(_dot(h, w1_ref[:, f * FF_CHUNK:(f + 1) * FF_CHUNK]), 0.0)
        acc = acc + _dot((t * t).astype(_BF), w2_ref[f * FF_CHUNK:(f + 1) * FF_CHUNK, :])
    o_ref[...] = _rms(acc, fg_ref[...]) if final_norm else acc


def _mlp(x, g, w1, w2, fg, final_norm):
    bsz, seq, _ = x.shape
    grid = (bsz, seq // TM_OUT)
    row = pl.BlockSpec((None, TM_OUT, D_MODEL), lambda b, i: (b, i, 0))
    return pl.pallas_call(
        functools.partial(_mlp_kernel, final_norm=final_norm), grid=grid,
        in_specs=[row, _const_spec(g.shape), _const_spec(w1.shape), _const_spec(w2.shape), _const_spec(fg.shape)],
        out_specs=row,
        out_shape=jax.ShapeDtypeStruct(x.shape, F32),
        compiler_params=_params(("parallel", "parallel")), name="mlp",
    )(x, g, w1, w2, fg)


def _rope_tables(positions):
    pos = positions.astype(F32)[..., None]
    lane = jnp.arange(LANES)

    def angles(rot_dim, theta):
        half = rot_dim // 2
        inv = jnp.exp(-math.log(theta) * jnp.arange(half, dtype=F32) * (2.0 / rot_dim))
        return pos * inv

    def roll_tables(head_dim, rot_dim, theta):
        half = rot_dim // 2
        ang = angles(rot_dim, theta)
        ln = lane % head_dim
        rot = ln < rot_dim
        cos = jnp.where(rot, jnp.take(jnp.cos(ang), ln % half, axis=-1), 1.0)
        sin = jnp.where(rot, jnp.take(jnp.sin(ang), ln % half, axis=-1), 0.0)
        sgn = jnp.stack([jnp.where(ln < half, -1.0, 0.0),
                         jnp.where((ln >= half) & rot, 1.0, 0.0)]).astype(F32)
        return cos, sin, sgn

    cd, sd, sgd = roll_tables(DSA_HEAD_DIM, DSA_HEAD_DIM // 4, ROPE_THETA)
    ci, si, sgi = roll_tables(IDX_DIM, IDX_DIM // 4, ROPE_THETA)
    half = RET_QK_DIM // 2
    ang = angles(RET_QK_DIM, RET_THETA)
    used = lane < RET_HEADS * half
    cr = jnp.where(used, jnp.take(jnp.cos(ang), lane % half, axis=-1), 1.0)
    sr = jnp.where(used, jnp.take(jnp.sin(ang), lane % half, axis=-1), 0.0)
    return cd, sd, ci, si, cr, sr, sgd, sgi


def _inproj_weights(wt):
    sizes = (SSM_WIDTH, DSA_WIDTH, DSA_HEAD_DIM, DSA_HEAD_DIM, IDX_HEADS * IDX_DIM, IDX_DIM, IDX_HEADS,
             RET_HEADS * RET_QK_DIM, RET_HEADS * RET_QK_DIM, RET_WIDTH, RET_WIDTH, 3 * D_MODEL)
    parts, off = [], 0
    for n in sizes:
        parts.append(wt[off:off + n, :])
        off += n
    wu, wdq, wdk, wdv, wiq, wik, wiw, wrq, wrk, wrv, wrg, wgt = parts
    half = RET_QK_DIM // 2
    zrows = lambda n: jnp.zeros((n, D_MODEL), wt.dtype)

    def ret_split(a):
        a = a.reshape(RET_HEADS, 2, half, D_MODEL)
        pad = zrows(LANES - RET_HEADS * half)
        return jnp.concatenate([a[:, 0].reshape(-1, D_MODEL), pad, a[:, 1].reshape(-1, D_MODEL), pad], axis=0)

    ws = (wu, wdq, jnp.concatenate([wdk, wdk], axis=0), jnp.concatenate([wdv, zrows(LANES - DSA_HEAD_DIM)], axis=0),
          wiq, jnp.concatenate([wik] * IDX_HEADS, axis=0), jnp.concatenate([wiw, zrows(SUBLANES - IDX_HEADS)], axis=0),
          ret_split(wrq), ret_split(wrk), wrv, wrg)
    return tuple(a.astype(_BF) for a in ws), wgt.astype(_BF)


def kernel(x, positions, norm1_g, w_in, ssm_lambda_re, ssm_lambda_im, ssm_log_step, ssm_b_re, ssm_b_im,
           ssm_c_re, ssm_c_im, ssm_d, ssm_glu_w, ssm_glu_b, ret_norm_g, w_proj_a, w_proj_b, w_proj_c,
           w_out, norm2_g, w_ff1, w_ff2, final_norm_g):
    depth = w_in.shape[0]
    tabs = _rope_tables(positions)
    fg = final_norm_g.reshape(1, D_MODEL)
    w_in_t = jnp.transpose(w_in, (2, 0, 1))
    for l in range(depth):
        g1 = norm1_g[l].reshape(1, D_MODEL)
        ws, wgt = _inproj_weights(w_in_t[:, l, :])
        u, dq, dk, dvt, iq, ik, iwt, rq, rk, rv, rg = _inproj(x, g1, tabs, ws)

        s5p = _s5_params(ssm_lambda_re[l], ssm_lambda_im[l], ssm_log_step[l], ssm_b_re[l], ssm_b_im[l],
                         ssm_c_re[l], ssm_c_im[l], ssm_d[l], ssm_glu_w[l], ssm_glu_b[l])
        ya = _s5(u, s5p)
        yb = _dsa(iq, iwt, dq, ik, dk, dvt)
        yc = _retention(rq, rk, rv, rg, ret_norm_g[l])

        x = _merge(x, g1, ya, yb, yc, wgt, w_proj_a[l].astype(_BF), w_proj_b[l].astype(_BF),
                   w_proj_c[l].astype(_BF), w_out[l].astype(_BF))
        x = _mlp(x, norm2_g[l].reshape(1, D_MODEL), w_ff1[l].astype(_BF), w_ff2[l].astype(_BF), fg,
                 final_norm=(l == depth - 1))
    return x
```
